```python
import math
import jax
import jax.numpy as jnp
from jax import lax
import numpy as np

D_MODEL = 2048
BATCH = 8
SEQ = 8192
DEPTH = 4

N_MIXERS = 3
MIX_WIDTH = D_MODEL
HEAD_DIM = 128
MEM_LEN = 256
MEM_HEADS = 4
MEM_WIDTH = MEM_HEADS * HEAD_DIM
MIXER_WIDTH = MIX_WIDTH - MEM_WIDTH
D_FF = 4 * D_MODEL
EPS = 1e-6
S5_GROUP = 16
S5_GROUPS = MIXER_WIDTH // S5_GROUP
S5_STATE = 64
S5_CHUNK = 128
GDN_HEADS = MIXER_WIDTH // HEAD_DIM
GDN_CONV = 4
GDN_CHUNK = 64
FOX_HEADS = MIXER_WIDTH // HEAD_DIM
FOX_BLOCK = 128
N_S5 = (DEPTH + 2) // N_MIXERS
N_GDN = (DEPTH + 1) // N_MIXERS
N_FOX = DEPTH // N_MIXERS
S5_IN = MIXER_WIDTH + MEM_WIDTH
GDN_IN = 4 * MIXER_WIDTH + 2 * GDN_HEADS + MEM_WIDTH
FOX_IN = 3 * MIXER_WIDTH + FOX_HEADS + MEM_WIDTH

kernel_name = 'hybrid_s5_gdn_fox_memory_trunk'


def _rmsnorm(x, gain):
    x32 = x.astype(jnp.float32)
    y = x32 * lax.rsqrt(jnp.mean(x32 * x32, axis=-1, keepdims=True) + EPS)
    return (y * gain.astype(jnp.float32)).astype(x.dtype)


def _l2norm(x):
    return x * lax.rsqrt(jnp.sum(x * x, axis=-1, keepdims=True) + EPS)


def _complex_affine_combine(e1, e2):
    a1r, a1i, b1r, b1i = e1
    a2r, a2i, b2r, b2i = e2
    return (a1r * a2r - a1i * a2i,
            a1r * a2i + a1i * a2r,
            a2r * b1r - a2i * b1i + b2r,
            a2r * b1i + a2i * b1r + b2i)


def _s5_mixer(u, lam_re, lam_im, log_dt, b_re, b_im, c_re, c_im, d_skip, w_glu, b_glu):
    bsz, seq, _ = u.shape
    f32 = jnp.float32
    n_chunks = seq // S5_CHUNK
    l_re, l_im = lam_re.astype(f32), lam_im.astype(f32)
    dt = jnp.exp(log_dt.astype(f32))[:, None]
    mag = jnp.exp(l_re * dt)
    a_re, a_im = mag * jnp.cos(l_im * dt), mag * jnp.sin(l_im * dt)
    den = l_re * l_re + l_im * l_im
    z_re = ((a_re - 1.0) * l_re + a_im * l_im) / den
    z_im = (a_im * l_re - (a_re - 1.0) * l_im) / den
    br, bi = b_re.astype(f32), b_im.astype(f32)
    bb_re = z_re[..., None] * br - z_im[..., None] * bi
    bb_im = z_re[..., None] * bi + z_im[..., None] * br
    cr, ci = c_re.astype(f32), c_im.astype(f32)
    blk = (bsz, S5_CHUNK, S5_GROUPS, S5_STATE)
    a_blk_re, a_blk_im = jnp.broadcast_to(a_re, blk), jnp.broadcast_to(a_im, blk)
    u32 = u.astype(f32)
    u_chunks = u32.reshape(bsz, n_chunks, S5_CHUNK, S5_GROUPS, S5_GROUP).transpose(1, 0, 2, 3, 4)

    def step(carry, uc):
        s_re, s_im = carry
        bu_re = jnp.einsum('blgc,gpc->blgp', uc, bb_re)
        bu_im = jnp.einsum('blgc,gpc->blgp', uc, bb_im)
        p_re, p_im, h_re, h_im = lax.associative_scan(
            _complex_affine_combine, (a_blk_re, a_blk_im, bu_re, bu_im), axis=1)
        h_re = h_re + p_re * s_re[:, None] - p_im * s_im[:, None]
        h_im = h_im + p_re * s_im[:, None] + p_im * s_re[:, None]
        y = jnp.einsum('blgp,gcp->blgc', h_re, cr) - jnp.einsum('blgp,gcp->blgc', h_im, ci)
        return (h_re[:, -1], h_im[:, -1]), y

    zeros = jnp.zeros((bsz, S5_GROUPS, S5_STATE), f32)
    _, y = lax.scan(step, (zeros, zeros), u_chunks)
    y = y.transpose(1, 0, 2, 3, 4).reshape(bsz, seq, MIXER_WIDTH)
    y = jax.nn.gelu(y + d_skip.astype(f32) * u32).astype(u.dtype)
    return y * jax.nn.sigmoid(y @ w_glu + b_glu)


def _causal_depthwise_conv(x, w):
    k = w.shape[0]
    return lax.conv_general_dilated(
        x, w[:, None, :].astype(x.dtype), window_strides=(1,), padding=[(k - 1, 0)],
        dimension_numbers=('NWC', 'WIO', 'NWC'), feature_group_count=x.shape[-1])


def _gdn_mixer(proj, conv_w, a_log, dt_bias, o_norm):
    bsz, seq, _ = proj.shape
    f32 = jnp.float32
    wd, nh, hd, cl = MIXER_WIDTH, GDN_HEADS, HEAD_DIM, GDN_CHUNK
    nc = seq // cl
    qkv = jax.nn.silu(_causal_depthwise_conv(proj[..., :3 * wd], conv_w)).astype(f32)
    gate = proj[..., 3 * wd:4 * wd].astype(f32).reshape(bsz, seq, nh, hd)
    a_in = proj[..., 4 * wd:4 * wd + nh].astype(f32)
    b_in = proj[..., 4 * wd + nh:4 * wd + 2 * nh].astype(f32)
    q = _l2norm(qkv[..., :wd].reshape(bsz, seq, nh, hd)) * hd ** -0.5
    k = _l2norm(qkv[..., wd:2 * wd].reshape(bsz, seq, nh, hd))
    v = qkv[..., 2 * wd:].reshape(bsz, seq, nh, hd)
    beta = jax.nn.sigmoid(b_in)
    g = -jnp.exp(a_log.astype(f32)) * jax.nn.softplus(a_in + dt_bias.astype(f32))

    def chunks(t):
        return t.reshape(bsz, nc, cl, nh, -1).transpose(0, 3, 1, 2, 4)

    q, k, v = chunks(q), chunks(k), chunks(v)
    beta = chunks(beta[..., None])
    gc = jnp.cumsum(chunks(g[..., None])[..., 0], axis=-1)
    idx = jnp.arange(cl)
    lower = idx[:, None] >= idx[None, :]
    strict = idx[:, None] > idx[None, :]
    decay = jnp.exp(jnp.where(lower, gc[..., :, None] - gc[..., None, :], -jnp.inf))
    kb, vb = k * beta, v * beta
    lmat = jnp.where(strict, jnp.einsum('bhncd,bhnsd->bhncs', kb, k) * decay, 0.0)
    rhs = jnp.concatenate([vb, kb * jnp.exp(gc)[..., None]], axis=-1)
    sol = lax.linalg.triangular_solve(lmat + jnp.eye(cl, dtype=f32), rhs, left_side=True,
                                      lower=True, unit_diagonal=True)
    u_c, w_c = sol[..., :hd], sol[..., hd:]
    attn_in = jnp.where(lower, jnp.einsum('bhncd,bhnsd->bhncs', q, k) * decay, 0.0)
    q_dec = q * jnp.exp(gc)[..., None]
    k_dec = k * jnp.exp(gc[..., -1:] - gc)[..., None]
    g_last = jnp.exp(gc[..., -1])
    xs = tuple(jnp.moveaxis(t, 2, 0) for t in (u_c, w_c, attn_in, q_dec, k_dec, g_last))

    def step(state, inp):
        u_i, w_i, a_i, qd_i, kd_i, gl_i = inp
        v_new = u_i - jnp.einsum('bhck,bhkv->bhcv', w_i, state)
        out = jnp.einsum('bhck,bhkv->bhcv', qd_i, state) + jnp.einsum('bhcs,bhsv->bhcv', a_i, v_new)
        state = state * gl_i[..., None, None] + jnp.einsum('bhck,bhcv->bhkv', kd_i, v_new)
        return state, out

    _, o = lax.scan(step, jnp.zeros((bsz, nh, hd, hd), f32), xs)
    o = o.transpose(1, 0, 3, 2, 4).reshape(bsz, seq, nh, hd)
    o = _rmsnorm(o, o_norm) * jax.nn.silu(gate)
    return o.reshape(bsz, seq, wd).astype(proj.dtype)


def _fox_mixer(proj, b_f):
    bsz, seq, _ = proj.shape
    f32 = jnp.float32
    wd, nh, hd = MIXER_WIDTH, FOX_HEADS, HEAD_DIM

    def heads(t):
        return t.reshape(bsz, seq, nh, hd).transpose(0, 2, 1, 3)

    q, k, v = heads(proj[..., :wd]), heads(proj[..., wd:2 * wd]), heads(proj[..., 2 * wd:3 * wd])
    f_logit = proj[..., 3 * wd:3 * wd + nh].astype(f32) + b_f.astype(f32)
    cum_f = jnp.cumsum(jax.nn.log_sigmoid(f_logit), axis=1).transpose(0, 2, 1)
    scale = hd ** -0.5
    q_off = jnp.arange(FOX_BLOCK)
    outs = []
    for blk in range(seq // FOX_BLOCK):
        q0, q1 = blk * FOX_BLOCK, (blk + 1) * FOX_BLOCK
        logits = (jnp.einsum('bhqd,bhkd->bhqk', q[:, :, q0:q1], k[:, :, :q1]).astype(f32) * scale
                  + cum_f[:, :, q0:q1, None] - cum_f[:, :, None, :q1])
        causal = (q0 + q_off)[:, None] >= jnp.arange(q1)[None, :]
        p = jax.nn.softmax(jnp.where(causal, logits, -jnp.inf), axis=-1).astype(v.dtype)
        outs.append(jnp.einsum('bhqk,bhkd->bhqd', p, v[:, :, :q1]))
    o = jnp.concatenate(outs, axis=2)
    return o.transpose(0, 2, 1, 3).reshape(bsz, seq, wd)


def _memory_attention(q_proj, mem_k, mem_v):
    bsz, seq, _ = q_proj.shape
    q = q_proj.reshape(bsz, seq, MEM_HEADS, HEAD_DIM)
    logits = jnp.einsum('bshd,bmhd->bhsm', q, mem_k).astype(jnp.float32) * HEAD_DIM ** -0.5
    p = jax.nn.softmax(logits, axis=-1).astype(mem_v.dtype)
    return jnp.einsum('bhsm,bmhd->bshd', p, mem_v).reshape(bsz, seq, MEM_WIDTH)


def _fwd_setup_inputs(seed: int = 0) -> dict:
    key = jax.random.key(seed)
    keys = iter(jax.random.split(key, 32))
    f32 = jnp.float32

    def normal(shape, scale):
        return scale * jax.random.normal(next(keys), shape, f32)

    def gain(shape):
        return 1.0 + normal(shape, 0.05)

    def uniform(shape, lo, hi):
        return jax.random.uniform(next(keys), shape, f32, lo, hi)

    x = normal((BATCH, SEQ, D_MODEL), 1.0)
    mem = normal((BATCH, MEM_LEN, D_MODEL), 1.0)
    mem_norm = gain((D_MODEL,))
    w_mem_kv = normal((D_MODEL, 2 * MEM_WIDTH), D_MODEL ** -0.5)
    norm1 = gain((DEPTH, D_MODEL))
    w_out = normal((DEPTH, MIX_WIDTH, D_MODEL), MIX_WIDTH ** -0.5)
    norm2 = gain((DEPTH, D_MODEL))
    w_up = normal((DEPTH, D_MODEL, D_FF), D_MODEL ** -0.5)
    w_down = normal((DEPTH, D_FF, D_MODEL), D_FF ** -0.5)
    norm_f = gain((D_MODEL,))
    s5_w_in = normal((N_S5, D_MODEL, S5_IN), D_MODEL ** -0.5)
    s5_lam_re = -0.5 * jnp.exp(normal((N_S5, S5_GROUPS, S5_STATE), 0.05))
    s5_lam_im = math.pi * jnp.arange(S5_STATE, dtype=f32) + normal((N_S5, S5_GROUPS, S5_STATE), 0.01)
    s5_log_dt = uniform((N_S5, S5_GROUPS), math.log(1e-3), math.log(1e-1))
    s5_b_re = normal((N_S5, S5_GROUPS, S5_STATE, S5_GROUP), (2 * S5_GROUP) ** -0.5)
    s5_b_im = normal((N_S5, S5_GROUPS, S5_STATE, S5_GROUP), (2 * S5_GROUP) ** -0.5)
    s5_c_re = normal((N_S5, S5_GROUPS, S5_GROUP, S5_STATE), S5_STATE ** -0.5)
    s5_c_im = normal((N_S5, S5_GROUPS, S5_GROUP, S5_STATE), S5_STATE ** -0.5)
    s5_d_skip = normal((N_S5, MIXER_WIDTH), 1.0)
    s5_w_glu = normal((N_S5, MIXER_WIDTH, MIXER_WIDTH), MIXER_WIDTH ** -0.5)
    s5_b_glu = normal((N_S5, MIXER_WIDTH), 0.01)
    gdn_w_in = normal((N_GDN, D_MODEL, GDN_IN), D_MODEL ** -0.5)
    gdn_conv_w = normal((N_GDN, GDN_CONV, 3 * MIXER_WIDTH), GDN_CONV ** -0.5)
    gdn_a_log = jnp.log(uniform((N_GDN, GDN_HEADS), 1.0, 16.0))
    dt = jnp.exp(uniform((N_GDN, GDN_HEADS), math.log(1e-3), math.log(1e-1)))
    gdn_dt_bias = dt + jnp.log(-jnp.expm1(-dt))
    gdn_o_norm = gain((N_GDN, HEAD_DIM))
    fox_w_in = normal((N_FOX, D_MODEL, FOX_IN), D_MODEL ** -0.5)
    fox_b_f = uniform((N_FOX, FOX_HEADS), 1.0, 6.0)
    return {'x': x, 'mem': mem, 'mem_norm': mem_norm, 'w_mem_kv': w_mem_kv,
            'norm1': norm1, 'w_out': w_out, 'norm2': norm2, 'w_up': w_up, 'w_down': w_down,
            'norm_f': norm_f,
            's5_w_in': s5_w_in, 's5_lam_re': s5_lam_re, 's5_lam_im': s5_lam_im,
            's5_log_dt': s5_log_dt, 's5_b_re': s5_b_re, 's5_b_im': s5_b_im,
            's5_c_re': s5_c_re, 's5_c_im': s5_c_im, 's5_d_skip': s5_d_skip,
            's5_w_glu': s5_w_glu, 's5_b_glu': s5_b_glu,
            'gdn_w_in': gdn_w_in, 'gdn_conv_w': gdn_conv_w, 'gdn_a_log': gdn_a_log,
            'gdn_dt_bias': gdn_dt_bias, 'gdn_o_norm': gdn_o_norm,
            'fox_w_in': fox_w_in, 'fox_b_f': fox_b_f}


def _fwd_reference(x, mem, mem_norm, w_mem_kv, norm1, w_out, norm2, w_up, w_down, norm_f,
              s5_w_in, s5_lam_re, s5_lam_im, s5_log_dt, s5_b_re, s5_b_im, s5_c_re, s5_c_im,
              s5_d_skip, s5_w_glu, s5_b_glu,
              gdn_w_in, gdn_conv_w, gdn_a_log, gdn_dt_bias, gdn_o_norm,
              fox_w_in, fox_b_f):
    bsz = x.shape[0]
    mkv = _rmsnorm(mem, mem_norm) @ w_mem_kv
    mem_k = mkv[..., :MEM_WIDTH].reshape(bsz, MEM_LEN, MEM_HEADS, HEAD_DIM)
    mem_v = mkv[..., MEM_WIDTH:].reshape(bsz, MEM_LEN, MEM_HEADS, HEAD_DIM)
    h = x
    for i in range(DEPTH):
        kind, j = i % N_MIXERS, i // N_MIXERS
        a = _rmsnorm(h, norm1[i])
        if kind == 0:
            proj = a @ s5_w_in[j]
            mix = _s5_mixer(proj[..., :-MEM_WIDTH], s5_lam_re[j], s5_lam_im[j], s5_log_dt[j],
                            s5_b_re[j], s5_b_im[j], s5_c_re[j], s5_c_im[j], s5_d_skip[j],
                            s5_w_glu[j], s5_b_glu[j])
        elif kind == 1:
            proj = a @ gdn_w_in[j]
            mix = _gdn_mixer(proj[..., :-MEM_WIDTH], gdn_conv_w[j], gdn_a_log[j],
                             gdn_dt_bias[j], gdn_o_norm[j])
        else:
            proj = a @ fox_w_in[j]
            mix = _fox_mixer(proj[..., :-MEM_WIDTH], fox_b_f[j])
        read = _memory_attention(proj[..., -MEM_WIDTH:], mem_k, mem_v)
        h = h + jnp.concatenate([mix, read], axis=-1) @ w_out[i]
        a = _rmsnorm(h, norm2[i])
        h = h + jnp.square(jax.nn.relu(a @ w_up[i])) @ w_down[i]
    return _rmsnorm(h, norm_f)


import jax as _jax
import jax.numpy as _jnp

TWIN_FORMAT = 'train_step'
FWD_PARAMS = ['x', 'mem', 'mem_norm', 'w_mem_kv', 'norm1', 'w_out', 'norm2', 'w_up', 'w_down', 'norm_f', 's5_w_in', 's5_lam_re', 's5_lam_im', 's5_log_dt', 's5_b_re', 's5_b_im', 's5_c_re', 's5_c_im', 's5_d_skip', 's5_w_glu', 's5_b_glu', 'gdn_w_in', 'gdn_conv_w', 'gdn_a_log', 'gdn_dt_bias', 'gdn_o_norm', 'fox_w_in', 'fox_b_f']
TWIN_WEIGHTS = ['mem_norm', 'w_mem_kv', 'norm1', 'w_out', 'norm2', 'w_up', 'w_down', 'norm_f', 's5_w_in', 's5_lam_re', 's5_lam_im', 's5_log_dt', 's5_b_re', 's5_b_im', 's5_c_re', 's5_c_im', 's5_d_skip', 's5_w_glu', 's5_b_glu', 'gdn_w_in', 'gdn_conv_w', 'gdn_a_log', 'gdn_dt_bias', 'gdn_o_norm', 'fox_w_in', 'fox_b_f']
TWIN_DIFF_INPUT = 'x'
TWIN_INPUTS = ['x', 'mem', 'mem_norm', 'w_mem_kv', 'norm1', 'w_out', 'norm2', 'w_up', 'w_down', 'norm_f', 's5_w_in', 's5_lam_re', 's5_lam_im', 's5_log_dt', 's5_b_re', 's5_b_im', 's5_c_re', 's5_c_im', 's5_d_skip', 's5_w_glu', 's5_b_glu', 'gdn_w_in', 'gdn_conv_w', 'gdn_a_log', 'gdn_dt_bias', 'gdn_o_norm', 'fox_w_in', 'fox_b_f', 'loss_target', 'm_mem_norm', 'm_w_mem_kv', 'm_norm1', 'm_w_out', 'm_norm2', 'm_w_up', 'm_w_down', 'm_norm_f', 'm_s5_w_in', 'm_s5_lam_re', 'm_s5_lam_im', 'm_s5_log_dt', 'm_s5_b_re', 'm_s5_b_im', 'm_s5_c_re', 'm_s5_c_im', 'm_s5_d_skip', 'm_s5_w_glu', 'm_s5_b_glu', 'm_gdn_w_in', 'm_gdn_conv_w', 'm_gdn_a_log', 'm_gdn_dt_bias', 'm_gdn_o_norm', 'm_fox_w_in', 'm_fox_b_f', 'v_mem_norm', 'v_w_mem_kv', 'v_norm1', 'v_w_out', 'v_norm2', 'v_w_up', 'v_w_down', 'v_norm_f', 'v_s5_w_in', 'v_s5_lam_re', 'v_s5_lam_im', 'v_s5_log_dt', 'v_s5_b_re', 'v_s5_b_im', 'v_s5_c_re', 'v_s5_c_im', 'v_s5_d_skip', 'v_s5_w_glu', 'v_s5_b_glu', 'v_gdn_w_in', 'v_gdn_conv_w', 'v_gdn_a_log', 'v_gdn_dt_bias', 'v_gdn_o_norm', 'v_fox_w_in', 'v_fox_b_f']
TWIN_OUTPUTS = ['loss', 'grad_x', 'grad_mem_norm', 'grad_w_mem_kv', 'grad_norm1', 'grad_w_out', 'grad_norm2', 'grad_w_up', 'grad_w_down', 'grad_norm_f', 'grad_s5_w_in', 'grad_s5_lam_re', 'grad_s5_lam_im', 'grad_s5_log_dt', 'grad_s5_b_re', 'grad_s5_b_im', 'grad_s5_c_re', 'grad_s5_c_im', 'grad_s5_d_skip', 'grad_s5_w_glu', 'grad_s5_b_glu', 'grad_gdn_w_in', 'grad_gdn_conv_w', 'grad_gdn_a_log', 'grad_gdn_dt_bias', 'grad_gdn_o_norm', 'grad_fox_w_in', 'grad_fox_b_f', 'delta_mem_norm', 'delta_w_mem_kv', 'delta_norm1', 'delta_w_out', 'delta_norm2', 'delta_w_up', 'delta_w_down', 'delta_norm_f', 'delta_s5_w_in', 'delta_s5_lam_re', 'delta_s5_lam_im', 'delta_s5_log_dt', 'delta_s5_b_re', 'delta_s5_b_im', 'delta_s5_c_re', 'delta_s5_c_im', 'delta_s5_d_skip', 'delta_s5_w_glu', 'delta_s5_b_glu', 'delta_gdn_w_in', 'delta_gdn_conv_w', 'delta_gdn_a_log', 'delta_gdn_dt_bias', 'delta_gdn_o_norm', 'delta_fox_w_in', 'delta_fox_b_f', 'new_m_mem_norm', 'new_m_w_mem_kv', 'new_m_norm1', 'new_m_w_out', 'new_m_norm2', 'new_m_w_up', 'new_m_w_down', 'new_m_norm_f', 'new_m_s5_w_in', 'new_m_s5_lam_re', 'new_m_s5_lam_im', 'new_m_s5_log_dt', 'new_m_s5_b_re', 'new_m_s5_b_im', 'new_m_s5_c_re', 'new_m_s5_c_im', 'new_m_s5_d_skip', 'new_m_s5_w_glu', 'new_m_s5_b_glu', 'new_m_gdn_w_in', 'new_m_gdn_conv_w', 'new_m_gdn_a_log', 'new_m_gdn_dt_bias', 'new_m_gdn_o_norm', 'new_m_fox_w_in', 'new_m_fox_b_f', 'new_v_mem_norm', 'new_v_w_mem_kv', 'new_v_norm1', 'new_v_w_out', 'new_v_norm2', 'new_v_w_up', 'new_v_w_down', 'new_v_norm_f', 'new_v_s5_w_in', 'new_v_s5_lam_re', 'new_v_s5_lam_im', 'new_v_s5_log_dt', 'new_v_s5_b_re', 'new_v_s5_b_im', 'new_v_s5_c_re', 'new_v_s5_c_im', 'new_v_s5_d_skip', 'new_v_s5_w_glu', 'new_v_s5_b_glu', 'new_v_gdn_w_in', 'new_v_gdn_conv_w', 'new_v_gdn_a_log', 'new_v_gdn_dt_bias', 'new_v_gdn_o_norm', 'new_v_fox_w_in', 'new_v_fox_b_f']
TWIN_LEAF_KINDS = {'loss': 'loss', 'grad_x': 'grad_x', 'grad_mem_norm': 'grad_w', 'grad_w_mem_kv': 'grad_w', 'grad_norm1': 'grad_w', 'grad_w_out': 'grad_w', 'grad_norm2': 'grad_w', 'grad_w_up': 'grad_w', 'grad_w_down': 'grad_w', 'grad_norm_f': 'grad_w', 'grad_s5_w_in': 'grad_w', 'grad_s5_lam_re': 'grad_w', 'grad_s5_lam_im': 'grad_w', 'grad_s5_log_dt': 'grad_w', 'grad_s5_b_re': 'grad_w', 'grad_s5_b_im': 'grad_w', 'grad_s5_c_re': 'grad_w', 'grad_s5_c_im': 'grad_w', 'grad_s5_d_skip': 'grad_w', 'grad_s5_w_glu': 'grad_w', 'grad_s5_b_glu': 'grad_w', 'grad_gdn_w_in': 'grad_w', 'grad_gdn_conv_w': 'grad_w', 'grad_gdn_a_log': 'grad_w', 'grad_gdn_dt_bias': 'grad_w', 'grad_gdn_o_norm': 'grad_w', 'grad_fox_w_in': 'grad_w', 'grad_fox_b_f': 'grad_w', 'delta_mem_norm': 'delta_w', 'delta_w_mem_kv': 'delta_w', 'delta_norm1': 'delta_w', 'delta_w_out': 'delta_w', 'delta_norm2': 'delta_w', 'delta_w_up': 'delta_w', 'delta_w_down': 'delta_w', 'delta_norm_f': 'delta_w', 'delta_s5_w_in': 'delta_w', 'delta_s5_lam_re': 'delta_w', 'delta_s5_lam_im': 'delta_w', 'delta_s5_log_dt': 'delta_w', 'delta_s5_b_re': 'delta_w', 'delta_s5_b_im': 'delta_w', 'delta_s5_c_re': 'delta_w', 'delta_s5_c_im': 'delta_w', 'delta_s5_d_skip': 'delta_w', 'delta_s5_w_glu': 'delta_w', 'delta_s5_b_glu': 'delta_w', 'delta_gdn_w_in': 'delta_w', 'delta_gdn_conv_w': 'delta_w', 'delta_gdn_a_log': 'delta_w', 'delta_gdn_dt_bias': 'delta_w', 'delta_gdn_o_norm': 'delta_w', 'delta_fox_w_in': 'delta_w', 'delta_fox_b_f': 'delta_w', 'new_m_mem_norm': 'new_m', 'new_m_w_mem_kv': 'new_m', 'new_m_norm1': 'new_m', 'new_m_w_out': 'new_m', 'new_m_norm2': 'new_m', 'new_m_w_up': 'new_m', 'new_m_w_down': 'new_m', 'new_m_norm_f': 'new_m', 'new_m_s5_w_in': 'new_m', 'new_m_s5_lam_re': 'new_m', 'new_m_s5_lam_im': 'new_m', 'new_m_s5_log_dt': 'new_m', 'new_m_s5_b_re': 'new_m', 'new_m_s5_b_im': 'new_m', 'new_m_s5_c_re': 'new_m', 'new_m_s5_c_im': 'new_m', 'new_m_s5_d_skip': 'new_m', 'new_m_s5_w_glu': 'new_m', 'new_m_s5_b_glu': 'new_m', 'new_m_gdn_w_in': 'new_m', 'new_m_gdn_conv_w': 'new_m', 'new_m_gdn_a_log': 'new_m', 'new_m_gdn_dt_bias': 'new_m', 'new_m_gdn_o_norm': 'new_m', 'new_m_fox_w_in': 'new_m', 'new_m_fox_b_f': 'new_m', 'new_v_mem_norm': 'new_v', 'new_v_w_mem_kv': 'new_v', 'new_v_norm1': 'new_v', 'new_v_w_out': 'new_v', 'new_v_norm2': 'new_v', 'new_v_w_up': 'new_v', 'new_v_w_down': 'new_v', 'new_v_norm_f': 'new_v', 'new_v_s5_w_in': 'new_v', 'new_v_s5_lam_re': 'new_v', 'new_v_s5_lam_im': 'new_v', 'new_v_s5_log_dt': 'new_v', 'new_v_s5_b_re': 'new_v', 'new_v_s5_b_im': 'new_v', 'new_v_s5_c_re': 'new_v', 'new_v_s5_c_im': 'new_v', 'new_v_s5_d_skip': 'new_v', 'new_v_s5_w_glu': 'new_v', 'new_v_s5_b_glu': 'new_v', 'new_v_gdn_w_in': 'new_v', 'new_v_gdn_conv_w': 'new_v', 'new_v_gdn_a_log': 'new_v', 'new_v_gdn_dt_bias': 'new_v', 'new_v_gdn_o_norm': 'new_v', 'new_v_fox_w_in': 'new_v', 'new_v_fox_b_f': 'new_v'}


def _forward(args):
    return _fwd_reference(*[args[k] for k in FWD_PARAMS])


def _output_shape():
    def fwd():
        inp = _fwd_setup_inputs(0)
        return _fwd_reference(*[inp[k] for k in FWD_PARAMS])
    out = _jax.eval_shape(fwd)
    return out.shape, out.dtype

N_MICROBATCH = 1
ADAM_LR = 0.001
ADAM_B1 = 0.9
ADAM_B2 = 0.999
ADAM_EPS = 1e-08
ADAM_WD = 0.01
ADAM_STEP = 10
PER_EXAMPLE_BATCH_AXIS = {'x': 0, 'mem': 0, 'loss_target': 0}
SHARED_INPUTS = []
_WEIGHT_DTYPES = {'mem_norm': _jnp.float32, 'w_mem_kv': _jnp.float32, 'norm1': _jnp.float32, 'w_out': _jnp.float32, 'norm2': _jnp.float32, 'w_up': _jnp.float32, 'w_down': _jnp.float32, 'norm_f': _jnp.float32, 's5_w_in': _jnp.float32, 's5_lam_re': _jnp.float32, 's5_lam_im': _jnp.float32, 's5_log_dt': _jnp.float32, 's5_b_re': _jnp.float32, 's5_b_im': _jnp.float32, 's5_c_re': _jnp.float32, 's5_c_im': _jnp.float32, 's5_d_skip': _jnp.float32, 's5_w_glu': _jnp.float32, 's5_b_glu': _jnp.float32, 'gdn_w_in': _jnp.float32, 'gdn_conv_w': _jnp.float32, 'gdn_a_log': _jnp.float32, 'gdn_dt_bias': _jnp.float32, 'gdn_o_norm': _jnp.float32, 'fox_w_in': _jnp.float32, 'fox_b_f': _jnp.float32}
MOMENT_SCALE = {'mem_norm': 1.883634e-02, 'w_mem_kv': 2.502990e-02, 'norm1': 7.946221e-02, 'w_out': 1.014365e-01, 'norm2': 1.336735e-01, 'w_up': 6.854095e-02, 'w_down': 2.600304e-01, 'norm_f': 3.303907e+01, 's5_w_in': 4.621950e-02, 's5_lam_re': 8.037941e-03, 's5_lam_im': 9.526046e-03, 's5_log_dt': 2.944411e+00, 's5_b_re': 3.820988e-03, 's5_b_im': 4.084142e-03, 's5_c_re': 5.321417e-03, 's5_c_im': 6.234661e-03, 's5_d_skip': 9.750076e-02, 's5_w_glu': 1.752622e-02, 's5_b_glu': 4.157288e-02, 'gdn_w_in': 5.057723e-02, 'gdn_conv_w': 5.356643e-02, 'gdn_a_log': 1.649720e-01, 'gdn_dt_bias': 1.630906e-01, 'gdn_o_norm': 2.977396e-01, 'fox_w_in': 9.586634e-02, 'fox_b_f': 1.789631e-01}


def _to_microbatches(a, axis):
    t = _jnp.moveaxis(a, axis, 0)
    t = t.reshape((N_MICROBATCH, t.shape[0] // N_MICROBATCH) + t.shape[1:])
    return _jnp.moveaxis(t, 1, axis + 1)


def setup_inputs(seed: int = 0) -> dict:
    inp = _fwd_setup_inputs(seed)
    key = _jax.random.fold_in(_jax.random.key(seed), 7919)
    shape, _ = _output_shape()
    out = dict(inp)
    out["loss_target"] = _jax.random.normal(_jax.random.fold_in(key, 0), shape, _jnp.float32)
    for i, name in enumerate(TWIN_WEIGHTS):
        w = inp[name].astype(_jnp.float32)
        if MOMENT_SCALE is None:
            s = _jnp.sqrt(_jnp.mean(_jnp.square(w)) + 1e-30)
        else:
            s = MOMENT_SCALE[name]
        km, kv = _jax.random.split(_jax.random.fold_in(key, i + 1))
        out[name] = w
        out["m_" + name] = s * _jax.random.normal(km, w.shape, _jnp.float32)
        out["v_" + name] = (s * s) * _jax.random.uniform(kv, w.shape, _jnp.float32, 0.5, 1.5)
    if N_MICROBATCH > 1:
        for name, axis in PER_EXAMPLE_BATCH_AXIS.items():
            out[name] = _to_microbatches(out[name], axis)
    return {'x': out['x'], 'mem': out['mem'], 'mem_norm': out['mem_norm'], 'w_mem_kv': out['w_mem_kv'], 'norm1': out['norm1'], 'w_out': out['w_out'], 'norm2': out['norm2'], 'w_up': out['w_up'], 'w_down': out['w_down'], 'norm_f': out['norm_f'], 's5_w_in': out['s5_w_in'], 's5_lam_re': out['s5_lam_re'], 's5_lam_im': out['s5_lam_im'], 's5_log_dt': out['s5_log_dt'], 's5_b_re': out['s5_b_re'], 's5_b_im': out['s5_b_im'], 's5_c_re': out['s5_c_re'], 's5_c_im': out['s5_c_im'], 's5_d_skip': out['s5_d_skip'], 's5_w_glu': out['s5_w_glu'], 's5_b_glu': out['s5_b_glu'], 'gdn_w_in': out['gdn_w_in'], 'gdn_conv_w': out['gdn_conv_w'], 'gdn_a_log': out['gdn_a_log'], 'gdn_dt_bias': out['gdn_dt_bias'], 'gdn_o_norm': out['gdn_o_norm'], 'fox_w_in': out['fox_w_in'], 'fox_b_f': out['fox_b_f'], 'loss_target': out['loss_target'], 'm_mem_norm': out['m_mem_norm'], 'm_w_mem_kv': out['m_w_mem_kv'], 'm_norm1': out['m_norm1'], 'm_w_out': out['m_w_out'], 'm_norm2': out['m_norm2'], 'm_w_up': out['m_w_up'], 'm_w_down': out['m_w_down'], 'm_norm_f': out['m_norm_f'], 'm_s5_w_in': out['m_s5_w_in'], 'm_s5_lam_re': out['m_s5_lam_re'], 'm_s5_lam_im': out['m_s5_lam_im'], 'm_s5_log_dt': out['m_s5_log_dt'], 'm_s5_b_re': out['m_s5_b_re'], 'm_s5_b_im': out['m_s5_b_im'], 'm_s5_c_re': out['m_s5_c_re'], 'm_s5_c_im': out['m_s5_c_im'], 'm_s5_d_skip': out['m_s5_d_skip'], 'm_s5_w_glu': out['m_s5_w_glu'], 'm_s5_b_glu': out['m_s5_b_glu'], 'm_gdn_w_in': out['m_gdn_w_in'], 'm_gdn_conv_w': out['m_gdn_conv_w'], 'm_gdn_a_log': out['m_gdn_a_log'], 'm_gdn_dt_bias': out['m_gdn_dt_bias'], 'm_gdn_o_norm': out['m_gdn_o_norm'], 'm_fox_w_in': out['m_fox_w_in'], 'm_fox_b_f': out['m_fox_b_f'], 'v_mem_norm': out['v_mem_norm'], 'v_w_mem_kv': out['v_w_mem_kv'], 'v_norm1': out['v_norm1'], 'v_w_out': out['v_w_out'], 'v_norm2': out['v_norm2'], 'v_w_up': out['v_w_up'], 'v_w_down': out['v_w_down'], 'v_norm_f': out['v_norm_f'], 'v_s5_w_in': out['v_s5_w_in'], 'v_s5_lam_re': out['v_s5_lam_re'], 'v_s5_lam_im': out['v_s5_lam_im'], 'v_s5_log_dt': out['v_s5_log_dt'], 'v_s5_b_re': out['v_s5_b_re'], 'v_s5_b_im': out['v_s5_b_im'], 'v_s5_c_re': out['v_s5_c_re'], 'v_s5_c_im': out['v_s5_c_im'], 'v_s5_d_skip': out['v_s5_d_skip'], 'v_s5_w_glu': out['v_s5_w_glu'], 'v_s5_b_glu': out['v_s5_b_glu'], 'v_gdn_w_in': out['v_gdn_w_in'], 'v_gdn_conv_w': out['v_gdn_conv_w'], 'v_gdn_a_log': out['v_gdn_a_log'], 'v_gdn_dt_bias': out['v_gdn_dt_bias'], 'v_gdn_o_norm': out['v_gdn_o_norm'], 'v_fox_w_in': out['v_fox_w_in'], 'v_fox_b_f': out['v_fox_b_f']}


def _loss(weights, diff, rest, loss_target):
    with _jax.named_scope("forward"):
        args = {**rest, TWIN_DIFF_INPUT: diff, **{k: w.astype(_WEIGHT_DTYPES[k]) for k, w in weights.items()}}
        y = _forward(args)
    with _jax.named_scope("loss_head"):
        err = _jnp.square(y.astype(_jnp.float32) - loss_target)
        return 0.5 * _jnp.sum(_jnp.mean(err, axis=-1)) if err.ndim else 0.5 * err


def _adamw(w, g, m, v):
    m = ADAM_B1 * m + (1.0 - ADAM_B1) * g
    v = ADAM_B2 * v + (1.0 - ADAM_B2) * _jnp.square(g)
    m_hat = m / (1.0 - ADAM_B1 ** ADAM_STEP)
    v_hat = v / (1.0 - ADAM_B2 ** ADAM_STEP)
    delta = -ADAM_LR * (m_hat / (_jnp.sqrt(v_hat) + ADAM_EPS) + ADAM_WD * w)
    return delta, m, v


def reference(x, mem, mem_norm, w_mem_kv, norm1, w_out, norm2, w_up, w_down, norm_f, s5_w_in, s5_lam_re, s5_lam_im, s5_log_dt, s5_b_re, s5_b_im, s5_c_re, s5_c_im, s5_d_skip, s5_w_glu, s5_b_glu, gdn_w_in, gdn_conv_w, gdn_a_log, gdn_dt_bias, gdn_o_norm, fox_w_in, fox_b_f, loss_target, m_mem_norm, m_w_mem_kv, m_norm1, m_w_out, m_norm2, m_w_up, m_w_down, m_norm_f, m_s5_w_in, m_s5_lam_re, m_s5_lam_im, m_s5_log_dt, m_s5_b_re, m_s5_b_im, m_s5_c_re, m_s5_c_im, m_s5_d_skip, m_s5_w_glu, m_s5_b_glu, m_gdn_w_in, m_gdn_conv_w, m_gdn_a_log, m_gdn_dt_bias, m_gdn_o_norm, m_fox_w_in, m_fox_b_f, v_mem_norm, v_w_mem_kv, v_norm1, v_w_out, v_norm2, v_w_up, v_w_down, v_norm_f, v_s5_w_in, v_s5_lam_re, v_s5_lam_im, v_s5_log_dt, v_s5_b_re, v_s5_b_im, v_s5_c_re, v_s5_c_im, v_s5_d_skip, v_s5_w_glu, v_s5_b_glu, v_gdn_w_in, v_gdn_conv_w, v_gdn_a_log, v_gdn_dt_bias, v_gdn_o_norm, v_fox_w_in, v_fox_b_f):
    given = dict(x=x, mem=mem, mem_norm=mem_norm, w_mem_kv=w_mem_kv, norm1=norm1, w_out=w_out, norm2=norm2, w_up=w_up, w_down=w_down, norm_f=norm_f, s5_w_in=s5_w_in, s5_lam_re=s5_lam_re, s5_lam_im=s5_lam_im, s5_log_dt=s5_log_dt, s5_b_re=s5_b_re, s5_b_im=s5_b_im, s5_c_re=s5_c_re, s5_c_im=s5_c_im, s5_d_skip=s5_d_skip, s5_w_glu=s5_w_glu, s5_b_glu=s5_b_glu, gdn_w_in=gdn_w_in, gdn_conv_w=gdn_conv_w, gdn_a_log=gdn_a_log, gdn_dt_bias=gdn_dt_bias, gdn_o_norm=gdn_o_norm, fox_w_in=fox_w_in, fox_b_f=fox_b_f, loss_target=loss_target, m_mem_norm=m_mem_norm, m_w_mem_kv=m_w_mem_kv, m_norm1=m_norm1, m_w_out=m_w_out, m_norm2=m_norm2, m_w_up=m_w_up, m_w_down=m_w_down, m_norm_f=m_norm_f, m_s5_w_in=m_s5_w_in, m_s5_lam_re=m_s5_lam_re, m_s5_lam_im=m_s5_lam_im, m_s5_log_dt=m_s5_log_dt, m_s5_b_re=m_s5_b_re, m_s5_b_im=m_s5_b_im, m_s5_c_re=m_s5_c_re, m_s5_c_im=m_s5_c_im, m_s5_d_skip=m_s5_d_skip, m_s5_w_glu=m_s5_w_glu, m_s5_b_glu=m_s5_b_glu, m_gdn_w_in=m_gdn_w_in, m_gdn_conv_w=m_gdn_conv_w, m_gdn_a_log=m_gdn_a_log, m_gdn_dt_bias=m_gdn_dt_bias, m_gdn_o_norm=m_gdn_o_norm, m_fox_w_in=m_fox_w_in, m_fox_b_f=m_fox_b_f, v_mem_norm=v_mem_norm, v_w_mem_kv=v_w_mem_kv, v_norm1=v_norm1, v_w_out=v_w_out, v_norm2=v_norm2, v_w_up=v_w_up, v_w_down=v_w_down, v_norm_f=v_norm_f, v_s5_w_in=v_s5_w_in, v_s5_lam_re=v_s5_lam_re, v_s5_lam_im=v_s5_lam_im, v_s5_log_dt=v_s5_log_dt, v_s5_b_re=v_s5_b_re, v_s5_b_im=v_s5_b_im, v_s5_c_re=v_s5_c_re, v_s5_c_im=v_s5_c_im, v_s5_d_skip=v_s5_d_skip, v_s5_w_glu=v_s5_w_glu, v_s5_b_glu=v_s5_b_glu, v_gdn_w_in=v_gdn_w_in, v_gdn_conv_w=v_gdn_conv_w, v_gdn_a_log=v_gdn_a_log, v_gdn_dt_bias=v_gdn_dt_bias, v_gdn_o_norm=v_gdn_o_norm, v_fox_w_in=v_fox_w_in, v_fox_b_f=v_fox_b_f)
    weights = {n: given[n] for n in TWIN_WEIGHTS}
    shared = {n: given[n] for n in SHARED_INPUTS}
    per_example = {n: given[n] for n in ['x', 'mem']}
    grad_fn = _jax.value_and_grad(_loss, argnums=(0, 1))

    def one_microbatch(ex, loss_target):
        ex = dict(ex)
        diff = ex.pop(TWIN_DIFF_INPUT)
        return grad_fn(weights, diff, {**shared, **ex}, loss_target)

    if N_MICROBATCH == 1:
        loss, (grad_w, grad_x) = one_microbatch(per_example, given["loss_target"])
    else:
        def body(carry, xs):
            loss_sum, grad_sum = carry
            l_k, (gw_k, gx_k) = one_microbatch(xs[0], xs[1])
            with _jax.named_scope("update"):
                return (loss_sum + l_k, _jax.tree.map(_jnp.add, grad_sum, gw_k)), gx_k

        init = (_jnp.zeros((), _jnp.float32), _jax.tree.map(_jnp.zeros_like, weights))
        (loss, grad_w), grad_x = _jax.lax.scan(body, init, (per_example, given["loss_target"]))
    with _jax.named_scope("update"):
        delta_w, new_m, new_v = {}, {}, {}
        for n in TWIN_WEIGHTS:
            delta_w[n], new_m[n], new_v[n] = _adamw(weights[n], grad_w[n], given["m_" + n], given["v_" + n])
    return (loss, grad_x, *[grad_w[n] for n in TWIN_WEIGHTS], *[delta_w[n] for n in TWIN_WEIGHTS],
            *[new_m[n] for n in TWIN_WEIGHTS], *[new_v[n] for n in TWIN_WEIGHTS])
```

```python
import functools
import math

import jax
import jax.numpy as jnp
from jax import lax
from jax.experimental import pallas as pl
from jax.experimental.pallas import tpu as pltpu

F32 = jnp.float32
BF16 = jnp.bfloat16

HEAD_DIM = 128
MEM_HEADS = 4
MEM_WIDTH = MEM_HEADS * HEAD_DIM
S5_GROUP = 16
S5_STATE = 64
GDN_CHUNK = 64
EPS = 1e-6
ADAM_LR, ADAM_B1, ADAM_B2, ADAM_EPS, ADAM_WD, ADAM_STEP = 0.001, 0.9, 0.999, 1e-08, 0.01, 10

N_DEV = 8
MESH = pl.DeviceIdType.MESH
VMEM_LIMIT = 56 * 1024 * 1024
HI = lax.Precision.HIGHEST


def _tile(n, prefs=(1024, 512, 256, 128)):
    for t in prefs:
        if n % t == 0:
            return t
    return n


def _cparams(sem=None):
    return pltpu.CompilerParams(dimension_semantics=sem, vmem_limit_bytes=VMEM_LIMIT)


def _dot(a, b, ca=1, cb=0, precision=None):
    return lax.dot_general(a, b, (((ca,), (cb,)), ((), ())), preferred_element_type=F32, precision=precision)


def _bdot(a, b):
    return _dot(a.astype(BF16), b.astype(BF16))


def mm(a, b, *, ta=False, tb=False, extras=(), epilogue=None, out_dtypes=(F32,), name, tm=None, tn=None, tk=None):
    M, K = (a.shape[1], a.shape[0]) if ta else a.shape
    N = b.shape[0] if tb else b.shape[1]
    assert K == (b.shape[1] if tb else b.shape[0]), (a.shape, b.shape, ta, tb)
    tm = tm or _tile(M)
    tn = tn or _tile(N)
    tk = tk or _tile(K, (512, 256, 128))
    nk = K // tk
    n_ex, n_out = len(extras), len(out_dtypes)
    ca, cb = (0 if ta else 1), (1 if tb else 0)

    def body(a_ref, b_ref, *rest):
        ex_refs, out_refs, acc = rest[:n_ex], rest[n_ex:n_ex + n_out], rest[-1]
        k = pl.program_id(2)

        @pl.when(k == 0)
        def _():
            acc[...] = jnp.zeros_like(acc)

        acc[...] += _dot(a_ref[...].astype(BF16), b_ref[...].astype(BF16), ca, cb)

        @pl.when(k == nk - 1)
        def _():
            res = acc[...]
            outs = (res,) if epilogue is None else epilogue(res, *[e[...] for e in ex_refs])
            for o, v in zip(out_refs, outs):
                o[...] = v.astype(o.dtype)

    a_spec = pl.BlockSpec((tk, tm), lambda i, j, k: (k, i)) if ta else pl.BlockSpec((tm, tk), lambda i, j, k: (i, k))
    b_spec = pl.BlockSpec((tn, tk), lambda i, j, k: (j, k)) if tb else pl.BlockSpec((tk, tn), lambda i, j, k: (k, j))
    ex_specs = [pl.BlockSpec((1, tn), lambda i, j, k: (0, j)) if e.shape[0] == 1 and M != 1
                else pl.BlockSpec((tm, tn), lambda i, j, k: (i, j)) for e in extras]
    outs = pl.pallas_call(
        body, name=name, grid=(M // tm, N // tn, nk),
        in_specs=[a_spec, b_spec] + ex_specs,
        out_specs=[pl.BlockSpec((tm, tn), lambda i, j, k: (i, j)) for _ in out_dtypes],
        out_shape=[jax.ShapeDtypeStruct((M, N), d) for d in out_dtypes],
        scratch_shapes=[pltpu.VMEM((tm, tn), F32)],
        compiler_params=_cparams(("parallel", "parallel", "arbitrary")),
    )(a, b, *extras)
    return outs[0] if n_out == 1 else outs


def rms_fwd(x, g, *, out_dtype, name):
    S, D = x.shape
    ts = _tile(S, (512, 256, 128))

    def body(x_ref, g_ref, y_ref):
        x = x_ref[...]
        r = lax.rsqrt(jnp.mean(x * x, axis=-1, keepdims=True) + EPS)
        y_ref[...] = (x * r * g_ref[...]).astype(y_ref.dtype)

    return pl.pallas_call(
        body, name=name, grid=(S // ts,),
        in_specs=[pl.BlockSpec((ts, D), lambda i: (i, 0)), pl.BlockSpec((1, D), lambda i: (0, 0))],
        out_specs=pl.BlockSpec((ts, D), lambda i: (i, 0)),
        out_shape=jax.ShapeDtypeStruct((S, D), out_dtype),
        compiler_params=_cparams(("parallel",)),
    )(x, g.reshape(1, D))


def rms_bwd(x, g, dy, dres, *, name):
    S, D = x.shape
    ts = _tile(S, (512, 256, 128))
    has_res = dres is not None

    def body(x_ref, g_ref, dy_ref, *rest):
        dx_ref, dg_ref = rest[-2:]
        x = x_ref[...]
        r = lax.rsqrt(jnp.mean(x * x, axis=-1, keepdims=True) + EPS)
        xh = x * r
        dy = dy_ref[...].astype(F32)
        dxh = dy * g_ref[...]
        dx = r * (dxh - xh * jnp.mean(dxh * xh, axis=-1, keepdims=True))
        if has_res:
            dx = dx + rest[0][...]
        dx_ref[...] = dx

        @pl.when(pl.program_id(0) == 0)
        def _():
            dg_ref[...] = jnp.zeros_like(dg_ref)

        dg_ref[...] += jnp.sum(dy * xh, axis=0, keepdims=True)

    tok = pl.BlockSpec((ts, D), lambda i: (i, 0))
    row = pl.BlockSpec((1, D), lambda i: (0, 0))
    return pl.pallas_call(
        body, name=name, grid=(S // ts,),
        in_specs=[tok, row, tok] + ([tok] if has_res else []),
        out_specs=[tok, row],
        out_shape=[jax.ShapeDtypeStruct((S, D), F32), jax.ShapeDtypeStruct((1, D), F32)],
        compiler_params=_cparams(("arbitrary",)),
    )(x, g.reshape(1, D), dy, *([dres] if has_res else []))


def final_loss(h, g, target, *, name):
    S, D = h.shape
    ts = _tile(S, (512, 256, 128))

    def body(x_ref, g_ref, t_ref, loss_ref, dx_ref, dg_ref):
        x = x_ref[...]
        r = lax.rsqrt(jnp.mean(x * x, axis=-1, keepdims=True) + EPS)
        xh = x * r
        err = xh * g_ref[...] - t_ref[...]
        dy = err * (1.0 / D)
        dxh = dy * g_ref[...]
        dx_ref[...] = r * (dxh - xh * jnp.mean(dxh * xh, axis=-1, keepdims=True))

        @pl.when(pl.program_id(0) == 0)
        def _():
            dg_ref[...] = jnp.zeros_like(dg_ref)
            loss_ref[...] = jnp.zeros_like(loss_ref)

        dg_ref[...] += jnp.sum(dy * xh, axis=0, keepdims=True)
        loss_ref[...] += jnp.sum(err * err, axis=0, keepdims=True) * (0.5 / D)

    tok = pl.BlockSpec((ts, D), lambda i: (i, 0))
    row = pl.BlockSpec((1, D), lambda i: (0, 0))
    return pl.pallas_call(
        body, name=name, grid=(S // ts,),
        in_specs=[tok, row, tok], out_specs=[row, tok, row],
        out_shape=[jax.ShapeDtypeStruct((1, D), F32), jax.ShapeDtypeStruct((S, D), F32),
                   jax.ShapeDtypeStruct((1, D), F32)],
        compiler_params=_cparams(("arbitrary",)),
    )(h, g.reshape(1, D), target)


def _mem_probs(q, k):
    s = _dot(q.astype(BF16), k.astype(BF16), 1, 1) * HEAD_DIM ** -0.5
    p = jnp.exp(s - jnp.max(s, axis=-1, keepdims=True))
    return p / jnp.sum(p, axis=-1, keepdims=True)


def mem_fwd(proj, q_col, mkv, *, name):
    S = proj.shape[0]
    L = mkv.shape[0]
    ts = _tile(S)

    def body(q_ref, kv_ref, o_ref):
        for h in range(MEM_HEADS):
            c = slice(h * HEAD_DIM, (h + 1) * HEAD_DIM)
            v = kv_ref[:, MEM_WIDTH + h * HEAD_DIM:MEM_WIDTH + (h + 1) * HEAD_DIM]
            p = _mem_probs(q_ref[:, c], kv_ref[:, c])
            o_ref[:, c] = _bdot(p, v).astype(o_ref.dtype)

    return pl.pallas_call(
        body, name=name, grid=(S // ts,),
        in_specs=[pl.BlockSpec((ts, MEM_WIDTH), lambda i: (i, q_col // MEM_WIDTH)),
                  pl.BlockSpec((L, 2 * MEM_WIDTH), lambda i: (0, 0))],
        out_specs=pl.BlockSpec((ts, MEM_WIDTH), lambda i: (i, 0)),
        out_shape=jax.ShapeDtypeStruct((S, MEM_WIDTH), BF16),
        compiler_params=_cparams(("parallel",)),
    )(proj, mkv)


def mem_bwd(proj, q_col, mkv, dcat, do_col, *, name):
    S = proj.shape[0]
    L = mkv.shape[0]
    ts = _tile(S)
    scale = HEAD_DIM ** -0.5

    def body(q_ref, kv_ref, do_ref, dq_ref, dkv_ref):
        @pl.when(pl.program_id(0) == 0)
        def _():
            dkv_ref[...] = jnp.zeros_like(dkv_ref)

        for h in range(MEM_HEADS):
            c = slice(h * HEAD_DIM, (h + 1) * HEAD_DIM)
            cv = slice(MEM_WIDTH + h * HEAD_DIM, MEM_WIDTH + (h + 1) * HEAD_DIM)
            q, k, v = q_ref[:, c].astype(BF16), kv_ref[:, c].astype(BF16), kv_ref[:, cv].astype(BF16)
            do = do_ref[:, c].astype(BF16)
            p = _mem_probs(q, k)
            dkv_ref[:, cv] += _dot(p.astype(BF16), do, 0, 0)
            dp = _dot(do, v, 1, 1)
            ds = (p * (dp - jnp.sum(dp * p, axis=-1, keepdims=True)) * scale).astype(BF16)
            dq_ref[:, c] = _dot(ds, k)
            dkv_ref[:, c] += _dot(ds, q, 0, 0)

    return pl.pallas_call(
        body, name=name, grid=(S // ts,),
        in_specs=[pl.BlockSpec((ts, MEM_WIDTH), lambda i: (i, q_col // MEM_WIDTH)),
                  pl.BlockSpec((L, 2 * MEM_WIDTH), lambda i: (0, 0)),
                  pl.BlockSpec((ts, MEM_WIDTH), lambda i: (i, do_col // MEM_WIDTH))],
        out_specs=[pl.BlockSpec((ts, MEM_WIDTH), lambda i: (i, 0)),
                   pl.BlockSpec((L, 2 * MEM_WIDTH), lambda i: (0, 0))],
        out_shape=[jax.ShapeDtypeStruct((S, MEM_WIDTH), F32), jax.ShapeDtypeStruct((L, 2 * MEM_WIDTH), F32)],
        compiler_params=_cparams(("arbitrary",)),
    )(proj, mkv, dcat)


def _lower_ones(n, strict=False):
    r = lax.broadcasted_iota(jnp.int32, (n, n), 0)
    c = lax.broadcasted_iota(jnp.int32, (n, n), 1)
    return (r > c if strict else r >= c).astype(F32)


def fox_gate_fwd(proj, f_col, b_f_row, *, name):
    S = proj.shape[0]
    tb = _tile(S, (256, 128))

    def body(f_ref, b_ref, c_ref, carry):
        @pl.when(pl.program_id(0) == 0)
        def _():
            carry[...] = jnp.zeros_like(carry)

        ls = jax.nn.log_sigmoid(f_ref[...] + b_ref[...])
        cum = _dot(_lower_ones(tb), ls, precision=HI) + carry[...]
        c_ref[...] = cum
        carry[...] = cum[tb - 1:tb, :]

    return pl.pallas_call(
        body, name=name, grid=(S // tb,),
        in_specs=[pl.BlockSpec((tb, 128), lambda i: (i, f_col // 128)), pl.BlockSpec((1, 128), lambda i: (0, 0))],
        out_specs=pl.BlockSpec((tb, 128), lambda i: (i, 0)),
        out_shape=jax.ShapeDtypeStruct((S, 128), F32),
        scratch_shapes=[pltpu.VMEM((1, 128), F32)],
        compiler_params=_cparams(("arbitrary",)),
    )(proj, b_f_row)


def fox_gate_bwd(dcf, proj, f_col, b_f_row, *, name):
    S = proj.shape[0]
    tb = _tile(S, (256, 128))
    nb = S // tb

    def body(d_ref, f_ref, b_ref, df_ref, db_ref, carry):
        @pl.when(pl.program_id(0) == 0)
        def _():
            carry[...] = jnp.zeros_like(carry)
            db_ref[...] = jnp.zeros_like(db_ref)

        upper = _lower_ones(tb).T
        rc = _dot(upper, d_ref[...], precision=HI) + carry[...]
        carry[...] = rc[0:1, :]
        df = rc * jax.nn.sigmoid(-(f_ref[...] + b_ref[...]))
        df_ref[...] = df
        db_ref[...] += jnp.sum(df, axis=0, keepdims=True)

    return pl.pallas_call(
        body, name=name, grid=(nb,),
        in_specs=[pl.BlockSpec((tb, 128), lambda i: (nb - 1 - i, 0)),
                  pl.BlockSpec((tb, 128), lambda i: (nb - 1 - i, f_col // 128)),
                  pl.BlockSpec((1, 128), lambda i: (0, 0))],
        out_specs=[pl.BlockSpec((tb, 128), lambda i: (nb - 1 - i, 0)), pl.BlockSpec((1, 128), lambda i: (0, 0))],
        out_shape=[jax.ShapeDtypeStruct((S, 128), F32), jax.ShapeDtypeStruct((1, 128), F32)],
        scratch_shapes=[pltpu.VMEM((1, 128), F32)],
        compiler_params=_cparams(("arbitrary",)),
    )(dcf, proj, b_f_row)


def _fox_block(S):
    return _tile(S, (512, 256, 128)) if S > 512 else S // 2


def _causal(s, qi, kj, bq):
    r = qi * bq + lax.broadcasted_iota(jnp.int32, s.shape, 0)
    c = kj * bq + lax.broadcasted_iota(jnp.int32, s.shape, 1)
    return jnp.where(r >= c, s, -jnp.inf)


def fox_fwd(proj, n_heads, cf_col, cf_row, *, name):
    S = proj.shape[0]
    H = n_heads
    bq = _fox_block(S)
    scale = HEAD_DIM ** -0.5

    def body(q_ref, k_ref, v_ref, cc_ref, cr_ref, o_ref, st_ref):
        qi = pl.program_id(1)
        q = q_ref[...].astype(BF16)
        cq = cc_ref[...]

        def step(j, carry):
            m, l, acc = carry
            rows = pl.ds(pl.multiple_of(j * bq, bq), bq)
            k = k_ref[rows, :].astype(BF16)
            v = v_ref[rows, :].astype(BF16)
            s = _causal(_dot(q, k, 1, 1) * scale + cq - cr_ref[j], qi, j, bq)
            m2 = jnp.maximum(m, jnp.max(s, axis=-1, keepdims=True))
            p = jnp.exp(s - m2)
            a = jnp.exp(m - m2)
            return m2, a * l + jnp.sum(p, axis=-1, keepdims=True), a * acc + _dot(p.astype(BF16), v)

        init = (jnp.full((bq, 1), -jnp.inf, F32), jnp.zeros((bq, 1), F32), jnp.zeros((bq, HEAD_DIM), F32))
        m, l, acc = lax.fori_loop(0, qi + 1, step, init)
        o_ref[...] = (acc / l).astype(o_ref.dtype)
        lane = lax.broadcasted_iota(jnp.int32, (bq, 128), 1)
        st_ref[...] = jnp.where(lane == 0, m + jnp.log(l), jnp.where(lane == 1, cq, 0.0))

    return pl.pallas_call(
        body, name=name, grid=(H, S // bq),
        in_specs=[pl.BlockSpec((bq, HEAD_DIM), lambda h, i: (i, h)),
                  pl.BlockSpec((S, HEAD_DIM), lambda h, i: (0, H + h)),
                  pl.BlockSpec((S, HEAD_DIM), lambda h, i: (0, 2 * H + h)),
                  pl.BlockSpec((None, bq, 1), lambda h, i: (h, i, 0)),
                  pl.BlockSpec((None, S // bq, 1, bq), lambda h, i: (h, 0, 0, 0))],
        out_specs=[pl.BlockSpec((bq, HEAD_DIM), lambda h, i: (i, h)),
                   pl.BlockSpec((None, bq, 128), lambda h, i: (h, i, 0))],
        out_shape=[jax.ShapeDtypeStruct((S, H * HEAD_DIM), BF16), jax.ShapeDtypeStruct((H, S, 128), F32)],
        compiler_params=_cparams(("parallel", "parallel")),
    )(proj, proj, proj, cf_col, cf_row)


def fox_delta(stats, o, dcat, *, name):
    H, S, _ = stats.shape
    ts = _tile(S)

    def body(st_ref, o_ref, do_ref, out_ref):
        d = jnp.sum(o_ref[...].astype(F32) * do_ref[...].astype(F32), axis=-1, keepdims=True)
        lane = lax.broadcasted_iota(jnp.int32, (ts, 128), 1)
        out_ref[...] = jnp.where(lane == 2, d, st_ref[...])

    return pl.pallas_call(
        body, name=name, grid=(H, S // ts),
        in_specs=[pl.BlockSpec((None, ts, 128), lambda h, i: (h, i, 0)),
                  pl.BlockSpec((ts, HEAD_DIM), lambda h, i: (i, h)),
                  pl.BlockSpec((ts, HEAD_DIM), lambda h, i: (i, h))],
        out_specs=pl.BlockSpec((None, ts, 128), lambda h, i: (h, i, 0)),
        out_shape=jax.ShapeDtypeStruct((H, S, 128), F32),
        compiler_params=_cparams(("parallel", "parallel")),
    )(stats, o, dcat)


def fox_bwd(proj, n_heads, stats, cf_row, dcat, *, name):
    S = proj.shape[0]
    H = n_heads
    bq = _fox_block(S)
    nq = S // bq
    scale = HEAD_DIM ** -0.5

    def body(q_ref, k_ref, v_ref, do_ref, st_ref, cr_ref, dq_ref, dk_ref, dv_ref, dst_ref, dcr_ref):
        kj = pl.program_id(1)

        @pl.when(kj == 0)
        def _():
            dq_ref[...] = jnp.zeros_like(dq_ref)
            dst_ref[...] = jnp.zeros_like(dst_ref)

        k = k_ref[...].astype(BF16)
        v = v_ref[...].astype(BF16)
        ck = cr_ref[...]
        lane = lax.broadcasted_iota(jnp.int32, (bq, 128), 1)

        def step(qi, carry):
            dk, dv, dck = carry
            rows = pl.ds(pl.multiple_of(qi * bq, bq), bq)
            q = q_ref[rows, :].astype(BF16)
            do = do_ref[rows, :].astype(BF16)
            st = st_ref[rows, :]
            lse, cq, delta = st[:, 0:1], st[:, 1:2], st[:, 2:3]
            s = _causal(_dot(q, k, 1, 1) * scale + cq - ck, qi, kj, bq)
            p = jnp.exp(s - lse)
            dv = dv + _dot(p.astype(BF16), do, 0, 0)
            ds = p * (_dot(do, v, 1, 1) - delta)
            dsb = (ds * scale).astype(BF16)
            dq_ref[rows, :] += _dot(dsb, k)
            dst_ref[rows, :] += jnp.where(lane == 0, jnp.sum(ds, axis=-1, keepdims=True), 0.0)
            return dk + _dot(dsb, q, 0, 0), dv, dck - jnp.sum(ds, axis=0, keepdims=True)

        init = (jnp.zeros((bq, HEAD_DIM), F32), jnp.zeros((bq, HEAD_DIM), F32), jnp.zeros((1, bq), F32))
        dk, dv, dck = lax.fori_loop(kj, nq, step, init)
        dk_ref[...] = dk
        dv_ref[...] = dv
        dcr_ref[...] = dck

    W = H * HEAD_DIM
    return pl.pallas_call(
        body, name=name, grid=(H, nq),
        in_specs=[pl.BlockSpec((S, HEAD_DIM), lambda h, j: (0, h)),
                  pl.BlockSpec((bq, HEAD_DIM), lambda h, j: (j, H + h)),
                  pl.BlockSpec((bq, HEAD_DIM), lambda h, j: (j, 2 * H + h)),
                  pl.BlockSpec((S, HEAD_DIM), lambda h, j: (0, h)),
                  pl.BlockSpec((None, S, 128), lambda h, j: (h, 0, 0)),
                  pl.BlockSpec((None, None, 1, bq), lambda h, j: (h, j, 0, 0))],
        out_specs=[pl.BlockSpec((S, HEAD_DIM), lambda h, j: (0, h)),
                   pl.BlockSpec((bq, HEAD_DIM), lambda h, j: (j, h)),
                   pl.BlockSpec((bq, HEAD_DIM), lambda h, j: (j, h)),
                   pl.BlockSpec((None, S, 128), lambda h, j: (h, 0, 0)),
                   pl.BlockSpec((None, None, 1, bq), lambda h, j: (h, j, 0, 0))],
        out_shape=[jax.ShapeDtypeStruct((S, W), F32), jax.ShapeDtypeStruct((S, W), F32),
                   jax.ShapeDtypeStruct((S, W), F32), jax.ShapeDtypeStruct((H, S, 128), F32),
                   jax.ShapeDtypeStruct((H, nq, 1, bq), F32)],
        compiler_params=_cparams(("parallel", "arbitrary")),
    )(proj, proj, proj, dcat, stats, cf_row)


def _pad_row(v, n=128):
    return jnp.pad(v.astype(F32), (0, n - v.shape[0])).reshape(1, n)


def fox_mixer_fwd(proj, f_col, b_f, tag):
    S = proj.shape[0]
    H = b_f.shape[0]
    bq = _fox_block(S)
    b_row = _pad_row(b_f)
    cf = fox_gate_fwd(proj, f_col, b_row, name=f"fox_gate_fwd_{tag}")
    cf_t = cf[:, :H].T
    o, stats = fox_fwd(proj, H, cf_t.reshape(H, S, 1), cf_t.reshape(H, S // bq, 1, bq), name=f"fox_fwd_{tag}")
    return o, (o, stats, cf_t, b_row)


def fox_mixer_bwd(proj, f_col, saved, dcat, tag):
    o, stats, cf_t, b_row = saved
    H, S = cf_t.shape
    bq = _fox_block(S)
    stats = fox_delta(stats, o, dcat, name=f"fox_delta_{tag}")
    dq, dk, dv, dst, dcr = fox_bwd(proj, H, stats, cf_t.reshape(H, S // bq, 1, bq), dcat, name=f"fox_bwd_{tag}")
    dcf = dst[:, :, 0] + dcr.reshape(H, S)
    dcf = jnp.pad(dcf.T, ((0, 0), (0, 128 - H)))
    df, db = fox_gate_bwd(dcf, proj, f_col, b_row, name=f"fox_gate_bwd_{tag}")
    return dq, dk, dv, df, db[0, :H]


def ew(fn, tok_ins, row_ins, tok_out_dtypes, n_row_out, *, width, name, tc=None, ts=None):
    S = tok_ins[0][0].shape[0]
    tc = tc or _tile(width, (512, 256, 128))
    ts = ts or _tile(S, (512, 256, 128))
    n_tok, n_row, n_to = len(tok_ins), len(row_ins), len(tok_out_dtypes)
    for _, col in tok_ins:
        assert col % tc == 0

    def body(*refs):
        ins = [r[...] for r in refs[:n_tok + n_row]]
        outs = fn(*ins)
        outs = outs if isinstance(outs, (tuple, list)) else (outs,)
        o_refs = refs[n_tok + n_row:]
        for o, v in zip(o_refs[:n_to], outs[:n_to]):
            o[...] = v.astype(o.dtype)
        if n_row_out:
            @pl.when(pl.program_id(1) == 0)
            def _():
                for o in o_refs[n_to:]:
                    o[...] = jnp.zeros_like(o)

            for o, v in zip(o_refs[n_to:], outs[n_to:]):
                o[...] += jnp.sum(v, axis=0, keepdims=True)

    def tok_spec(col):
        return pl.BlockSpec((ts, tc), lambda j, i: (i, col // tc + j))

    row_spec = pl.BlockSpec((1, tc), lambda j, i: (0, j))
    res = pl.pallas_call(
        body, name=name, grid=(width // tc, S // ts),
        in_specs=[tok_spec(col) for _, col in tok_ins] + [row_spec] * n_row,
        out_specs=[tok_spec(0)] * n_to + [row_spec] * n_row_out,
        out_shape=[jax.ShapeDtypeStruct((S, width), d) for d in tok_out_dtypes]
        + [jax.ShapeDtypeStruct((1, width), F32)] * n_row_out,
        compiler_params=_cparams(("parallel", "arbitrary" if n_row_out else "parallel")),
    )(*[a for a, _ in tok_ins], *row_ins)
    return res[0] if len(res) == 1 else res


S5_SUPER = 128 // S5_GROUP
S5_COLS = S5_SUPER * S5_STATE


def _shift_rows(x, d, up=False):
    T = x.shape[0]
    if d % 8 == 0:
        z = jnp.zeros((d, x.shape[1]), x.dtype)
        return jnp.concatenate([x[d:], z], axis=0) if up else jnp.concatenate([z, x[:T - d]], axis=0)
    row = lax.broadcasted_iota(jnp.int32, x.shape, 0)
    if up:
        return jnp.where(row < T - d, pltpu.roll(x, T - d, axis=0), 0.0)
    return jnp.where(row >= d, pltpu.roll(x, d, axis=0), 0.0)


def _s5_scan(bre, bim, apre, apim, up):
    hre, him = bre, bim
    for k in range(apre.shape[0]):
        are, aim = apre[k:k + 1], apim[k:k + 1]
        sre, sim = _shift_rows(hre, 1 << k, up), _shift_rows(him, 1 << k, up)
        hre, him = hre + are * sre - aim * sim, him + are * sim + aim * sre
    return hre, him


def _s5_states(u, bbre, bbim, apre, apim, start):
    T = u.shape[0]
    bre, bim = _bdot(u, bbre), _bdot(u, bbim)
    are, aim = apre[0:1], apim[0:1]
    sr, si = start[0:1], start[1:2]
    first = lax.broadcasted_iota(jnp.int32, bre.shape, 0) == 0
    bre = bre + jnp.where(first, are * sr - aim * si, 0.0)
    bim = bim + jnp.where(first, are * si + aim * sr, 0.0)
    return _s5_scan(bre, bim, apre, apim, False)


def _s5_block(S):
    return _tile(S, (256, 128))


def s5_scan_fwd(proj, W, bbre, bbim, cre, cim, apre, apim, *, name):
    S = proj.shape[0]
    T = _s5_block(S)
    NJ, nblk, LT = W // 128, S // T, apre.shape[1]

    def body(u_ref, bbre_ref, bbim_ref, cre_ref, cim_ref, apre_ref, apim_ref, y_ref, st_ref, carry):
        @pl.when(pl.program_id(1) == 0)
        def _():
            carry[...] = jnp.zeros_like(carry)

        st_ref[...] = carry[...]
        hre, him = _s5_states(u_ref[...], bbre_ref[...], bbim_ref[...], apre_ref[...], apim_ref[...], carry[...])
        carry[0:1, :] = hre[T - 1:T]
        carry[1:2, :] = him[T - 1:T]
        y_ref[...] = _bdot(hre, cre_ref[...]) - _bdot(him, cim_ref[...])

    mat = lambda r, c: pl.BlockSpec((None, r, c), lambda j, t: (j, 0, 0))
    return pl.pallas_call(
        body, name=name, grid=(NJ, nblk),
        in_specs=[pl.BlockSpec((T, 128), lambda j, t: (t, j)), mat(128, S5_COLS), mat(128, S5_COLS),
                  mat(S5_COLS, 128), mat(S5_COLS, 128), mat(LT, S5_COLS), mat(LT, S5_COLS)],
        out_specs=[pl.BlockSpec((T, 128), lambda j, t: (t, j)),
                   pl.BlockSpec((None, None, 2, S5_COLS), lambda j, t: (j, t, 0, 0))],
        out_shape=[jax.ShapeDtypeStruct((S, W), F32), jax.ShapeDtypeStruct((NJ, nblk, 2, S5_COLS), F32)],
        scratch_shapes=[pltpu.VMEM((2, S5_COLS), F32)],
        compiler_params=_cparams(("parallel", "arbitrary")),
    )(proj, bbre, bbim, cre, cim, apre, apim)


def s5_scan_bwd(proj, W, dy, d_skip_row, bbre, bbim, cre, cim, apre, apim, starts, *, name):
    S = proj.shape[0]
    T = _s5_block(S)
    NJ, nblk, LT = W // 128, S // T, apre.shape[1]

    def body(u_ref, dy_ref, d_ref, bbre_ref, bbim_ref, cre_ref, cim_ref, apre_ref, apim_ref, st_ref,
             du_ref, dbbre_ref, dbbim_ref, dcre_ref, dcim_ref, da_ref, gcarry):
        @pl.when(pl.program_id(1) == 0)
        def _():
            gcarry[...] = jnp.zeros_like(gcarry)
            for r in (dbbre_ref, dbbim_ref, dcre_ref, dcim_ref, da_ref):
                r[...] = jnp.zeros_like(r)

        u, dy, start = u_ref[...], dy_ref[...], st_ref[...]
        apre, apim = apre_ref[...], apim_ref[...]
        bbre, bbim, cre, cim = (r[...].astype(BF16) for r in (bbre_ref, bbim_ref, cre_ref, cim_ref))
        hre, him = _s5_states(u, bbre, bbim, apre, apim, start)
        dyb, ub = dy.astype(BF16), u.astype(BF16)
        dhre, dhim = _dot(dyb, cre, 1, 1), -_dot(dyb, cim, 1, 1)
        are, aim = apre[0:1], apim[0:1]
        gr, gi = gcarry[0:1], gcarry[1:2]
        last = lax.broadcasted_iota(jnp.int32, dhre.shape, 0) == T - 1
        dhre = dhre + jnp.where(last, are * gr + aim * gi, 0.0)
        dhim = dhim + jnp.where(last, are * gi - aim * gr, 0.0)
        gre, gim = _s5_scan(dhre, dhim, apre, -apim, True)
        gcarry[0:1, :] = gre[0:1]
        gcarry[1:2, :] = gim[0:1]
        greb, gimb = gre.astype(BF16), gim.astype(BF16)
        du_ref[...] = _dot(greb, bbre, 1, 1) + _dot(gimb, bbim, 1, 1) + dy * d_ref[...]
        dbbre_ref[...] += _dot(ub, greb, 0, 0)
        dbbim_ref[...] += _dot(ub, gimb, 0, 0)
        dcre_ref[...] += _dot(hre.astype(BF16), dyb, 0, 0)
        dcim_ref[...] -= _dot(him.astype(BF16), dyb, 0, 0)
        first = lax.broadcasted_iota(jnp.int32, hre.shape, 0) == 0
        pre = _shift_rows(hre, 1) + jnp.where(first, start[0:1], 0.0)
        pim = _shift_rows(him, 1) + jnp.where(first, start[1:2], 0.0)
        da_ref[0:1, :] += jnp.sum(gre * pre + gim * pim, axis=0, keepdims=True)
        da_ref[1:2, :] += jnp.sum(gim * pre - gre * pim, axis=0, keepdims=True)

    mat = lambda r, c: pl.BlockSpec((None, r, c), lambda j, t: (j, 0, 0))
    tok = pl.BlockSpec((T, 128), lambda j, t: (nblk - 1 - t, j))
    return pl.pallas_call(
        body, name=name, grid=(NJ, nblk),
        in_specs=[tok, tok, pl.BlockSpec((1, 128), lambda j, t: (0, j)), mat(128, S5_COLS), mat(128, S5_COLS),
                  mat(S5_COLS, 128), mat(S5_COLS, 128), mat(LT, S5_COLS), mat(LT, S5_COLS),
                  pl.BlockSpec((None, None, 2, S5_COLS), lambda j, t: (j, nblk - 1 - t, 0, 0))],
        out_specs=[tok, mat(128, S5_COLS), mat(128, S5_COLS), mat(S5_COLS, 128), mat(S5_COLS, 128), mat(2, S5_COLS)],
        out_shape=[jax.ShapeDtypeStruct((S, W), F32),
                   jax.ShapeDtypeStruct((NJ, 128, S5_COLS), F32), jax.ShapeDtypeStruct((NJ, 128, S5_COLS), F32),
                   jax.ShapeDtypeStruct((NJ, S5_COLS, 128), F32), jax.ShapeDtypeStruct((NJ, S5_COLS, 128), F32),
                   jax.ShapeDtypeStruct((NJ, 2, S5_COLS), F32)],
        scratch_shapes=[pltpu.VMEM((2, S5_COLS), F32)],
        compiler_params=_cparams(("parallel", "arbitrary")),
    )(proj, dy, d_skip_row, bbre, bbim, cre, cim, apre, apim, starts)


def _s5_discretize(lam_re, lam_im, log_dt, b_re, b_im):
    dt = jnp.exp(log_dt)[:, None]
    mag = jnp.exp(lam_re * dt)
    a_re, a_im = mag * jnp.cos(lam_im * dt), mag * jnp.sin(lam_im * dt)
    den = lam_re * lam_re + lam_im * lam_im
    z_re = ((a_re - 1.0) * lam_re + a_im * lam_im) / den
    z_im = (a_im * lam_re - (a_re - 1.0) * lam_im) / den
    bb_re = z_re[..., None] * b_re - z_im[..., None] * b_im
    bb_im = z_re[..., None] * b_im + z_im[..., None] * b_re
    return a_re, a_im, bb_re, bb_im


def _blockdiag(x):
    G, r, c = x.shape
    x = x.reshape(G // S5_SUPER, S5_SUPER, r, c)
    eye = jnp.eye(S5_SUPER, dtype=x.dtype)
    return (x[:, :, :, None, :] * eye[None, :, None, :, None]).reshape(G // S5_SUPER, S5_SUPER * r, S5_SUPER * c)


def _blockdiag_t(x, r, c):
    NJ = x.shape[0]
    x = x.reshape(NJ, S5_SUPER, r, S5_SUPER, c)
    return jnp.stack([x[:, i, :, i, :] for i in range(S5_SUPER)], axis=1).reshape(NJ * S5_SUPER, r, c)


def _gelu(x):
    c = math.sqrt(2.0 / math.pi)
    return 0.5 * x * (1.0 + jnp.tanh(c * (x + 0.044715 * x * x * x)))


def _gelu_grad(x):
    c = math.sqrt(2.0 / math.pi)
    t = jnp.tanh(c * (x + 0.044715 * x * x * x))
    return 0.5 * (1.0 + t) + 0.5 * x * (1.0 - t * t) * c * (1.0 + 3 * 0.044715 * x * x)


def s5_mixer_fwd(proj, W, p, w_glu_bf, tag):
    S = proj.shape[0]
    T = _s5_block(S)
    a_re, a_im, bb_re, bb_im = _s5_discretize(p["lam_re"], p["lam_im"], p["log_dt"], p["b_re"], p["b_im"])
    NJ = W // 128
    pows_re, pows_im = [a_re], [a_im]
    for _ in range(int(math.log2(T)) - 1):
        r, i = pows_re[-1], pows_im[-1]
        pows_re.append(r * r - i * i)
        pows_im.append(2.0 * r * i)
    apre = jnp.stack([x.reshape(NJ, S5_COLS) for x in pows_re], axis=1)
    apim = jnp.stack([x.reshape(NJ, S5_COLS) for x in pows_im], axis=1)
    bbre = _blockdiag(jnp.swapaxes(bb_re, 1, 2))
    bbim = _blockdiag(jnp.swapaxes(bb_im, 1, 2))
    cre = _blockdiag(jnp.swapaxes(p["c_re"], 1, 2))
    cim = _blockdiag(jnp.swapaxes(p["c_im"], 1, 2))
    mats = (bbre, bbim, cre, cim, apre, apim)
    y, starts = s5_scan_fwd(proj, W, *mats, name=f"s5_scan_fwd_{tag}")
    d_row = p["d_skip"].reshape(1, W)
    yy = ew(lambda y, u, d: _gelu(y + d * u), [(y, 0), (proj, 0)], [d_row], [F32], 0, width=W,
            name=f"s5_act_fwd_{tag}")
    mix, z = mm(yy, w_glu_bf, extras=(yy, p["b_glu"].reshape(1, W)), out_dtypes=(BF16, F32),
                epilogue=lambda acc, yy, b: (yy * jax.nn.sigmoid(acc + b), acc + b), name=f"s5_glu_{tag}")
    return mix, (mats, starts, y, yy, z, d_row)


def s5_mixer_bwd(proj, W, p, w_glu_bf, saved, dcat, tag):
    mats, starts, y, yy, z, d_row = saved

    def glu_bwd(dm, yy, z):
        sg = jax.nn.sigmoid(z)
        dz = dm.astype(F32) * yy * sg * (1.0 - sg)
        return dm.astype(F32) * sg, dz, dz

    dyy1, dz, db_glu = ew(glu_bwd, [(dcat, 0), (yy, 0), (z, 0)], [], [F32, F32], 1, width=W, name=f"s5_glu_bwd_{tag}")
    dw_glu = mm(yy, dz, ta=True, name=f"s5_dwglu_{tag}")
    dyy = mm(dz, w_glu_bf, tb=True, extras=(dyy1,), epilogue=lambda acc, e: (acc + e,), name=f"s5_dyy_{tag}")

    def act_bwd(dyy, y, u, d):
        dpre = dyy * _gelu_grad(y + d * u)
        return dpre, dpre * u

    dy, dd = ew(act_bwd, [(dyy, 0), (y, 0), (proj, 0)], [d_row], [F32], 1, width=W, name=f"s5_act_bwd_{tag}")
    du, dbbre, dbbim, dcre, dcim, da = s5_scan_bwd(proj, W, dy, d_row, *mats, starts, name=f"s5_scan_bwd_{tag}")
    G = W // S5_GROUP
    dbb_re = jnp.swapaxes(_blockdiag_t(dbbre, S5_GROUP, S5_STATE), 1, 2)
    dbb_im = jnp.swapaxes(_blockdiag_t(dbbim, S5_GROUP, S5_STATE), 1, 2)
    dc_re = jnp.swapaxes(_blockdiag_t(dcre, S5_STATE, S5_GROUP), 1, 2)
    dc_im = jnp.swapaxes(_blockdiag_t(dcim, S5_STATE, S5_GROUP), 1, 2)
    da_re, da_im = da[:, 0, :].reshape(G, S5_STATE), da[:, 1, :].reshape(G, S5_STATE)
    _, vjp = jax.vjp(_s5_discretize, p["lam_re"], p["lam_im"], p["log_dt"], p["b_re"], p["b_im"])
    dlam_re, dlam_im, dlog_dt, db_re, db_im = vjp((da_re, da_im, dbb_re, dbb_im))
    grads = dict(lam_re=dlam_re, lam_im=dlam_im, log_dt=dlog_dt, b_re=db_re, b_im=db_im, c_re=dc_re, c_im=dc_im,
                 d_skip=dd.reshape(W), w_glu=dw_glu, b_glu=db_glu.reshape(W))
    return du, grads


CONV_ROWS = 256


def _conv_taps(ext, w):
    acc = ext * w[3:4]
    for j in range(1, 4):
        acc = acc + pltpu.roll(ext, j, axis=0) * w[3 - j:4 - j]
    return acc


def gdn_conv_fwd(proj, width, conv_w, *, name):
    S = proj.shape[0]
    tc = _tile(width, (256, 128))
    T = min(CONV_ROWS, S)
    n = S // T

    def body(x_ref, w_ref, y_ref):
        w = w_ref[...]

        def chunk(c, carry):
            base = pl.multiple_of(c * T, T)
            prev = x_ref[pl.ds(pl.multiple_of(jnp.maximum(base - 8, 0), 8), 8), :]
            ext = jnp.concatenate([jnp.where(c > 0, prev, 0.0), x_ref[pl.ds(base, T), :]], axis=0)
            pre = _conv_taps(ext, w)[8:]
            y_ref[pl.ds(base, T), :] = pre * jax.nn.sigmoid(pre)
            return carry

        lax.fori_loop(0, n, chunk, 0)

    return pl.pallas_call(
        body, name=name, grid=(width // tc,),
        in_specs=[pl.BlockSpec((S, tc), lambda j: (0, j)), pl.BlockSpec((4, tc), lambda j: (0, j))],
        out_specs=pl.BlockSpec((S, tc), lambda j: (0, j)),
        out_shape=jax.ShapeDtypeStruct((S, width), F32),
        compiler_params=_cparams(("parallel",)),
    )(proj, conv_w)


def gdn_conv_bwd(proj, width, conv_w, dy, *, name):
    S = proj.shape[0]
    tc = _tile(width, (256, 128))
    T = min(CONV_ROWS, S)
    n = S // T
    E = T + 16

    def body(x_ref, w_ref, dy_ref, dx_ref, dw_ref):
        w = w_ref[...]

        def halo(ref, start, keep):
            start = pl.multiple_of(jnp.clip(start, 0, S - 8), 8)
            return jnp.where(keep, ref[pl.ds(start, 8), :], 0.0)

        def chunk(c, dw):
            base = pl.multiple_of(c * T, T)
            rows = pl.ds(base, T)
            ext = jnp.concatenate([halo(x_ref, base - 8, c > 0), x_ref[rows, :], halo(x_ref, base + T, c < n - 1)], axis=0)
            dye = jnp.concatenate([jnp.zeros((8, tc), F32), dy_ref[rows, :], halo(dy_ref, base + T, c < n - 1)], axis=0)
            pre = _conv_taps(ext, w)
            sg = jax.nn.sigmoid(pre)
            dpre = dye * (sg * (1.0 + pre * (1.0 - sg)))
            dx = dpre * w[3:4]
            for j in range(1, 4):
                dx = dx + pltpu.roll(dpre, E - j, axis=0) * w[3 - j:4 - j]
            dx_ref[rows, :] = dx[8:8 + T]
            own = dpre[8:8 + T]
            parts = [jnp.sum(own * pltpu.roll(ext, 3 - i, axis=0)[8:8 + T], axis=0, keepdims=True) if i < 3
                     else jnp.sum(own * ext[8:8 + T], axis=0, keepdims=True) for i in range(4)]
            return dw + jnp.concatenate(parts, axis=0)

        dw_ref[...] = lax.fori_loop(0, n, chunk, jnp.zeros((4, tc), F32))

    return pl.pallas_call(
        body, name=name, grid=(width // tc,),
        in_specs=[pl.BlockSpec((S, tc), lambda j: (0, j)), pl.BlockSpec((4, tc), lambda j: (0, j)),
                  pl.BlockSpec((S, tc), lambda j: (0, j))],
        out_specs=[pl.BlockSpec((S, tc), lambda j: (0, j)), pl.BlockSpec((4, tc), lambda j: (0, j))],
        out_shape=[jax.ShapeDtypeStruct((S, width), F32), jax.ShapeDtypeStruct((4, width), F32)],
        compiler_params=_cparams(("parallel",)),
    )(proj, conv_w, dy)


@functools.partial(jax.custom_vjp, nondiff_argnums=(0,))
def _bein(spec, a, b):
    return jnp.einsum(spec, a.astype(BF16), b.astype(BF16), preferred_element_type=F32)


def _bein_fwd(spec, a, b):
    return _bein(spec, a, b), (a, b)


def _bein_bwd(spec, res, g):
    a, b = res
    ins, out = spec.split("->")
    sa, sb = ins.split(",")
    return _bein(f"{out},{sb}->{sa}", g, b), _bein(f"{sa},{out}->{sb}", a, g)


_bein.defvjp(_bein_fwd, _bein_bwd)


def _hmm(a, b):
    return jnp.einsum("ncs,nsd->ncd", a, b, precision=HI, preferred_element_type=F32)


def _inv_unit_lower(L):
    C = L.shape[-1]
    r = lax.broadcasted_iota(jnp.int32, L.shape, 1)
    c = lax.broadcasted_iota(jnp.int32, L.shape, 2)
    eye = (r == c).astype(F32)
    D = jnp.where(jnp.right_shift(r, 4) == jnp.right_shift(c, 4), L, 0.0)
    D2 = _hmm(D, D)
    D4 = _hmm(D2, D2)
    D8 = _hmm(D4, D4)
    dinv = _hmm(_hmm(_hmm(eye - D, eye + D2), eye + D4), eye + D8)
    N = _hmm(dinv, L - D)
    return _hmm(_hmm(eye - N, eye + _hmm(N, N)), dinv)


def _softplus(x):
    return jnp.maximum(x, 0.0) + jnp.log(1.0 + jnp.exp(-jnp.abs(x)))


def _gdn_prep(qc, kc, vc, ab, a_log_row, dt_row, h, n_heads):
    R = qc.shape[0]
    C = GDN_CHUNK
    n = R // C
    lane = lax.broadcasted_iota(jnp.int32, (1, 128), 1)
    pick = lambda x, i: jnp.sum(jnp.where(lane == i, x, 0.0), axis=-1, keepdims=True)
    a_in, b_in = pick(ab, h).reshape(n, C, 1), pick(ab, n_heads + h).reshape(n, C, 1)
    a_log, dt_bias = pick(a_log_row, h), pick(dt_row, h)
    q3, k3, v = qc.reshape(n, C, 128), kc.reshape(n, C, 128), vc.reshape(n, C, 128)
    q = q3 * lax.rsqrt(jnp.sum(q3 * q3, axis=-1, keepdims=True) + EPS) * HEAD_DIM ** -0.5
    k = k3 * lax.rsqrt(jnp.sum(k3 * k3, axis=-1, keepdims=True) + EPS)
    beta = jax.nn.sigmoid(b_in)
    g = -jnp.exp(a_log) * _softplus(a_in + dt_bias)
    r = lax.broadcasted_iota(jnp.int32, (n, C, C), 1)
    c = lax.broadcasted_iota(jnp.int32, (n, C, C), 2)
    gc = _hmm((r >= c).astype(F32), jnp.broadcast_to(g, (n, C, C)))
    gcol = gc[:, :, 0:1]
    grow = jnp.sum(jnp.where(r == c, gc, 0.0), axis=1, keepdims=True)
    decay = jnp.exp(jnp.where(r >= c, gc - grow, -jnp.inf))
    kb, vb = k * beta, v * beta
    lmat = jnp.where(r > c, _bein("ncd,nsd->ncs", kb, k) * decay, 0.0)
    eg = jnp.exp(gcol)
    rhs = jnp.concatenate([vb, kb * eg], axis=-1)
    sol = _hmm(_inv_unit_lower(lmat), rhs)
    attn = jnp.where(r >= c, _bein("ncd,nsd->ncs", q, k) * decay, 0.0)
    glast = gcol[:, C - 1:C, :]
    k_dec = k * jnp.exp(glast - gcol)
    g_last = jnp.broadcast_to(jnp.exp(glast), (n, 1, 128))
    return (sol[..., :128].reshape(R, 128), sol[..., 128:].reshape(R, 128), attn.reshape(R, C),
            (q * eg).reshape(R, 128), k_dec.reshape(R, 128), g_last)


def _gdn_rows(S):
    return _tile(S, (256, 128, 64))


def gdn_prep_fwd(qkv, proj, ab_col, a_log_row, dt_row, H, *, name):
    S = qkv.shape[0]
    R = _gdn_rows(S)
    n, nc = R // GDN_CHUNK, S // GDN_CHUNK

    def body(q_ref, k_ref, v_ref, ab_ref, al_ref, dt_ref, u_ref, w_ref, at_ref, qd_ref, kd_ref, gl_ref):
        outs = _gdn_prep(q_ref[...], k_ref[...], v_ref[...], ab_ref[...], al_ref[...], dt_ref[...],
                         pl.program_id(1), H)
        for r, v in zip((u_ref, w_ref, at_ref, qd_ref, kd_ref, gl_ref), outs):
            r[...] = v

    head = lambda off: pl.BlockSpec((R, 128), lambda i, h: (i, off + h))
    row = pl.BlockSpec((1, 128), lambda i, h: (0, 0))
    big = jax.ShapeDtypeStruct((S, H * 128), F32)
    return pl.pallas_call(
        body, name=name, grid=(S // R, H),
        in_specs=[head(0), head(H), head(2 * H), pl.BlockSpec((R, 128), lambda i, h: (i, ab_col // 128)), row, row],
        out_specs=[head(0), head(0), pl.BlockSpec((None, R, GDN_CHUNK), lambda i, h: (h, i, 0)), head(0), head(0),
                   pl.BlockSpec((None, n, 1, 128), lambda i, h: (h, i, 0, 0))],
        out_shape=[big, big, jax.ShapeDtypeStruct((H, S, GDN_CHUNK), F32), big, big,
                   jax.ShapeDtypeStruct((H, nc, 1, 128), F32)],
        compiler_params=_cparams(("parallel", "parallel")),
    )(qkv, qkv, qkv, proj, a_log_row, dt_row)


def gdn_prep_bwd(qkv, proj, ab_col, a_log_row, dt_row, H, cts, *, name):
    S = qkv.shape[0]
    R = _gdn_rows(S)
    n, nc = R // GDN_CHUNK, S // GDN_CHUNK

    def body(q_ref, k_ref, v_ref, ab_ref, al_ref, dt_ref, du_ref, dw_ref, dat_ref, dqd_ref, dkd_ref, dgl_ref,
             dq_ref, dk_ref, dv_ref, dab_ref, dal_ref, ddt_ref):
        i, h = pl.program_id(0), pl.program_id(1)

        @pl.when(h == 0)
        def _():
            dab_ref[...] = jnp.zeros_like(dab_ref)

        @pl.when((h == 0) & (i == 0))
        def _():
            dal_ref[...] = jnp.zeros_like(dal_ref)
            ddt_ref[...] = jnp.zeros_like(ddt_ref)

        f = lambda q, k, v, ab, al, dt: _gdn_prep(q, k, v, ab, al, dt, h, H)
        _, vjp = jax.vjp(f, q_ref[...], k_ref[...], v_ref[...], ab_ref[...], al_ref[...], dt_ref[...])
        dq, dk, dv, dab, dal, ddt = vjp(tuple(r[...] for r in (du_ref, dw_ref, dat_ref, dqd_ref, dkd_ref, dgl_ref)))
        dq_ref[...] = dq
        dk_ref[...] = dk
        dv_ref[...] = dv
        dab_ref[...] += dab
        dal_ref[...] += dal
        ddt_ref[...] += ddt

    head = lambda off: pl.BlockSpec((R, 128), lambda i, h: (i, off + h))
    row = pl.BlockSpec((1, 128), lambda i, h: (0, 0))
    at = pl.BlockSpec((None, R, GDN_CHUNK), lambda i, h: (h, i, 0))
    gl = pl.BlockSpec((None, n, 1, 128), lambda i, h: (h, i, 0, 0))
    W = H * 128
    dq, dk, dv, dab, dal, ddt = pl.pallas_call(
        body, name=name, grid=(S // R, H),
        in_specs=[head(0), head(H), head(2 * H), pl.BlockSpec((R, 128), lambda i, h: (i, ab_col // 128)), row, row,
                  head(0), head(0), at, head(0), head(0), gl],
        out_specs=[head(0), head(0), head(0), pl.BlockSpec((R, 128), lambda i, h: (i, 0)), row, row],
        out_shape=[jax.ShapeDtypeStruct((S, W), F32)] * 3 + [jax.ShapeDtypeStruct((S, 128), F32)]
        + [jax.ShapeDtypeStruct((1, 128), F32)] * 2,
        compiler_params=_cparams(("arbitrary", "arbitrary")),
    )(qkv, qkv, qkv, proj, a_log_row, dt_row, *cts)
    return dq, dk, dv, dab, dal, ddt


def gdn_scan_fwd(u, w, attn, qd, kd, gl, *, name):
    S = u.shape[0]
    H = attn.shape[0]
    C = GDN_CHUNK
    R = _gdn_rows(S)
    n, nc = R // C, S // C

    def body(u_ref, w_ref, at_ref, qd_ref, kd_ref, gl_ref, o_ref, st_ref, state):
        @pl.when(pl.program_id(1) == 0)
        def _():
            state[...] = jnp.zeros_like(state)

        for c in range(n):
            rows = slice(c * C, (c + 1) * C)
            s = state[...]
            st_ref[c] = s
            sb = s.astype(BF16)
            v_new = u_ref[rows, :] - _dot(w_ref[rows, :].astype(BF16), sb)
            vb = v_new.astype(BF16)
            o_ref[rows, :] = _dot(qd_ref[rows, :].astype(BF16), sb) + _dot(at_ref[rows, :].astype(BF16), vb)
            state[...] = s * gl_ref[c] + _dot(kd_ref[rows, :].astype(BF16), vb, 0, 0)

    head = pl.BlockSpec((R, 128), lambda h, i: (i, h))
    return pl.pallas_call(
        body, name=name, grid=(H, S // R),
        in_specs=[head, head, pl.BlockSpec((None, R, C), lambda h, i: (h, i, 0)), head, head,
                  pl.BlockSpec((None, n, 1, 128), lambda h, i: (h, i, 0, 0))],
        out_specs=[head, pl.BlockSpec((None, n, 128, 128), lambda h, i: (h, i, 0, 0))],
        out_shape=[jax.ShapeDtypeStruct((S, H * 128), F32), jax.ShapeDtypeStruct((H, nc, 128, 128), F32)],
        scratch_shapes=[pltpu.VMEM((128, 128), F32)],
        compiler_params=_cparams(("parallel", "arbitrary")),
    )(u, w, attn, qd, kd, gl)


def gdn_scan_bwd(u, w, attn, qd, kd, gl, states, do, *, name):
    S = u.shape[0]
    H = attn.shape[0]
    C = GDN_CHUNK
    R = _gdn_rows(S)
    n, nc, nb = R // C, S // C, S // R

    def body(u_ref, w_ref, at_ref, qd_ref, kd_ref, gl_ref, st_ref, do_ref,
             du_ref, dw_ref, dat_ref, dqd_ref, dkd_ref, dgl_ref, dstate):
        @pl.when(pl.program_id(1) == 0)
        def _():
            dstate[...] = jnp.zeros_like(dstate)

        lane = lax.broadcasted_iota(jnp.int32, (1, 128), 1)
        for c in reversed(range(n)):
            rows = slice(c * C, (c + 1) * C)
            s = st_ref[c]
            sb = s.astype(BF16)
            ds2 = dstate[...]
            ds2b = ds2.astype(BF16)
            wb, qdb, kdb, atb = (r[rows, :].astype(BF16) for r in (w_ref, qd_ref, kd_ref, at_ref))
            dob = do_ref[rows, :].astype(BF16)
            v_new = u_ref[rows, :] - _dot(wb, sb)
            vb = v_new.astype(BF16)
            dv = _dot(atb, dob, 0, 0) + _dot(kdb, ds2b)
            dvb = dv.astype(BF16)
            du_ref[rows, :] = dv
            dw_ref[rows, :] = -_dot(dvb, sb, 1, 1)
            dat_ref[rows, :] = _dot(dob, vb, 1, 1)
            dqd_ref[rows, :] = _dot(dob, sb, 1, 1)
            dkd_ref[rows, :] = _dot(vb, ds2b, 1, 1)
            dgl = jnp.sum(jnp.sum(ds2 * s, axis=1, keepdims=True), axis=0, keepdims=True)
            dgl_ref[c] = jnp.where(lane == 0, dgl, 0.0)
            dstate[...] = ds2 * gl_ref[c] + _dot(qdb, dob, 0, 0) - _dot(wb, dvb, 0, 0)

    head = pl.BlockSpec((R, 128), lambda h, i: (nb - 1 - i, h))
    at = pl.BlockSpec((None, R, C), lambda h, i: (h, nb - 1 - i, 0))
    glb = pl.BlockSpec((None, n, 1, 128), lambda h, i: (h, nb - 1 - i, 0, 0))
    big = jax.ShapeDtypeStruct((S, H * 128), F32)
    return pl.pallas_call(
        body, name=name, grid=(H, nb),
        in_specs=[head, head, at, head, head, glb,
                  pl.BlockSpec((None, n, 128, 128), lambda h, i: (h, nb - 1 - i, 0, 0)), head],
        out_specs=[head, head, at, head, head, glb],
        out_shape=[big, big, jax.ShapeDtypeStruct((H, S, C), F32), big, big,
                   jax.ShapeDtypeStruct((H, nc, 1, 128), F32)],
        scratch_shapes=[pltpu.VMEM((128, 128), F32)],
        compiler_params=_cparams(("parallel", "arbitrary")),
    )(u, w, attn, qd, kd, gl, states, do)


def _head_rms(o):
    return lax.rsqrt(jnp.mean(o * o, axis=-1, keepdims=True) + EPS)


def gdn_mixer_fwd(proj, W, p, tag):
    H = W // HEAD_DIM
    ab_col = 4 * W + MEM_WIDTH
    qkv = gdn_conv_fwd(proj, 3 * W, p["conv_w"], name=f"gdn_conv_fwd_{tag}")
    al_row, dt_row = _pad_row(p["a_log"]), _pad_row(p["dt_bias"])
    pre = gdn_prep_fwd(qkv, proj, ab_col, al_row, dt_row, H, name=f"gdn_prep_fwd_{tag}")
    o, states = gdn_scan_fwd(*pre, name=f"gdn_scan_fwd_{tag}")
    gn_row = jnp.tile(p["o_norm"].reshape(1, HEAD_DIM), (1, H))

    def gate_fwd(o, gate, gn):
        return o * _head_rms(o) * gn * (gate * jax.nn.sigmoid(gate))

    mix = ew(gate_fwd, [(o, 0), (proj, 3 * W)], [gn_row], [BF16], 0, width=W, tc=HEAD_DIM, name=f"gdn_gate_fwd_{tag}")
    return mix, (qkv, pre, states, o, gn_row, al_row, dt_row)


def gdn_mixer_bwd(proj, W, p, saved, dcat, tag):
    qkv, pre, states, o, gn_row, al_row, dt_row = saved
    H = W // HEAD_DIM
    ab_col = 4 * W + MEM_WIDTH

    def gate_bwd(dm, o, gate, gn):
        dm = dm.astype(F32)
        r = _head_rms(o)
        xh = o * r
        sg = jax.nn.sigmoid(gate)
        dy = dm * gate * sg
        dgate = dm * xh * gn * (sg * (1.0 + gate * (1.0 - sg)))
        dxh = dy * gn
        do = r * (dxh - xh * jnp.mean(dxh * xh, axis=-1, keepdims=True))
        return do, dgate, dy * xh

    do, dgate, dgn = ew(gate_bwd, [(dcat, 0), (o, 0), (proj, 3 * W)], [gn_row], [F32, F32], 1, width=W, tc=HEAD_DIM,
                        name=f"gdn_gate_bwd_{tag}")
    cts = gdn_scan_bwd(*pre, states, do, name=f"gdn_scan_bwd_{tag}")
    dq, dk, dv, dab, dal, ddt = gdn_prep_bwd(qkv, proj, ab_col, al_row, dt_row, H, cts, name=f"gdn_prep_bwd_{tag}")
    dqkv = jnp.concatenate([dq, dk, dv], axis=1)
    dx, dconv = gdn_conv_bwd(proj, 3 * W, p["conv_w"], dqkv, name=f"gdn_conv_bwd_{tag}")
    grads = dict(conv_w=dconv, a_log=dal[0, :H], dt_bias=ddt[0, :H], o_norm=dgn.reshape(H, HEAD_DIM).sum(axis=0))
    return dx, dgate, dab, grads


def exchange(p, *, same_block, name):
    shape = p.shape if same_block else p.shape[1:]

    def body(p_ref, out_ref, send_sems, recv_sems, local_sem):
        pos = (lax.axis_index("x"), lax.axis_index("y"), lax.axis_index("c"))
        me = 4 * pos[0] + 2 * pos[1] + pos[2]
        block = (lambda j: p_ref) if same_block else (lambda j: p_ref.at[j])
        mine = pltpu.make_async_copy(block(me), out_ref.at[me], local_sem)
        mine.start()
        copies = []
        for m in range(1, N_DEV):
            peer = tuple(1 - v if (m >> (2 - b)) & 1 else v for b, v in enumerate(pos))
            cp = pltpu.make_async_remote_copy(
                src_ref=block(4 * peer[0] + 2 * peer[1] + peer[2]), dst_ref=out_ref.at[me],
                send_sem=send_sems.at[m - 1], recv_sem=recv_sems.at[m - 1], device_id=peer, device_id_type=MESH)
            cp.start()
            copies.append(cp)
        for cp in copies:
            cp.wait()
        mine.wait()

    return pl.pallas_call(
        body, name=name,
        in_specs=[pl.BlockSpec(memory_space=pl.ANY)], out_specs=pl.BlockSpec(memory_space=pl.ANY),
        out_shape=jax.ShapeDtypeStruct((N_DEV,) + tuple(shape), p.dtype),
        scratch_shapes=[pltpu.SemaphoreType.DMA((N_DEV - 1,)), pltpu.SemaphoreType.DMA((N_DEV - 1,)),
                        pltpu.SemaphoreType.DMA],
    )(p)


def adamw(parts, w, m, v, *, name):
    P, R, C = parts.shape
    tr = PACK_ROWS if R % PACK_ROWS == 0 else R
    c1, c2 = 1.0 - ADAM_B1 ** ADAM_STEP, 1.0 - ADAM_B2 ** ADAM_STEP

    def body(p_ref, w_ref, m_ref, v_ref, g_ref, d_ref, nm_ref, nv_ref):
        g = p_ref[0].astype(F32)
        for s in range(1, P):
            g = g + p_ref[s].astype(F32)
        m = ADAM_B1 * m_ref[...] + (1.0 - ADAM_B1) * g
        v = ADAM_B2 * v_ref[...] + (1.0 - ADAM_B2) * (g * g)
        g_ref[...] = g
        nm_ref[...] = m
        nv_ref[...] = v
        d_ref[...] = -ADAM_LR * ((m / c1) / (jnp.sqrt(v / c2) + ADAM_EPS) + ADAM_WD * w_ref[...])

    blk = pl.BlockSpec((tr, C), lambda i: (i, 0))
    return pl.pallas_call(
        body, name=name, grid=(R // tr,),
        in_specs=[pl.BlockSpec((P, tr, C), lambda i: (0, i, 0)), blk, blk, blk],
        out_specs=[blk] * 4, out_shape=[jax.ShapeDtypeStruct((R, C), F32)] * 4,
        compiler_params=_cparams(("parallel",)),
    )(parts, w, m, v)


def sum_parts(parts, *, name):
    P, R, C = parts.shape
    tr = PACK_ROWS if R % PACK_ROWS == 0 else R

    def body(p_ref, o_ref):
        g = p_ref[0].astype(F32)
        for s in range(1, P):
            g = g + p_ref[s].astype(F32)
        o_ref[...] = g

    return pl.pallas_call(
        body, name=name, grid=(R // tr,),
        in_specs=[pl.BlockSpec((P, tr, C), lambda i: (0, i, 0))], out_specs=pl.BlockSpec((tr, C), lambda i: (i, 0)),
        out_shape=jax.ShapeDtypeStruct((R, C), F32), compiler_params=_cparams(("parallel",)),
    )(parts)


PACK_COLS = 1024
PACK_ROWS = 256


def _pack(arrays, cols, lead=()):
    n_lead = len(lead)
    flat = [a.reshape(lead + (-1,)) for a in arrays]
    total = sum(f.shape[-1] for f in flat)
    rows = -(-total // cols)
    mult = PACK_ROWS if rows > PACK_ROWS else 8
    rows = -(-rows // mult) * mult
    pad = rows * cols - total
    if pad:
        flat.append(jnp.zeros(lead + (pad,), flat[0].dtype))
    return jnp.concatenate(flat, axis=n_lead).reshape(lead + (rows, cols))


def _unpack(buf, shapes, lead=()):
    flat = buf.reshape(lead + (-1,))
    out, off = [], 0
    for s in shapes:
        n = math.prod(s)
        out.append(lax.slice_in_dim(flat, off, off + n, axis=len(lead)).reshape(lead + tuple(s)))
        off += n
    return out


def _to_shards(full, axis):
    s = full.shape
    return jnp.moveaxis(full.reshape(s[:axis] + (N_DEV, s[axis] // N_DEV) + s[axis + 1:]), axis, 0)


def _from_shards(g, axis):
    m = jnp.moveaxis(g, 0, axis)
    s = m.shape
    return m.reshape(s[:axis] + (s[axis] * s[axis + 1],) + s[axis + 2:])


BIG = (("w_mem_kv", 0), ("w_out", 1), ("w_up", 2), ("w_down", 1), ("s5_w_in", 1), ("s5_w_glu", 1),
       ("gdn_w_in", 2), ("fox_w_in", 1))
SMALL_SHARDED = (("s5_d_skip", 1), ("s5_b_glu", 1), ("gdn_conv_w", 2))
REPLICATED = ("mem_norm", "norm1", "norm2", "norm_f", "s5_lam_re", "s5_lam_im", "s5_log_dt", "s5_b_re", "s5_b_im",
              "s5_c_re", "s5_c_im", "gdn_a_log", "gdn_dt_bias", "gdn_o_norm", "fox_b_f")
WEIGHTS = ("mem_norm", "w_mem_kv", "norm1", "w_out", "norm2", "w_up", "w_down", "norm_f", "s5_w_in", "s5_lam_re",
           "s5_lam_im", "s5_log_dt", "s5_b_re", "s5_b_im", "s5_c_re", "s5_c_im", "s5_d_skip", "s5_w_glu", "s5_b_glu",
           "gdn_w_in", "gdn_conv_w", "gdn_a_log", "gdn_dt_bias", "gdn_o_norm", "fox_w_in", "fox_b_f")


def _relu2(acc):
    r = jnp.maximum(acc, 0.0)
    return acc, r * r


def _relu2_grad(acc, u):
    return (acc * 2.0 * jnp.maximum(u, 0.0),)


def _add(acc, e):
    return (acc + e,)


def _permute_in(w, kind, W):
    if kind == 0:
        return w
    n_main = (4 if kind == 1 else 3) * W
    n_small = w.shape[1] - n_main - MEM_WIDTH
    small = jnp.pad(w[:, n_main:n_main + n_small], ((0, 0), (0, MEM_WIDTH - n_small)))
    return jnp.concatenate([w[:, :n_main], w[:, n_main + n_small:], small], axis=1)


def _unpermute_in(dw, kind, W, n_small):
    if kind == 0:
        return dw
    n_main = (4 if kind == 1 else 3) * W
    return jnp.concatenate([dw[:, :n_main], dw[:, n_main + MEM_WIDTH:n_main + MEM_WIDTH + n_small],
                            dw[:, n_main:n_main + MEM_WIDTH]], axis=1)


def kernel(x, mem, mem_norm, w_mem_kv, norm1, w_out, norm2, w_up, w_down, norm_f, s5_w_in, s5_lam_re, s5_lam_im, s5_log_dt, s5_b_re, s5_b_im, s5_c_re, s5_c_im, s5_d_skip, s5_w_glu, s5_b_glu, gdn_w_in, gdn_conv_w, gdn_a_log, gdn_dt_bias, gdn_o_norm, fox_w_in, fox_b_f, loss_target, m_mem_norm, m_w_mem_kv, m_norm1, m_w_out, m_norm2, m_w_up, m_w_down, m_norm_f, m_s5_w_in, m_s5_lam_re, m_s5_lam_im, m_s5_log_dt, m_s5_b_re, m_s5_b_im, m_s5_c_re, m_s5_c_im, m_s5_d_skip, m_s5_w_glu, m_s5_b_glu, m_gdn_w_in, m_gdn_conv_w, m_gdn_a_log, m_gdn_dt_bias, m_gdn_o_norm, m_fox_w_in, m_fox_b_f, v_mem_norm, v_w_mem_kv, v_norm1, v_w_out, v_norm2, v_w_up, v_w_down, v_norm_f, v_s5_w_in, v_s5_lam_re, v_s5_lam_im, v_s5_log_dt, v_s5_b_re, v_s5_b_im, v_s5_c_re, v_s5_c_im, v_s5_d_skip, v_s5_w_glu, v_s5_b_glu, v_gdn_w_in, v_gdn_conv_w, v_gdn_a_log, v_gdn_dt_bias, v_gdn_o_norm, v_fox_w_in, v_fox_b_f):
    args = dict(locals())
    wsh = {n: args[n] for n in WEIGHTS}
    msh = {n: args["m_" + n] for n in WEIGHTS}
    vsh = {n: args["v_" + n] for n in WEIGHTS}
    h0, memx, target = x[0], mem[0], loss_target[0]
    S, D = h0.shape
    W = D - MEM_WIDTH
    depth = norm1.shape[0]
    me = 4 * lax.axis_index("x") + 2 * lax.axis_index("y") + lax.axis_index("c")

    big_shapes = [wsh[n].shape for n, _ in BIG]
    small_shapes = [wsh[n].shape + (2,) for n, _ in SMALL_SHARDED]
    send = _pack([wsh[n].astype(BF16) for n, _ in BIG]
                 + [lax.bitcast_convert_type(wsh[n], BF16) for n, _ in SMALL_SHARDED], PACK_COLS)
    got = exchange(send, same_block=True, name="gather_weights")
    parts = _unpack(got, big_shapes + small_shapes, lead=(N_DEV,))
    full = {}
    for (n, ax), g in zip(BIG, parts[:len(BIG)]):
        full[n] = _from_shards(g, ax)
    for (n, ax), g in zip(SMALL_SHARDED, parts[len(BIG):]):
        full[n] = _from_shards(lax.bitcast_convert_type(g, F32), ax)
    for n in REPLICATED:
        full[n] = wsh[n]

    def layer_params(i):
        kind, j = i % 3, i // 3
        if kind == 0:
            p = {k: full["s5_" + k][j] for k in ("lam_re", "lam_im", "log_dt", "b_re", "b_im", "c_re", "c_im",
                                                   "d_skip", "b_glu")}
            return kind, j, p, full["s5_w_in"][j], 0
        if kind == 1:
            p = {k: full["gdn_" + k][j] for k in ("conv_w", "a_log", "dt_bias", "o_norm")}
            return kind, j, p, _permute_in(full["gdn_w_in"][j], 1, W), 2 * (W // HEAD_DIM)
        return kind, j, {"b_f": full["fox_b_f"][j]}, _permute_in(full["fox_w_in"][j], 2, W), W // HEAD_DIM

    memn = rms_fwd(memx, full["mem_norm"], out_dtype=BF16, name="mem_rms")
    mkv = mm(memn, full["w_mem_kv"], name="mem_kv")
    h = h0
    saved = []
    for i in range(depth):
        kind, j, p, w_in, n_small = layer_params(i)
        a = rms_fwd(h, full["norm1"][i], out_dtype=BF16, name=f"rms1_{i}")
        proj = mm(a, w_in, name=f"in_proj_{i}")
        if kind == 0:
            mix, ms = s5_mixer_fwd(proj, W, p, full["s5_w_glu"][j], f"l{i}")
            mem_col = W
        elif kind == 1:
            mix, ms = gdn_mixer_fwd(proj, W, p, f"l{i}")
            mem_col = 4 * W
        else:
            mix, ms = fox_mixer_fwd(proj, 3 * W + MEM_WIDTH, p["b_f"], f"l{i}")
            mem_col = 3 * W
        read = mem_fwd(proj, mem_col, mkv, name=f"mem_fwd_{i}")
        cat = jnp.concatenate([mix, read], axis=1)
        h_mid = mm(cat, full["w_out"][i], extras=(h,), epilogue=_add, name=f"out_proj_{i}")
        a2 = rms_fwd(h_mid, full["norm2"][i], out_dtype=BF16, name=f"rms2_{i}")
        u, act = mm(a2, full["w_up"][i], epilogue=_relu2, out_dtypes=(F32, BF16), name=f"up_{i}")
        h_next = mm(act, full["w_down"][i], extras=(h_mid,), epilogue=_add, name=f"down_{i}")
        saved.append((h, a, proj, ms, mem_col, cat, h_mid, a2, u, act))
        h = h_next

    loss_row, dh, dnf = final_loss(h, full["norm_f"], target, name="final_loss")
    loss = lax.psum(jnp.sum(loss_row), ("x", "y", "c"))

    grads = {n: [None] * full[n].shape[0] for n in ("norm1", "norm2", "w_out", "w_up", "w_down")}
    for pre, cnt in (("s5_", (depth + 2) // 3), ("gdn_", (depth + 1) // 3), ("fox_", depth // 3)):
        for n in WEIGHTS:
            if n.startswith(pre):
                grads[n] = [None] * cnt
    dmkv = None
    for i in reversed(range(depth)):
        kind, j, p, w_in, n_small = layer_params(i)
        h_in, a, proj, ms, mem_col, cat, h_mid, a2, u, act = saved[i]
        du = mm(dh, full["w_down"][i], tb=True, extras=(u,), epilogue=_relu2_grad, out_dtypes=(BF16,), name=f"d_act_{i}")
        grads["w_down"][i] = mm(act, dh, ta=True, name=f"dw_down_{i}")
        da2 = mm(du, full["w_up"][i], tb=True, name=f"d_a2_{i}")
        grads["w_up"][i] = mm(a2, du, ta=True, name=f"dw_up_{i}")
        dh_mid, dn2 = rms_bwd(h_mid, full["norm2"][i], da2, dh, name=f"rms2_bwd_{i}")
        grads["norm2"][i] = dn2[0]
        dcat = mm(dh_mid, full["w_out"][i], tb=True, name=f"d_cat_{i}")
        grads["w_out"][i] = mm(cat, dh_mid, ta=True, name=f"dw_out_{i}")
        dq_mem, dmkv_i = mem_bwd(proj, mem_col, mkv, dcat, W, name=f"mem_bwd_{i}")
        dmkv = dmkv_i if dmkv is None else dmkv + dmkv_i
        if kind == 0:
            dmain, g = s5_mixer_bwd(proj, W, p, full["s5_w_glu"][j], ms, dcat, f"l{i}")
            dproj = jnp.concatenate([dmain, dq_mem], axis=1)
            for k, val in g.items():
                grads["s5_" + k][j] = val
        elif kind == 1:
            dx, dgate, dab, g = gdn_mixer_bwd(proj, W, p, ms, dcat, f"l{i}")
            dproj = jnp.concatenate([dx, dgate, dq_mem, jnp.pad(dab, ((0, 0), (0, MEM_WIDTH - 128)))], axis=1)
            for k, val in g.items():
                grads["gdn_" + k][j] = val
        else:
            dq, dk, dv, df, db = fox_mixer_bwd(proj, 3 * W + MEM_WIDTH, ms, dcat, f"l{i}")
            dproj = jnp.concatenate([dq, dk, dv, dq_mem, jnp.pad(df, ((0, 0), (0, MEM_WIDTH - 128)))], axis=1)
            grads["fox_b_f"][j] = db
        da = mm(dproj, w_in, tb=True, name=f"d_a_{i}")
        dw_in = _unpermute_in(mm(a, dproj, ta=True, name=f"dw_in_{i}"), kind, W, n_small)
        grads[("s5_w_in", "gdn_w_in", "fox_w_in")[kind]][j] = dw_in
        dh, dn1 = rms_bwd(h_in, full["norm1"][i], da, dh_mid, name=f"rms1_bwd_{i}")
        grads["norm1"][i] = dn1[0]
    grad_x = dh[None]
    grads = {n: jnp.stack(v) for n, v in grads.items()}
    grads["norm_f"] = dnf[0]
    grads["w_mem_kv"] = mm(memn, dmkv, ta=True, name="dw_mem_kv")
    dmemn = mm(dmkv, full["w_mem_kv"], tb=True, name="d_memn")
    _, dmn = rms_bwd(memx, full["mem_norm"], dmemn, None, name="mem_rms_bwd")
    grads["mem_norm"] = dmn[0]

    send = _pack([_to_shards(grads[n], ax).astype(BF16) for n, ax in BIG], PACK_COLS, lead=(N_DEV,))
    shares = exchange(send, same_block=False, name="scatter_grads")
    outs = adamw(shares, *[_pack([t[n] for n, _ in BIG], PACK_COLS) for t in (wsh, msh, vsh)], name="adamw_big")
    big_out = [dict(zip([n for n, _ in BIG], _unpack(o, big_shapes))) for o in outs]

    small_names = list(REPLICATED) + [n for n, _ in SMALL_SHARDED]
    send = _pack([grads[n] for n in small_names], 128)
    total = sum_parts(exchange(send, same_block=True, name="gather_small_grads"), name="sum_small_grads")
    gsmall = dict(zip(small_names, _unpack(total, [grads[n].shape for n in small_names])))
    for n, ax in SMALL_SHARDED:
        width = wsh[n].shape[ax]
        gsmall[n] = lax.dynamic_slice_in_dim(gsmall[n], me * width, width, axis=ax)
    outs = adamw(_pack([gsmall[n] for n in small_names], 128)[None],
                 *[_pack([t[n] for n in small_names], 128) for t in (wsh, msh, vsh)], name="adamw_small")
    small_out = [dict(zip(small_names, _unpack(o, [wsh[n].shape for n in small_names]))) for o in outs]

    res = [{**b, **s} for b, s in zip(big_out, small_out)]
    return (loss, grad_x, *[r[n] for r in res for n in WEIGHTS])
```

```python
import functools
import math

import jax
import jax.numpy as jnp
from jax import lax
from jax.experimental import pallas as pl
from jax.experimental.pallas import tpu as pltpu

F32 = jnp.float32
BF16 = jnp.bfloat16

HEAD_DIM = 128
MEM_HEADS = 4
MEM_WIDTH = MEM_HEADS * HEAD_DIM
S5_GROUP = 16
S5_STATE = 64
GDN_CHUNK = 64
EPS = 1e-6
ADAM_LR, ADAM_B1, ADAM_B2, ADAM_EPS, ADAM_WD, ADAM_STEP = 0.001, 0.9, 0.999, 1e-08, 0.01, 10

N_DEV = 8
MESH = pl.DeviceIdType.MESH
VMEM_LIMIT = 56 * 1024 * 1024
HI = lax.Precision.HIGHEST


def _tile(n, prefs=(1024, 512, 256, 128)):
    for t in prefs:
        if n % t == 0:
            return t
    return n


def _cparams(sem=None):
    return pltpu.CompilerParams(dimension_semantics=sem, vmem_limit_bytes=VMEM_LIMIT)


def _dot(a, b, ca=1, cb=0, precision=None):
    return lax.dot_general(a, b, (((ca,), (cb,)), ((), ())), preferred_element_type=F32, precision=precision)


def _bdot(a, b):
    return _dot(a.astype(BF16), b.astype(BF16))


def _exchange_copies(p_refs, out_refs, same_block, send_sems, recv_sems, local_sems):
    pos = (lax.axis_index("x"), lax.axis_index("y"), lax.axis_index("c"))
    me = 4 * pos[0] + 2 * pos[1] + pos[2]
    copies = []
    for n, (p_ref, out_ref, same) in enumerate(zip(p_refs, out_refs, same_block)):
        block = (lambda j, r=p_ref: r) if same else (lambda j, r=p_ref: r.at[j])
        copies.append(pltpu.make_async_copy(block(me), out_ref.at[me], local_sems.at[n]))
        for m in range(1, N_DEV):
            peer = tuple(1 - v if (m >> (2 - b)) & 1 else v for b, v in enumerate(pos))
            sem = n * (N_DEV - 1) + m - 1
            copies.append(pltpu.make_async_remote_copy(
                src_ref=block(4 * peer[0] + 2 * peer[1] + peer[2]), dst_ref=out_ref.at[me],
                send_sem=send_sems.at[sem], recv_sem=recv_sems.at[sem], device_id=peer, device_id_type=MESH))
    return copies


def _exchange_shapes(arrays, same_block):
    shapes = [jax.ShapeDtypeStruct((N_DEV,) + tuple(p.shape if same else p.shape[1:]), p.dtype)
              for p, same in zip(arrays, same_block)]
    n = len(arrays)
    sems = [pltpu.SemaphoreType.DMA((n * (N_DEV - 1),)), pltpu.SemaphoreType.DMA((n * (N_DEV - 1),)),
            pltpu.SemaphoreType.DMA((n,))]
    return shapes, sems


def _mm_tk(K, tm, tn, a_bytes, b_bytes, out_bytes):
    for tk in (2048, 1024, 512, 256, 128):
        if K % tk == 0 and 2 * tk * (tm * a_bytes + tn * b_bytes) + (2 * out_bytes + 4) * tm * tn <= 40 * 2 ** 20:
            return tk
    return K


def mm(a, b, *, ta=False, tb=False, extras=(), epilogue=None, out_dtypes=(F32,), name, b_shard8=False,
       out_shard8=False, comm=()):
    M, K = (a.shape[1], a.shape[0]) if ta else a.shape
    if b_shard8:
        brows, bcols = b.shape[1], b.shape[2] * N_DEV
    else:
        brows, bcols = b.shape
    N = brows if tb else bcols
    assert K == (bcols if tb else brows), (a.shape, b.shape, ta, tb)
    tm = _tile(M)
    tn = _tile(N // N_DEV if (out_shard8 or (b_shard8 and not tb)) else N)
    out_bytes = sum(jnp.dtype(d).itemsize for d in out_dtypes) + 4 * sum(e.shape[0] != 1 for e in extras)
    tk = _mm_tk(K // N_DEV if (b_shard8 and tb) else K, tm, tn, a.dtype.itemsize, b.dtype.itemsize, out_bytes)
    ni, nj, nk = M // tm, N // tn, K // tk
    n_ex, n_out, n_comm = len(extras), len(out_dtypes), len(comm)
    ca, cb = (0 if ta else 1), (1 if tb else 0)
    same_block = [s for _, s in comm]

    def body(a_ref, b_ref, *rest):
        ex_refs, rest = rest[:n_ex], rest[n_ex:]
        cin, rest = rest[:n_comm], rest[n_comm:]
        out_refs, rest = rest[:n_out], rest[n_out:]
        cout, rest = rest[:n_comm], rest[n_comm:]
        acc = rest[0]
        i, j, k = pl.program_id(0), pl.program_id(1), pl.program_id(2)
        if n_comm:
            @pl.when((i == 0) & (j == 0) & (k == 0))
            def _():
                for cp in _exchange_copies(cin, cout, same_block, *rest[1:]):
                    cp.start()

        def finish(res):
            outs = (res,) if epilogue is None else epilogue(res, *[e[...] for e in ex_refs])
            for o, v in zip(out_refs, outs):
                o[...] = v.astype(o.dtype)

        part = _dot(a_ref[...].astype(BF16), b_ref[...].astype(BF16), ca, cb)
        if nk == 1:
            finish(part)
        else:
            @pl.when(k == 0)
            def _():
                acc[...] = part

            @pl.when(k > 0)
            def _():
                acc[...] += part

            @pl.when(k == nk - 1)
            def _():
                finish(acc[...])

        if n_comm:
            @pl.when((i == ni - 1) & (j == nj - 1) & (k == nk - 1))
            def _():
                for cp in _exchange_copies(cin, cout, same_block, *rest[1:]):
                    cp.wait()

    a_spec = pl.BlockSpec((tk, tm), lambda i, j, k: (k, i)) if ta else pl.BlockSpec((tm, tk), lambda i, j, k: (i, k))
    if b_shard8 and tb:
        kper = bcols // N_DEV // tk
        b_spec = pl.BlockSpec((None, tn, tk), lambda i, j, k: (k // kper, j, k % kper))
    elif b_shard8:
        nper = bcols // N_DEV // tn
        b_spec = pl.BlockSpec((None, tk, tn), lambda i, j, k: (j // nper, k, j % nper))
    elif tb:
        b_spec = pl.BlockSpec((tn, tk), lambda i, j, k: (j, k))
    else:
        b_spec = pl.BlockSpec((tk, tn), lambda i, j, k: (k, j))
    ex_specs = [pl.BlockSpec((1, tn), lambda i, j, k: (0, j)) if e.shape[0] == 1 and M != 1
                else pl.BlockSpec((tm, tn), lambda i, j, k: (i, j)) for e in extras]
    if out_shard8:
        nper = N // N_DEV // tn
        out_spec = pl.BlockSpec((None, tm, tn), lambda i, j, k: (j // nper, i, j % nper))
        out_shape = (N_DEV, M, N // N_DEV)
    else:
        out_spec = pl.BlockSpec((tm, tn), lambda i, j, k: (i, j))
        out_shape = (M, N)
    comm_shapes, sems = _exchange_shapes([p for p, _ in comm], same_block) if n_comm else ([], [])
    any_spec = pl.BlockSpec(memory_space=pl.ANY)
    outs = pl.pallas_call(
        body, name=name, grid=(ni, nj, nk),
        in_specs=[a_spec, b_spec] + ex_specs + [any_spec] * n_comm,
        out_specs=[out_spec] * n_out + [any_spec] * n_comm,
        out_shape=[jax.ShapeDtypeStruct(out_shape, d) for d in out_dtypes] + comm_shapes,
        scratch_shapes=[pltpu.VMEM((tm, tn) if nk > 1 else (8, 128), F32)] + sems,
        compiler_params=_cparams(("arbitrary",) * 3 if n_comm else ("parallel", "parallel", "arbitrary")),
    )(a, b, *extras, *[p for p, _ in comm])
    if n_comm:
        return (outs[0] if n_out == 1 else tuple(outs[:n_out])), list(outs[n_out:])
    return outs[0] if n_out == 1 else outs


def rms_fwd(x, g, *, out_dtype, name):
    S, D = x.shape
    ts = _tile(S, (512, 256, 128))

    def body(x_ref, g_ref, y_ref):
        x = x_ref[...]
        r = lax.rsqrt(jnp.mean(x * x, axis=-1, keepdims=True) + EPS)
        y_ref[...] = (x * r * g_ref[...]).astype(y_ref.dtype)

    return pl.pallas_call(
        body, name=name, grid=(S // ts,),
        in_specs=[pl.BlockSpec((ts, D), lambda i: (i, 0)), pl.BlockSpec((1, D), lambda i: (0, 0))],
        out_specs=pl.BlockSpec((ts, D), lambda i: (i, 0)),
        out_shape=jax.ShapeDtypeStruct((S, D), out_dtype),
        compiler_params=_cparams(("parallel",)),
    )(x, g.reshape(1, D))


def rms_bwd(x, g, dy, dres, *, name):
    S, D = x.shape
    ts = _tile(S, (512, 256, 128))
    has_res = dres is not None

    def body(x_ref, g_ref, dy_ref, *rest):
        dx_ref, dg_ref = rest[-2:]
        x = x_ref[...]
        r = lax.rsqrt(jnp.mean(x * x, axis=-1, keepdims=True) + EPS)
        xh = x * r
        dy = dy_ref[...].astype(F32)
        dxh = dy * g_ref[...]
        dx = r * (dxh - xh * jnp.mean(dxh * xh, axis=-1, keepdims=True))
        if has_res:
            dx = dx + rest[0][...]
        dx_ref[...] = dx

        @pl.when(pl.program_id(0) == 0)
        def _():
            dg_ref[...] = jnp.zeros_like(dg_ref)

        dg_ref[...] += jnp.sum(dy * xh, axis=0, keepdims=True)

    tok = pl.BlockSpec((ts, D), lambda i: (i, 0))
    row = pl.BlockSpec((1, D), lambda i: (0, 0))
    return pl.pallas_call(
        body, name=name, grid=(S // ts,),
        in_specs=[tok, row, tok] + ([tok] if has_res else []),
        out_specs=[tok, row],
        out_shape=[jax.ShapeDtypeStruct((S, D), F32), jax.ShapeDtypeStruct((1, D), F32)],
        compiler_params=_cparams(("arbitrary",)),
    )(x, g.reshape(1, D), dy, *([dres] if has_res else []))


def final_loss(h, g, target, *, name):
    S, D = h.shape
    ts = _tile(S, (512, 256, 128))

    def body(x_ref, g_ref, t_ref, loss_ref, dx_ref, dg_ref):
        x = x_ref[...]
        r = lax.rsqrt(jnp.mean(x * x, axis=-1, keepdims=True) + EPS)
        xh = x * r
        err = xh * g_ref[...] - t_ref[...]
        dy = err * (1.0 / D)
        dxh = dy * g_ref[...]
        dx_ref[...] = r * (dxh - xh * jnp.mean(dxh * xh, axis=-1, keepdims=True))

        @pl.when(pl.program_id(0) == 0)
        def _():
            dg_ref[...] = jnp.zeros_like(dg_ref)
            loss_ref[...] = jnp.zeros_like(loss_ref)

        dg_ref[...] += jnp.sum(dy * xh, axis=0, keepdims=True)
        loss_ref[...] += jnp.sum(err * err, axis=0, keepdims=True) * (0.5 / D)

    tok = pl.BlockSpec((ts, D), lambda i: (i, 0))
    row = pl.BlockSpec((1, D), lambda i: (0, 0))
    return pl.pallas_call(
        body, name=name, grid=(S // ts,),
        in_specs=[tok, row, tok], out_specs=[row, tok, row],
        out_shape=[jax.ShapeDtypeStruct((1, D), F32), jax.ShapeDtypeStruct((S, D), F32),
                   jax.ShapeDtypeStruct((1, D), F32)],
        compiler_params=_cparams(("arbitrary",)),
    )(h, g.reshape(1, D), target)


def _mem_probs(q, k):
    s = _dot(q.astype(BF16), k.astype(BF16), 1, 1) * HEAD_DIM ** -0.5
    p = jnp.exp(s - jnp.max(s, axis=-1, keepdims=True))
    return p / jnp.sum(p, axis=-1, keepdims=True)


def mem_fwd(proj, q_col, mkv, *, name):
    S = proj.shape[0]
    L = mkv.shape[0]
    ts = _tile(S)

    def body(q_ref, kv_ref, o_ref):
        for h in range(MEM_HEADS):
            c = slice(h * HEAD_DIM, (h + 1) * HEAD_DIM)
            v = kv_ref[:, MEM_WIDTH + h * HEAD_DIM:MEM_WIDTH + (h + 1) * HEAD_DIM]
            p = _mem_probs(q_ref[:, c], kv_ref[:, c])
            o_ref[:, c] = _bdot(p, v).astype(o_ref.dtype)

    return pl.pallas_call(
        body, name=name, grid=(S // ts,),
        in_specs=[pl.BlockSpec((ts, MEM_WIDTH), lambda i: (i, q_col // MEM_WIDTH)),
                  pl.BlockSpec((L, 2 * MEM_WIDTH), lambda i: (0, 0))],
        out_specs=pl.BlockSpec((ts, MEM_WIDTH), lambda i: (i, 0)),
        out_shape=jax.ShapeDtypeStruct((S, MEM_WIDTH), BF16),
        compiler_params=_cparams(("parallel",)),
    )(proj, mkv)


def mem_bwd(proj, q_col, mkv, dcat, do_col, *, name):
    S = proj.shape[0]
    L = mkv.shape[0]
    ts = _tile(S)
    scale = HEAD_DIM ** -0.5

    def body(q_ref, kv_ref, do_ref, dq_ref, dkv_ref):
        @pl.when(pl.program_id(0) == 0)
        def _():
            dkv_ref[...] = jnp.zeros_like(dkv_ref)

        for h in range(MEM_HEADS):
            c = slice(h * HEAD_DIM, (h + 1) * HEAD_DIM)
            cv = slice(MEM_WIDTH + h * HEAD_DIM, MEM_WIDTH + (h + 1) * HEAD_DIM)
            q, k, v = q_ref[:, c].astype(BF16), kv_ref[:, c].astype(BF16), kv_ref[:, cv].astype(BF16)
            do = do_ref[:, c].astype(BF16)
            p = _mem_probs(q, k)
            dkv_ref[:, cv] += _dot(p.astype(BF16), do, 0, 0)
            dp = _dot(do, v, 1, 1)
            ds = (p * (dp - jnp.sum(dp * p, axis=-1, keepdims=True)) * scale).astype(BF16)
            dq_ref[:, c] = _dot(ds, k)
            dkv_ref[:, c] += _dot(ds, q, 0, 0)

    return pl.pallas_call(
        body, name=name, grid=(S // ts,),
        in_specs=[pl.BlockSpec((ts, MEM_WIDTH), lambda i: (i, q_col // MEM_WIDTH)),
                  pl.BlockSpec((L, 2 * MEM_WIDTH), lambda i: (0, 0)),
                  pl.BlockSpec((ts, MEM_WIDTH), lambda i: (i, do_col // MEM_WIDTH))],
        out_specs=[pl.BlockSpec((ts, MEM_WIDTH), lambda i: (i, 0)),
                   pl.BlockSpec((L, 2 * MEM_WIDTH), lambda i: (0, 0))],
        out_shape=[jax.ShapeDtypeStruct((S, MEM_WIDTH), F32), jax.ShapeDtypeStruct((L, 2 * MEM_WIDTH), F32)],
        compiler_params=_cparams(("arbitrary",)),
    )(proj, mkv, dcat)


def _lower_ones(n, strict=False):
    r = lax.broadcasted_iota(jnp.int32, (n, n), 0)
    c = lax.broadcasted_iota(jnp.int32, (n, n), 1)
    return (r > c if strict else r >= c).astype(F32)


def fox_gate_fwd(proj, f_col, b_f_row, *, name):
    S = proj.shape[0]
    tb = _tile(S, (256, 128))

    def body(f_ref, b_ref, c_ref, carry):
        @pl.when(pl.program_id(0) == 0)
        def _():
            carry[...] = jnp.zeros_like(carry)

        ls = jax.nn.log_sigmoid(f_ref[...] + b_ref[...])
        cum = _dot(_lower_ones(tb), ls, precision=HI) + carry[...]
        c_ref[...] = cum
        carry[...] = cum[tb - 1:tb, :]

    return pl.pallas_call(
        body, name=name, grid=(S // tb,),
        in_specs=[pl.BlockSpec((tb, 128), lambda i: (i, f_col // 128)), pl.BlockSpec((1, 128), lambda i: (0, 0))],
        out_specs=pl.BlockSpec((tb, 128), lambda i: (i, 0)),
        out_shape=jax.ShapeDtypeStruct((S, 128), F32),
        scratch_shapes=[pltpu.VMEM((1, 128), F32)],
        compiler_params=_cparams(("arbitrary",)),
    )(proj, b_f_row)


def fox_gate_bwd(dcf, proj, f_col, b_f_row, *, name):
    S = proj.shape[0]
    tb = _tile(S, (256, 128))
    nb = S // tb

    def body(d_ref, f_ref, b_ref, df_ref, db_ref, carry):
        @pl.when(pl.program_id(0) == 0)
        def _():
            carry[...] = jnp.zeros_like(carry)
            db_ref[...] = jnp.zeros_like(db_ref)

        upper = _lower_ones(tb).T
        rc = _dot(upper, d_ref[...], precision=HI) + carry[...]
        carry[...] = rc[0:1, :]
        df = rc * jax.nn.sigmoid(-(f_ref[...] + b_ref[...]))
        df_ref[...] = df
        db_ref[...] += jnp.sum(df, axis=0, keepdims=True)

    return pl.pallas_call(
        body, name=name, grid=(nb,),
        in_specs=[pl.BlockSpec((tb, 128), lambda i: (nb - 1 - i, 0)),
                  pl.BlockSpec((tb, 128), lambda i: (nb - 1 - i, f_col // 128)),
                  pl.BlockSpec((1, 128), lambda i: (0, 0))],
        out_specs=[pl.BlockSpec((tb, 128), lambda i: (nb - 1 - i, 0)), pl.BlockSpec((1, 128), lambda i: (0, 0))],
        out_shape=[jax.ShapeDtypeStruct((S, 128), F32), jax.ShapeDtypeStruct((1, 128), F32)],
        scratch_shapes=[pltpu.VMEM((1, 128), F32)],
        compiler_params=_cparams(("arbitrary",)),
    )(dcf, proj, b_f_row)


def _fox_block(S):
    return _tile(S, (512, 256, 128)) if S > 512 else S // 2


def _causal(s, qi, kj, bq):
    r = qi * bq + lax.broadcasted_iota(jnp.int32, s.shape, 0)
    c = kj * bq + lax.broadcasted_iota(jnp.int32, s.shape, 1)
    return jnp.where(r >= c, s, -jnp.inf)


def fox_fwd(proj, n_heads, cf_col, cf_row, *, name):
    S = proj.shape[0]
    H = n_heads
    bq = _fox_block(S)
    scale = HEAD_DIM ** -0.5

    def body(q_ref, k_ref, v_ref, cc_ref, cr_ref, o_ref, st_ref):
        qi = pl.program_id(1)
        q = q_ref[...].astype(BF16)
        cq = cc_ref[...]

        def step(j, carry, diagonal=False):
            m, l, acc = carry
            rows = pl.ds(pl.multiple_of(j * bq, bq), bq)
            k = k_ref[rows, :].astype(BF16)
            v = v_ref[rows, :].astype(BF16)
            s = _dot(q, k, 1, 1) * scale + cq - cr_ref[j]
            if diagonal:
                s = _causal(s, qi, j, bq)
            m2 = jnp.maximum(m, jnp.max(s, axis=-1, keepdims=True))
            p = jnp.exp(s - m2)
            a = jnp.exp(m - m2)
            return m2, a * l + jnp.sum(p, axis=-1, keepdims=True), a * acc + _dot(p.astype(BF16), v)

        init = (jnp.full((bq, 1), -jnp.inf, F32), jnp.zeros((bq, 1), F32), jnp.zeros((bq, HEAD_DIM), F32))
        m, l, acc = step(qi, lax.fori_loop(0, qi, step, init), diagonal=True)
        o_ref[...] = (acc / l).astype(o_ref.dtype)
        lane = lax.broadcasted_iota(jnp.int32, (bq, 128), 1)
        st_ref[...] = jnp.where(lane == 0, m + jnp.log(l), jnp.where(lane == 1, cq, 0.0))

    return pl.pallas_call(
        body, name=name, grid=(H, S // bq),
        in_specs=[pl.BlockSpec((bq, HEAD_DIM), lambda h, i: (i, h)),
                  pl.BlockSpec((S, HEAD_DIM), lambda h, i: (0, H + h)),
                  pl.BlockSpec((S, HEAD_DIM), lambda h, i: (0, 2 * H + h)),
                  pl.BlockSpec((None, bq, 1), lambda h, i: (h, i, 0)),
                  pl.BlockSpec((None, S // bq, 1, bq), lambda h, i: (h, 0, 0, 0))],
        out_specs=[pl.BlockSpec((bq, HEAD_DIM), lambda h, i: (i, h)),
                   pl.BlockSpec((None, bq, 128), lambda h, i: (h, i, 0))],
        out_shape=[jax.ShapeDtypeStruct((S, H * HEAD_DIM), BF16), jax.ShapeDtypeStruct((H, S, 128), F32)],
        compiler_params=_cparams(("parallel", "parallel")),
    )(proj, proj, proj, cf_col, cf_row)


def fox_delta(stats, o, dcat, *, name):
    H, S, _ = stats.shape
    ts = _tile(S)

    def body(st_ref, o_ref, do_ref, out_ref):
        d = jnp.sum(o_ref[...].astype(F32) * do_ref[...].astype(F32), axis=-1, keepdims=True)
        lane = lax.broadcasted_iota(jnp.int32, (ts, 128), 1)
        out_ref[...] = jnp.where(lane == 2, d, st_ref[...])

    return pl.pallas_call(
        body, name=name, grid=(H, S // ts),
        in_specs=[pl.BlockSpec((None, ts, 128), lambda h, i: (h, i, 0)),
                  pl.BlockSpec((ts, HEAD_DIM), lambda h, i: (i, h)),
                  pl.BlockSpec((ts, HEAD_DIM), lambda h, i: (i, h))],
        out_specs=pl.BlockSpec((None, ts, 128), lambda h, i: (h, i, 0)),
        out_shape=jax.ShapeDtypeStruct((H, S, 128), F32),
        compiler_params=_cparams(("parallel", "parallel")),
    )(stats, o, dcat)


def fox_bwd(proj, n_heads, stats, cf_row, dcat, *, name):
    S = proj.shape[0]
    H = n_heads
    bq = _fox_block(S)
    nq = S // bq
    scale = HEAD_DIM ** -0.5

    def body(q_ref, k_ref, v_ref, do_ref, st_ref, cr_ref, dq_ref, dk_ref, dv_ref, dst_ref, dcr_ref):
        kj = pl.program_id(1)

        @pl.when(kj == 0)
        def _():
            dq_ref[...] = jnp.zeros_like(dq_ref)
            dst_ref[...] = jnp.zeros_like(dst_ref)

        k = k_ref[...].astype(BF16)
        v = v_ref[...].astype(BF16)
        ck = cr_ref[...]
        lane = lax.broadcasted_iota(jnp.int32, (bq, 128), 1)

        def step(qi, carry, diagonal=False):
            dk, dv, dck = carry
            rows = pl.ds(pl.multiple_of(qi * bq, bq), bq)
            q = q_ref[rows, :].astype(BF16)
            do = do_ref[rows, :].astype(BF16)
            st = st_ref[rows, :]
            lse, cq, delta = st[:, 0:1], st[:, 1:2], st[:, 2:3]
            s = _dot(q, k, 1, 1) * scale + cq - ck
            if diagonal:
                s = _causal(s, qi, kj, bq)
            p = jnp.exp(s - lse)
            dv = dv + _dot(p.astype(BF16), do, 0, 0)
            ds = p * (_dot(do, v, 1, 1) - delta)
            dsb = (ds * scale).astype(BF16)
            dq_ref[rows, :] += _dot(dsb, k)
            dst_ref[rows, :] += jnp.where(lane == 0, jnp.sum(ds, axis=-1, keepdims=True), 0.0)
            return dk + _dot(dsb, q, 0, 0), dv, dck - jnp.sum(ds, axis=0, keepdims=True)

        init = (jnp.zeros((bq, HEAD_DIM), F32), jnp.zeros((bq, HEAD_DIM), F32), jnp.zeros((1, bq), F32))
        dk, dv, dck = lax.fori_loop(kj + 1, nq, step, step(kj, init, diagonal=True))
        dk_ref[...] = dk
        dv_ref[...] = dv
        dcr_ref[...] = dck

    W = H * HEAD_DIM
    return pl.pallas_call(
        body, name=name, grid=(H, nq),
        in_specs=[pl.BlockSpec((S, HEAD_DIM), lambda h, j: (0, h)),
                  pl.BlockSpec((bq, HEAD_DIM), lambda h, j: (j, H + h)),
                  pl.BlockSpec((bq, HEAD_DIM), lambda h, j: (j, 2 * H + h)),
                  pl.BlockSpec((S, HEAD_DIM), lambda h, j: (0, h)),
                  pl.BlockSpec((None, S, 128), lambda h, j: (h, 0, 0)),
                  pl.BlockSpec((None, None, 1, bq), lambda h, j: (h, j, 0, 0))],
        out_specs=[pl.BlockSpec((S, HEAD_DIM), lambda h, j: (0, h)),
                   pl.BlockSpec((bq, HEAD_DIM), lambda h, j: (j, h)),
                   pl.BlockSpec((bq, HEAD_DIM), lambda h, j: (j, h)),
                   pl.BlockSpec((None, S, 128), lambda h, j: (h, 0, 0)),
                   pl.BlockSpec((None, None, 1, bq), lambda h, j: (h, j, 0, 0))],
        out_shape=[jax.ShapeDtypeStruct((S, W), F32), jax.ShapeDtypeStruct((S, W), F32),
                   jax.ShapeDtypeStruct((S, W), F32), jax.ShapeDtypeStruct((H, S, 128), F32),
                   jax.ShapeDtypeStruct((H, nq, 1, bq), F32)],
        compiler_params=_cparams(("parallel", "arbitrary")),
    )(proj, proj, proj, dcat, stats, cf_row)


def _pad_row(v, n=128):
    return jnp.pad(v.astype(F32), (0, n - v.shape[0])).reshape(1, n)


def fox_mixer_fwd(proj, f_col, b_f, tag):
    S = proj.shape[0]
    H = b_f.shape[0]
    bq = _fox_block(S)
    b_row = _pad_row(b_f)
    cf = fox_gate_fwd(proj, f_col, b_row, name=f"fox_gate_fwd_{tag}")
    cf_t = cf[:, :H].T
    o, stats = fox_fwd(proj, H, cf_t.reshape(H, S, 1), cf_t.reshape(H, S // bq, 1, bq), name=f"fox_fwd_{tag}")
    return o, (o, stats, cf_t, b_row)


def fox_mixer_bwd(proj, f_col, saved, dcat, tag):
    o, stats, cf_t, b_row = saved
    H, S = cf_t.shape
    bq = _fox_block(S)
    stats = fox_delta(stats, o, dcat, name=f"fox_delta_{tag}")
    dq, dk, dv, dst, dcr = fox_bwd(proj, H, stats, cf_t.reshape(H, S // bq, 1, bq), dcat, name=f"fox_bwd_{tag}")
    dcf = dst[:, :, 0] + dcr.reshape(H, S)
    dcf = jnp.pad(dcf.T, ((0, 0), (0, 128 - H)))
    df, db = fox_gate_bwd(dcf, proj, f_col, b_row, name=f"fox_gate_bwd_{tag}")
    return dq, dk, dv, df, db[0, :H]


def ew(fn, tok_ins, row_ins, tok_out_dtypes, n_row_out, *, width, name, tc=None, ts=None):
    S = tok_ins[0][0].shape[0]
    tc = tc or _tile(width, (512, 256, 128))
    ts = ts or _tile(S, (512, 256, 128))
    n_tok, n_row, n_to = len(tok_ins), len(row_ins), len(tok_out_dtypes)
    for _, col in tok_ins:
        assert col % tc == 0

    def body(*refs):
        ins = [r[...] for r in refs[:n_tok + n_row]]
        outs = fn(*ins)
        outs = outs if isinstance(outs, (tuple, list)) else (outs,)
        o_refs = refs[n_tok + n_row:]
        for o, v in zip(o_refs[:n_to], outs[:n_to]):
            o[...] = v.astype(o.dtype)
        if n_row_out:
            @pl.when(pl.program_id(1) == 0)
            def _():
                for o in o_refs[n_to:]:
                    o[...] = jnp.zeros_like(o)

            for o, v in zip(o_refs[n_to:], outs[n_to:]):
                o[...] += jnp.sum(v, axis=0, keepdims=True)

    def tok_spec(col):
        return pl.BlockSpec((ts, tc), lambda j, i: (i, col // tc + j))

    row_spec = pl.BlockSpec((1, tc), lambda j, i: (0, j))
    res = pl.pallas_call(
        body, name=name, grid=(width // tc, S // ts),
        in_specs=[tok_spec(col) for _, col in tok_ins] + [row_spec] * n_row,
        out_specs=[tok_spec(0)] * n_to + [row_spec] * n_row_out,
        out_shape=[jax.ShapeDtypeStruct((S, width), d) for d in tok_out_dtypes]
        + [jax.ShapeDtypeStruct((1, width), F32)] * n_row_out,
        compiler_params=_cparams(("parallel", "arbitrary" if n_row_out else "parallel")),
    )(*[a for a, _ in tok_ins], *row_ins)
    return res[0] if len(res) == 1 else res


S5_SUPER = 128 // S5_GROUP
S5_COLS = S5_SUPER * S5_STATE


def _shift_rows(x, d, up=False):
    T = x.shape[0]
    if d % 8 == 0:
        z = jnp.zeros((d, x.shape[1]), x.dtype)
        return jnp.concatenate([x[d:], z], axis=0) if up else jnp.concatenate([z, x[:T - d]], axis=0)
    row = lax.broadcasted_iota(jnp.int32, x.shape, 0)
    if up:
        return jnp.where(row < T - d, pltpu.roll(x, T - d, axis=0), 0.0)
    return jnp.where(row >= d, pltpu.roll(x, d, axis=0), 0.0)


def _s5_scan(bre, bim, apre, apim, up):
    hre, him = bre, bim
    for k in range(apre.shape[0]):
        are, aim = apre[k:k + 1], apim[k:k + 1]
        sre, sim = _shift_rows(hre, 1 << k, up), _shift_rows(him, 1 << k, up)
        hre, him = hre + are * sre - aim * sim, him + are * sim + aim * sre
    return hre, him


def _s5_states(u, bbre, bbim, apre, apim, start):
    T = u.shape[0]
    bre, bim = _bdot(u, bbre), _bdot(u, bbim)
    are, aim = apre[0:1], apim[0:1]
    sr, si = start[0:1], start[1:2]
    first = lax.broadcasted_iota(jnp.int32, bre.shape, 0) == 0
    bre = bre + jnp.where(first, are * sr - aim * si, 0.0)
    bim = bim + jnp.where(first, are * si + aim * sr, 0.0)
    return _s5_scan(bre, bim, apre, apim, False)


def _s5_block(S):
    return _tile(S, (256, 128))


def s5_scan_fwd(proj, W, bbre, bbim, cre, cim, apre, apim, *, name):
    S = proj.shape[0]
    T = _s5_block(S)
    NJ, nblk, LT = W // 128, S // T, apre.shape[1]

    def body(u_ref, bbre_ref, bbim_ref, cre_ref, cim_ref, apre_ref, apim_ref, y_ref, st_ref, carry):
        @pl.when(pl.program_id(1) == 0)
        def _():
            carry[...] = jnp.zeros_like(carry)

        st_ref[...] = carry[...]
        hre, him = _s5_states(u_ref[...], bbre_ref[...], bbim_ref[...], apre_ref[...], apim_ref[...], carry[...])
        carry[0:1, :] = hre[T - 1:T]
        carry[1:2, :] = him[T - 1:T]
        y_ref[...] = _bdot(hre, cre_ref[...]) - _bdot(him, cim_ref[...])

    mat = lambda r, c: pl.BlockSpec((None, r, c), lambda j, t: (j, 0, 0))
    return pl.pallas_call(
        body, name=name, grid=(NJ, nblk),
        in_specs=[pl.BlockSpec((T, 128), lambda j, t: (t, j)), mat(128, S5_COLS), mat(128, S5_COLS),
                  mat(S5_COLS, 128), mat(S5_COLS, 128), mat(LT, S5_COLS), mat(LT, S5_COLS)],
        out_specs=[pl.BlockSpec((T, 128), lambda j, t: (t, j)),
                   pl.BlockSpec((None, None, 2, S5_COLS), lambda j, t: (j, t, 0, 0))],
        out_shape=[jax.ShapeDtypeStruct((S, W), F32), jax.ShapeDtypeStruct((NJ, nblk, 2, S5_COLS), F32)],
        scratch_shapes=[pltpu.VMEM((2, S5_COLS), F32)],
        compiler_params=_cparams(("parallel", "arbitrary")),
    )(proj, bbre, bbim, cre, cim, apre, apim)


def s5_scan_bwd(proj, W, dy, d_skip_row, bbre, bbim, cre, cim, apre, apim, starts, *, name):
    S = proj.shape[0]
    T = _s5_block(S)
    NJ, nblk, LT = W // 128, S // T, apre.shape[1]

    def body(u_ref, dy_ref, d_ref, bbre_ref, bbim_ref, cre_ref, cim_ref, apre_ref, apim_ref, st_ref,
             du_ref, dbbre_ref, dbbim_ref, dcre_ref, dcim_ref, da_ref, gcarry):
        @pl.when(pl.program_id(1) == 0)
        def _():
            gcarry[...] = jnp.zeros_like(gcarry)
            for r in (dbbre_ref, dbbim_ref, dcre_ref, dcim_ref, da_ref):
                r[...] = jnp.zeros_like(r)

        u, dy, start = u_ref[...], dy_ref[...], st_ref[...]
        apre, apim = apre_ref[...], apim_ref[...]
        bbre, bbim, cre, cim = (r[...].astype(BF16) for r in (bbre_ref, bbim_ref, cre_ref, cim_ref))
        hre, him = _s5_states(u, bbre, bbim, apre, apim, start)
        dyb, ub = dy.astype(BF16), u.astype(BF16)
        dhre, dhim = _dot(dyb, cre, 1, 1), -_dot(dyb, cim, 1, 1)
        are, aim = apre[0:1], apim[0:1]
        gr, gi = gcarry[0:1], gcarry[1:2]
        last = lax.broadcasted_iota(jnp.int32, dhre.shape, 0) == T - 1
        dhre = dhre + jnp.where(last, are * gr + aim * gi, 0.0)
        dhim = dhim + jnp.where(last, are * gi - aim * gr, 0.0)
        gre, gim = _s5_scan(dhre, dhim, apre, -apim, True)
        gcarry[0:1, :] = gre[0:1]
        gcarry[1:2, :] = gim[0:1]
        greb, gimb = gre.astype(BF16), gim.astype(BF16)
        du_ref[...] = _dot(greb, bbre, 1, 1) + _dot(gimb, bbim, 1, 1) + dy * d_ref[...]
        dbbre_ref[...] += _dot(ub, greb, 0, 0)
        dbbim_ref[...] += _dot(ub, gimb, 0, 0)
        dcre_ref[...] += _dot(hre.astype(BF16), dyb, 0, 0)
        dcim_ref[...] -= _dot(him.astype(BF16), dyb, 0, 0)
        first = lax.broadcasted_iota(jnp.int32, hre.shape, 0) == 0
        pre = _shift_rows(hre, 1) + jnp.where(first, start[0:1], 0.0)
        pim = _shift_rows(him, 1) + jnp.where(first, start[1:2], 0.0)
        da_ref[0:1, :] += jnp.sum(gre * pre + gim * pim, axis=0, keepdims=True)
        da_ref[1:2, :] += jnp.sum(gim * pre - gre * pim, axis=0, keepdims=True)

    mat = lambda r, c: pl.BlockSpec((None, r, c), lambda j, t: (j, 0, 0))
    tok = pl.BlockSpec((T, 128), lambda j, t: (nblk - 1 - t, j))
    return pl.pallas_call(
        body, name=name, grid=(NJ, nblk),
        in_specs=[tok, tok, pl.BlockSpec((1, 128), lambda j, t: (0, j)), mat(128, S5_COLS), mat(128, S5_COLS),
                  mat(S5_COLS, 128), mat(S5_COLS, 128), mat(LT, S5_COLS), mat(LT, S5_COLS),
                  pl.BlockSpec((None, None, 2, S5_COLS), lambda j, t: (j, nblk - 1 - t, 0, 0))],
        out_specs=[tok, mat(128, S5_COLS), mat(128, S5_COLS), mat(S5_COLS, 128), mat(S5_COLS, 128), mat(2, S5_COLS)],
        out_shape=[jax.ShapeDtypeStruct((S, W), F32),
                   jax.ShapeDtypeStruct((NJ, 128, S5_COLS), F32), jax.ShapeDtypeStruct((NJ, 128, S5_COLS), F32),
                   jax.ShapeDtypeStruct((NJ, S5_COLS, 128), F32), jax.ShapeDtypeStruct((NJ, S5_COLS, 128), F32),
                   jax.ShapeDtypeStruct((NJ, 2, S5_COLS), F32)],
        scratch_shapes=[pltpu.VMEM((2, S5_COLS), F32)],
        compiler_params=_cparams(("parallel", "arbitrary")),
    )(proj, dy, d_skip_row, bbre, bbim, cre, cim, apre, apim, starts)


def _s5_discretize(lam_re, lam_im, log_dt, b_re, b_im):
    dt = jnp.exp(log_dt)[:, None]
    mag = jnp.exp(lam_re * dt)
    a_re, a_im = mag * jnp.cos(lam_im * dt), mag * jnp.sin(lam_im * dt)
    den = lam_re * lam_re + lam_im * lam_im
    z_re = ((a_re - 1.0) * lam_re + a_im * lam_im) / den
    z_im = (a_im * lam_re - (a_re - 1.0) * lam_im) / den
    bb_re = z_re[..., None] * b_re - z_im[..., None] * b_im
    bb_im = z_re[..., None] * b_im + z_im[..., None] * b_re
    return a_re, a_im, bb_re, bb_im


def _blockdiag(x):
    G, r, c = x.shape
    x = x.reshape(G // S5_SUPER, S5_SUPER, r, c)
    eye = jnp.eye(S5_SUPER, dtype=x.dtype)
    return (x[:, :, :, None, :] * eye[None, :, None, :, None]).reshape(G // S5_SUPER, S5_SUPER * r, S5_SUPER * c)


def _blockdiag_t(x, r, c):
    NJ = x.shape[0]
    x = x.reshape(NJ, S5_SUPER, r, S5_SUPER, c)
    return jnp.stack([x[:, i, :, i, :] for i in range(S5_SUPER)], axis=1).reshape(NJ * S5_SUPER, r, c)


def _gelu(x):
    c = math.sqrt(2.0 / math.pi)
    return 0.5 * x * (1.0 + jnp.tanh(c * (x + 0.044715 * x * x * x)))


def _gelu_grad(x):
    c = math.sqrt(2.0 / math.pi)
    t = jnp.tanh(c * (x + 0.044715 * x * x * x))
    return 0.5 * (1.0 + t) + 0.5 * x * (1.0 - t * t) * c * (1.0 + 3 * 0.044715 * x * x)


def s5_mixer_fwd(proj, W, p, w_glu_bf, tag):
    S = proj.shape[0]
    T = _s5_block(S)
    a_re, a_im, bb_re, bb_im = _s5_discretize(p["lam_re"], p["lam_im"], p["log_dt"], p["b_re"], p["b_im"])
    NJ = W // 128
    pows_re, pows_im = [a_re], [a_im]
    for _ in range(int(math.log2(T)) - 1):
        r, i = pows_re[-1], pows_im[-1]
        pows_re.append(r * r - i * i)
        pows_im.append(2.0 * r * i)
    apre = jnp.stack([x.reshape(NJ, S5_COLS) for x in pows_re], axis=1)
    apim = jnp.stack([x.reshape(NJ, S5_COLS) for x in pows_im], axis=1)
    bbre = _blockdiag(jnp.swapaxes(bb_re, 1, 2))
    bbim = _blockdiag(jnp.swapaxes(bb_im, 1, 2))
    cre = _blockdiag(jnp.swapaxes(p["c_re"], 1, 2))
    cim = _blockdiag(jnp.swapaxes(p["c_im"], 1, 2))
    mats = (bbre, bbim, cre, cim, apre, apim)
    y, starts = s5_scan_fwd(proj, W, *mats, name=f"s5_scan_fwd_{tag}")
    d_row = p["d_skip"].reshape(1, W)
    yy = ew(lambda y, u, d: _gelu(y + d * u), [(y, 0), (proj, 0)], [d_row], [F32], 0, width=W,
            name=f"s5_act_fwd_{tag}")
    mix, z = mm(yy, w_glu_bf, extras=(yy, p["b_glu"].reshape(1, W)), out_dtypes=(BF16, F32),
                epilogue=lambda acc, yy, b: (yy * jax.nn.sigmoid(acc + b), acc + b), name=f"s5_glu_{tag}")
    return mix, (mats, starts, y, yy, z, d_row)


def s5_mixer_bwd(proj, W, p, w_glu_bf, saved, dcat, tag):
    mats, starts, y, yy, z, d_row = saved

    def glu_bwd(dm, yy, z):
        sg = jax.nn.sigmoid(z)
        dz = dm.astype(F32) * yy * sg * (1.0 - sg)
        return dm.astype(F32) * sg, dz, dz

    dyy1, dz, db_glu = ew(glu_bwd, [(dcat, 0), (yy, 0), (z, 0)], [], [F32, F32], 1, width=W, name=f"s5_glu_bwd_{tag}")
    dw_glu = mm(yy, dz, ta=True, out_dtypes=(BF16,), name=f"s5_dwglu_{tag}")
    dyy = mm(dz, w_glu_bf, tb=True, extras=(dyy1,), epilogue=lambda acc, e: (acc + e,), name=f"s5_dyy_{tag}")

    def act_bwd(dyy, y, u, d):
        dpre = dyy * _gelu_grad(y + d * u)
        return dpre, dpre * u

    dy, dd = ew(act_bwd, [(dyy, 0), (y, 0), (proj, 0)], [d_row], [F32], 1, width=W, name=f"s5_act_bwd_{tag}")
    du, dbbre, dbbim, dcre, dcim, da = s5_scan_bwd(proj, W, dy, d_row, *mats, starts, name=f"s5_scan_bwd_{tag}")
    G = W // S5_GROUP
    dbb_re = jnp.swapaxes(_blockdiag_t(dbbre, S5_GROUP, S5_STATE), 1, 2)
    dbb_im = jnp.swapaxes(_blockdiag_t(dbbim, S5_GROUP, S5_STATE), 1, 2)
    dc_re = jnp.swapaxes(_blockdiag_t(dcre, S5_STATE, S5_GROUP), 1, 2)
    dc_im = jnp.swapaxes(_blockdiag_t(dcim, S5_STATE, S5_GROUP), 1, 2)
    da_re, da_im = da[:, 0, :].reshape(G, S5_STATE), da[:, 1, :].reshape(G, S5_STATE)
    _, vjp = jax.vjp(_s5_discretize, p["lam_re"], p["lam_im"], p["log_dt"], p["b_re"], p["b_im"])
    dlam_re, dlam_im, dlog_dt, db_re, db_im = vjp((da_re, da_im, dbb_re, dbb_im))
    grads = dict(lam_re=dlam_re, lam_im=dlam_im, log_dt=dlog_dt, b_re=db_re, b_im=db_im, c_re=dc_re, c_im=dc_im,
                 d_skip=dd.reshape(W), w_glu=dw_glu, b_glu=db_glu.reshape(W))
    return du, grads


CONV_ROWS = 256


def _conv_taps(ext, w):
    acc = ext * w[3:4]
    for j in range(1, 4):
        acc = acc + pltpu.roll(ext, j, axis=0) * w[3 - j:4 - j]
    return acc


def gdn_conv_fwd(proj, width, conv_w, *, name):
    S = proj.shape[0]
    tc = _tile(width, (256, 128))
    T = min(CONV_ROWS, S)
    n = S // T

    def body(x_ref, w_ref, y_ref):
        w = w_ref[...]

        def chunk(c, carry):
            base = pl.multiple_of(c * T, T)
            prev = x_ref[pl.ds(pl.multiple_of(jnp.maximum(base - 8, 0), 8), 8), :]
            ext = jnp.concatenate([jnp.where(c > 0, prev, 0.0), x_ref[pl.ds(base, T), :]], axis=0)
            pre = _conv_taps(ext, w)[8:]
            y_ref[pl.ds(base, T), :] = pre * jax.nn.sigmoid(pre)
            return carry

        lax.fori_loop(0, n, chunk, 0)

    return pl.pallas_call(
        body, name=name, grid=(width // tc,),
        in_specs=[pl.BlockSpec((S, tc), lambda j: (0, j)), pl.BlockSpec((4, tc), lambda j: (0, j))],
        out_specs=pl.BlockSpec((S, tc), lambda j: (0, j)),
        out_shape=jax.ShapeDtypeStruct((S, width), F32),
        compiler_params=_cparams(("parallel",)),
    )(proj, conv_w)


def gdn_conv_bwd(proj, width, conv_w, dy, *, name):
    S = proj.shape[0]
    tc = _tile(width, (256, 128))
    T = min(CONV_ROWS, S)
    n = S // T
    E = T + 16

    def body(x_ref, w_ref, dy_ref, dx_ref, dw_ref):
        w = w_ref[...]

        def halo(ref, start, keep):
            start = pl.multiple_of(jnp.clip(start, 0, S - 8), 8)
            return jnp.where(keep, ref[pl.ds(start, 8), :], 0.0)

        def chunk(c, dw):
            base = pl.multiple_of(c * T, T)
            rows = pl.ds(base, T)
            ext = jnp.concatenate([halo(x_ref, base - 8, c > 0), x_ref[rows, :], halo(x_ref, base + T, c < n - 1)], axis=0)
            dye = jnp.concatenate([jnp.zeros((8, tc), F32), dy_ref[rows, :], halo(dy_ref, base + T, c < n - 1)], axis=0)
            pre = _conv_taps(ext, w)
            sg = jax.nn.sigmoid(pre)
            dpre = dye * (sg * (1.0 + pre * (1.0 - sg)))
            dx = dpre * w[3:4]
            for j in range(1, 4):
                dx = dx + pltpu.roll(dpre, E - j, axis=0) * w[3 - j:4 - j]
            dx_ref[rows, :] = dx[8:8 + T]
            own = dpre[8:8 + T]
            parts = [jnp.sum(own * pltpu.roll(ext, 3 - i, axis=0)[8:8 + T], axis=0, keepdims=True) if i < 3
                     else jnp.sum(own * ext[8:8 + T], axis=0, keepdims=True) for i in range(4)]
            return dw + jnp.concatenate(parts, axis=0)

        dw_ref[...] = lax.fori_loop(0, n, chunk, jnp.zeros((4, tc), F32))

    return pl.pallas_call(
        body, name=name, grid=(width // tc,),
        in_specs=[pl.BlockSpec((S, tc), lambda j: (0, j)), pl.BlockSpec((4, tc), lambda j: (0, j)),
                  pl.BlockSpec((S, tc), lambda j: (0, j))],
        out_specs=[pl.BlockSpec((S, tc), lambda j: (0, j)), pl.BlockSpec((4, tc), lambda j: (0, j))],
        out_shape=[jax.ShapeDtypeStruct((S, width), F32), jax.ShapeDtypeStruct((4, width), F32)],
        compiler_params=_cparams(("parallel",)),
    )(proj, conv_w, dy)


@functools.partial(jax.custom_vjp, nondiff_argnums=(0,))
def _bein(spec, a, b):
    return jnp.einsum(spec, a.astype(BF16), b.astype(BF16), preferred_element_type=F32)


def _bein_fwd(spec, a, b):
    return _bein(spec, a, b), (a, b)


def _bein_bwd(spec, res, g):
    a, b = res
    ins, out = spec.split("->")
    sa, sb = ins.split(",")
    return _bein(f"{out},{sb}->{sa}", g, b), _bein(f"{sa},{out}->{sb}", a, g)


_bein.defvjp(_bein_fwd, _bein_bwd)


def _hmm(a, b):
    return jnp.einsum("ncs,nsd->ncd", a, b, precision=HI, preferred_element_type=F32)


def _inv_unit_lower(L):
    C = L.shape[-1]
    r = lax.broadcasted_iota(jnp.int32, L.shape, 1)
    c = lax.broadcasted_iota(jnp.int32, L.shape, 2)
    eye = (r == c).astype(F32)
    D = jnp.where(jnp.right_shift(r, 4) == jnp.right_shift(c, 4), L, 0.0)
    D2 = _hmm(D, D)
    D4 = _hmm(D2, D2)
    D8 = _hmm(D4, D4)
    dinv = _hmm(_hmm(_hmm(eye - D, eye + D2), eye + D4), eye + D8)
    N = _hmm(dinv, L - D)
    return _hmm(_hmm(eye - N, eye + _hmm(N, N)), dinv)


def _softplus(x):
    return jnp.maximum(x, 0.0) + jnp.log(1.0 + jnp.exp(-jnp.abs(x)))


def _gdn_prep(qc, kc, vc, ab, a_log_row, dt_row, h, n_heads):
    R = qc.shape[0]
    C = GDN_CHUNK
    n = R // C
    lane = lax.broadcasted_iota(jnp.int32, (1, 128), 1)
    pick = lambda x, i: jnp.sum(jnp.where(lane == i, x, 0.0), axis=-1, keepdims=True)
    a_in, b_in = pick(ab, h).reshape(n, C, 1), pick(ab, n_heads + h).reshape(n, C, 1)
    a_log, dt_bias = pick(a_log_row, h), pick(dt_row, h)
    q3, k3, v = qc.reshape(n, C, 128), kc.reshape(n, C, 128), vc.reshape(n, C, 128)
    q = q3 * lax.rsqrt(jnp.sum(q3 * q3, axis=-1, keepdims=True) + EPS) * HEAD_DIM ** -0.5
    k = k3 * lax.rsqrt(jnp.sum(k3 * k3, axis=-1, keepdims=True) + EPS)
    beta = jax.nn.sigmoid(b_in)
    g = -jnp.exp(a_log) * _softplus(a_in + dt_bias)
    r = lax.broadcasted_iota(jnp.int32, (n, C, C), 1)
    c = lax.broadcasted_iota(jnp.int32, (n, C, C), 2)
    gc = _hmm((r >= c).astype(F32), jnp.broadcast_to(g, (n, C, C)))
    gcol = gc[:, :, 0:1]
    grow = jnp.sum(jnp.where(r == c, gc, 0.0), axis=1, keepdims=True)
    decay = jnp.exp(jnp.where(r >= c, gc - grow, -jnp.inf))
    kb, vb = k * beta, v * beta
    lmat = jnp.where(r > c, _bein("ncd,nsd->ncs", kb, k) * decay, 0.0)
    eg = jnp.exp(gcol)
    rhs = jnp.concatenate([vb, kb * eg], axis=-1)
    sol = _hmm(_inv_unit_lower(lmat), rhs)
    attn = jnp.where(r >= c, _bein("ncd,nsd->ncs", q, k) * decay, 0.0)
    glast = gcol[:, C - 1:C, :]
    k_dec = k * jnp.exp(glast - gcol)
    g_last = jnp.broadcast_to(jnp.exp(glast), (n, 1, 128))
    return (sol[..., :128].reshape(R, 128), sol[..., 128:].reshape(R, 128), attn.reshape(R, C),
            (q * eg).reshape(R, 128), k_dec.reshape(R, 128), g_last)


def _gdn_rows(S):
    return _tile(S, (256, 128, 64))


def gdn_prep_fwd(qkv, proj, ab_col, a_log_row, dt_row, H, *, name):
    S = qkv.shape[0]
    R = _gdn_rows(S)
    n, nc = R // GDN_CHUNK, S // GDN_CHUNK

    def body(q_ref, k_ref, v_ref, ab_ref, al_ref, dt_ref, u_ref, w_ref, at_ref, qd_ref, kd_ref, gl_ref):
        outs = _gdn_prep(q_ref[...], k_ref[...], v_ref[...], ab_ref[...], al_ref[...], dt_ref[...],
                         pl.program_id(1), H)
        for r, v in zip((u_ref, w_ref, at_ref, qd_ref, kd_ref, gl_ref), outs):
            r[...] = v

    head = lambda off: pl.BlockSpec((R, 128), lambda i, h: (i, off + h))
    row = pl.BlockSpec((1, 128), lambda i, h: (0, 0))
    big = jax.ShapeDtypeStruct((S, H * 128), F32)
    return pl.pallas_call(
        body, name=name, grid=(S // R, H),
        in_specs=[head(0), head(H), head(2 * H), pl.BlockSpec((R, 128), lambda i, h: (i, ab_col // 128)), row, row],
        out_specs=[head(0), head(0), pl.BlockSpec((None, R, GDN_CHUNK), lambda i, h: (h, i, 0)), head(0), head(0),
                   pl.BlockSpec((None, n, 1, 128), lambda i, h: (h, i, 0, 0))],
        out_shape=[big, big, jax.ShapeDtypeStruct((H, S, GDN_CHUNK), F32), big, big,
                   jax.ShapeDtypeStruct((H, nc, 1, 128), F32)],
        compiler_params=_cparams(("parallel", "parallel")),
    )(qkv, qkv, qkv, proj, a_log_row, dt_row)


def gdn_prep_bwd(qkv, proj, ab_col, a_log_row, dt_row, H, cts, *, name):
    S = qkv.shape[0]
    R = _gdn_rows(S)
    n, nc = R // GDN_CHUNK, S // GDN_CHUNK

    def body(q_ref, k_ref, v_ref, ab_ref, al_ref, dt_ref, du_ref, dw_ref, dat_ref, dqd_ref, dkd_ref, dgl_ref,
             dq_ref, dk_ref, dv_ref, dab_ref, dal_ref, ddt_ref):
        i, h = pl.program_id(0), pl.program_id(1)

        @pl.when(h == 0)
        def _():
            dab_ref[...] = jnp.zeros_like(dab_ref)

        @pl.when((h == 0) & (i == 0))
        def _():
            dal_ref[...] = jnp.zeros_like(dal_ref)
            ddt_ref[...] = jnp.zeros_like(ddt_ref)

        f = lambda q, k, v, ab, al, dt: _gdn_prep(q, k, v, ab, al, dt, h, H)
        _, vjp = jax.vjp(f, q_ref[...], k_ref[...], v_ref[...], ab_ref[...], al_ref[...], dt_ref[...])
        dq, dk, dv, dab, dal, ddt = vjp(tuple(r[...] for r in (du_ref, dw_ref, dat_ref, dqd_ref, dkd_ref, dgl_ref)))
        dq_ref[...] = dq
        dk_ref[...] = dk
        dv_ref[...] = dv
        dab_ref[...] += dab
        dal_ref[...] += dal
        ddt_ref[...] += ddt

    head = lambda off: pl.BlockSpec((R, 128), lambda i, h: (i, off + h))
    row = pl.BlockSpec((1, 128), lambda i, h: (0, 0))
    at = pl.BlockSpec((None, R, GDN_CHUNK), lambda i, h: (h, i, 0))
    gl = pl.BlockSpec((None, n, 1, 128), lambda i, h: (h, i, 0, 0))
    W = H * 128
    dq, dk, dv, dab, dal, ddt = pl.pallas_call(
        body, name=name, grid=(S // R, H),
        in_specs=[head(0), head(H), head(2 * H), pl.BlockSpec((R, 128), lambda i, h: (i, ab_col // 128)), row, row,
                  head(0), head(0), at, head(0), head(0), gl],
        out_specs=[head(0), head(0), head(0), pl.BlockSpec((R, 128), lambda i, h: (i, 0)), row, row],
        out_shape=[jax.ShapeDtypeStruct((S, W), F32)] * 3 + [jax.ShapeDtypeStruct((S, 128), F32)]
        + [jax.ShapeDtypeStruct((1, 128), F32)] * 2,
        compiler_params=_cparams(("arbitrary", "arbitrary")),
    )(qkv, qkv, qkv, proj, a_log_row, dt_row, *cts)
    return dq, dk, dv, dab, dal, ddt


def gdn_scan_fwd(u, w, attn, qd, kd, gl, *, name):
    S = u.shape[0]
    H = attn.shape[0]
    C = GDN_CHUNK
    R = _gdn_rows(S)
    n, nc = R // C, S // C

    def body(u_ref, w_ref, at_ref, qd_ref, kd_ref, gl_ref, o_ref, st_ref, state):
        @pl.when(pl.program_id(1) == 0)
        def _():
            state[...] = jnp.zeros_like(state)

        for c in range(n):
            rows = slice(c * C, (c + 1) * C)
            s = state[...]
            st_ref[c] = s
            sb = s.astype(BF16)
            v_new = u_ref[rows, :] - _dot(w_ref[rows, :].astype(BF16), sb)
            vb = v_new.astype(BF16)
            o_ref[rows, :] = _dot(qd_ref[rows, :].astype(BF16), sb) + _dot(at_ref[rows, :].astype(BF16), vb)
            state[...] = s * gl_ref[c] + _dot(kd_ref[rows, :].astype(BF16), vb, 0, 0)

    head = pl.BlockSpec((R, 128), lambda h, i: (i, h))
    return pl.pallas_call(
        body, name=name, grid=(H, S // R),
        in_specs=[head, head, pl.BlockSpec((None, R, C), lambda h, i: (h, i, 0)), head, head,
                  pl.BlockSpec((None, n, 1, 128), lambda h, i: (h, i, 0, 0))],
        out_specs=[head, pl.BlockSpec((None, n, 128, 128), lambda h, i: (h, i, 0, 0))],
        out_shape=[jax.ShapeDtypeStruct((S, H * 128), F32), jax.ShapeDtypeStruct((H, nc, 128, 128), F32)],
        scratch_shapes=[pltpu.VMEM((128, 128), F32)],
        compiler_params=_cparams(("parallel", "arbitrary")),
    )(u, w, attn, qd, kd, gl)


def gdn_scan_bwd(u, w, attn, qd, kd, gl, states, do, *, name):
    S = u.shape[0]
    H = attn.shape[0]
    C = GDN_CHUNK
    R = _gdn_rows(S)
    n, nc, nb = R // C, S // C, S // R

    def body(u_ref, w_ref, at_ref, qd_ref, kd_ref, gl_ref, st_ref, do_ref,
             du_ref, dw_ref, dat_ref, dqd_ref, dkd_ref, dgl_ref, dstate):
        @pl.when(pl.program_id(1) == 0)
        def _():
            dstate[...] = jnp.zeros_like(dstate)

        lane = lax.broadcasted_iota(jnp.int32, (1, 128), 1)
        for c in reversed(range(n)):
            rows = slice(c * C, (c + 1) * C)
            s = st_ref[c]
            sb = s.astype(BF16)
            ds2 = dstate[...]
            ds2b = ds2.astype(BF16)
            wb, qdb, kdb, atb = (r[rows, :].astype(BF16) for r in (w_ref, qd_ref, kd_ref, at_ref))
            dob = do_ref[rows, :].astype(BF16)
            v_new = u_ref[rows, :] - _dot(wb, sb)
            vb = v_new.astype(BF16)
            dv = _dot(atb, dob, 0, 0) + _dot(kdb, ds2b)
            dvb = dv.astype(BF16)
            du_ref[rows, :] = dv
            dw_ref[rows, :] = -_dot(dvb, sb, 1, 1)
            dat_ref[rows, :] = _dot(dob, vb, 1, 1)
            dqd_ref[rows, :] = _dot(dob, sb, 1, 1)
            dkd_ref[rows, :] = _dot(vb, ds2b, 1, 1)
            dgl = jnp.sum(jnp.sum(ds2 * s, axis=1, keepdims=True), axis=0, keepdims=True)
            dgl_ref[c] = jnp.where(lane == 0, dgl, 0.0)
            dstate[...] = ds2 * gl_ref[c] + _dot(qdb, dob, 0, 0) - _dot(wb, dvb, 0, 0)

    head = pl.BlockSpec((R, 128), lambda h, i: (nb - 1 - i, h))
    at = pl.BlockSpec((None, R, C), lambda h, i: (h, nb - 1 - i, 0))
    glb = pl.BlockSpec((None, n, 1, 128), lambda h, i: (h, nb - 1 - i, 0, 0))
    big = jax.ShapeDtypeStruct((S, H * 128), F32)
    return pl.pallas_call(
        body, name=name, grid=(H, nb),
        in_specs=[head, head, at, head, head, glb,
                  pl.BlockSpec((None, n, 128, 128), lambda h, i: (h, nb - 1 - i, 0, 0)), head],
        out_specs=[head, head, at, head, head, glb],
        out_shape=[big, big, jax.ShapeDtypeStruct((H, S, C), F32), big, big,
                   jax.ShapeDtypeStruct((H, nc, 1, 128), F32)],
        scratch_shapes=[pltpu.VMEM((128, 128), F32)],
        compiler_params=_cparams(("parallel", "arbitrary")),
    )(u, w, attn, qd, kd, gl, states, do)


def _head_rms(o):
    return lax.rsqrt(jnp.mean(o * o, axis=-1, keepdims=True) + EPS)


def gdn_mixer_fwd(proj, W, p, tag):
    H = W // HEAD_DIM
    ab_col = 4 * W + MEM_WIDTH
    qkv = gdn_conv_fwd(proj, 3 * W, p["conv_w"], name=f"gdn_conv_fwd_{tag}")
    al_row, dt_row = _pad_row(p["a_log"]), _pad_row(p["dt_bias"])
    pre = gdn_prep_fwd(qkv, proj, ab_col, al_row, dt_row, H, name=f"gdn_prep_fwd_{tag}")
    o, states = gdn_scan_fwd(*pre, name=f"gdn_scan_fwd_{tag}")
    gn_row = jnp.tile(p["o_norm"].reshape(1, HEAD_DIM), (1, H))

    def gate_fwd(o, gate, gn):
        return o * _head_rms(o) * gn * (gate * jax.nn.sigmoid(gate))

    mix = ew(gate_fwd, [(o, 0), (proj, 3 * W)], [gn_row], [BF16], 0, width=W, tc=HEAD_DIM, name=f"gdn_gate_fwd_{tag}")
    return mix, (qkv, pre, states, o, gn_row, al_row, dt_row)


def gdn_mixer_bwd(proj, W, p, saved, dcat, tag):
    qkv, pre, states, o, gn_row, al_row, dt_row = saved
    H = W // HEAD_DIM
    ab_col = 4 * W + MEM_WIDTH

    def gate_bwd(dm, o, gate, gn):
        dm = dm.astype(F32)
        r = _head_rms(o)
        xh = o * r
        sg = jax.nn.sigmoid(gate)
        dy = dm * gate * sg
        dgate = dm * xh * gn * (sg * (1.0 + gate * (1.0 - sg)))
        dxh = dy * gn
        do = r * (dxh - xh * jnp.mean(dxh * xh, axis=-1, keepdims=True))
        return do, dgate, dy * xh

    do, dgate, dgn = ew(gate_bwd, [(dcat, 0), (o, 0), (proj, 3 * W)], [gn_row], [F32, F32], 1, width=W, tc=HEAD_DIM,
                        name=f"gdn_gate_bwd_{tag}")
    cts = gdn_scan_bwd(*pre, states, do, name=f"gdn_scan_bwd_{tag}")
    dq, dk, dv, dab, dal, ddt = gdn_prep_bwd(qkv, proj, ab_col, al_row, dt_row, H, cts, name=f"gdn_prep_bwd_{tag}")
    dqkv = jnp.concatenate([dq, dk, dv], axis=1)
    dx, dconv = gdn_conv_bwd(proj, 3 * W, p["conv_w"], dqkv, name=f"gdn_conv_bwd_{tag}")
    grads = dict(conv_w=dconv, a_log=dal[0, :H], dt_bias=ddt[0, :H], o_norm=dgn.reshape(H, HEAD_DIM).sum(axis=0))
    return dx, dgate, dab, grads


def exchange(arrays, same_block, *, name):
    n = len(arrays)

    def body(*refs):
        copies = _exchange_copies(refs[:n], refs[n:2 * n], same_block, *refs[2 * n:])
        for cp in copies:
            cp.start()
        for cp in copies:
            cp.wait()

    shapes, sems = _exchange_shapes(arrays, same_block)
    any_spec = pl.BlockSpec(memory_space=pl.ANY)
    return pl.pallas_call(body, name=name, in_specs=[any_spec] * n, out_specs=[any_spec] * n, out_shape=shapes,
                          scratch_shapes=sems)(*arrays)


def _row_tile(R, row_bytes):
    for t in (512, 256, 128, 64, 32, 16, 8):
        if R % t == 0 and 2 * t * row_bytes <= 24 * 2 ** 20:
            return t
    return R


def adamw(parts, w, m, v, *, name):
    P, R, C = parts.shape
    tr = _row_tile(R, C * (P * parts.dtype.itemsize + 7 * 4))
    c1, c2 = 1.0 - ADAM_B1 ** ADAM_STEP, 1.0 - ADAM_B2 ** ADAM_STEP

    def body(p_ref, w_ref, m_ref, v_ref, g_ref, d_ref, nm_ref, nv_ref):
        g = p_ref[0].astype(F32)
        for s in range(1, P):
            g = g + p_ref[s].astype(F32)
        m = ADAM_B1 * m_ref[...] + (1.0 - ADAM_B1) * g
        v = ADAM_B2 * v_ref[...] + (1.0 - ADAM_B2) * (g * g)
        g_ref[...] = g
        nm_ref[...] = m
        nv_ref[...] = v
        d_ref[...] = -ADAM_LR * ((m / c1) / (jnp.sqrt(v / c2) + ADAM_EPS) + ADAM_WD * w_ref[...])

    blk = pl.BlockSpec((tr, C), lambda i: (i, 0))
    return pl.pallas_call(
        body, name=name, grid=(R // tr,),
        in_specs=[pl.BlockSpec((P, tr, C), lambda i: (0, i, 0)), blk, blk, blk],
        out_specs=[blk] * 4, out_shape=[jax.ShapeDtypeStruct((R, C), F32)] * 4,
        compiler_params=_cparams(("parallel",)),
    )(parts, w, m, v)


def sum_parts(parts, *, name):
    P, R, C = parts.shape
    tr = _row_tile(R, C * (P * parts.dtype.itemsize + 4))

    def body(p_ref, o_ref):
        g = p_ref[0].astype(F32)
        for s in range(1, P):
            g = g + p_ref[s].astype(F32)
        o_ref[...] = g

    return pl.pallas_call(
        body, name=name, grid=(R // tr,),
        in_specs=[pl.BlockSpec((P, tr, C), lambda i: (0, i, 0))], out_specs=pl.BlockSpec((tr, C), lambda i: (i, 0)),
        out_shape=jax.ShapeDtypeStruct((R, C), F32), compiler_params=_cparams(("parallel",)),
    )(parts)


PACK_COLS = 1024
PACK_ROWS = 256


def _pack(arrays, cols, lead=()):
    n_lead = len(lead)
    flat = [a.reshape(lead + (-1,)) for a in arrays]
    total = sum(f.shape[-1] for f in flat)
    rows = -(-total // cols)
    mult = PACK_ROWS if rows > PACK_ROWS else 8
    rows = -(-rows // mult) * mult
    pad = rows * cols - total
    if pad:
        flat.append(jnp.zeros(lead + (pad,), flat[0].dtype))
    return jnp.concatenate(flat, axis=n_lead).reshape(lead + (rows, cols))


def _unpack(buf, shapes, lead=()):
    flat = buf.reshape(lead + (-1,))
    out, off = [], 0
    for s in shapes:
        n = math.prod(s)
        out.append(lax.slice_in_dim(flat, off, off + n, axis=len(lead)).reshape(lead + tuple(s)))
        off += n
    return out


def _to_shards(full, axis):
    s = full.shape
    return jnp.moveaxis(full.reshape(s[:axis] + (N_DEV, s[axis] // N_DEV) + s[axis + 1:]), axis, 0)


def _from_shards(g, axis):
    m = jnp.moveaxis(g, 0, axis)
    s = m.shape
    return m.reshape(s[:axis] + (s[axis] * s[axis + 1],) + s[axis + 2:])


BIG = (("w_mem_kv", 0), ("w_out", 1), ("w_up", 2), ("w_down", 1), ("s5_w_in", 1), ("s5_w_glu", 1),
       ("gdn_w_in", 2), ("fox_w_in", 1))
SMALL_SHARDED = (("s5_d_skip", 1), ("s5_b_glu", 1), ("gdn_conv_w", 2))
REPLICATED = ("mem_norm", "norm1", "norm2", "norm_f", "s5_lam_re", "s5_lam_im", "s5_log_dt", "s5_b_re", "s5_b_im",
              "s5_c_re", "s5_c_im", "gdn_a_log", "gdn_dt_bias", "gdn_o_norm", "fox_b_f")
WEIGHTS = ("mem_norm", "w_mem_kv", "norm1", "w_out", "norm2", "w_up", "w_down", "norm_f", "s5_w_in", "s5_lam_re",
           "s5_lam_im", "s5_log_dt", "s5_b_re", "s5_b_im", "s5_c_re", "s5_c_im", "s5_d_skip", "s5_w_glu", "s5_b_glu",
           "gdn_w_in", "gdn_conv_w", "gdn_a_log", "gdn_dt_bias", "gdn_o_norm", "fox_w_in", "fox_b_f")


def _relu2(acc):
    r = jnp.maximum(acc, 0.0)
    return acc, r * r


def _relu2_grad(acc, u):
    return (acc * 2.0 * jnp.maximum(u, 0.0),)


def _add(acc, e):
    return (acc + e,)


def _permute_in(w, kind, W):
    if kind == 0:
        return w
    n_main = (4 if kind == 1 else 3) * W
    n_small = w.shape[1] - n_main - MEM_WIDTH
    small = jnp.pad(w[:, n_main:n_main + n_small], ((0, 0), (0, MEM_WIDTH - n_small)))
    return jnp.concatenate([w[:, :n_main], w[:, n_main + n_small:], small], axis=1)


def _unpermute_in(dw, kind, W, n_small):
    if kind == 0:
        return dw
    n_main = (4 if kind == 1 else 3) * W
    return jnp.concatenate([dw[:, :n_main], dw[:, n_main + MEM_WIDTH:n_main + MEM_WIDTH + n_small],
                            dw[:, n_main:n_main + MEM_WIDTH]], axis=1)


def kernel(x, mem, mem_norm, w_mem_kv, norm1, w_out, norm2, w_up, w_down, norm_f, s5_w_in, s5_lam_re, s5_lam_im, s5_log_dt, s5_b_re, s5_b_im, s5_c_re, s5_c_im, s5_d_skip, s5_w_glu, s5_b_glu, gdn_w_in, gdn_conv_w, gdn_a_log, gdn_dt_bias, gdn_o_norm, fox_w_in, fox_b_f, loss_target, m_mem_norm, m_w_mem_kv, m_norm1, m_w_out, m_norm2, m_w_up, m_w_down, m_norm_f, m_s5_w_in, m_s5_lam_re, m_s5_lam_im, m_s5_log_dt, m_s5_b_re, m_s5_b_im, m_s5_c_re, m_s5_c_im, m_s5_d_skip, m_s5_w_glu, m_s5_b_glu, m_gdn_w_in, m_gdn_conv_w, m_gdn_a_log, m_gdn_dt_bias, m_gdn_o_norm, m_fox_w_in, m_fox_b_f, v_mem_norm, v_w_mem_kv, v_norm1, v_w_out, v_norm2, v_w_up, v_w_down, v_norm_f, v_s5_w_in, v_s5_lam_re, v_s5_lam_im, v_s5_log_dt, v_s5_b_re, v_s5_b_im, v_s5_c_re, v_s5_c_im, v_s5_d_skip, v_s5_w_glu, v_s5_b_glu, v_gdn_w_in, v_gdn_conv_w, v_gdn_a_log, v_gdn_dt_bias, v_gdn_o_norm, v_fox_w_in, v_fox_b_f):
    args = dict(locals())
    wsh = {n: args[n] for n in WEIGHTS}
    msh = {n: args["m_" + n] for n in WEIGHTS}
    vsh = {n: args["v_" + n] for n in WEIGHTS}
    h0, memx, target = x[0], mem[0], loss_target[0]
    S, D = h0.shape
    W = D - MEM_WIDTH
    depth = norm1.shape[0]
    me = 4 * lax.axis_index("x") + 2 * lax.axis_index("y") + lax.axis_index("c")

    bf = lambda t: t.astype(BF16)
    rows2d = lambda g: g.reshape(-1, g.shape[-1])

    def layer_sends(i, group):
        kind, j = i % 3, i // 3
        if group == "up":
            return {"w_up": bf(wsh["w_up"][i])}
        if group == "down":
            return {"w_down": bf(wsh["w_down"][i])}
        d = {"w_out": bf(wsh["w_out"][i])}
        if kind == 0:
            d["w_in"], d["w_glu"] = bf(wsh["s5_w_in"][j]), bf(wsh["s5_w_glu"][j])
        elif kind == 1:
            d["w_in"] = bf(wsh["gdn_w_in"][j])
        else:
            d["w_in"] = bf(_permute_in(wsh["fox_w_in"][j], 2, W))
        return d

    def as_comm(d):
        return [(v, True) for v in d.values()]

    first = {**layer_sends(0, "in"), **layer_sends(0, "up"), **layer_sends(0, "down")}
    small_w = _pack([wsh[n] for n, _ in SMALL_SHARDED], 128)
    got = exchange(list(first.values()) + [bf(w_mem_kv), small_w], [True] * (len(first) + 2), name="gather_first")
    gathered = [dict(zip(first, got)) if i == 0 else {} for i in range(depth)]
    full = {n: wsh[n] for n in REPLICATED}
    full["w_mem_kv"] = rows2d(got[len(first)])
    for (n, ax), g in zip(SMALL_SHARDED, _unpack(got[-1], [wsh[n].shape for n, _ in SMALL_SHARDED], lead=(N_DEV,))):
        full[n] = _from_shards(g, ax)

    def layer_params(i):
        kind, j = i % 3, i // 3
        g = gathered[i]
        w = {n: rows2d(g[n]) for n in ("w_out", "w_down", "w_glu") if n in g}
        w["w_up"] = g["w_up"]
        w["w_in"] = _permute_in(_from_shards(g["w_in"], 1), 1, W) if kind == 1 else rows2d(g["w_in"])
        if kind == 0:
            p = {k: full["s5_" + k][j] for k in ("lam_re", "lam_im", "log_dt", "b_re", "b_im", "c_re", "c_im",
                                                   "d_skip", "b_glu")}
            return kind, j, p, w, 0
        if kind == 1:
            p = {k: full["gdn_" + k][j] for k in ("conv_w", "a_log", "dt_bias", "o_norm")}
            return kind, j, p, w, 2 * (W // HEAD_DIM)
        return kind, j, {"b_f": full["fox_b_f"][j]}, w, W // HEAD_DIM

    def hosted(i, group):
        if i + 1 >= depth:
            return {}, []
        d = layer_sends(i + 1, group)
        return {"comm": as_comm(d)}, list(d)

    def keep(i, names, res):
        if not names:
            return res
        outs, got = res
        gathered[i + 1].update(zip(names, got))
        return outs

    memn = rms_fwd(memx, full["mem_norm"], out_dtype=BF16, name="mem_rms")
    mkv = mm(memn, full["w_mem_kv"], name="mem_kv")
    h = h0
    saved = []
    weights = []
    for i in range(depth):
        kind, j, p, w, n_small = layer_params(i)
        weights.append((w, p, n_small))
        a = rms_fwd(h, full["norm1"][i], out_dtype=BF16, name=f"rms1_{i}")
        kw, names = hosted(i, "in")
        proj = keep(i, names, mm(a, w["w_in"], name=f"in_proj_{i}", **kw))
        if kind == 0:
            mix, ms = s5_mixer_fwd(proj, W, p, w["w_glu"], f"l{i}")
            mem_col = W
        elif kind == 1:
            mix, ms = gdn_mixer_fwd(proj, W, p, f"l{i}")
            mem_col = 4 * W
        else:
            mix, ms = fox_mixer_fwd(proj, 3 * W + MEM_WIDTH, p["b_f"], f"l{i}")
            mem_col = 3 * W
        read = mem_fwd(proj, mem_col, mkv, name=f"mem_fwd_{i}")
        cat = jnp.concatenate([mix, read], axis=1)
        h_mid = mm(cat, w["w_out"], extras=(h,), epilogue=_add, name=f"out_proj_{i}")
        a2 = rms_fwd(h_mid, full["norm2"][i], out_dtype=BF16, name=f"rms2_{i}")
        kw, names = hosted(i, "up")
        u, act = keep(i, names, mm(a2, w["w_up"], b_shard8=True, epilogue=_relu2, out_dtypes=(F32, BF16),
                                   name=f"up_{i}", **kw))
        kw, names = hosted(i, "down")
        h_next = keep(i, names, mm(act, w["w_down"], extras=(h_mid,), epilogue=_add, name=f"down_{i}", **kw))
        saved.append((h, a, proj, ms, mem_col, cat, h_mid, a2, u, act))
        h = h_next

    loss_row, dh, dnf = final_loss(h, full["norm_f"], target, name="final_loss")
    loss = lax.psum(jnp.sum(loss_row), ("x", "y", "c"))

    grads = {n: [None] * full[n].shape[0] for n in ("norm1", "norm2")}
    for pre, cnt in (("s5_", (depth + 2) // 3), ("gdn_", (depth + 1) // 3), ("fox_", depth // 3)):
        for n in REPLICATED + tuple(n for n, _ in SMALL_SHARDED):
            if n.startswith(pre):
                grads[n] = [None] * cnt
    shares = {}
    by_dest = lambda g: g.reshape((N_DEV, g.shape[0] // N_DEV) + g.shape[1:])
    pending_up, pending_rest = {}, {}

    def carry(pending):
        return {"comm": [(v, False) for v in pending.values()]} if pending else {}

    def landed(pending, res):
        if not pending:
            return res
        outs, got = res
        shares.update(zip(pending, got))
        return outs

    dmkv = None
    for i in reversed(range(depth)):
        kind, j = i % 3, i // 3
        w, p, n_small = weights[i]
        h_in, a, proj, ms, mem_col, cat, h_mid, a2, u, act = saved[i]
        du = landed(pending_up, mm(dh, w["w_down"], tb=True, extras=(u,), epilogue=_relu2_grad, out_dtypes=(BF16,),
                                   name=f"d_act_{i}", **carry(pending_up)))
        dw_down = landed(pending_rest, mm(act, dh, ta=True, out_dtypes=(BF16,), name=f"dw_down_{i}",
                                          **carry(pending_rest)))
        pending = {("w_down", i): by_dest(dw_down)}
        da2 = landed(pending, mm(du, w["w_up"], tb=True, b_shard8=True, name=f"d_a2_{i}", **carry(pending)))
        pending_up = {("w_up", i): mm(a2, du, ta=True, out_shard8=True, out_dtypes=(BF16,), name=f"dw_up_{i}")}
        dh_mid, dn2 = rms_bwd(h_mid, full["norm2"][i], da2, dh, name=f"rms2_bwd_{i}")
        grads["norm2"][i] = dn2[0]
        dcat = mm(dh_mid, w["w_out"], tb=True, name=f"d_cat_{i}")
        pending_rest = {("w_out", i): by_dest(mm(cat, dh_mid, ta=True, out_dtypes=(BF16,), name=f"dw_out_{i}"))}
        dq_mem, dmkv_i = mem_bwd(proj, mem_col, mkv, dcat, W, name=f"mem_bwd_{i}")
        dmkv = dmkv_i if dmkv is None else dmkv + dmkv_i
        if kind == 0:
            dmain, g = s5_mixer_bwd(proj, W, p, w["w_glu"], ms, dcat, f"l{i}")
            dproj = jnp.concatenate([dmain, dq_mem], axis=1)
            pending_rest[("s5_w_glu", j)] = by_dest(g.pop("w_glu"))
            for k, val in g.items():
                grads["s5_" + k][j] = val
        elif kind == 1:
            dx, dgate, dab, g = gdn_mixer_bwd(proj, W, p, ms, dcat, f"l{i}")
            dproj = jnp.concatenate([dx, dgate, dq_mem, jnp.pad(dab, ((0, 0), (0, MEM_WIDTH - 128)))], axis=1)
            for k, val in g.items():
                grads["gdn_" + k][j] = val
        else:
            dq, dk, dv, df, db = fox_mixer_bwd(proj, 3 * W + MEM_WIDTH, ms, dcat, f"l{i}")
            dproj = jnp.concatenate([dq, dk, dv, dq_mem, jnp.pad(df, ((0, 0), (0, MEM_WIDTH - 128)))], axis=1)
            grads["fox_b_f"][j] = db
        da = mm(dproj, w["w_in"], tb=True, name=f"d_a_{i}")
        in_name = ("s5_w_in", "gdn_w_in", "fox_w_in")[kind]
        if kind == 1:
            dw_in = _unpermute_in(mm(a, dproj, ta=True, name=f"dw_in_{i}"), 1, W, n_small)
            pending_rest[(in_name, j)] = bf(_to_shards(dw_in, 1))
        else:
            pending_rest[(in_name, j)] = by_dest(mm(a, dproj, ta=True, out_dtypes=(BF16,), name=f"dw_in_{i}"))
        dh, dn1 = rms_bwd(h_in, full["norm1"][i], da, dh_mid, name=f"rms1_bwd_{i}")
        grads["norm1"][i] = dn1[0]
    grad_x = dh[None]
    grads = {n: jnp.stack(v) for n, v in grads.items()}
    grads["norm_f"] = dnf[0]
    pending_rest[("w_mem_kv", None)] = by_dest(mm(memn, dmkv, ta=True, out_dtypes=(BF16,), name="dw_mem_kv"))
    dmemn = mm(dmkv, full["w_mem_kv"], tb=True, name="d_memn")
    _, dmn = rms_bwd(memx, full["mem_norm"], dmemn, None, name="mem_rms_bwd")
    grads["mem_norm"] = dmn[0]

    small_names = list(REPLICATED) + [n for n, _ in SMALL_SHARDED]
    last = {**pending_up, **pending_rest}
    got = exchange(list(last.values()) + [_pack([grads[n] for n in small_names], 128)],
                   [False] * len(last) + [True], name="exchange_last")
    shares.update(zip(last, got))

    big_out = [{}, {}, {}, {}]
    for n, _ in BIG:
        per_layer = []
        for idx in ([None] if n == "w_mem_kv" else range(wsh[n].shape[0])):
            key = (n, idx)
            local = [t[n] if idx is None else t[n][idx] for t in (wsh, msh, vsh)]
            part = shares[key]
            if n == "fox_w_in":
                total = sum_parts(part, name=f"sum_{n}_{idx}")
                part = _unpermute_in(total, 2, W, W // HEAD_DIM)[None]
            per_layer.append(adamw(part, *local, name=f"adamw_{n}_{idx}"))
        for o, vals in zip(big_out, zip(*per_layer)):
            o[n] = vals[0] if n == "w_mem_kv" else jnp.stack(vals)

    total = sum_parts(got[-1], name="sum_small_grads")
    gsmall = dict(zip(small_names, _unpack(total, [grads[n].shape for n in small_names])))
    for n, ax in SMALL_SHARDED:
        width = wsh[n].shape[ax]
        gsmall[n] = lax.dynamic_slice_in_dim(gsmall[n], me * width, width, axis=ax)
    outs = adamw(_pack([gsmall[n] for n in small_names], 128)[None],
                 *[_pack([t[n] for n in small_names], 128) for t in (wsh, msh, vsh)], name="adamw_small")
    small_out = [dict(zip(small_names, _unpack(o, [wsh[n].shape for n in small_names]))) for o in outs]

    res = [{**b, **s} for b, s in zip(big_out, small_out)]
    return (loss, grad_x, *[r[n] for r in res for n in WEIGHTS])
```

```python
import functools
import math

import jax
import jax.numpy as jnp
from jax import lax
from jax.experimental import pallas as pl
from jax.experimental.pallas import tpu as pltpu

F32 = jnp.float32
BF16 = jnp.bfloat16

HEAD_DIM = 128
MEM_HEADS = 4
MEM_WIDTH = MEM_HEADS * HEAD_DIM
S5_GROUP = 16
S5_STATE = 64
GDN_CHUNK = 64
EPS = 1e-6
ADAM_LR, ADAM_B1, ADAM_B2, ADAM_EPS, ADAM_WD, ADAM_STEP = 0.001, 0.9, 0.999, 1e-08, 0.01, 10

N_DEV = 8
MESH = pl.DeviceIdType.MESH
VMEM_LIMIT = 56 * 1024 * 1024
HI = lax.Precision.HIGHEST


def _tile(n, prefs=(1024, 512, 256, 128)):
    for t in prefs:
        if n % t == 0:
            return t
    return n


def _cparams(sem=None):
    return pltpu.CompilerParams(dimension_semantics=sem, vmem_limit_bytes=VMEM_LIMIT)


def _dot(a, b, ca=1, cb=0, precision=None):
    return lax.dot_general(a, b, (((ca,), (cb,)), ((), ())), preferred_element_type=F32, precision=precision)


def _bdot(a, b):
    return _dot(a.astype(BF16), b.astype(BF16))


def _exchange_copies(p_refs, out_refs, same_block, send_sems, recv_sems, local_sems):
    pos = (lax.axis_index("x"), lax.axis_index("y"), lax.axis_index("c"))
    me = 4 * pos[0] + 2 * pos[1] + pos[2]
    copies = []
    for n, (p_ref, out_ref, same) in enumerate(zip(p_refs, out_refs, same_block)):
        block = (lambda j, r=p_ref: r) if same else (lambda j, r=p_ref: r.at[j])
        copies.append(pltpu.make_async_copy(block(me), out_ref.at[me], local_sems.at[n]))
        for m in range(1, N_DEV):
            peer = tuple(1 - v if (m >> (2 - b)) & 1 else v for b, v in enumerate(pos))
            sem = n * (N_DEV - 1) + m - 1
            copies.append(pltpu.make_async_remote_copy(
                src_ref=block(4 * peer[0] + 2 * peer[1] + peer[2]), dst_ref=out_ref.at[me],
                send_sem=send_sems.at[sem], recv_sem=recv_sems.at[sem], device_id=peer, device_id_type=MESH))
    return copies


def _exchange_shapes(arrays, same_block):
    shapes = [jax.ShapeDtypeStruct((N_DEV,) + tuple(p.shape if same else p.shape[1:]), p.dtype)
              for p, same in zip(arrays, same_block)]
    n = len(arrays)
    sems = [pltpu.SemaphoreType.DMA((n * (N_DEV - 1),)), pltpu.SemaphoreType.DMA((n * (N_DEV - 1),)),
            pltpu.SemaphoreType.DMA((n,))]
    return shapes, sems


def _mm_tk(K, tm, tn, a_bytes, b_bytes, out_bytes):
    for tk in (2048, 1024, 512, 256, 128):
        if K % tk == 0 and 2 * tk * (tm * a_bytes + tn * b_bytes) + (2 * out_bytes + 4) * tm * tn <= 40 * 2 ** 20:
            return tk
    return K


def mm(a, b, *, ta=False, tb=False, extras=(), epilogue=None, out_dtypes=(F32,), name, b_shard8=False,
       out_shard8=False, comm=()):
    M, K = (a.shape[1], a.shape[0]) if ta else a.shape
    if b_shard8:
        brows, bcols = b.shape[1], b.shape[2] * N_DEV
    else:
        brows, bcols = b.shape
    N = brows if tb else bcols
    assert K == (bcols if tb else brows), (a.shape, b.shape, ta, tb)
    tm = _tile(M)
    tn = _tile(N // N_DEV if (out_shard8 or (b_shard8 and not tb)) else N)
    out_bytes = sum(jnp.dtype(d).itemsize for d in out_dtypes) + 4 * sum(e.shape[0] != 1 for e in extras)
    tk = _mm_tk(K // N_DEV if (b_shard8 and tb) else K, tm, tn, a.dtype.itemsize, b.dtype.itemsize, out_bytes)
    ni, nj, nk = M // tm, N // tn, K // tk
    n_ex, n_out, n_comm = len(extras), len(out_dtypes), len(comm)
    ca, cb = (0 if ta else 1), (1 if tb else 0)
    same_block = [s for _, s in comm]

    def body(a_ref, b_ref, *rest):
        ex_refs, rest = rest[:n_ex], rest[n_ex:]
        cin, rest = rest[:n_comm], rest[n_comm:]
        out_refs, rest = rest[:n_out], rest[n_out:]
        cout, rest = rest[:n_comm], rest[n_comm:]
        acc = rest[0]
        i, j, k = pl.program_id(0), pl.program_id(1), pl.program_id(2)
        if n_comm:
            @pl.when((i == 0) & (j == 0) & (k == 0))
            def _():
                for cp in _exchange_copies(cin, cout, same_block, *rest[1:]):
                    cp.start()

        def finish(res):
            outs = (res,) if epilogue is None else epilogue(res, *[e[...] for e in ex_refs])
            for o, v in zip(out_refs, outs):
                o[...] = v.astype(o.dtype)

        part = _dot(a_ref[...].astype(BF16), b_ref[...].astype(BF16), ca, cb)
        if nk == 1:
            finish(part)
        else:
            @pl.when(k == 0)
            def _():
                acc[...] = part

            @pl.when(k > 0)
            def _():
                acc[...] += part

            @pl.when(k == nk - 1)
            def _():
                finish(acc[...])

        if n_comm:
            @pl.when((i == ni - 1) & (j == nj - 1) & (k == nk - 1))
            def _():
                for cp in _exchange_copies(cin, cout, same_block, *rest[1:]):
                    cp.wait()

    a_spec = pl.BlockSpec((tk, tm), lambda i, j, k: (k, i)) if ta else pl.BlockSpec((tm, tk), lambda i, j, k: (i, k))
    if b_shard8 and tb:
        kper = bcols // N_DEV // tk
        b_spec = pl.BlockSpec((None, tn, tk), lambda i, j, k: (k // kper, j, k % kper))
    elif b_shard8:
        nper = bcols // N_DEV // tn
        b_spec = pl.BlockSpec((None, tk, tn), lambda i, j, k: (j // nper, k, j % nper))
    elif tb:
        b_spec = pl.BlockSpec((tn, tk), lambda i, j, k: (j, k))
    else:
        b_spec = pl.BlockSpec((tk, tn), lambda i, j, k: (k, j))
    ex_specs = [pl.BlockSpec((1, tn), lambda i, j, k: (0, j)) if e.shape[0] == 1 and M != 1
                else pl.BlockSpec((tm, tn), lambda i, j, k: (i, j)) for e in extras]
    if out_shard8:
        nper = N // N_DEV // tn
        out_spec = pl.BlockSpec((None, tm, tn), lambda i, j, k: (j // nper, i, j % nper))
        out_shape = (N_DEV, M, N // N_DEV)
    else:
        out_spec = pl.BlockSpec((tm, tn), lambda i, j, k: (i, j))
        out_shape = (M, N)
    comm_shapes, sems = _exchange_shapes([p for p, _ in comm], same_block) if n_comm else ([], [])
    any_spec = pl.BlockSpec(memory_space=pl.ANY)
    outs = pl.pallas_call(
        body, name=name, grid=(ni, nj, nk),
        in_specs=[a_spec, b_spec] + ex_specs + [any_spec] * n_comm,
        out_specs=[out_spec] * n_out + [any_spec] * n_comm,
        out_shape=[jax.ShapeDtypeStruct(out_shape, d) for d in out_dtypes] + comm_shapes,
        scratch_shapes=[pltpu.VMEM((tm, tn) if nk > 1 else (8, 128), F32)] + sems,
        compiler_params=_cparams(("arbitrary",) * 3 if n_comm else ("parallel", "parallel", "arbitrary")),
    )(a, b, *extras, *[p for p, _ in comm])
    if n_comm:
        return (outs[0] if n_out == 1 else tuple(outs[:n_out])), list(outs[n_out:])
    return outs[0] if n_out == 1 else outs


def rms_fwd(x, g, *, out_dtype, name):
    S, D = x.shape
    ts = _tile(S, (512, 256, 128))

    def body(x_ref, g_ref, y_ref):
        x = x_ref[...]
        r = lax.rsqrt(jnp.mean(x * x, axis=-1, keepdims=True) + EPS)
        y_ref[...] = (x * r * g_ref[...]).astype(y_ref.dtype)

    return pl.pallas_call(
        body, name=name, grid=(S // ts,),
        in_specs=[pl.BlockSpec((ts, D), lambda i: (i, 0)), pl.BlockSpec((1, D), lambda i: (0, 0))],
        out_specs=pl.BlockSpec((ts, D), lambda i: (i, 0)),
        out_shape=jax.ShapeDtypeStruct((S, D), out_dtype),
        compiler_params=_cparams(("parallel",)),
    )(x, g.reshape(1, D))


def rms_bwd(x, g, dy, dres, *, name):
    S, D = x.shape
    ts = _tile(S, (512, 256, 128))
    has_res = dres is not None

    def body(x_ref, g_ref, dy_ref, *rest):
        dx_ref, dg_ref = rest[-2:]
        x = x_ref[...]
        r = lax.rsqrt(jnp.mean(x * x, axis=-1, keepdims=True) + EPS)
        xh = x * r
        dy = dy_ref[...].astype(F32)
        dxh = dy * g_ref[...]
        dx = r * (dxh - xh * jnp.mean(dxh * xh, axis=-1, keepdims=True))
        if has_res:
            dx = dx + rest[0][...]
        dx_ref[...] = dx

        @pl.when(pl.program_id(0) == 0)
        def _():
            dg_ref[...] = jnp.zeros_like(dg_ref)

        dg_ref[...] += jnp.sum(dy * xh, axis=0, keepdims=True)

    tok = pl.BlockSpec((ts, D), lambda i: (i, 0))
    row = pl.BlockSpec((1, D), lambda i: (0, 0))
    return pl.pallas_call(
        body, name=name, grid=(S // ts,),
        in_specs=[tok, row, tok] + ([tok] if has_res else []),
        out_specs=[tok, row],
        out_shape=[jax.ShapeDtypeStruct((S, D), F32), jax.ShapeDtypeStruct((1, D), F32)],
        compiler_params=_cparams(("arbitrary",)),
    )(x, g.reshape(1, D), dy, *([dres] if has_res else []))


def final_loss(h, g, target, *, name):
    S, D = h.shape
    ts = _tile(S, (512, 256, 128))

    def body(x_ref, g_ref, t_ref, loss_ref, dx_ref, dg_ref):
        x = x_ref[...]
        r = lax.rsqrt(jnp.mean(x * x, axis=-1, keepdims=True) + EPS)
        xh = x * r
        err = xh * g_ref[...] - t_ref[...]
        dy = err * (1.0 / D)
        dxh = dy * g_ref[...]
        dx_ref[...] = r * (dxh - xh * jnp.mean(dxh * xh, axis=-1, keepdims=True))

        @pl.when(pl.program_id(0) == 0)
        def _():
            dg_ref[...] = jnp.zeros_like(dg_ref)
            loss_ref[...] = jnp.zeros_like(loss_ref)

        dg_ref[...] += jnp.sum(dy * xh, axis=0, keepdims=True)
        loss_ref[...] += jnp.sum(err * err, axis=0, keepdims=True) * (0.5 / D)

    tok = pl.BlockSpec((ts, D), lambda i: (i, 0))
    row = pl.BlockSpec((1, D), lambda i: (0, 0))
    return pl.pallas_call(
        body, name=name, grid=(S // ts,),
        in_specs=[tok, row, tok], out_specs=[row, tok, row],
        out_shape=[jax.ShapeDtypeStruct((1, D), F32), jax.ShapeDtypeStruct((S, D), F32),
                   jax.ShapeDtypeStruct((1, D), F32)],
        compiler_params=_cparams(("arbitrary",)),
    )(h, g.reshape(1, D), target)


def _mem_probs(q, k):
    s = _dot(q.astype(BF16), k.astype(BF16), 1, 1) * HEAD_DIM ** -0.5
    p = jnp.exp(s - jnp.max(s, axis=-1, keepdims=True))
    return p / jnp.sum(p, axis=-1, keepdims=True)


def mem_fwd(proj, q_col, mkv, *, name):
    S = proj.shape[0]
    L = mkv.shape[0]
    ts = _tile(S)

    def body(q_ref, kv_ref, o_ref):
        for h in range(MEM_HEADS):
            c = slice(h * HEAD_DIM, (h + 1) * HEAD_DIM)
            v = kv_ref[:, MEM_WIDTH + h * HEAD_DIM:MEM_WIDTH + (h + 1) * HEAD_DIM]
            p = _mem_probs(q_ref[:, c], kv_ref[:, c])
            o_ref[:, c] = _bdot(p, v).astype(o_ref.dtype)

    return pl.pallas_call(
        body, name=name, grid=(S // ts,),
        in_specs=[pl.BlockSpec((ts, MEM_WIDTH), lambda i: (i, q_col // MEM_WIDTH)),
                  pl.BlockSpec((L, 2 * MEM_WIDTH), lambda i: (0, 0))],
        out_specs=pl.BlockSpec((ts, MEM_WIDTH), lambda i: (i, 0)),
        out_shape=jax.ShapeDtypeStruct((S, MEM_WIDTH), BF16),
        compiler_params=_cparams(("parallel",)),
    )(proj, mkv)


def mem_bwd(proj, q_col, mkv, dcat, do_col, *, name):
    S = proj.shape[0]
    L = mkv.shape[0]
    ts = _tile(S)
    scale = HEAD_DIM ** -0.5

    def body(q_ref, kv_ref, do_ref, dq_ref, dkv_ref):
        @pl.when(pl.program_id(0) == 0)
        def _():
            dkv_ref[...] = jnp.zeros_like(dkv_ref)

        for h in range(MEM_HEADS):
            c = slice(h * HEAD_DIM, (h + 1) * HEAD_DIM)
            cv = slice(MEM_WIDTH + h * HEAD_DIM, MEM_WIDTH + (h + 1) * HEAD_DIM)
            q, k, v = q_ref[:, c].astype(BF16), kv_ref[:, c].astype(BF16), kv_ref[:, cv].astype(BF16)
            do = do_ref[:, c].astype(BF16)
            p = _mem_probs(q, k)
            dkv_ref[:, cv] += _dot(p.astype(BF16), do, 0, 0)
            dp = _dot(do, v, 1, 1)
            ds = (p * (dp - jnp.sum(dp * p, axis=-1, keepdims=True)) * scale).astype(BF16)
            dq_ref[:, c] = _dot(ds, k).astype(dq_ref.dtype)
            dkv_ref[:, c] += _dot(ds, q, 0, 0)

    return pl.pallas_call(
        body, name=name, grid=(S // ts,),
        in_specs=[pl.BlockSpec((ts, MEM_WIDTH), lambda i: (i, q_col // MEM_WIDTH)),
                  pl.BlockSpec((L, 2 * MEM_WIDTH), lambda i: (0, 0)),
                  pl.BlockSpec((ts, MEM_WIDTH), lambda i: (i, do_col // MEM_WIDTH))],
        out_specs=[pl.BlockSpec((ts, MEM_WIDTH), lambda i: (i, 0)),
                   pl.BlockSpec((L, 2 * MEM_WIDTH), lambda i: (0, 0))],
        out_shape=[jax.ShapeDtypeStruct((S, MEM_WIDTH), BF16), jax.ShapeDtypeStruct((L, 2 * MEM_WIDTH), F32)],
        compiler_params=_cparams(("arbitrary",)),
    )(proj, mkv, dcat)


def _lower_ones(n, strict=False):
    r = lax.broadcasted_iota(jnp.int32, (n, n), 0)
    c = lax.broadcasted_iota(jnp.int32, (n, n), 1)
    return (r > c if strict else r >= c).astype(F32)


def fox_gate_fwd(proj, f_col, b_f_row, *, name):
    S = proj.shape[0]
    tb = _tile(S, (256, 128))

    def body(f_ref, b_ref, c_ref, carry):
        @pl.when(pl.program_id(0) == 0)
        def _():
            carry[...] = jnp.zeros_like(carry)

        ls = jax.nn.log_sigmoid(f_ref[...] + b_ref[...])
        cum = _dot(_lower_ones(tb), ls, precision=HI) + carry[...]
        c_ref[...] = cum
        carry[...] = cum[tb - 1:tb, :]

    return pl.pallas_call(
        body, name=name, grid=(S // tb,),
        in_specs=[pl.BlockSpec((tb, 128), lambda i: (i, f_col // 128)), pl.BlockSpec((1, 128), lambda i: (0, 0))],
        out_specs=pl.BlockSpec((tb, 128), lambda i: (i, 0)),
        out_shape=jax.ShapeDtypeStruct((S, 128), F32),
        scratch_shapes=[pltpu.VMEM((1, 128), F32)],
        compiler_params=_cparams(("arbitrary",)),
    )(proj, b_f_row)


def fox_gate_bwd(dcf, proj, f_col, b_f_row, *, name):
    S = proj.shape[0]
    tb = _tile(S, (256, 128))
    nb = S // tb

    def body(d_ref, f_ref, b_ref, df_ref, db_ref, carry):
        @pl.when(pl.program_id(0) == 0)
        def _():
            carry[...] = jnp.zeros_like(carry)
            db_ref[...] = jnp.zeros_like(db_ref)

        upper = _lower_ones(tb).T
        rc = _dot(upper, d_ref[...], precision=HI) + carry[...]
        carry[...] = rc[0:1, :]
        df = rc * jax.nn.sigmoid(-(f_ref[...] + b_ref[...]))
        df_ref[...] = df
        db_ref[...] += jnp.sum(df, axis=0, keepdims=True)

    return pl.pallas_call(
        body, name=name, grid=(nb,),
        in_specs=[pl.BlockSpec((tb, 128), lambda i: (nb - 1 - i, 0)),
                  pl.BlockSpec((tb, 128), lambda i: (nb - 1 - i, f_col // 128)),
                  pl.BlockSpec((1, 128), lambda i: (0, 0))],
        out_specs=[pl.BlockSpec((tb, 128), lambda i: (nb - 1 - i, 0)), pl.BlockSpec((1, 128), lambda i: (0, 0))],
        out_shape=[jax.ShapeDtypeStruct((S, 128), F32), jax.ShapeDtypeStruct((1, 128), F32)],
        scratch_shapes=[pltpu.VMEM((1, 128), F32)],
        compiler_params=_cparams(("arbitrary",)),
    )(dcf, proj, b_f_row)


def _fox_block(S):
    return _tile(S, (512, 256, 128)) if S > 512 else S // 2


def _causal(s, qi, kj, bq):
    r = qi * bq + lax.broadcasted_iota(jnp.int32, s.shape, 0)
    c = kj * bq + lax.broadcasted_iota(jnp.int32, s.shape, 1)
    return jnp.where(r >= c, s, -jnp.inf)


def fox_fwd(proj, n_heads, cf_col, cf_row, *, name):
    S = proj.shape[0]
    H = n_heads
    bq = _fox_block(S)
    scale = HEAD_DIM ** -0.5

    def body(q_ref, k_ref, v_ref, cc_ref, cr_ref, o_ref, st_ref):
        qi = pl.program_id(1)
        q = q_ref[...].astype(BF16)
        cq = cc_ref[...]

        def step(j, carry, diagonal=False):
            m, l, acc = carry
            rows = pl.ds(pl.multiple_of(j * bq, bq), bq)
            k = k_ref[rows, :].astype(BF16)
            v = v_ref[rows, :].astype(BF16)
            s = _dot(q, k, 1, 1) * scale + cq - cr_ref[j]
            if diagonal:
                s = _causal(s, qi, j, bq)
            m2 = jnp.maximum(m, jnp.max(s, axis=-1, keepdims=True))
            p = jnp.exp(s - m2)
            a = jnp.exp(m - m2)
            return m2, a * l + jnp.sum(p, axis=-1, keepdims=True), a * acc + _dot(p.astype(BF16), v)

        init = (jnp.full((bq, 1), -jnp.inf, F32), jnp.zeros((bq, 1), F32), jnp.zeros((bq, HEAD_DIM), F32))
        m, l, acc = step(qi, lax.fori_loop(0, qi, step, init), diagonal=True)
        o_ref[...] = (acc / l).astype(o_ref.dtype)
        lane = lax.broadcasted_iota(jnp.int32, (bq, 128), 1)
        st_ref[...] = jnp.where(lane == 0, m + jnp.log(l), jnp.where(lane == 1, cq, 0.0))

    return pl.pallas_call(
        body, name=name, grid=(H, S // bq),
        in_specs=[pl.BlockSpec((bq, HEAD_DIM), lambda h, i: (i, h)),
                  pl.BlockSpec((S, HEAD_DIM), lambda h, i: (0, H + h)),
                  pl.BlockSpec((S, HEAD_DIM), lambda h, i: (0, 2 * H + h)),
                  pl.BlockSpec((None, bq, 1), lambda h, i: (h, i, 0)),
                  pl.BlockSpec((None, S // bq, 1, bq), lambda h, i: (h, 0, 0, 0))],
        out_specs=[pl.BlockSpec((bq, HEAD_DIM), lambda h, i: (i, h)),
                   pl.BlockSpec((None, bq, 128), lambda h, i: (h, i, 0))],
        out_shape=[jax.ShapeDtypeStruct((S, H * HEAD_DIM), BF16), jax.ShapeDtypeStruct((H, S, 128), F32)],
        compiler_params=_cparams(("parallel", "parallel")),
    )(proj, proj, proj, cf_col, cf_row)


def fox_delta(stats, o, dcat, *, name):
    H, S, _ = stats.shape
    ts = _tile(S)

    def body(st_ref, o_ref, do_ref, out_ref):
        d = jnp.sum(o_ref[...].astype(F32) * do_ref[...].astype(F32), axis=-1, keepdims=True)
        lane = lax.broadcasted_iota(jnp.int32, (ts, 128), 1)
        out_ref[...] = jnp.where(lane == 2, d, st_ref[...])

    return pl.pallas_call(
        body, name=name, grid=(H, S // ts),
        in_specs=[pl.BlockSpec((None, ts, 128), lambda h, i: (h, i, 0)),
                  pl.BlockSpec((ts, HEAD_DIM), lambda h, i: (i, h)),
                  pl.BlockSpec((ts, HEAD_DIM), lambda h, i: (i, h))],
        out_specs=pl.BlockSpec((None, ts, 128), lambda h, i: (h, i, 0)),
        out_shape=jax.ShapeDtypeStruct((H, S, 128), F32),
        compiler_params=_cparams(("parallel", "parallel")),
    )(stats, o, dcat)


def fox_bwd(proj, n_heads, stats, cf_row, dcat, *, name):
    S = proj.shape[0]
    H = n_heads
    bq = _fox_block(S)
    nq = S // bq
    scale = HEAD_DIM ** -0.5

    def body(q_ref, k_ref, v_ref, do_ref, st_ref, cr_ref, dq_ref, dk_ref, dv_ref, dst_ref, dcr_ref):
        kj = pl.program_id(1)

        @pl.when(kj == 0)
        def _():
            dq_ref[...] = jnp.zeros_like(dq_ref)
            dst_ref[...] = jnp.zeros_like(dst_ref)

        k = k_ref[...].astype(BF16)
        v = v_ref[...].astype(BF16)
        ck = cr_ref[...]
        lane = lax.broadcasted_iota(jnp.int32, (bq, 128), 1)

        def step(qi, carry, diagonal=False):
            dk, dv, dck = carry
            rows = pl.ds(pl.multiple_of(qi * bq, bq), bq)
            q = q_ref[rows, :].astype(BF16)
            do = do_ref[rows, :].astype(BF16)
            st = st_ref[rows, :]
            lse, cq, delta = st[:, 0:1], st[:, 1:2], st[:, 2:3]
            s = _dot(q, k, 1, 1) * scale + cq - ck
            if diagonal:
                s = _causal(s, qi, kj, bq)
            p = jnp.exp(s - lse)
            dv = dv + _dot(p.astype(BF16), do, 0, 0)
            ds = p * (_dot(do, v, 1, 1) - delta)
            dsb = (ds * scale).astype(BF16)
            dq_ref[rows, :] += _dot(dsb, k)
            dst_ref[rows, :] += jnp.where(lane == 0, jnp.sum(ds, axis=-1, keepdims=True), 0.0)
            return dk + _dot(dsb, q, 0, 0), dv, dck - jnp.sum(ds, axis=0, keepdims=True)

        init = (jnp.zeros((bq, HEAD_DIM), F32), jnp.zeros((bq, HEAD_DIM), F32), jnp.zeros((1, bq), F32))
        dk, dv, dck = lax.fori_loop(kj + 1, nq, step, step(kj, init, diagonal=True))
        dk_ref[...] = dk
        dv_ref[...] = dv
        dcr_ref[...] = dck

    W = H * HEAD_DIM
    return pl.pallas_call(
        body, name=name, grid=(H, nq),
        in_specs=[pl.BlockSpec((S, HEAD_DIM), lambda h, j: (0, h)),
                  pl.BlockSpec((bq, HEAD_DIM), lambda h, j: (j, H + h)),
                  pl.BlockSpec((bq, HEAD_DIM), lambda h, j: (j, 2 * H + h)),
                  pl.BlockSpec((S, HEAD_DIM), lambda h, j: (0, h)),
                  pl.BlockSpec((None, S, 128), lambda h, j: (h, 0, 0)),
                  pl.BlockSpec((None, None, 1, bq), lambda h, j: (h, j, 0, 0))],
        out_specs=[pl.BlockSpec((S, HEAD_DIM), lambda h, j: (0, h)),
                   pl.BlockSpec((bq, HEAD_DIM), lambda h, j: (j, h)),
                   pl.BlockSpec((bq, HEAD_DIM), lambda h, j: (j, h)),
                   pl.BlockSpec((None, S, 128), lambda h, j: (h, 0, 0)),
                   pl.BlockSpec((None, None, 1, bq), lambda h, j: (h, j, 0, 0))],
        out_shape=[jax.ShapeDtypeStruct((S, W), F32), jax.ShapeDtypeStruct((S, W), F32),
                   jax.ShapeDtypeStruct((S, W), F32), jax.ShapeDtypeStruct((H, S, 128), F32),
                   jax.ShapeDtypeStruct((H, nq, 1, bq), F32)],
        compiler_params=_cparams(("parallel", "arbitrary")),
    )(proj, proj, proj, dcat, stats, cf_row)


def _pad_row(v, n=128):
    return jnp.pad(v.astype(F32), (0, n - v.shape[0])).reshape(1, n)


def fox_mixer_fwd(proj, f_col, b_f, tag):
    S = proj.shape[0]
    H = b_f.shape[0]
    bq = _fox_block(S)
    b_row = _pad_row(b_f)
    cf = fox_gate_fwd(proj, f_col, b_row, name=f"fox_gate_fwd_{tag}")
    cf_t = cf[:, :H].T
    o, stats = fox_fwd(proj, H, cf_t.reshape(H, S, 1), cf_t.reshape(H, S // bq, 1, bq), name=f"fox_fwd_{tag}")
    return o, (o, stats, cf_t, b_row)


def fox_mixer_bwd(proj, f_col, saved, dcat, tag):
    o, stats, cf_t, b_row = saved
    H, S = cf_t.shape
    bq = _fox_block(S)
    stats = fox_delta(stats, o, dcat, name=f"fox_delta_{tag}")
    dq, dk, dv, dst, dcr = fox_bwd(proj, H, stats, cf_t.reshape(H, S // bq, 1, bq), dcat, name=f"fox_bwd_{tag}")
    dcf = dst[:, :, 0] + dcr.reshape(H, S)
    dcf = jnp.pad(dcf.T, ((0, 0), (0, 128 - H)))
    df, db = fox_gate_bwd(dcf, proj, f_col, b_row, name=f"fox_gate_bwd_{tag}")
    return dq, dk, dv, df, db[0, :H]


def ew(fn, tok_ins, row_ins, tok_out_dtypes, n_row_out, *, width, name, tc=None, ts=None):
    S = tok_ins[0][0].shape[0]
    tc = tc or _tile(width, (512, 256, 128))
    ts = ts or _tile(S, (512, 256, 128))
    n_tok, n_row, n_to = len(tok_ins), len(row_ins), len(tok_out_dtypes)
    for _, col in tok_ins:
        assert col % tc == 0

    def body(*refs):
        ins = [r[...] for r in refs[:n_tok + n_row]]
        outs = fn(*ins)
        outs = outs if isinstance(outs, (tuple, list)) else (outs,)
        o_refs = refs[n_tok + n_row:]
        for o, v in zip(o_refs[:n_to], outs[:n_to]):
            o[...] = v.astype(o.dtype)
        if n_row_out:
            @pl.when(pl.program_id(1) == 0)
            def _():
                for o in o_refs[n_to:]:
                    o[...] = jnp.zeros_like(o)

            for o, v in zip(o_refs[n_to:], outs[n_to:]):
                o[...] += jnp.sum(v, axis=0, keepdims=True)

    def tok_spec(col):
        return pl.BlockSpec((ts, tc), lambda j, i: (i, col // tc + j))

    row_spec = pl.BlockSpec((1, tc), lambda j, i: (0, j))
    res = pl.pallas_call(
        body, name=name, grid=(width // tc, S // ts),
        in_specs=[tok_spec(col) for _, col in tok_ins] + [row_spec] * n_row,
        out_specs=[tok_spec(0)] * n_to + [row_spec] * n_row_out,
        out_shape=[jax.ShapeDtypeStruct((S, width), d) for d in tok_out_dtypes]
        + [jax.ShapeDtypeStruct((1, width), F32)] * n_row_out,
        compiler_params=_cparams(("parallel", "arbitrary" if n_row_out else "parallel")),
    )(*[a for a, _ in tok_ins], *row_ins)
    return res[0] if len(res) == 1 else res


S5_SUPER = 128 // S5_GROUP
S5_COLS = S5_SUPER * S5_STATE


def _shift_rows(x, d, up=False):
    T = x.shape[0]
    if d % 8 == 0:
        z = jnp.zeros((d, x.shape[1]), x.dtype)
        return jnp.concatenate([x[d:], z], axis=0) if up else jnp.concatenate([z, x[:T - d]], axis=0)
    row = lax.broadcasted_iota(jnp.int32, x.shape, 0)
    if up:
        return jnp.where(row < T - d, pltpu.roll(x, T - d, axis=0), 0.0)
    return jnp.where(row >= d, pltpu.roll(x, d, axis=0), 0.0)


def _s5_scan(bre, bim, tab, hre_s, him_s, carry, up):
    T, N = bre.shape
    G = T // 8
    hre, him = bre.reshape(G, 8, N), bim.reshape(G, 8, N)
    for k in range(3):
        shift = (8 - (1 << k)) if up else (1 << k)
        sre, sim = pltpu.roll(hre, shift, axis=1), pltpu.roll(him, shift, axis=1)
        cre, cim = tab[0, k], tab[1, k]
        hre, him = hre + cre * sre - cim * sim, him + cre * sim + cim * sre
    hre_s[...] = hre
    him_s[...] = him
    are, aim = tab[0, 3], tab[1, 3]
    edge = slice(0, 1) if up else slice(7, 8)

    def tile(i, c):
        g = G - 1 - i if up else i
        r = hre_s[g] + are * c[0] - aim * c[1]
        m = him_s[g] + are * c[1] + aim * c[0]
        hre_s[g] = r
        him_s[g] = m
        return r[edge], m[edge]

    carry = lax.fori_loop(0, G, tile, carry)
    return hre_s[...].reshape(T, N), him_s[...].reshape(T, N), carry


def _s5_states(u, bbre, bbim, tab, hre_s, him_s, start):
    return _s5_scan(_bdot(u, bbre), _bdot(u, bbim), tab, hre_s, him_s, (start[0:1], start[1:2]), False)


def _s5_tables(a_re, a_im, NJ):
    are, aim = a_re.reshape(NJ, 1, S5_COLS), a_im.reshape(NJ, 1, S5_COLS)
    pows = [(are, aim)]
    for _ in range(7):
        r, i = pows[-1]
        pows.append((r * are - i * aim, r * aim + i * are))
    row = jnp.arange(8).reshape(1, 8, 1)
    tabs = []
    for up in (False, True):
        re, im = [], []
        for d in (1, 2, 4):
            keep = (row <= 7 - d) if up else (row >= d)
            re.append(jnp.where(keep, pows[d - 1][0], 0.0))
            im.append(jnp.where(keep, -pows[d - 1][1] if up else pows[d - 1][1], 0.0))
        order = range(7, -1, -1) if up else range(8)
        re.append(jnp.concatenate([pows[n][0] for n in order], axis=1))
        im.append(jnp.concatenate([-pows[n][1] if up else pows[n][1] for n in order], axis=1))
        tabs.append(jnp.stack([jnp.stack(re, axis=1), jnp.stack(im, axis=1)], axis=1))
    return tabs


def _s5_block(S):
    return _tile(S, (256, 128))


def _s5_specs(T):
    mat = lambda r, c: pl.BlockSpec((None, r, c), lambda j, t: (j, 0, 0))
    tab = pl.BlockSpec((None, 2, 4, 8, S5_COLS), lambda j, t: (j, 0, 0, 0, 0))
    scratch = pltpu.VMEM((T // 8, 8, S5_COLS), F32)
    return mat, tab, scratch


def s5_scan_fwd(proj, W, bbre, bbim, cre, cim, tab_f, tab_r, *, name):
    S = proj.shape[0]
    T = _s5_block(S)
    NJ, nblk = W // 128, S // T

    def body(u_ref, bbre_ref, bbim_ref, cre_ref, cim_ref, tab_ref, y_ref, st_ref, carry, hre_s, him_s):
        @pl.when(pl.program_id(1) == 0)
        def _():
            carry[...] = jnp.zeros_like(carry)

        st_ref[...] = carry[...]
        hre, him, (lre, lim) = _s5_states(u_ref[...], bbre_ref[...], bbim_ref[...], tab_ref[...], hre_s, him_s,
                                          carry[...])
        carry[0:1, :] = lre
        carry[1:2, :] = lim
        y_ref[...] = _bdot(hre, cre_ref[...]) - _bdot(him, cim_ref[...])

    mat, tab, scratch = _s5_specs(T)
    return pl.pallas_call(
        body, name=name, grid=(NJ, nblk),
        in_specs=[pl.BlockSpec((T, 128), lambda j, t: (t, j)), mat(128, S5_COLS), mat(128, S5_COLS),
                  mat(S5_COLS, 128), mat(S5_COLS, 128), tab],
        out_specs=[pl.BlockSpec((T, 128), lambda j, t: (t, j)),
                   pl.BlockSpec((None, None, 2, S5_COLS), lambda j, t: (j, t, 0, 0))],
        out_shape=[jax.ShapeDtypeStruct((S, W), F32), jax.ShapeDtypeStruct((NJ, nblk, 2, S5_COLS), F32)],
        scratch_shapes=[pltpu.VMEM((2, S5_COLS), F32), scratch, scratch],
        compiler_params=_cparams(("parallel", "arbitrary")),
    )(proj, bbre, bbim, cre, cim, tab_f)


def s5_scan_bwd(proj, W, dy, d_skip_row, bbre, bbim, cre, cim, tab_f, tab_r, starts, *, name):
    S = proj.shape[0]
    T = _s5_block(S)
    NJ, nblk = W // 128, S // T

    def body(u_ref, dy_ref, d_ref, bbre_ref, bbim_ref, cre_ref, cim_ref, tabf_ref, tabr_ref, st_ref,
             du_ref, dbbre_ref, dbbim_ref, dcre_ref, dcim_ref, da_ref, gcarry, hre_s, him_s, gre_s, gim_s):
        @pl.when(pl.program_id(1) == 0)
        def _():
            gcarry[...] = jnp.zeros_like(gcarry)
            for r in (dbbre_ref, dbbim_ref, dcre_ref, dcim_ref, da_ref):
                r[...] = jnp.zeros_like(r)

        u, dy, start = u_ref[...], dy_ref[...], st_ref[...]
        bbre, bbim, cre, cim = (r[...].astype(BF16) for r in (bbre_ref, bbim_ref, cre_ref, cim_ref))
        hre, him, _ = _s5_states(u, bbre, bbim, tabf_ref[...], hre_s, him_s, start)
        dyb, ub = dy.astype(BF16), u.astype(BF16)
        gre, gim, (fre, fim) = _s5_scan(_dot(dyb, cre, 1, 1), -_dot(dyb, cim, 1, 1), tabr_ref[...], gre_s, gim_s,
                                        (gcarry[0:1], gcarry[1:2]), True)
        gcarry[0:1, :] = fre
        gcarry[1:2, :] = fim
        greb, gimb = gre.astype(BF16), gim.astype(BF16)
        du_ref[...] = (_dot(greb, bbre, 1, 1) + _dot(gimb, bbim, 1, 1) + dy * d_ref[...]).astype(du_ref.dtype)
        dbbre_ref[...] += _dot(ub, greb, 0, 0)
        dbbim_ref[...] += _dot(ub, gimb, 0, 0)
        dcre_ref[...] += _dot(hre.astype(BF16), dyb, 0, 0)
        dcim_ref[...] -= _dot(him.astype(BF16), dyb, 0, 0)
        first = lax.broadcasted_iota(jnp.int32, hre.shape, 0) == 0
        pre = _shift_rows(hre, 1) + jnp.where(first, start[0:1], 0.0)
        pim = _shift_rows(him, 1) + jnp.where(first, start[1:2], 0.0)
        da_ref[0:1, :] += jnp.sum(gre * pre + gim * pim, axis=0, keepdims=True)
        da_ref[1:2, :] += jnp.sum(gim * pre - gre * pim, axis=0, keepdims=True)

    mat, tab, scratch = _s5_specs(T)
    tok = pl.BlockSpec((T, 128), lambda j, t: (nblk - 1 - t, j))
    return pl.pallas_call(
        body, name=name, grid=(NJ, nblk),
        in_specs=[tok, tok, pl.BlockSpec((1, 128), lambda j, t: (0, j)), mat(128, S5_COLS), mat(128, S5_COLS),
                  mat(S5_COLS, 128), mat(S5_COLS, 128), tab, tab,
                  pl.BlockSpec((None, None, 2, S5_COLS), lambda j, t: (j, nblk - 1 - t, 0, 0))],
        out_specs=[tok, mat(128, S5_COLS), mat(128, S5_COLS), mat(S5_COLS, 128), mat(S5_COLS, 128), mat(2, S5_COLS)],
        out_shape=[jax.ShapeDtypeStruct((S, W), BF16),
                   jax.ShapeDtypeStruct((NJ, 128, S5_COLS), F32), jax.ShapeDtypeStruct((NJ, 128, S5_COLS), F32),
                   jax.ShapeDtypeStruct((NJ, S5_COLS, 128), F32), jax.ShapeDtypeStruct((NJ, S5_COLS, 128), F32),
                   jax.ShapeDtypeStruct((NJ, 2, S5_COLS), F32)],
        scratch_shapes=[pltpu.VMEM((2, S5_COLS), F32)] + [scratch] * 4,
        compiler_params=_cparams(("parallel", "arbitrary")),
    )(proj, dy, d_skip_row, bbre, bbim, cre, cim, tab_f, tab_r, starts)


def _s5_discretize(lam_re, lam_im, log_dt, b_re, b_im):
    dt = jnp.exp(log_dt)[:, None]
    mag = jnp.exp(lam_re * dt)
    a_re, a_im = mag * jnp.cos(lam_im * dt), mag * jnp.sin(lam_im * dt)
    den = lam_re * lam_re + lam_im * lam_im
    z_re = ((a_re - 1.0) * lam_re + a_im * lam_im) / den
    z_im = (a_im * lam_re - (a_re - 1.0) * lam_im) / den
    bb_re = z_re[..., None] * b_re - z_im[..., None] * b_im
    bb_im = z_re[..., None] * b_im + z_im[..., None] * b_re
    return a_re, a_im, bb_re, bb_im


def _blockdiag(x):
    G, r, c = x.shape
    x = x.reshape(G // S5_SUPER, S5_SUPER, r, c)
    eye = jnp.eye(S5_SUPER, dtype=x.dtype)
    return (x[:, :, :, None, :] * eye[None, :, None, :, None]).reshape(G // S5_SUPER, S5_SUPER * r, S5_SUPER * c)


def _blockdiag_t(x, r, c):
    NJ = x.shape[0]
    x = x.reshape(NJ, S5_SUPER, r, S5_SUPER, c)
    return jnp.stack([x[:, i, :, i, :] for i in range(S5_SUPER)], axis=1).reshape(NJ * S5_SUPER, r, c)


def _gelu(x):
    c = math.sqrt(2.0 / math.pi)
    return 0.5 * x * (1.0 + jnp.tanh(c * (x + 0.044715 * x * x * x)))


def _gelu_grad(x):
    c = math.sqrt(2.0 / math.pi)
    t = jnp.tanh(c * (x + 0.044715 * x * x * x))
    return 0.5 * (1.0 + t) + 0.5 * x * (1.0 - t * t) * c * (1.0 + 3 * 0.044715 * x * x)


def s5_mixer_fwd(proj, W, p, w_glu_bf, tag):
    a_re, a_im, bb_re, bb_im = _s5_discretize(p["lam_re"], p["lam_im"], p["log_dt"], p["b_re"], p["b_im"])
    tab_f, tab_r = _s5_tables(a_re, a_im, W // 128)
    bbre = _blockdiag(jnp.swapaxes(bb_re, 1, 2))
    bbim = _blockdiag(jnp.swapaxes(bb_im, 1, 2))
    cre = _blockdiag(jnp.swapaxes(p["c_re"], 1, 2))
    cim = _blockdiag(jnp.swapaxes(p["c_im"], 1, 2))
    mats = (bbre, bbim, cre, cim, tab_f, tab_r)
    y, starts = s5_scan_fwd(proj, W, *mats, name=f"s5_scan_fwd_{tag}")
    d_row = p["d_skip"].reshape(1, W)
    yy = ew(lambda y, u, d: _gelu(y + d * u), [(y, 0), (proj, 0)], [d_row], [F32], 0, width=W,
            name=f"s5_act_fwd_{tag}")
    mix, z = mm(yy, w_glu_bf, extras=(yy, p["b_glu"].reshape(1, W)), out_dtypes=(BF16, F32),
                epilogue=lambda acc, yy, b: (yy * jax.nn.sigmoid(acc + b), acc + b), name=f"s5_glu_{tag}")
    return mix, (mats, starts, y, yy, z, d_row)


def s5_mixer_bwd(proj, W, p, w_glu_bf, saved, dcat, tag):
    mats, starts, y, yy, z, d_row = saved

    def glu_bwd(dm, yy, z):
        sg = jax.nn.sigmoid(z)
        dz = dm.astype(F32) * yy * sg * (1.0 - sg)
        return dm.astype(F32) * sg, dz, dz

    dyy1, dz, db_glu = ew(glu_bwd, [(dcat, 0), (yy, 0), (z, 0)], [], [F32, F32], 1, width=W, name=f"s5_glu_bwd_{tag}")
    dw_glu = mm(yy, dz, ta=True, out_dtypes=(BF16,), name=f"s5_dwglu_{tag}")
    dyy = mm(dz, w_glu_bf, tb=True, extras=(dyy1,), epilogue=lambda acc, e: (acc + e,), name=f"s5_dyy_{tag}")

    def act_bwd(dyy, y, u, d):
        dpre = dyy * _gelu_grad(y + d * u)
        return dpre, dpre * u

    dy, dd = ew(act_bwd, [(dyy, 0), (y, 0), (proj, 0)], [d_row], [F32], 1, width=W, name=f"s5_act_bwd_{tag}")
    du, dbbre, dbbim, dcre, dcim, da = s5_scan_bwd(proj, W, dy, d_row, *mats, starts, name=f"s5_scan_bwd_{tag}")
    G = W // S5_GROUP
    dbb_re = jnp.swapaxes(_blockdiag_t(dbbre, S5_GROUP, S5_STATE), 1, 2)
    dbb_im = jnp.swapaxes(_blockdiag_t(dbbim, S5_GROUP, S5_STATE), 1, 2)
    dc_re = jnp.swapaxes(_blockdiag_t(dcre, S5_STATE, S5_GROUP), 1, 2)
    dc_im = jnp.swapaxes(_blockdiag_t(dcim, S5_STATE, S5_GROUP), 1, 2)
    da_re, da_im = da[:, 0, :].reshape(G, S5_STATE), da[:, 1, :].reshape(G, S5_STATE)
    _, vjp = jax.vjp(_s5_discretize, p["lam_re"], p["lam_im"], p["log_dt"], p["b_re"], p["b_im"])
    dlam_re, dlam_im, dlog_dt, db_re, db_im = vjp((da_re, da_im, dbb_re, dbb_im))
    grads = dict(lam_re=dlam_re, lam_im=dlam_im, log_dt=dlog_dt, b_re=db_re, b_im=db_im, c_re=dc_re, c_im=dc_im,
                 d_skip=dd.reshape(W), w_glu=dw_glu, b_glu=db_glu.reshape(W))
    return du, grads


CONV_ROWS = 256


def _conv_taps(ext, w):
    acc = ext * w[3:4]
    for j in range(1, 4):
        acc = acc + pltpu.roll(ext, j, axis=0) * w[3 - j:4 - j]
    return acc


def gdn_conv_fwd(proj, width, conv_w, *, name):
    S = proj.shape[0]
    tc = _tile(width, (256, 128))
    T = min(CONV_ROWS, S)
    n = S // T

    def body(x_ref, w_ref, y_ref):
        w = w_ref[...]

        def chunk(c, carry):
            base = pl.multiple_of(c * T, T)
            prev = x_ref[pl.ds(pl.multiple_of(jnp.maximum(base - 8, 0), 8), 8), :]
            ext = jnp.concatenate([jnp.where(c > 0, prev, 0.0), x_ref[pl.ds(base, T), :]], axis=0)
            pre = _conv_taps(ext, w)[8:]
            y_ref[pl.ds(base, T), :] = pre * jax.nn.sigmoid(pre)
            return carry

        lax.fori_loop(0, n, chunk, 0)

    return pl.pallas_call(
        body, name=name, grid=(width // tc,),
        in_specs=[pl.BlockSpec((S, tc), lambda j: (0, j)), pl.BlockSpec((4, tc), lambda j: (0, j))],
        out_specs=pl.BlockSpec((S, tc), lambda j: (0, j)),
        out_shape=jax.ShapeDtypeStruct((S, width), F32),
        compiler_params=_cparams(("parallel",)),
    )(proj, conv_w)


def gdn_conv_bwd(proj, width, conv_w, dy, *, name):
    S = proj.shape[0]
    tc = _tile(width // 3, (256, 128))
    per = width // 3 // tc
    T = min(CONV_ROWS, S)
    n = S // T
    E = T + 16

    def body(x_ref, w_ref, dy_ref, dx_ref, dw_ref):
        w = w_ref[...]

        def halo(ref, start, keep):
            start = pl.multiple_of(jnp.clip(start, 0, S - 8), 8)
            return jnp.where(keep, ref[pl.ds(start, 8), :], 0.0)

        def chunk(c, dw):
            base = pl.multiple_of(c * T, T)
            rows = pl.ds(base, T)
            ext = jnp.concatenate([halo(x_ref, base - 8, c > 0), x_ref[rows, :], halo(x_ref, base + T, c < n - 1)], axis=0)
            dye = jnp.concatenate([jnp.zeros((8, tc), F32), dy_ref[rows, :], halo(dy_ref, base + T, c < n - 1)], axis=0)
            pre = _conv_taps(ext, w)
            sg = jax.nn.sigmoid(pre)
            dpre = dye * (sg * (1.0 + pre * (1.0 - sg)))
            dx = dpre * w[3:4]
            for j in range(1, 4):
                dx = dx + pltpu.roll(dpre, E - j, axis=0) * w[3 - j:4 - j]
            dx_ref[rows, :] = dx[8:8 + T].astype(dx_ref.dtype)
            own = dpre[8:8 + T]
            parts = [jnp.sum(own * pltpu.roll(ext, 3 - i, axis=0)[8:8 + T], axis=0, keepdims=True) if i < 3
                     else jnp.sum(own * ext[8:8 + T], axis=0, keepdims=True) for i in range(4)]
            return dw + jnp.concatenate(parts, axis=0)

        dw_ref[...] = lax.fori_loop(0, n, chunk, jnp.zeros((4, tc), F32))

    return pl.pallas_call(
        body, name=name, grid=(width // tc,),
        in_specs=[pl.BlockSpec((S, tc), lambda j: (0, j)), pl.BlockSpec((4, tc), lambda j: (0, j)),
                  pl.BlockSpec((None, S, tc), lambda j: (j // per, 0, j % per))],
        out_specs=[pl.BlockSpec((S, tc), lambda j: (0, j)), pl.BlockSpec((4, tc), lambda j: (0, j))],
        out_shape=[jax.ShapeDtypeStruct((S, width), BF16), jax.ShapeDtypeStruct((4, width), F32)],
        compiler_params=_cparams(("parallel",)),
    )(proj, conv_w, dy)


@functools.partial(jax.custom_vjp, nondiff_argnums=(0,))
def _bein(spec, a, b):
    return jnp.einsum(spec, a.astype(BF16), b.astype(BF16), preferred_element_type=F32)


def _bein_fwd(spec, a, b):
    return _bein(spec, a, b), (a, b)


def _bein_bwd(spec, res, g):
    a, b = res
    ins, out = spec.split("->")
    sa, sb = ins.split(",")
    return _bein(f"{out},{sb}->{sa}", g, b), _bein(f"{sa},{out}->{sb}", a, g)


_bein.defvjp(_bein_fwd, _bein_bwd)


def _hmm(a, b):
    return jnp.einsum("ncs,nsd->ncd", a, b, precision=HI, preferred_element_type=F32)


def _inv_unit_lower(L):
    C = L.shape[-1]
    r = lax.broadcasted_iota(jnp.int32, L.shape, 1)
    c = lax.broadcasted_iota(jnp.int32, L.shape, 2)
    eye = (r == c).astype(F32)
    D = jnp.where(jnp.right_shift(r, 4) == jnp.right_shift(c, 4), L, 0.0)
    D2 = _hmm(D, D)
    D4 = _hmm(D2, D2)
    D8 = _hmm(D4, D4)
    dinv = _hmm(_hmm(_hmm(eye - D, eye + D2), eye + D4), eye + D8)
    N = _hmm(dinv, L - D)
    return _hmm(_hmm(eye - N, eye + _hmm(N, N)), dinv)


def _softplus(x):
    return jnp.maximum(x, 0.0) + jnp.log(1.0 + jnp.exp(-jnp.abs(x)))


def _gdn_local(qc, kc, vc, ab, a_log_row, dt_row, h, n_heads):
    R = qc.shape[0]
    C = GDN_CHUNK
    n = R // C
    lane = lax.broadcasted_iota(jnp.int32, (1, 128), 1)
    pick = lambda x, i: jnp.sum(jnp.where(lane == i, x, 0.0), axis=-1, keepdims=True)
    a_in, b_in = pick(ab, h).reshape(n, C, 1), pick(ab, n_heads + h).reshape(n, C, 1)
    a_log, dt_bias = pick(a_log_row, h), pick(dt_row, h)
    q3, k3, v = qc.reshape(n, C, 128), kc.reshape(n, C, 128), vc.reshape(n, C, 128)
    q = q3 * lax.rsqrt(jnp.sum(q3 * q3, axis=-1, keepdims=True) + EPS) * HEAD_DIM ** -0.5
    k = k3 * lax.rsqrt(jnp.sum(k3 * k3, axis=-1, keepdims=True) + EPS)
    beta = jax.nn.sigmoid(b_in)
    g = -jnp.exp(a_log) * _softplus(a_in + dt_bias)
    r = lax.broadcasted_iota(jnp.int32, (n, C, C), 1)
    c = lax.broadcasted_iota(jnp.int32, (n, C, C), 2)
    gc = _hmm((r >= c).astype(F32), jnp.broadcast_to(g, (n, C, C)))
    gcol = gc[:, :, 0:1]
    grow = jnp.sum(jnp.where(r == c, gc, 0.0), axis=1, keepdims=True)
    decay = jnp.exp(jnp.where(r >= c, gc - grow, -jnp.inf))
    kb, vb = k * beta, v * beta
    lmat = jnp.where(r > c, _bein("ncd,nsd->ncs", kb, k) * decay, 0.0)
    eg = jnp.exp(gcol)
    rhs = jnp.concatenate([vb, kb * eg], axis=-1)
    attn = jnp.where(r >= c, _bein("ncd,nsd->ncs", q, k) * decay, 0.0)
    glast = gcol[:, C - 1:C, :]
    k_dec = k * jnp.exp(glast - gcol)
    g_last = jnp.broadcast_to(jnp.exp(glast), (n, 1, 128))
    return lmat, rhs, attn.reshape(R, C), (q * eg).reshape(R, 128), k_dec.reshape(R, 128), g_last


def _gdn_rows(S):
    return _tile(S, (256, 128, 64))


def gdn_prep_fwd(qkv, proj, ab_col, a_log_row, dt_row, H, *, name):
    S = qkv.shape[0]
    R = _gdn_rows(S)
    n, nc = R // GDN_CHUNK, S // GDN_CHUNK

    def body(q_ref, k_ref, v_ref, ab_ref, al_ref, dt_ref, u_ref, w_ref, at_ref, qd_ref, kd_ref, gl_ref, t_ref):
        lmat, rhs, attn, qd, kd, gl = _gdn_local(q_ref[...], k_ref[...], v_ref[...], ab_ref[...], al_ref[...],
                                                 dt_ref[...], pl.program_id(1), H)
        tinv = _inv_unit_lower(lmat)
        sol = _hmm(tinv, rhs)
        outs = (sol[..., :128].reshape(R, 128), sol[..., 128:].reshape(R, 128), attn, qd, kd, gl,
                tinv.reshape(R, GDN_CHUNK))
        for r, v in zip((u_ref, w_ref, at_ref, qd_ref, kd_ref, gl_ref, t_ref), outs):
            r[...] = v

    head = lambda off: pl.BlockSpec((R, 128), lambda i, h: (i, off + h))
    row = pl.BlockSpec((1, 128), lambda i, h: (0, 0))
    big = jax.ShapeDtypeStruct((S, H * 128), F32)
    sq = pl.BlockSpec((None, R, GDN_CHUNK), lambda i, h: (h, i, 0))
    outs = pl.pallas_call(
        body, name=name, grid=(S // R, H),
        in_specs=[head(0), head(H), head(2 * H), pl.BlockSpec((R, 128), lambda i, h: (i, ab_col // 128)), row, row],
        out_specs=[head(0), head(0), sq, head(0), head(0), pl.BlockSpec((None, n, 1, 128), lambda i, h: (h, i, 0, 0)),
                   sq],
        out_shape=[big, big, jax.ShapeDtypeStruct((H, S, GDN_CHUNK), F32), big, big,
                   jax.ShapeDtypeStruct((H, nc, 1, 128), F32), jax.ShapeDtypeStruct((H, S, GDN_CHUNK), F32)],
        compiler_params=_cparams(("parallel", "parallel")),
    )(qkv, qkv, qkv, proj, a_log_row, dt_row)
    return tuple(outs[:6]), outs[6]


def gdn_prep_bwd(qkv, proj, ab_col, a_log_row, dt_row, H, u, w, tinv, cts, *, name):
    S = qkv.shape[0]
    R = _gdn_rows(S)
    C = GDN_CHUNK
    n, nc = R // C, S // C

    def body(q_ref, k_ref, v_ref, ab_ref, al_ref, dt_ref, u_ref, w_ref, t_ref,
             du_ref, dw_ref, dat_ref, dqd_ref, dkd_ref, dgl_ref,
             dqkv_ref, dab_ref, dal_ref, ddt_ref):
        i, h = pl.program_id(0), pl.program_id(1)

        @pl.when(h == 0)
        def _():
            dab_ref[...] = jnp.zeros_like(dab_ref)

        @pl.when((h == 0) & (i == 0))
        def _():
            dal_ref[...] = jnp.zeros_like(dal_ref)
            ddt_ref[...] = jnp.zeros_like(ddt_ref)

        tinv_t = jnp.swapaxes(t_ref[...].reshape(n, C, C), 1, 2)
        dsol = jnp.concatenate([du_ref[...], dw_ref[...]], axis=-1).reshape(n, C, 256)
        sol = jnp.concatenate([u_ref[...], w_ref[...]], axis=-1).reshape(n, C, 256)
        drhs = _hmm(tinv_t, dsol)
        dlmat = -jnp.einsum("ncd,nsd->ncs", drhs, sol, precision=HI, preferred_element_type=F32)
        f = lambda q, k, v, ab, al, dt: _gdn_local(q, k, v, ab, al, dt, h, H)
        _, vjp = jax.vjp(f, q_ref[...], k_ref[...], v_ref[...], ab_ref[...], al_ref[...], dt_ref[...])
        dq, dk, dv, dab, dal, ddt = vjp((dlmat, drhs, dat_ref[...], dqd_ref[...], dkd_ref[...], dgl_ref[...]))
        dqkv_ref[0] = dq
        dqkv_ref[1] = dk
        dqkv_ref[2] = dv
        dab_ref[...] += dab
        dal_ref[...] += dal
        ddt_ref[...] += ddt

    head = lambda off: pl.BlockSpec((R, 128), lambda i, h: (i, off + h))
    row = pl.BlockSpec((1, 128), lambda i, h: (0, 0))
    at = pl.BlockSpec((None, R, GDN_CHUNK), lambda i, h: (h, i, 0))
    gl = pl.BlockSpec((None, n, 1, 128), lambda i, h: (h, i, 0, 0))
    W = H * 128
    return pl.pallas_call(
        body, name=name, grid=(S // R, H),
        in_specs=[head(0), head(H), head(2 * H), pl.BlockSpec((R, 128), lambda i, h: (i, ab_col // 128)), row, row,
                  head(0), head(0), at, head(0), head(0), at, head(0), head(0), gl],
        out_specs=[pl.BlockSpec((3, R, 128), lambda i, h: (0, i, h)), pl.BlockSpec((R, 128), lambda i, h: (i, 0)),
                   row, row],
        out_shape=[jax.ShapeDtypeStruct((3, S, W), F32), jax.ShapeDtypeStruct((S, 128), F32)]
        + [jax.ShapeDtypeStruct((1, 128), F32)] * 2,
        compiler_params=_cparams(("arbitrary", "arbitrary")),
    )(qkv, qkv, qkv, proj, a_log_row, dt_row, u, w, tinv, *cts)


def gdn_scan_fwd(u, w, attn, qd, kd, gl, *, name):
    S = u.shape[0]
    H = attn.shape[0]
    C = GDN_CHUNK
    R = _gdn_rows(S)
    n, nc = R // C, S // C

    def body(u_ref, w_ref, at_ref, qd_ref, kd_ref, gl_ref, o_ref, st_ref, state):
        @pl.when(pl.program_id(1) == 0)
        def _():
            state[...] = jnp.zeros_like(state)

        for c in range(n):
            rows = slice(c * C, (c + 1) * C)
            s = state[...]
            st_ref[c] = s
            sb = s.astype(BF16)
            v_new = u_ref[rows, :] - _dot(w_ref[rows, :].astype(BF16), sb)
            vb = v_new.astype(BF16)
            o_ref[rows, :] = _dot(qd_ref[rows, :].astype(BF16), sb) + _dot(at_ref[rows, :].astype(BF16), vb)
            state[...] = s * gl_ref[c] + _dot(kd_ref[rows, :].astype(BF16), vb, 0, 0)

    head = pl.BlockSpec((R, 128), lambda h, i: (i, h))
    return pl.pallas_call(
        body, name=name, grid=(H, S // R),
        in_specs=[head, head, pl.BlockSpec((None, R, C), lambda h, i: (h, i, 0)), head, head,
                  pl.BlockSpec((None, n, 1, 128), lambda h, i: (h, i, 0, 0))],
        out_specs=[head, pl.BlockSpec((None, n, 128, 128), lambda h, i: (h, i, 0, 0))],
        out_shape=[jax.ShapeDtypeStruct((S, H * 128), F32), jax.ShapeDtypeStruct((H, nc, 128, 128), F32)],
        scratch_shapes=[pltpu.VMEM((128, 128), F32)],
        compiler_params=_cparams(("parallel", "arbitrary")),
    )(u, w, attn, qd, kd, gl)


def gdn_scan_bwd(u, w, attn, qd, kd, gl, states, do, *, name):
    S = u.shape[0]
    H = attn.shape[0]
    C = GDN_CHUNK
    R = _gdn_rows(S)
    n, nc, nb = R // C, S // C, S // R

    def body(u_ref, w_ref, at_ref, qd_ref, kd_ref, gl_ref, st_ref, do_ref,
             du_ref, dw_ref, dat_ref, dqd_ref, dkd_ref, dgl_ref, dstate):
        @pl.when(pl.program_id(1) == 0)
        def _():
            dstate[...] = jnp.zeros_like(dstate)

        lane = lax.broadcasted_iota(jnp.int32, (1, 128), 1)
        for c in reversed(range(n)):
            rows = slice(c * C, (c + 1) * C)
            s = st_ref[c]
            sb = s.astype(BF16)
            ds2 = dstate[...]
            ds2b = ds2.astype(BF16)
            wb, qdb, kdb, atb = (r[rows, :].astype(BF16) for r in (w_ref, qd_ref, kd_ref, at_ref))
            dob = do_ref[rows, :].astype(BF16)
            v_new = u_ref[rows, :] - _dot(wb, sb)
            vb = v_new.astype(BF16)
            dv = _dot(atb, dob, 0, 0) + _dot(kdb, ds2b)
            dvb = dv.astype(BF16)
            du_ref[rows, :] = dv
            dw_ref[rows, :] = -_dot(dvb, sb, 1, 1)
            dat_ref[rows, :] = _dot(dob, vb, 1, 1)
            dqd_ref[rows, :] = _dot(dob, sb, 1, 1)
            dkd_ref[rows, :] = _dot(vb, ds2b, 1, 1)
            dgl = jnp.sum(jnp.sum(ds2 * s, axis=1, keepdims=True), axis=0, keepdims=True)
            dgl_ref[c] = jnp.where(lane == 0, dgl, 0.0)
            dstate[...] = ds2 * gl_ref[c] + _dot(qdb, dob, 0, 0) - _dot(wb, dvb, 0, 0)

    head = pl.BlockSpec((R, 128), lambda h, i: (nb - 1 - i, h))
    at = pl.BlockSpec((None, R, C), lambda h, i: (h, nb - 1 - i, 0))
    glb = pl.BlockSpec((None, n, 1, 128), lambda h, i: (h, nb - 1 - i, 0, 0))
    big = jax.ShapeDtypeStruct((S, H * 128), F32)
    return pl.pallas_call(
        body, name=name, grid=(H, nb),
        in_specs=[head, head, at, head, head, glb,
                  pl.BlockSpec((None, n, 128, 128), lambda h, i: (h, nb - 1 - i, 0, 0)), head],
        out_specs=[head, head, at, head, head, glb],
        out_shape=[big, big, jax.ShapeDtypeStruct((H, S, C), F32), big, big,
                   jax.ShapeDtypeStruct((H, nc, 1, 128), F32)],
        scratch_shapes=[pltpu.VMEM((128, 128), F32)],
        compiler_params=_cparams(("parallel", "arbitrary")),
    )(u, w, attn, qd, kd, gl, states, do)


def _head_rms(o):
    return lax.rsqrt(jnp.mean(o * o, axis=-1, keepdims=True) + EPS)


def gdn_mixer_fwd(proj, W, p, tag):
    H = W // HEAD_DIM
    ab_col = 4 * W + MEM_WIDTH
    qkv = gdn_conv_fwd(proj, 3 * W, p["conv_w"], name=f"gdn_conv_fwd_{tag}")
    al_row, dt_row = _pad_row(p["a_log"]), _pad_row(p["dt_bias"])
    pre, tinv = gdn_prep_fwd(qkv, proj, ab_col, al_row, dt_row, H, name=f"gdn_prep_fwd_{tag}")
    o, states = gdn_scan_fwd(*pre, name=f"gdn_scan_fwd_{tag}")
    gn_row = jnp.tile(p["o_norm"].reshape(1, HEAD_DIM), (1, H))

    def gate_fwd(o, gate, gn):
        return o * _head_rms(o) * gn * (gate * jax.nn.sigmoid(gate))

    mix = ew(gate_fwd, [(o, 0), (proj, 3 * W)], [gn_row], [BF16], 0, width=W, tc=HEAD_DIM, name=f"gdn_gate_fwd_{tag}")
    return mix, (qkv, pre, tinv, states, o, gn_row, al_row, dt_row)


def gdn_mixer_bwd(proj, W, p, saved, dcat, tag):
    qkv, pre, tinv, states, o, gn_row, al_row, dt_row = saved
    H = W // HEAD_DIM
    ab_col = 4 * W + MEM_WIDTH

    def gate_bwd(dm, o, gate, gn):
        dm = dm.astype(F32)
        r = _head_rms(o)
        xh = o * r
        sg = jax.nn.sigmoid(gate)
        dy = dm * gate * sg
        dgate = dm * xh * gn * (sg * (1.0 + gate * (1.0 - sg)))
        dxh = dy * gn
        do = r * (dxh - xh * jnp.mean(dxh * xh, axis=-1, keepdims=True))
        return do, dgate, dy * xh

    do, dgate, dgn = ew(gate_bwd, [(dcat, 0), (o, 0), (proj, 3 * W)], [gn_row], [F32, BF16], 1, width=W, tc=HEAD_DIM,
                        name=f"gdn_gate_bwd_{tag}")
    cts = gdn_scan_bwd(*pre, states, do, name=f"gdn_scan_bwd_{tag}")
    dqkv, dab, dal, ddt = gdn_prep_bwd(qkv, proj, ab_col, al_row, dt_row, H, pre[0], pre[1], tinv, cts,
                                       name=f"gdn_prep_bwd_{tag}")
    dx, dconv = gdn_conv_bwd(proj, 3 * W, p["conv_w"], dqkv, name=f"gdn_conv_bwd_{tag}")
    grads = dict(conv_w=dconv, a_log=dal[0, :H], dt_bias=ddt[0, :H], o_norm=dgn.reshape(H, HEAD_DIM).sum(axis=0))
    return dx, dgate, dab, grads


def exchange(arrays, same_block, *, name):
    n = len(arrays)

    def body(*refs):
        copies = _exchange_copies(refs[:n], refs[n:2 * n], same_block, *refs[2 * n:])
        for cp in copies:
            cp.start()
        for cp in copies:
            cp.wait()

    shapes, sems = _exchange_shapes(arrays, same_block)
    any_spec = pl.BlockSpec(memory_space=pl.ANY)
    return pl.pallas_call(body, name=name, in_specs=[any_spec] * n, out_specs=[any_spec] * n, out_shape=shapes,
                          scratch_shapes=sems)(*arrays)


def _row_tile(R, row_bytes):
    for t in (512, 256, 128, 64, 32, 16, 8):
        if R % t == 0 and 2 * t * row_bytes <= 24 * 2 ** 20:
            return t
    return R


def adamw(parts, w, m, v, *, name):
    P, R, C = parts.shape
    tr = _row_tile(R, C * (P * parts.dtype.itemsize + 7 * 4))
    c1, c2 = 1.0 - ADAM_B1 ** ADAM_STEP, 1.0 - ADAM_B2 ** ADAM_STEP

    def body(p_ref, w_ref, m_ref, v_ref, g_ref, d_ref, nm_ref, nv_ref):
        g = p_ref[0].astype(F32)
        for s in range(1, P):
            g = g + p_ref[s].astype(F32)
        m = ADAM_B1 * m_ref[...] + (1.0 - ADAM_B1) * g
        v = ADAM_B2 * v_ref[...] + (1.0 - ADAM_B2) * (g * g)
        g_ref[...] = g
        nm_ref[...] = m
        nv_ref[...] = v
        d_ref[...] = -ADAM_LR * ((m / c1) / (jnp.sqrt(v / c2) + ADAM_EPS) + ADAM_WD * w_ref[...])

    blk = pl.BlockSpec((tr, C), lambda i: (i, 0))
    return pl.pallas_call(
        body, name=name, grid=(R // tr,),
        in_specs=[pl.BlockSpec((P, tr, C), lambda i: (0, i, 0)), blk, blk, blk],
        out_specs=[blk] * 4, out_shape=[jax.ShapeDtypeStruct((R, C), F32)] * 4,
        compiler_params=_cparams(("parallel",)),
    )(parts, w, m, v)


def sum_parts(parts, *, name):
    P, R, C = parts.shape
    tr = _row_tile(R, C * (P * parts.dtype.itemsize + 4))

    def body(p_ref, o_ref):
        g = p_ref[0].astype(F32)
        for s in range(1, P):
            g = g + p_ref[s].astype(F32)
        o_ref[...] = g

    return pl.pallas_call(
        body, name=name, grid=(R // tr,),
        in_specs=[pl.BlockSpec((P, tr, C), lambda i: (0, i, 0))], out_specs=pl.BlockSpec((tr, C), lambda i: (i, 0)),
        out_shape=jax.ShapeDtypeStruct((R, C), F32), compiler_params=_cparams(("parallel",)),
    )(parts)


PACK_COLS = 1024
PACK_ROWS = 256


def _pack(arrays, cols, lead=()):
    n_lead = len(lead)
    flat = [a.reshape(lead + (-1,)) for a in arrays]
    total = sum(f.shape[-1] for f in flat)
    rows = -(-total // cols)
    mult = PACK_ROWS if rows > PACK_ROWS else 8
    rows = -(-rows // mult) * mult
    pad = rows * cols - total
    if pad:
        flat.append(jnp.zeros(lead + (pad,), flat[0].dtype))
    return jnp.concatenate(flat, axis=n_lead).reshape(lead + (rows, cols))


def _unpack(buf, shapes, lead=()):
    flat = buf.reshape(lead + (-1,))
    out, off = [], 0
    for s in shapes:
        n = math.prod(s)
        out.append(lax.slice_in_dim(flat, off, off + n, axis=len(lead)).reshape(lead + tuple(s)))
        off += n
    return out


def _to_shards(full, axis):
    s = full.shape
    return jnp.moveaxis(full.reshape(s[:axis] + (N_DEV, s[axis] // N_DEV) + s[axis + 1:]), axis, 0)


def _from_shards(g, axis):
    m = jnp.moveaxis(g, 0, axis)
    s = m.shape
    return m.reshape(s[:axis] + (s[axis] * s[axis + 1],) + s[axis + 2:])


BIG = (("w_mem_kv", 0), ("w_out", 1), ("w_up", 2), ("w_down", 1), ("s5_w_in", 1), ("s5_w_glu", 1),
       ("gdn_w_in", 2), ("fox_w_in", 1))
SMALL_SHARDED = (("s5_d_skip", 1), ("s5_b_glu", 1), ("gdn_conv_w", 2))
REPLICATED = ("mem_norm", "norm1", "norm2", "norm_f", "s5_lam_re", "s5_lam_im", "s5_log_dt", "s5_b_re", "s5_b_im",
              "s5_c_re", "s5_c_im", "gdn_a_log", "gdn_dt_bias", "gdn_o_norm", "fox_b_f")
WEIGHTS = ("mem_norm", "w_mem_kv", "norm1", "w_out", "norm2", "w_up", "w_down", "norm_f", "s5_w_in", "s5_lam_re",
           "s5_lam_im", "s5_log_dt", "s5_b_re", "s5_b_im", "s5_c_re", "s5_c_im", "s5_d_skip", "s5_w_glu", "s5_b_glu",
           "gdn_w_in", "gdn_conv_w", "gdn_a_log", "gdn_dt_bias", "gdn_o_norm", "fox_w_in", "fox_b_f")


def _relu2(acc):
    r = jnp.maximum(acc, 0.0)
    return acc, r * r


def _relu2_grad(acc, u):
    return (acc * 2.0 * jnp.maximum(u, 0.0),)


def _add(acc, e):
    return (acc + e,)


def _permute_in(w, kind, W):
    if kind == 0:
        return w
    n_main = (4 if kind == 1 else 3) * W
    n_small = w.shape[1] - n_main - MEM_WIDTH
    small = jnp.pad(w[:, n_main:n_main + n_small], ((0, 0), (0, MEM_WIDTH - n_small)))
    return jnp.concatenate([w[:, :n_main], w[:, n_main + n_small:], small], axis=1)


def _unpermute_in(dw, kind, W, n_small):
    if kind == 0:
        return dw
    n_main = (4 if kind == 1 else 3) * W
    return jnp.concatenate([dw[:, :n_main], dw[:, n_main + MEM_WIDTH:n_main + MEM_WIDTH + n_small],
                            dw[:, n_main:n_main + MEM_WIDTH]], axis=1)


def kernel(x, mem, mem_norm, w_mem_kv, norm1, w_out, norm2, w_up, w_down, norm_f, s5_w_in, s5_lam_re, s5_lam_im, s5_log_dt, s5_b_re, s5_b_im, s5_c_re, s5_c_im, s5_d_skip, s5_w_glu, s5_b_glu, gdn_w_in, gdn_conv_w, gdn_a_log, gdn_dt_bias, gdn_o_norm, fox_w_in, fox_b_f, loss_target, m_mem_norm, m_w_mem_kv, m_norm1, m_w_out, m_norm2, m_w_up, m_w_down, m_norm_f, m_s5_w_in, m_s5_lam_re, m_s5_lam_im, m_s5_log_dt, m_s5_b_re, m_s5_b_im, m_s5_c_re, m_s5_c_im, m_s5_d_skip, m_s5_w_glu, m_s5_b_glu, m_gdn_w_in, m_gdn_conv_w, m_gdn_a_log, m_gdn_dt_bias, m_gdn_o_norm, m_fox_w_in, m_fox_b_f, v_mem_norm, v_w_mem_kv, v_norm1, v_w_out, v_norm2, v_w_up, v_w_down, v_norm_f, v_s5_w_in, v_s5_lam_re, v_s5_lam_im, v_s5_log_dt, v_s5_b_re, v_s5_b_im, v_s5_c_re, v_s5_c_im, v_s5_d_skip, v_s5_w_glu, v_s5_b_glu, v_gdn_w_in, v_gdn_conv_w, v_gdn_a_log, v_gdn_dt_bias, v_gdn_o_norm, v_fox_w_in, v_fox_b_f):
    args = dict(locals())
    wsh = {n: args[n] for n in WEIGHTS}
    msh = {n: args["m_" + n] for n in WEIGHTS}
    vsh = {n: args["v_" + n] for n in WEIGHTS}
    h0, memx, target = x[0], mem[0], loss_target[0]
    S, D = h0.shape
    W = D - MEM_WIDTH
    depth = norm1.shape[0]
    me = 4 * lax.axis_index("x") + 2 * lax.axis_index("y") + lax.axis_index("c")

    bf = lambda t: t.astype(BF16)
    rows2d = lambda g: g.reshape(-1, g.shape[-1])

    def layer_sends(i, group):
        kind, j = i % 3, i // 3
        if group == "up":
            return {"w_up": bf(wsh["w_up"][i])}
        if group == "down":
            return {"w_down": bf(wsh["w_down"][i])}
        d = {"w_out": bf(wsh["w_out"][i])}
        if kind == 0:
            d["w_in"], d["w_glu"] = bf(wsh["s5_w_in"][j]), bf(wsh["s5_w_glu"][j])
        elif kind == 1:
            d["w_in"] = bf(wsh["gdn_w_in"][j])
        else:
            d["w_in"] = bf(_permute_in(wsh["fox_w_in"][j], 2, W))
        return d

    def as_comm(d):
        return [(v, True) for v in d.values()]

    first = {**layer_sends(0, "in"), **layer_sends(0, "up"), **layer_sends(0, "down")}
    small_w = _pack([wsh[n] for n, _ in SMALL_SHARDED], 128)
    got = exchange(list(first.values()) + [bf(w_mem_kv), small_w], [True] * (len(first) + 2), name="gather_first")
    gathered = [dict(zip(first, got)) if i == 0 else {} for i in range(depth)]
    full = {n: wsh[n] for n in REPLICATED}
    full["w_mem_kv"] = rows2d(got[len(first)])
    for (n, ax), g in zip(SMALL_SHARDED, _unpack(got[-1], [wsh[n].shape for n, _ in SMALL_SHARDED], lead=(N_DEV,))):
        full[n] = _from_shards(g, ax)

    def layer_params(i):
        kind, j = i % 3, i // 3
        g = gathered[i]
        w = {n: rows2d(g[n]) for n in ("w_out", "w_down", "w_glu") if n in g}
        w["w_up"] = g["w_up"]
        w["w_in"] = _permute_in(_from_shards(g["w_in"], 1), 1, W) if kind == 1 else rows2d(g["w_in"])
        if kind == 0:
            p = {k: full["s5_" + k][j] for k in ("lam_re", "lam_im", "log_dt", "b_re", "b_im", "c_re", "c_im",
                                                   "d_skip", "b_glu")}
            return kind, j, p, w, 0
        if kind == 1:
            p = {k: full["gdn_" + k][j] for k in ("conv_w", "a_log", "dt_bias", "o_norm")}
            return kind, j, p, w, 2 * (W // HEAD_DIM)
        return kind, j, {"b_f": full["fox_b_f"][j]}, w, W // HEAD_DIM

    def hosted(i, group):
        if i + 1 >= depth:
            return {}, []
        d = layer_sends(i + 1, group)
        return {"comm": as_comm(d)}, list(d)

    def keep(i, names, res):
        if not names:
            return res
        outs, got = res
        gathered[i + 1].update(zip(names, got))
        return outs

    memn = rms_fwd(memx, full["mem_norm"], out_dtype=BF16, name="mem_rms")
    mkv = mm(memn, full["w_mem_kv"], name="mem_kv")
    h = h0
    saved = []
    weights = []
    for i in range(depth):
        kind, j, p, w, n_small = layer_params(i)
        weights.append((w, p, n_small))
        a = rms_fwd(h, full["norm1"][i], out_dtype=BF16, name=f"rms1_{i}")
        kw, names = hosted(i, "in")
        proj = keep(i, names, mm(a, w["w_in"], name=f"in_proj_{i}", **kw))
        if kind == 0:
            mix, ms = s5_mixer_fwd(proj, W, p, w["w_glu"], f"l{i}")
            mem_col = W
        elif kind == 1:
            mix, ms = gdn_mixer_fwd(proj, W, p, f"l{i}")
            mem_col = 4 * W
        else:
            mix, ms = fox_mixer_fwd(proj, 3 * W + MEM_WIDTH, p["b_f"], f"l{i}")
            mem_col = 3 * W
        read = mem_fwd(proj, mem_col, mkv, name=f"mem_fwd_{i}")
        cat = jnp.concatenate([mix, read], axis=1)
        h_mid = mm(cat, w["w_out"], extras=(h,), epilogue=_add, name=f"out_proj_{i}")
        a2 = rms_fwd(h_mid, full["norm2"][i], out_dtype=BF16, name=f"rms2_{i}")
        kw, names = hosted(i, "up")
        u, act = keep(i, names, mm(a2, w["w_up"], b_shard8=True, epilogue=_relu2, out_dtypes=(F32, BF16),
                                   name=f"up_{i}", **kw))
        kw, names = hosted(i, "down")
        h_next = keep(i, names, mm(act, w["w_down"], extras=(h_mid,), epilogue=_add, name=f"down_{i}", **kw))
        saved.append((h, a, proj, ms, mem_col, cat, h_mid, a2, u, act))
        h = h_next

    loss_row, dh, dnf = final_loss(h, full["norm_f"], target, name="final_loss")
    loss = lax.psum(jnp.sum(loss_row), ("x", "y", "c"))

    grads = {n: [None] * full[n].shape[0] for n in ("norm1", "norm2")}
    for pre, cnt in (("s5_", (depth + 2) // 3), ("gdn_", (depth + 1) // 3), ("fox_", depth // 3)):
        for n in REPLICATED + tuple(n for n, _ in SMALL_SHARDED):
            if n.startswith(pre):
                grads[n] = [None] * cnt
    shares = {}
    by_dest = lambda g: g.reshape((N_DEV, g.shape[0] // N_DEV) + g.shape[1:])
    pending_up, pending_rest = {}, {}

    def carry(pending):
        return {"comm": [(v, False) for v in pending.values()]} if pending else {}

    def landed(pending, res):
        if not pending:
            return res
        outs, got = res
        shares.update(zip(pending, got))
        return outs

    dmkv = None
    for i in reversed(range(depth)):
        kind, j = i % 3, i // 3
        w, p, n_small = weights[i]
        h_in, a, proj, ms, mem_col, cat, h_mid, a2, u, act = saved[i]
        du = landed(pending_up, mm(dh, w["w_down"], tb=True, extras=(u,), epilogue=_relu2_grad, out_dtypes=(BF16,),
                                   name=f"d_act_{i}", **carry(pending_up)))
        dw_down = landed(pending_rest, mm(act, dh, ta=True, out_dtypes=(BF16,), name=f"dw_down_{i}",
                                          **carry(pending_rest)))
        pending = {("w_down", i): by_dest(dw_down)}
        da2 = landed(pending, mm(du, w["w_up"], tb=True, b_shard8=True, name=f"d_a2_{i}", **carry(pending)))
        pending_up = {("w_up", i): mm(a2, du, ta=True, out_shard8=True, out_dtypes=(BF16,), name=f"dw_up_{i}")}
        dh_mid, dn2 = rms_bwd(h_mid, full["norm2"][i], da2, dh, name=f"rms2_bwd_{i}")
        grads["norm2"][i] = dn2[0]
        dcat = mm(dh_mid, w["w_out"], tb=True, name=f"d_cat_{i}")
        pending_rest = {("w_out", i): by_dest(mm(cat, dh_mid, ta=True, out_dtypes=(BF16,), name=f"dw_out_{i}"))}
        dq_mem, dmkv_i = mem_bwd(proj, mem_col, mkv, dcat, W, name=f"mem_bwd_{i}")
        dmkv = dmkv_i if dmkv is None else dmkv + dmkv_i
        if kind == 0:
            dmain, g = s5_mixer_bwd(proj, W, p, w["w_glu"], ms, dcat, f"l{i}")
            dproj = jnp.concatenate([dmain, dq_mem], axis=1)
            pending_rest[("s5_w_glu", j)] = by_dest(g.pop("w_glu"))
            for k, val in g.items():
                grads["s5_" + k][j] = val
        elif kind == 1:
            dx, dgate, dab, g = gdn_mixer_bwd(proj, W, p, ms, dcat, f"l{i}")
            dproj = jnp.concatenate([dx, dgate, dq_mem, bf(jnp.pad(dab, ((0, 0), (0, MEM_WIDTH - 128))))], axis=1)
            for k, val in g.items():
                grads["gdn_" + k][j] = val
        else:
            dq, dk, dv, df, db = fox_mixer_bwd(proj, 3 * W + MEM_WIDTH, ms, dcat, f"l{i}")
            dproj = jnp.concatenate([dq, dk, dv, dq_mem.astype(F32), jnp.pad(df, ((0, 0), (0, MEM_WIDTH - 128)))],
                                    axis=1)
            grads["fox_b_f"][j] = db
        da = mm(dproj, w["w_in"], tb=True, name=f"d_a_{i}")
        in_name = ("s5_w_in", "gdn_w_in", "fox_w_in")[kind]
        if kind == 1:
            dw_in = _unpermute_in(mm(a, dproj, ta=True, name=f"dw_in_{i}"), 1, W, n_small)
            pending_rest[(in_name, j)] = bf(_to_shards(dw_in, 1))
        else:
            pending_rest[(in_name, j)] = by_dest(mm(a, dproj, ta=True, out_dtypes=(BF16,), name=f"dw_in_{i}"))
        dh, dn1 = rms_bwd(h_in, full["norm1"][i], da, dh_mid, name=f"rms1_bwd_{i}")
        grads["norm1"][i] = dn1[0]
    grad_x = dh[None]
    grads = {n: jnp.stack(v) for n, v in grads.items()}
    grads["norm_f"] = dnf[0]
    pending_rest[("w_mem_kv", None)] = by_dest(mm(memn, dmkv, ta=True, out_dtypes=(BF16,), name="dw_mem_kv"))
    dmemn = mm(dmkv, full["w_mem_kv"], tb=True, name="d_memn")
    _, dmn = rms_bwd(memx, full["mem_norm"], dmemn, None, name="mem_rms_bwd")
    grads["mem_norm"] = dmn[0]

    small_names = list(REPLICATED) + [n for n, _ in SMALL_SHARDED]
    last = {**pending_up, **pending_rest}
    got = exchange(list(last.values()) + [_pack([grads[n] for n in small_names], 128)],
                   [False] * len(last) + [True], name="exchange_last")
    shares.update(zip(last, got))

    big_out = [{}, {}, {}, {}]
    for n, _ in BIG:
        per_layer = []
        for idx in ([None] if n == "w_mem_kv" else range(wsh[n].shape[0])):
            key = (n, idx)
            local = [t[n] if idx is None else t[n][idx] for t in (wsh, msh, vsh)]
            part = shares[key]
            if n == "fox_w_in":
                total = sum_parts(part, name=f"sum_{n}_{idx}")
                part = _unpermute_in(total, 2, W, W // HEAD_DIM)[None]
            per_layer.append(adamw(part, *local, name=f"adamw_{n}_{idx}"))
        for o, vals in zip(big_out, zip(*per_layer)):
            o[n] = vals[0] if n == "w_mem_kv" else jnp.stack(vals)

    total = sum_parts(got[-1], name="sum_small_grads")
    gsmall = dict(zip(small_names, _unpack(total, [grads[n].shape for n in small_names])))
    for n, ax in SMALL_SHARDED:
        width = wsh[n].shape[ax]
        gsmall[n] = lax.dynamic_slice_in_dim(gsmall[n], me * width, width, axis=ax)
    outs = adamw(_pack([gsmall[n] for n in small_names], 128)[None],
                 *[_pack([t[n] for n in small_names], 128) for t in (wsh, msh, vsh)], name="adamw_small")
    small_out = [dict(zip(small_names, _unpack(o, [wsh[n].shape for n in small_names]))) for o in outs]

    res = [{**b, **s} for b, s in zip(big_out, small_out)]
    return (loss, grad_x, *[r[n] for r in res for n in WEIGHTS])
```

```python
import functools
import math

import jax
import jax.numpy as jnp
from jax import lax
from jax.experimental import pallas as pl
from jax.experimental.pallas import tpu as pltpu

F32 = jnp.float32
BF16 = jnp.bfloat16

HEAD_DIM = 128
MEM_HEADS = 4
MEM_WIDTH = MEM_HEADS * HEAD_DIM
S5_GROUP = 16
S5_STATE = 64
GDN_CHUNK = 64
EPS = 1e-6
ADAM_LR, ADAM_B1, ADAM_B2, ADAM_EPS, ADAM_WD, ADAM_STEP = 0.001, 0.9, 0.999, 1e-08, 0.01, 10

N_DEV = 8
MESH = pl.DeviceIdType.MESH
VMEM_LIMIT = 56 * 1024 * 1024
HI = lax.Precision.HIGHEST


def _tile(n, prefs=(1024, 512, 256, 128)):
    for t in prefs:
        if n % t == 0:
            return t
    return n


def _cparams(sem=None):
    return pltpu.CompilerParams(dimension_semantics=sem, vmem_limit_bytes=VMEM_LIMIT)


def _dot(a, b, ca=1, cb=0, precision=None):
    return lax.dot_general(a, b, (((ca,), (cb,)), ((), ())), preferred_element_type=F32, precision=precision)


def _bdot(a, b):
    return _dot(a.astype(BF16), b.astype(BF16))


SIBLING = 1
OTHER_CHIPS = (2, 4, 6)


class _Exchange:
    def __init__(self, p_refs, out_refs, same_block, send_sems, recv_sems, local_sems):
        self.pos = (lax.axis_index("x"), lax.axis_index("y"), lax.axis_index("c"))
        self.arrays = list(zip(p_refs, out_refs, same_block))
        self.sems = (send_sems, recv_sems, local_sems)

    def _dev(self, m):
        return tuple(1 - v if (m >> (2 - b)) & 1 else v for b, v in enumerate(self.pos))

    def _slot(self, m):
        d = self._dev(m)
        return 4 * d[0] + 2 * d[1] + d[2]

    def _copy(self, n, src, slot, sem, to):
        k = n * (N_DEV - 1) + sem - 1
        return pltpu.make_async_remote_copy(
            src_ref=src, dst_ref=self.arrays[n][1].at[slot], send_sem=self.sems[0].at[k], recv_sem=self.sems[1].at[k],
            device_id=self._dev(to), device_id_type=MESH)

    def _local(self, n):
        p_ref, out_ref, same = self.arrays[n]
        return pltpu.make_async_copy(p_ref if same else p_ref.at[self._slot(0)], out_ref.at[self._slot(0)],
                                     self.sems[2].at[n])

    def _sends(self, n):
        p_ref, _, same = self.arrays[n]
        if same:
            return [self._copy(n, p_ref, self._slot(0), m, m) for m in (SIBLING,) + OTHER_CHIPS]
        return [self._copy(n, p_ref.at[self._slot(m)], self._slot(0), m, m) for m in range(1, N_DEV)]

    def _passed_on(self, n, m):
        return self._copy(n, self.arrays[n][1].at[self._slot(m)], self._slot(m), m ^ SIBLING, SIBLING)

    def _arrival(self, n, m):
        return self._copy(n, self.arrays[n][1].at[self._slot(m)], self._slot(m), m, m)

    def start(self):
        for n in range(len(self.arrays)):
            self._local(n).start()
            for cp in self._sends(n):
                cp.start()

    def finish(self):
        for n, (_, _, same) in enumerate(self.arrays):
            if same:
                for m in OTHER_CHIPS:
                    self._arrival(n, m).wait_recv()
                    self._passed_on(n, m).start()
        for n, (_, _, same) in enumerate(self.arrays):
            if same:
                for m in OTHER_CHIPS:
                    self._arrival(n, m ^ SIBLING).wait_recv()
                    self._passed_on(n, m).wait_send()
                self._arrival(n, SIBLING).wait_recv()
                for cp in self._sends(n):
                    cp.wait_send()
            else:
                for cp in self._sends(n):
                    cp.wait()
            self._local(n).wait()


def _exchange_shapes(arrays, same_block):
    shapes = [jax.ShapeDtypeStruct((N_DEV,) + tuple(p.shape if same else p.shape[1:]), p.dtype)
              for p, same in zip(arrays, same_block)]
    n = len(arrays)
    sems = [pltpu.SemaphoreType.DMA((n * (N_DEV - 1),)), pltpu.SemaphoreType.DMA((n * (N_DEV - 1),)),
            pltpu.SemaphoreType.DMA((n,))]
    return shapes, sems


def _mm_tk(K, tm, tn, a_bytes, b_bytes, out_bytes):
    for tk in (2048, 1024, 512, 256, 128):
        if K % tk == 0 and 2 * tk * (tm * a_bytes + tn * b_bytes) + (2 * out_bytes + 4) * tm * tn <= 40 * 2 ** 20:
            return tk
    return K


def mm(a, b, *, ta=False, tb=False, extras=(), epilogue=None, out_dtypes=(F32,), name, b_shard8=False,
       out_shard8=False, comm=()):
    M, K = (a.shape[1], a.shape[0]) if ta else a.shape
    if b_shard8:
        brows, bcols = b.shape[1], b.shape[2] * N_DEV
    else:
        brows, bcols = b.shape
    N = brows if tb else bcols
    assert K == (bcols if tb else brows), (a.shape, b.shape, ta, tb)
    tm = _tile(M)
    tn = _tile(N // N_DEV if (out_shard8 or (b_shard8 and not tb)) else N)
    out_bytes = sum(jnp.dtype(d).itemsize for d in out_dtypes) + 4 * sum(e.shape[0] != 1 for e in extras)
    tk = _mm_tk(K // N_DEV if (b_shard8 and tb) else K, tm, tn, a.dtype.itemsize, b.dtype.itemsize, out_bytes)
    ni, nj, nk = M // tm, N // tn, K // tk
    n_ex, n_out, n_comm = len(extras), len(out_dtypes), len(comm)
    ca, cb = (0 if ta else 1), (1 if tb else 0)
    same_block = [s for _, s in comm]

    def body(a_ref, b_ref, *rest):
        ex_refs, rest = rest[:n_ex], rest[n_ex:]
        cin, rest = rest[:n_comm], rest[n_comm:]
        out_refs, rest = rest[:n_out], rest[n_out:]
        cout, rest = rest[:n_comm], rest[n_comm:]
        acc = rest[0]
        i, j, k = pl.program_id(0), pl.program_id(1), pl.program_id(2)
        if n_comm:
            @pl.when((i == 0) & (j == 0) & (k == 0))
            def _():
                _Exchange(cin, cout, same_block, *rest[1:]).start()

        def finish(res):
            outs = (res,) if epilogue is None else epilogue(res, *[e[...] for e in ex_refs])
            for o, v in zip(out_refs, outs):
                o[...] = v.astype(o.dtype)

        part = _dot(a_ref[...].astype(BF16), b_ref[...].astype(BF16), ca, cb)
        if nk == 1:
            finish(part)
        else:
            @pl.when(k == 0)
            def _():
                acc[...] = part

            @pl.when(k > 0)
            def _():
                acc[...] += part

            @pl.when(k == nk - 1)
            def _():
                finish(acc[...])

        if n_comm:
            @pl.when((i == ni - 1) & (j == nj - 1) & (k == nk - 1))
            def _():
                _Exchange(cin, cout, same_block, *rest[1:]).finish()

    a_spec = pl.BlockSpec((tk, tm), lambda i, j, k: (k, i)) if ta else pl.BlockSpec((tm, tk), lambda i, j, k: (i, k))
    if b_shard8 and tb:
        kper = bcols // N_DEV // tk
        b_spec = pl.BlockSpec((None, tn, tk), lambda i, j, k: (k // kper, j, k % kper))
    elif b_shard8:
        nper = bcols // N_DEV // tn
        b_spec = pl.BlockSpec((None, tk, tn), lambda i, j, k: (j // nper, k, j % nper))
    elif tb:
        b_spec = pl.BlockSpec((tn, tk), lambda i, j, k: (j, k))
    else:
        b_spec = pl.BlockSpec((tk, tn), lambda i, j, k: (k, j))
    ex_specs = [pl.BlockSpec((1, tn), lambda i, j, k: (0, j)) if e.shape[0] == 1 and M != 1
                else pl.BlockSpec((tm, tn), lambda i, j, k: (i, j)) for e in extras]
    if out_shard8:
        nper = N // N_DEV // tn
        out_spec = pl.BlockSpec((None, tm, tn), lambda i, j, k: (j // nper, i, j % nper))
        out_shape = (N_DEV, M, N // N_DEV)
    else:
        out_spec = pl.BlockSpec((tm, tn), lambda i, j, k: (i, j))
        out_shape = (M, N)
    comm_shapes, sems = _exchange_shapes([p for p, _ in comm], same_block) if n_comm else ([], [])
    any_spec = pl.BlockSpec(memory_space=pl.ANY)
    outs = pl.pallas_call(
        body, name=name, grid=(ni, nj, nk),
        in_specs=[a_spec, b_spec] + ex_specs + [any_spec] * n_comm,
        out_specs=[out_spec] * n_out + [any_spec] * n_comm,
        out_shape=[jax.ShapeDtypeStruct(out_shape, d) for d in out_dtypes] + comm_shapes,
        scratch_shapes=[pltpu.VMEM((tm, tn) if nk > 1 else (8, 128), F32)] + sems,
        compiler_params=_cparams(("arbitrary",) * 3 if n_comm else ("parallel", "parallel", "arbitrary")),
    )(a, b, *extras, *[p for p, _ in comm])
    if n_comm:
        return (outs[0] if n_out == 1 else tuple(outs[:n_out])), list(outs[n_out:])
    return outs[0] if n_out == 1 else outs


def rms_fwd(x, g, *, out_dtype, name):
    S, D = x.shape
    ts = _tile(S, (512, 256, 128))

    def body(x_ref, g_ref, y_ref):
        x = x_ref[...]
        r = lax.rsqrt(jnp.mean(x * x, axis=-1, keepdims=True) + EPS)
        y_ref[...] = (x * r * g_ref[...]).astype(y_ref.dtype)

    return pl.pallas_call(
        body, name=name, grid=(S // ts,),
        in_specs=[pl.BlockSpec((ts, D), lambda i: (i, 0)), pl.BlockSpec((1, D), lambda i: (0, 0))],
        out_specs=pl.BlockSpec((ts, D), lambda i: (i, 0)),
        out_shape=jax.ShapeDtypeStruct((S, D), out_dtype),
        compiler_params=_cparams(("parallel",)),
    )(x, g.reshape(1, D))


def rms_bwd(x, g, dy, dres, *, name):
    S, D = x.shape
    ts = _tile(S, (512, 256, 128))
    has_res = dres is not None

    def body(x_ref, g_ref, dy_ref, *rest):
        dx_ref, dg_ref = rest[-2:]
        x = x_ref[...]
        r = lax.rsqrt(jnp.mean(x * x, axis=-1, keepdims=True) + EPS)
        xh = x * r
        dy = dy_ref[...].astype(F32)
        dxh = dy * g_ref[...]
        dx = r * (dxh - xh * jnp.mean(dxh * xh, axis=-1, keepdims=True))
        if has_res:
            dx = dx + rest[0][...]
        dx_ref[...] = dx

        @pl.when(pl.program_id(0) == 0)
        def _():
            dg_ref[...] = jnp.zeros_like(dg_ref)

        dg_ref[...] += jnp.sum(dy * xh, axis=0, keepdims=True)

    tok = pl.BlockSpec((ts, D), lambda i: (i, 0))
    row = pl.BlockSpec((1, D), lambda i: (0, 0))
    return pl.pallas_call(
        body, name=name, grid=(S // ts,),
        in_specs=[tok, row, tok] + ([tok] if has_res else []),
        out_specs=[tok, row],
        out_shape=[jax.ShapeDtypeStruct((S, D), F32), jax.ShapeDtypeStruct((1, D), F32)],
        compiler_params=_cparams(("arbitrary",)),
    )(x, g.reshape(1, D), dy, *([dres] if has_res else []))


def final_loss(h, g, target, *, name):
    S, D = h.shape
    ts = _tile(S, (512, 256, 128))

    def body(x_ref, g_ref, t_ref, loss_ref, dx_ref, dg_ref):
        x = x_ref[...]
        r = lax.rsqrt(jnp.mean(x * x, axis=-1, keepdims=True) + EPS)
        xh = x * r
        err = xh * g_ref[...] - t_ref[...]
        dy = err * (1.0 / D)
        dxh = dy * g_ref[...]
        dx_ref[...] = r * (dxh - xh * jnp.mean(dxh * xh, axis=-1, keepdims=True))

        @pl.when(pl.program_id(0) == 0)
        def _():
            dg_ref[...] = jnp.zeros_like(dg_ref)
            loss_ref[...] = jnp.zeros_like(loss_ref)

        dg_ref[...] += jnp.sum(dy * xh, axis=0, keepdims=True)
        loss_ref[...] += jnp.sum(err * err, axis=0, keepdims=True) * (0.5 / D)

    tok = pl.BlockSpec((ts, D), lambda i: (i, 0))
    row = pl.BlockSpec((1, D), lambda i: (0, 0))
    return pl.pallas_call(
        body, name=name, grid=(S // ts,),
        in_specs=[tok, row, tok], out_specs=[row, tok, row],
        out_shape=[jax.ShapeDtypeStruct((1, D), F32), jax.ShapeDtypeStruct((S, D), F32),
                   jax.ShapeDtypeStruct((1, D), F32)],
        compiler_params=_cparams(("arbitrary",)),
    )(h, g.reshape(1, D), target)


def _mem_probs(q, k):
    s = _dot(q.astype(BF16), k.astype(BF16), 1, 1) * HEAD_DIM ** -0.5
    p = jnp.exp(s - jnp.max(s, axis=-1, keepdims=True))
    return p / jnp.sum(p, axis=-1, keepdims=True)


def mem_fwd(proj, q_col, mkv, *, name):
    S = proj.shape[0]
    L = mkv.shape[0]
    ts = _tile(S)

    def body(q_ref, kv_ref, o_ref):
        for h in range(MEM_HEADS):
            c = slice(h * HEAD_DIM, (h + 1) * HEAD_DIM)
            v = kv_ref[:, MEM_WIDTH + h * HEAD_DIM:MEM_WIDTH + (h + 1) * HEAD_DIM]
            p = _mem_probs(q_ref[:, c], kv_ref[:, c])
            o_ref[:, c] = _bdot(p, v).astype(o_ref.dtype)

    return pl.pallas_call(
        body, name=name, grid=(S // ts,),
        in_specs=[pl.BlockSpec((ts, MEM_WIDTH), lambda i: (i, q_col // MEM_WIDTH)),
                  pl.BlockSpec((L, 2 * MEM_WIDTH), lambda i: (0, 0))],
        out_specs=pl.BlockSpec((ts, MEM_WIDTH), lambda i: (i, 0)),
        out_shape=jax.ShapeDtypeStruct((S, MEM_WIDTH), BF16),
        compiler_params=_cparams(("parallel",)),
    )(proj, mkv)


def mem_bwd(proj, q_col, mkv, dcat, do_col, *, name):
    S = proj.shape[0]
    L = mkv.shape[0]
    ts = _tile(S)
    scale = HEAD_DIM ** -0.5

    def body(q_ref, kv_ref, do_ref, dq_ref, dkv_ref):
        @pl.when(pl.program_id(0) == 0)
        def _():
            dkv_ref[...] = jnp.zeros_like(dkv_ref)

        for h in range(MEM_HEADS):
            c = slice(h * HEAD_DIM, (h + 1) * HEAD_DIM)
            cv = slice(MEM_WIDTH + h * HEAD_DIM, MEM_WIDTH + (h + 1) * HEAD_DIM)
            q, k, v = q_ref[:, c].astype(BF16), kv_ref[:, c].astype(BF16), kv_ref[:, cv].astype(BF16)
            do = do_ref[:, c].astype(BF16)
            p = _mem_probs(q, k)
            dkv_ref[:, cv] += _dot(p.astype(BF16), do, 0, 0)
            dp = _dot(do, v, 1, 1)
            ds = (p * (dp - jnp.sum(dp * p, axis=-1, keepdims=True)) * scale).astype(BF16)
            dq_ref[:, c] = _dot(ds, k).astype(dq_ref.dtype)
            dkv_ref[:, c] += _dot(ds, q, 0, 0)

    return pl.pallas_call(
        body, name=name, grid=(S // ts,),
        in_specs=[pl.BlockSpec((ts, MEM_WIDTH), lambda i: (i, q_col // MEM_WIDTH)),
                  pl.BlockSpec((L, 2 * MEM_WIDTH), lambda i: (0, 0)),
                  pl.BlockSpec((ts, MEM_WIDTH), lambda i: (i, do_col // MEM_WIDTH))],
        out_specs=[pl.BlockSpec((ts, MEM_WIDTH), lambda i: (i, 0)),
                   pl.BlockSpec((L, 2 * MEM_WIDTH), lambda i: (0, 0))],
        out_shape=[jax.ShapeDtypeStruct((S, MEM_WIDTH), BF16), jax.ShapeDtypeStruct((L, 2 * MEM_WIDTH), F32)],
        compiler_params=_cparams(("arbitrary",)),
    )(proj, mkv, dcat)


def _lower_ones(n, strict=False):
    r = lax.broadcasted_iota(jnp.int32, (n, n), 0)
    c = lax.broadcasted_iota(jnp.int32, (n, n), 1)
    return (r > c if strict else r >= c).astype(F32)


def fox_gate_fwd(proj, f_col, b_f_row, *, name):
    S = proj.shape[0]
    tb = _tile(S, (256, 128))

    def body(f_ref, b_ref, c_ref, carry):
        @pl.when(pl.program_id(0) == 0)
        def _():
            carry[...] = jnp.zeros_like(carry)

        ls = jax.nn.log_sigmoid(f_ref[...] + b_ref[...])
        cum = _dot(_lower_ones(tb), ls, precision=HI) + carry[...]
        c_ref[...] = cum
        carry[...] = cum[tb - 1:tb, :]

    return pl.pallas_call(
        body, name=name, grid=(S // tb,),
        in_specs=[pl.BlockSpec((tb, 128), lambda i: (i, f_col // 128)), pl.BlockSpec((1, 128), lambda i: (0, 0))],
        out_specs=pl.BlockSpec((tb, 128), lambda i: (i, 0)),
        out_shape=jax.ShapeDtypeStruct((S, 128), F32),
        scratch_shapes=[pltpu.VMEM((1, 128), F32)],
        compiler_params=_cparams(("arbitrary",)),
    )(proj, b_f_row)


def fox_gate_bwd(dcf, proj, f_col, b_f_row, *, name):
    S = proj.shape[0]
    tb = _tile(S, (256, 128))
    nb = S // tb

    def body(d_ref, f_ref, b_ref, df_ref, db_ref, carry):
        @pl.when(pl.program_id(0) == 0)
        def _():
            carry[...] = jnp.zeros_like(carry)
            db_ref[...] = jnp.zeros_like(db_ref)

        upper = _lower_ones(tb).T
        rc = _dot(upper, d_ref[...], precision=HI) + carry[...]
        carry[...] = rc[0:1, :]
        df = rc * jax.nn.sigmoid(-(f_ref[...] + b_ref[...]))
        df_ref[...] = df
        db_ref[...] += jnp.sum(df, axis=0, keepdims=True)

    return pl.pallas_call(
        body, name=name, grid=(nb,),
        in_specs=[pl.BlockSpec((tb, 128), lambda i: (nb - 1 - i, 0)),
                  pl.BlockSpec((tb, 128), lambda i: (nb - 1 - i, f_col // 128)),
                  pl.BlockSpec((1, 128), lambda i: (0, 0))],
        out_specs=[pl.BlockSpec((tb, 128), lambda i: (nb - 1 - i, 0)), pl.BlockSpec((1, 128), lambda i: (0, 0))],
        out_shape=[jax.ShapeDtypeStruct((S, 128), F32), jax.ShapeDtypeStruct((1, 128), F32)],
        scratch_shapes=[pltpu.VMEM((1, 128), F32)],
        compiler_params=_cparams(("arbitrary",)),
    )(dcf, proj, b_f_row)


def _fox_block(S):
    return _tile(S, (512, 256, 128)) if S > 512 else S // 2


FOX_Q_BLOCKS = 2


def _causal(s, row0):
    r = row0 + lax.broadcasted_iota(jnp.int32, s.shape, 0)
    c = lax.broadcasted_iota(jnp.int32, s.shape, 1)
    return jnp.where(r >= c, s, -jnp.inf)


def fox_fwd(proj, n_heads, cf_col, cf_row, *, name):
    S = proj.shape[0]
    H = n_heads
    bk = _fox_block(S)
    qmul = FOX_Q_BLOCKS if S % (FOX_Q_BLOCKS * bk) == 0 else 1
    bq = qmul * bk
    scale = HEAD_DIM ** -0.5

    def body(q_ref, k_ref, v_ref, cc_ref, cr_ref, o_ref, st_ref):
        qi = pl.program_id(1)
        q = q_ref[...].astype(BF16)
        cq = cc_ref[...]

        def step(j, carry, diagonal=None):
            rows = pl.ds(pl.multiple_of(j * bk, bk), bk)
            k = k_ref[rows, :].astype(BF16)
            v = v_ref[rows, :].astype(BF16)
            m, l, acc = carry
            s = _dot(q, k, 1, 1) * scale + cq - cr_ref[j]
            if diagonal is not None:
                s = _causal(s, -diagonal * bk)
            m2 = jnp.maximum(m, jnp.max(s, axis=-1, keepdims=True))
            p = jnp.exp(s - m2)
            a = jnp.exp(m - m2)
            return m2, a * l + jnp.sum(p, axis=-1, keepdims=True), a * acc + _dot(p.astype(BF16), v)

        init = (jnp.full((bq, 1), -jnp.inf, F32), jnp.zeros((bq, 1), F32), jnp.zeros((bq, HEAD_DIM), F32))
        carry = lax.fori_loop(0, qi * qmul, step, init)
        for d in range(qmul):
            carry = step(qi * qmul + d, carry, diagonal=d)
        m, l, acc = carry
        o_ref[...] = (acc / l).astype(o_ref.dtype)
        lane = lax.broadcasted_iota(jnp.int32, (bq, 128), 1)
        st_ref[...] = jnp.where(lane == 0, m + jnp.log(l), jnp.where(lane == 1, cq, 0.0))

    return pl.pallas_call(
        body, name=name, grid=(H, S // bq),
        in_specs=[pl.BlockSpec((bq, HEAD_DIM), lambda h, i: (i, h)),
                  pl.BlockSpec((S, HEAD_DIM), lambda h, i: (0, H + h)),
                  pl.BlockSpec((S, HEAD_DIM), lambda h, i: (0, 2 * H + h)),
                  pl.BlockSpec((None, bq, 1), lambda h, i: (h, i, 0)),
                  pl.BlockSpec((None, S // bk, 1, bk), lambda h, i: (h, 0, 0, 0))],
        out_specs=[pl.BlockSpec((bq, HEAD_DIM), lambda h, i: (i, h)),
                   pl.BlockSpec((None, bq, 128), lambda h, i: (h, i, 0))],
        out_shape=[jax.ShapeDtypeStruct((S, H * HEAD_DIM), BF16), jax.ShapeDtypeStruct((H, S, 128), F32)],
        compiler_params=_cparams(("parallel", "parallel")),
    )(proj, proj, proj, cf_col, cf_row)


def fox_delta(stats, o, dcat, *, name):
    H, S, _ = stats.shape
    ts = _tile(S)

    def body(st_ref, o_ref, do_ref, out_ref):
        d = jnp.sum(o_ref[...].astype(F32) * do_ref[...].astype(F32), axis=-1, keepdims=True)
        lane = lax.broadcasted_iota(jnp.int32, (ts, 128), 1)
        out_ref[...] = jnp.where(lane == 2, d, st_ref[...])

    return pl.pallas_call(
        body, name=name, grid=(H, S // ts),
        in_specs=[pl.BlockSpec((None, ts, 128), lambda h, i: (h, i, 0)),
                  pl.BlockSpec((ts, HEAD_DIM), lambda h, i: (i, h)),
                  pl.BlockSpec((ts, HEAD_DIM), lambda h, i: (i, h))],
        out_specs=pl.BlockSpec((None, ts, 128), lambda h, i: (h, i, 0)),
        out_shape=jax.ShapeDtypeStruct((H, S, 128), F32),
        compiler_params=_cparams(("parallel", "parallel")),
    )(stats, o, dcat)


def fox_bwd(proj, n_heads, stats, cf_row, dcat, *, name):
    S = proj.shape[0]
    H = n_heads
    bq = _fox_block(S)
    nq = S // bq
    scale = HEAD_DIM ** -0.5

    def body(q_ref, k_ref, v_ref, do_ref, st_ref, cr_ref, dq_ref, dk_ref, dv_ref, dst_ref, dcr_ref):
        kj = pl.program_id(1)

        @pl.when(kj == 0)
        def _():
            dq_ref[...] = jnp.zeros_like(dq_ref)
            dst_ref[...] = jnp.zeros_like(dst_ref)

        k = k_ref[...].astype(BF16)
        v = v_ref[...].astype(BF16)
        ck = cr_ref[...]
        lane = lax.broadcasted_iota(jnp.int32, (bq, 128), 1)

        def step(qi, carry, diagonal=False):
            dk, dv, dck = carry
            rows = pl.ds(pl.multiple_of(qi * bq, bq), bq)
            q = q_ref[rows, :].astype(BF16)
            do = do_ref[rows, :].astype(BF16)
            st = st_ref[rows, :]
            lse, cq, delta = st[:, 0:1], st[:, 1:2], st[:, 2:3]
            s = _dot(q, k, 1, 1) * scale + cq - ck
            if diagonal:
                s = _causal(s, 0)
            p = jnp.exp(s - lse)
            dv = dv + _dot(p.astype(BF16), do, 0, 0)
            ds = p * (_dot(do, v, 1, 1) - delta)
            dsb = (ds * scale).astype(BF16)
            dq_ref[rows, :] += _dot(dsb, k)
            dst_ref[rows, :] += jnp.where(lane == 0, jnp.sum(ds, axis=-1, keepdims=True), 0.0)
            return dk + _dot(dsb, q, 0, 0), dv, dck - jnp.sum(ds, axis=0, keepdims=True)

        init = (jnp.zeros((bq, HEAD_DIM), F32), jnp.zeros((bq, HEAD_DIM), F32), jnp.zeros((1, bq), F32))
        dk, dv, dck = lax.fori_loop(kj + 1, nq, step, step(kj, init, diagonal=True))
        dk_ref[...] = dk
        dv_ref[...] = dv
        dcr_ref[...] = dck

    W = H * HEAD_DIM
    return pl.pallas_call(
        body, name=name, grid=(H, nq),
        in_specs=[pl.BlockSpec((S, HEAD_DIM), lambda h, j: (0, h)),
                  pl.BlockSpec((bq, HEAD_DIM), lambda h, j: (j, H + h)),
                  pl.BlockSpec((bq, HEAD_DIM), lambda h, j: (j, 2 * H + h)),
                  pl.BlockSpec((S, HEAD_DIM), lambda h, j: (0, h)),
                  pl.BlockSpec((None, S, 128), lambda h, j: (h, 0, 0)),
                  pl.BlockSpec((None, None, 1, bq), lambda h, j: (h, j, 0, 0))],
        out_specs=[pl.BlockSpec((S, HEAD_DIM), lambda h, j: (0, h)),
                   pl.BlockSpec((bq, HEAD_DIM), lambda h, j: (j, h)),
                   pl.BlockSpec((bq, HEAD_DIM), lambda h, j: (j, h)),
                   pl.BlockSpec((None, S, 128), lambda h, j: (h, 0, 0)),
                   pl.BlockSpec((None, None, 1, bq), lambda h, j: (h, j, 0, 0))],
        out_shape=[jax.ShapeDtypeStruct((S, W), F32), jax.ShapeDtypeStruct((S, W), F32),
                   jax.ShapeDtypeStruct((S, W), F32), jax.ShapeDtypeStruct((H, S, 128), F32),
                   jax.ShapeDtypeStruct((H, nq, 1, bq), F32)],
        compiler_params=_cparams(("parallel", "arbitrary")),
    )(proj, proj, proj, dcat, stats, cf_row)


def _pad_row(v, n=128):
    return jnp.pad(v.astype(F32), (0, n - v.shape[0])).reshape(1, n)


def fox_mixer_fwd(proj, f_col, b_f, tag):
    S = proj.shape[0]
    H = b_f.shape[0]
    bq = _fox_block(S)
    b_row = _pad_row(b_f)
    cf = fox_gate_fwd(proj, f_col, b_row, name=f"fox_gate_fwd_{tag}")
    cf_t = cf[:, :H].T
    o, stats = fox_fwd(proj, H, cf_t.reshape(H, S, 1), cf_t.reshape(H, S // bq, 1, bq), name=f"fox_fwd_{tag}")
    return o, (o, stats, cf_t, b_row)


def fox_mixer_bwd(proj, f_col, saved, dcat, tag):
    o, stats, cf_t, b_row = saved
    H, S = cf_t.shape
    bq = _fox_block(S)
    stats = fox_delta(stats, o, dcat, name=f"fox_delta_{tag}")
    dq, dk, dv, dst, dcr = fox_bwd(proj, H, stats, cf_t.reshape(H, S // bq, 1, bq), dcat, name=f"fox_bwd_{tag}")
    dcf = jnp.sum(dst, axis=-1) + dcr.reshape(H, S)
    dcf = jnp.pad(dcf.T, ((0, 0), (0, 128 - H)))
    df, db = fox_gate_bwd(dcf, proj, f_col, b_row, name=f"fox_gate_bwd_{tag}")
    return dq, dk, dv, df, db[0, :H]


def ew(fn, tok_ins, row_ins, tok_out_dtypes, n_row_out, *, width, name, tc=None, ts=None):
    S = tok_ins[0][0].shape[0]
    tc = tc or _tile(width, (512, 256, 128))
    ts = ts or _tile(S, (512, 256, 128))
    n_tok, n_row, n_to = len(tok_ins), len(row_ins), len(tok_out_dtypes)
    for _, col in tok_ins:
        assert col % tc == 0

    def body(*refs):
        ins = [r[...] for r in refs[:n_tok + n_row]]
        outs = fn(*ins)
        outs = outs if isinstance(outs, (tuple, list)) else (outs,)
        o_refs = refs[n_tok + n_row:]
        for o, v in zip(o_refs[:n_to], outs[:n_to]):
            o[...] = v.astype(o.dtype)
        if n_row_out:
            @pl.when(pl.program_id(1) == 0)
            def _():
                for o in o_refs[n_to:]:
                    o[...] = jnp.zeros_like(o)

            for o, v in zip(o_refs[n_to:], outs[n_to:]):
                o[...] += jnp.sum(v, axis=0, keepdims=True)

    def tok_spec(col):
        return pl.BlockSpec((ts, tc), lambda j, i: (i, col // tc + j))

    row_spec = pl.BlockSpec((1, tc), lambda j, i: (0, j))
    res = pl.pallas_call(
        body, name=name, grid=(width // tc, S // ts),
        in_specs=[tok_spec(col) for _, col in tok_ins] + [row_spec] * n_row,
        out_specs=[tok_spec(0)] * n_to + [row_spec] * n_row_out,
        out_shape=[jax.ShapeDtypeStruct((S, width), d) for d in tok_out_dtypes]
        + [jax.ShapeDtypeStruct((1, width), F32)] * n_row_out,
        compiler_params=_cparams(("parallel", "arbitrary" if n_row_out else "parallel")),
    )(*[a for a, _ in tok_ins], *row_ins)
    return res[0] if len(res) == 1 else res


S5_SUPER = 128 // S5_GROUP
S5_COLS = S5_SUPER * S5_STATE


def _shift_rows(x, d, up=False):
    T = x.shape[0]
    if d % 8 == 0:
        z = jnp.zeros((d, x.shape[1]), x.dtype)
        return jnp.concatenate([x[d:], z], axis=0) if up else jnp.concatenate([z, x[:T - d]], axis=0)
    row = lax.broadcasted_iota(jnp.int32, x.shape, 0)
    if up:
        return jnp.where(row < T - d, pltpu.roll(x, T - d, axis=0), 0.0)
    return jnp.where(row >= d, pltpu.roll(x, d, axis=0), 0.0)


def _s5_scan(bre, bim, tab, hre_s, him_s, carry, up):
    T, N = bre.shape
    G = T // 8
    hre, him = bre.reshape(G, 8, N), bim.reshape(G, 8, N)
    for k in range(3):
        shift = (8 - (1 << k)) if up else (1 << k)
        sre, sim = pltpu.roll(hre, shift, axis=1), pltpu.roll(him, shift, axis=1)
        cre, cim = tab[0, k], tab[1, k]
        hre, him = hre + cre * sre - cim * sim, him + cre * sim + cim * sre
    hre_s[...] = hre
    him_s[...] = him
    are, aim = tab[0, 3], tab[1, 3]
    edge = slice(0, 1) if up else slice(7, 8)

    def tile(i, c):
        g = G - 1 - i if up else i
        r = hre_s[g] + are * c[0] - aim * c[1]
        m = him_s[g] + are * c[1] + aim * c[0]
        hre_s[g] = r
        him_s[g] = m
        return r[edge], m[edge]

    carry = lax.fori_loop(0, G, tile, carry)
    return hre_s[...].reshape(T, N), him_s[...].reshape(T, N), carry


def _s5_states(u, bbre, bbim, tab, hre_s, him_s, start):
    return _s5_scan(_bdot(u, bbre), _bdot(u, bbim), tab, hre_s, him_s, (start[0:1], start[1:2]), False)


def _s5_tables(a_re, a_im, NJ):
    are, aim = a_re.reshape(NJ, 1, S5_COLS), a_im.reshape(NJ, 1, S5_COLS)
    pows = [(are, aim)]
    for _ in range(7):
        r, i = pows[-1]
        pows.append((r * are - i * aim, r * aim + i * are))
    row = jnp.arange(8).reshape(1, 8, 1)
    tabs = []
    for up in (False, True):
        re, im = [], []
        for d in (1, 2, 4):
            keep = (row <= 7 - d) if up else (row >= d)
            re.append(jnp.where(keep, pows[d - 1][0], 0.0))
            im.append(jnp.where(keep, -pows[d - 1][1] if up else pows[d - 1][1], 0.0))
        order = range(7, -1, -1) if up else range(8)
        re.append(jnp.concatenate([pows[n][0] for n in order], axis=1))
        im.append(jnp.concatenate([-pows[n][1] if up else pows[n][1] for n in order], axis=1))
        tabs.append(jnp.stack([jnp.stack(re, axis=1), jnp.stack(im, axis=1)], axis=1))
    return tabs


def _s5_block(S):
    return _tile(S, (256, 128))


def _s5_specs(T):
    mat = lambda r, c: pl.BlockSpec((None, r, c), lambda j, t: (j, 0, 0))
    tab = pl.BlockSpec((None, 2, 4, 8, S5_COLS), lambda j, t: (j, 0, 0, 0, 0))
    scratch = pltpu.VMEM((T // 8, 8, S5_COLS), F32)
    return mat, tab, scratch


def s5_scan_fwd(proj, W, bbre, bbim, cre, cim, tab_f, tab_r, *, name):
    S = proj.shape[0]
    T = _s5_block(S)
    NJ, nblk = W // 128, S // T

    def body(u_ref, bbre_ref, bbim_ref, cre_ref, cim_ref, tab_ref, y_ref, st_ref, carry, hre_s, him_s):
        @pl.when(pl.program_id(1) == 0)
        def _():
            carry[...] = jnp.zeros_like(carry)

        st_ref[...] = carry[...]
        hre, him, (lre, lim) = _s5_states(u_ref[...], bbre_ref[...], bbim_ref[...], tab_ref[...], hre_s, him_s,
                                          carry[...])
        carry[0:1, :] = lre
        carry[1:2, :] = lim
        y_ref[...] = _bdot(hre, cre_ref[...]) - _bdot(him, cim_ref[...])

    mat, tab, scratch = _s5_specs(T)
    return pl.pallas_call(
        body, name=name, grid=(NJ, nblk),
        in_specs=[pl.BlockSpec((T, 128), lambda j, t: (t, j)), mat(128, S5_COLS), mat(128, S5_COLS),
                  mat(S5_COLS, 128), mat(S5_COLS, 128), tab],
        out_specs=[pl.BlockSpec((T, 128), lambda j, t: (t, j)),
                   pl.BlockSpec((None, None, 2, S5_COLS), lambda j, t: (j, t, 0, 0))],
        out_shape=[jax.ShapeDtypeStruct((S, W), F32), jax.ShapeDtypeStruct((NJ, nblk, 2, S5_COLS), F32)],
        scratch_shapes=[pltpu.VMEM((2, S5_COLS), F32), scratch, scratch],
        compiler_params=_cparams(("parallel", "arbitrary")),
    )(proj, bbre, bbim, cre, cim, tab_f)


def s5_scan_bwd(proj, W, dy, d_skip_row, bbre, bbim, cre, cim, tab_f, tab_r, starts, *, name):
    S = proj.shape[0]
    T = _s5_block(S)
    NJ, nblk = W // 128, S // T

    def body(u_ref, dy_ref, d_ref, bbre_ref, bbim_ref, cre_ref, cim_ref, tabf_ref, tabr_ref, st_ref,
             du_ref, dbbre_ref, dbbim_ref, dcre_ref, dcim_ref, da_ref, gcarry, hre_s, him_s, gre_s, gim_s):
        @pl.when(pl.program_id(1) == 0)
        def _():
            gcarry[...] = jnp.zeros_like(gcarry)
            for r in (dbbre_ref, dbbim_ref, dcre_ref, dcim_ref, da_ref):
                r[...] = jnp.zeros_like(r)

        u, dy, start = u_ref[...], dy_ref[...], st_ref[...]
        bbre, bbim, cre, cim = (r[...].astype(BF16) for r in (bbre_ref, bbim_ref, cre_ref, cim_ref))
        hre, him, _ = _s5_states(u, bbre, bbim, tabf_ref[...], hre_s, him_s, start)
        dyb, ub = dy.astype(BF16), u.astype(BF16)
        gre, gim, (fre, fim) = _s5_scan(_dot(dyb, cre, 1, 1), -_dot(dyb, cim, 1, 1), tabr_ref[...], gre_s, gim_s,
                                        (gcarry[0:1], gcarry[1:2]), True)
        gcarry[0:1, :] = fre
        gcarry[1:2, :] = fim
        greb, gimb = gre.astype(BF16), gim.astype(BF16)
        du_ref[...] = (_dot(greb, bbre, 1, 1) + _dot(gimb, bbim, 1, 1) + dy * d_ref[...]).astype(du_ref.dtype)
        dbbre_ref[...] += _dot(ub, greb, 0, 0)
        dbbim_ref[...] += _dot(ub, gimb, 0, 0)
        dcre_ref[...] += _dot(hre.astype(BF16), dyb, 0, 0)
        dcim_ref[...] -= _dot(him.astype(BF16), dyb, 0, 0)
        first = lax.broadcasted_iota(jnp.int32, hre.shape, 0) == 0
        pre = _shift_rows(hre, 1) + jnp.where(first, start[0:1], 0.0)
        pim = _shift_rows(him, 1) + jnp.where(first, start[1:2], 0.0)
        da_ref[0:1, :] += jnp.sum(gre * pre + gim * pim, axis=0, keepdims=True)
        da_ref[1:2, :] += jnp.sum(gim * pre - gre * pim, axis=0, keepdims=True)

    mat, tab, scratch = _s5_specs(T)
    tok = pl.BlockSpec((T, 128), lambda j, t: (nblk - 1 - t, j))
    return pl.pallas_call(
        body, name=name, grid=(NJ, nblk),
        in_specs=[tok, tok, pl.BlockSpec((1, 128), lambda j, t: (0, j)), mat(128, S5_COLS), mat(128, S5_COLS),
                  mat(S5_COLS, 128), mat(S5_COLS, 128), tab, tab,
                  pl.BlockSpec((None, None, 2, S5_COLS), lambda j, t: (j, nblk - 1 - t, 0, 0))],
        out_specs=[tok, mat(128, S5_COLS), mat(128, S5_COLS), mat(S5_COLS, 128), mat(S5_COLS, 128), mat(2, S5_COLS)],
        out_shape=[jax.ShapeDtypeStruct((S, W), BF16),
                   jax.ShapeDtypeStruct((NJ, 128, S5_COLS), F32), jax.ShapeDtypeStruct((NJ, 128, S5_COLS), F32),
                   jax.ShapeDtypeStruct((NJ, S5_COLS, 128), F32), jax.ShapeDtypeStruct((NJ, S5_COLS, 128), F32),
                   jax.ShapeDtypeStruct((NJ, 2, S5_COLS), F32)],
        scratch_shapes=[pltpu.VMEM((2, S5_COLS), F32)] + [scratch] * 4,
        compiler_params=_cparams(("parallel", "arbitrary")),
    )(proj, dy, d_skip_row, bbre, bbim, cre, cim, tab_f, tab_r, starts)


def _s5_discretize(lam_re, lam_im, log_dt, b_re, b_im):
    dt = jnp.exp(log_dt)[:, None]
    mag = jnp.exp(lam_re * dt)
    a_re, a_im = mag * jnp.cos(lam_im * dt), mag * jnp.sin(lam_im * dt)
    den = lam_re * lam_re + lam_im * lam_im
    z_re = ((a_re - 1.0) * lam_re + a_im * lam_im) / den
    z_im = (a_im * lam_re - (a_re - 1.0) * lam_im) / den
    bb_re = z_re[..., None] * b_re - z_im[..., None] * b_im
    bb_im = z_re[..., None] * b_im + z_im[..., None] * b_re
    return a_re, a_im, bb_re, bb_im


def _blockdiag(x):
    G, r, c = x.shape
    x = x.reshape(G // S5_SUPER, S5_SUPER, r, c)
    eye = jnp.eye(S5_SUPER, dtype=x.dtype)
    return (x[:, :, :, None, :] * eye[None, :, None, :, None]).reshape(G // S5_SUPER, S5_SUPER * r, S5_SUPER * c)


def _blockdiag_t(x, r, c):
    NJ = x.shape[0]
    x = x.reshape(NJ, S5_SUPER, r, S5_SUPER, c)
    return jnp.stack([x[:, i, :, i, :] for i in range(S5_SUPER)], axis=1).reshape(NJ * S5_SUPER, r, c)


def _gelu(x):
    c = math.sqrt(2.0 / math.pi)
    return 0.5 * x * (1.0 + jnp.tanh(c * (x + 0.044715 * x * x * x)))


def _gelu_grad(x):
    c = math.sqrt(2.0 / math.pi)
    t = jnp.tanh(c * (x + 0.044715 * x * x * x))
    return 0.5 * (1.0 + t) + 0.5 * x * (1.0 - t * t) * c * (1.0 + 3 * 0.044715 * x * x)


def s5_mixer_fwd(proj, W, p, w_glu_bf, tag):
    a_re, a_im, bb_re, bb_im = _s5_discretize(p["lam_re"], p["lam_im"], p["log_dt"], p["b_re"], p["b_im"])
    tab_f, tab_r = _s5_tables(a_re, a_im, W // 128)
    bbre = _blockdiag(jnp.swapaxes(bb_re, 1, 2))
    bbim = _blockdiag(jnp.swapaxes(bb_im, 1, 2))
    cre = _blockdiag(jnp.swapaxes(p["c_re"], 1, 2))
    cim = _blockdiag(jnp.swapaxes(p["c_im"], 1, 2))
    mats = (bbre, bbim, cre, cim, tab_f, tab_r)
    y, starts = s5_scan_fwd(proj, W, *mats, name=f"s5_scan_fwd_{tag}")
    d_row = p["d_skip"].reshape(1, W)
    yy = ew(lambda y, u, d: _gelu(y + d * u), [(y, 0), (proj, 0)], [d_row], [F32], 0, width=W,
            name=f"s5_act_fwd_{tag}")
    mix, z = mm(yy, w_glu_bf, extras=(yy, p["b_glu"].reshape(1, W)), out_dtypes=(BF16, F32),
                epilogue=lambda acc, yy, b: (yy * jax.nn.sigmoid(acc + b), acc + b), name=f"s5_glu_{tag}")
    return mix, (mats, starts, y, yy, z, d_row)


def s5_mixer_bwd(proj, W, p, w_glu_bf, saved, dcat, tag):
    mats, starts, y, yy, z, d_row = saved

    def glu_bwd(dm, yy, z):
        sg = jax.nn.sigmoid(z)
        dz = dm.astype(F32) * yy * sg * (1.0 - sg)
        return dm.astype(F32) * sg, dz, dz

    dyy1, dz, db_glu = ew(glu_bwd, [(dcat, 0), (yy, 0), (z, 0)], [], [F32, F32], 1, width=W, name=f"s5_glu_bwd_{tag}")
    dw_glu = mm(yy, dz, ta=True, out_dtypes=(BF16,), name=f"s5_dwglu_{tag}")
    dyy = mm(dz, w_glu_bf, tb=True, extras=(dyy1,), epilogue=lambda acc, e: (acc + e,), name=f"s5_dyy_{tag}")

    def act_bwd(dyy, y, u, d):
        dpre = dyy * _gelu_grad(y + d * u)
        return dpre, dpre * u

    dy, dd = ew(act_bwd, [(dyy, 0), (y, 0), (proj, 0)], [d_row], [F32], 1, width=W, name=f"s5_act_bwd_{tag}")
    du, dbbre, dbbim, dcre, dcim, da = s5_scan_bwd(proj, W, dy, d_row, *mats, starts, name=f"s5_scan_bwd_{tag}")
    G = W // S5_GROUP
    dbb_re = jnp.swapaxes(_blockdiag_t(dbbre, S5_GROUP, S5_STATE), 1, 2)
    dbb_im = jnp.swapaxes(_blockdiag_t(dbbim, S5_GROUP, S5_STATE), 1, 2)
    dc_re = jnp.swapaxes(_blockdiag_t(dcre, S5_STATE, S5_GROUP), 1, 2)
    dc_im = jnp.swapaxes(_blockdiag_t(dcim, S5_STATE, S5_GROUP), 1, 2)
    da_re, da_im = da[:, 0, :].reshape(G, S5_STATE), da[:, 1, :].reshape(G, S5_STATE)
    _, vjp = jax.vjp(_s5_discretize, p["lam_re"], p["lam_im"], p["log_dt"], p["b_re"], p["b_im"])
    dlam_re, dlam_im, dlog_dt, db_re, db_im = vjp((da_re, da_im, dbb_re, dbb_im))
    grads = dict(lam_re=dlam_re, lam_im=dlam_im, log_dt=dlog_dt, b_re=db_re, b_im=db_im, c_re=dc_re, c_im=dc_im,
                 d_skip=dd.reshape(W), w_glu=dw_glu, b_glu=db_glu.reshape(W))
    return du, grads


CONV_ROWS = 256


def _conv_taps(ext, w):
    acc = ext * w[3:4]
    for j in range(1, 4):
        acc = acc + pltpu.roll(ext, j, axis=0) * w[3 - j:4 - j]
    return acc


def gdn_conv_fwd(proj, width, conv_w, *, name):
    S = proj.shape[0]
    tc = _tile(width, (256, 128))
    T = min(CONV_ROWS, S)
    n = S // T

    def body(x_ref, w_ref, y_ref):
        w = w_ref[...]

        def chunk(c, carry):
            base = pl.multiple_of(c * T, T)
            prev = x_ref[pl.ds(pl.multiple_of(jnp.maximum(base - 8, 0), 8), 8), :]
            ext = jnp.concatenate([jnp.where(c > 0, prev, 0.0), x_ref[pl.ds(base, T), :]], axis=0)
            pre = _conv_taps(ext, w)[8:]
            y_ref[pl.ds(base, T), :] = pre * jax.nn.sigmoid(pre)
            return carry

        lax.fori_loop(0, n, chunk, 0)

    return pl.pallas_call(
        body, name=name, grid=(width // tc,),
        in_specs=[pl.BlockSpec((S, tc), lambda j: (0, j)), pl.BlockSpec((4, tc), lambda j: (0, j))],
        out_specs=pl.BlockSpec((S, tc), lambda j: (0, j)),
        out_shape=jax.ShapeDtypeStruct((S, width), F32),
        compiler_params=_cparams(("parallel",)),
    )(proj, conv_w)


def gdn_conv_bwd(proj, width, conv_w, dy, *, name):
    S = proj.shape[0]
    tc = _tile(width // 3, (256, 128))
    per = width // 3 // tc
    T = min(CONV_ROWS, S)
    n = S // T
    E = T + 16

    def body(x_ref, w_ref, dy_ref, dx_ref, dw_ref):
        w = w_ref[...]

        def halo(ref, start, keep):
            start = pl.multiple_of(jnp.clip(start, 0, S - 8), 8)
            return jnp.where(keep, ref[pl.ds(start, 8), :], 0.0)

        def chunk(c, dw):
            base = pl.multiple_of(c * T, T)
            rows = pl.ds(base, T)
            ext = jnp.concatenate([halo(x_ref, base - 8, c > 0), x_ref[rows, :], halo(x_ref, base + T, c < n - 1)], axis=0)
            dye = jnp.concatenate([jnp.zeros((8, tc), F32), dy_ref[rows, :], halo(dy_ref, base + T, c < n - 1)], axis=0)
            pre = _conv_taps(ext, w)
            sg = jax.nn.sigmoid(pre)
            dpre = dye * (sg * (1.0 + pre * (1.0 - sg)))
            dx = dpre * w[3:4]
            for j in range(1, 4):
                dx = dx + pltpu.roll(dpre, E - j, axis=0) * w[3 - j:4 - j]
            dx_ref[rows, :] = dx[8:8 + T].astype(dx_ref.dtype)
            own = dpre[8:8 + T]
            parts = [jnp.sum(own * pltpu.roll(ext, 3 - i, axis=0)[8:8 + T], axis=0, keepdims=True) if i < 3
                     else jnp.sum(own * ext[8:8 + T], axis=0, keepdims=True) for i in range(4)]
            return dw + jnp.concatenate(parts, axis=0)

        dw_ref[...] = lax.fori_loop(0, n, chunk, jnp.zeros((4, tc), F32))

    return pl.pallas_call(
        body, name=name, grid=(width // tc,),
        in_specs=[pl.BlockSpec((S, tc), lambda j: (0, j)), pl.BlockSpec((4, tc), lambda j: (0, j)),
                  pl.BlockSpec((None, S, tc), lambda j: (j // per, 0, j % per))],
        out_specs=[pl.BlockSpec((S, tc), lambda j: (0, j)), pl.BlockSpec((4, tc), lambda j: (0, j))],
        out_shape=[jax.ShapeDtypeStruct((S, width), BF16), jax.ShapeDtypeStruct((4, width), F32)],
        compiler_params=_cparams(("parallel",)),
    )(proj, conv_w, dy)


@functools.partial(jax.custom_vjp, nondiff_argnums=(0,))
def _bein(spec, a, b):
    return jnp.einsum(spec, a.astype(BF16), b.astype(BF16), preferred_element_type=F32)


def _bein_fwd(spec, a, b):
    return _bein(spec, a, b), (a, b)


def _bein_bwd(spec, res, g):
    a, b = res
    ins, out = spec.split("->")
    sa, sb = ins.split(",")
    return _bein(f"{out},{sb}->{sa}", g, b), _bein(f"{sa},{out}->{sb}", a, g)


_bein.defvjp(_bein_fwd, _bein_bwd)


def _hmm(a, b, precision=lax.Precision.HIGH):
    return jnp.einsum("ncs,nsd->ncd", a, b, precision=precision, preferred_element_type=F32)


def _inv_unit_lower(L):
    C = L.shape[-1]
    r = lax.broadcasted_iota(jnp.int32, L.shape, 1)
    c = lax.broadcasted_iota(jnp.int32, L.shape, 2)
    eye = (r == c).astype(F32)
    D = jnp.where(jnp.right_shift(r, 4) == jnp.right_shift(c, 4), L, 0.0)
    D2 = _hmm(D, D)
    D4 = _hmm(D2, D2)
    D8 = _hmm(D4, D4)
    dinv = _hmm(_hmm(_hmm(eye - D, eye + D2), eye + D4), eye + D8)
    N = _hmm(dinv, L - D)
    return _hmm(_hmm(eye - N, eye + _hmm(N, N)), dinv)


def _softplus(x):
    return jnp.maximum(x, 0.0) + jnp.log(1.0 + jnp.exp(-jnp.abs(x)))


def _gdn_local(qc, kc, vc, ab, a_log_row, dt_row, h, n_heads):
    R = qc.shape[0]
    C = GDN_CHUNK
    n = R // C
    lane = lax.broadcasted_iota(jnp.int32, (1, 128), 1)
    pick = lambda x, i: jnp.sum(jnp.where(lane == i, x, 0.0), axis=-1, keepdims=True)
    a_in, b_in = pick(ab, h).reshape(n, C, 1), pick(ab, n_heads + h).reshape(n, C, 1)
    a_log, dt_bias = pick(a_log_row, h), pick(dt_row, h)
    q3, k3, v = qc.reshape(n, C, 128), kc.reshape(n, C, 128), vc.reshape(n, C, 128)
    q = q3 * lax.rsqrt(jnp.sum(q3 * q3, axis=-1, keepdims=True) + EPS) * HEAD_DIM ** -0.5
    k = k3 * lax.rsqrt(jnp.sum(k3 * k3, axis=-1, keepdims=True) + EPS)
    beta = jax.nn.sigmoid(b_in)
    g = -jnp.exp(a_log) * _softplus(a_in + dt_bias)
    r = lax.broadcasted_iota(jnp.int32, (n, C, C), 1)
    c = lax.broadcasted_iota(jnp.int32, (n, C, C), 2)
    gc = _hmm((r >= c).astype(F32), jnp.broadcast_to(g, (n, C, C)), precision=HI)
    gcol = gc[:, :, 0:1]
    grow = jnp.sum(jnp.where(r == c, gc, 0.0), axis=1, keepdims=True)
    decay = jnp.exp(jnp.where(r >= c, gc - grow, -jnp.inf))
    kb, vb = k * beta, v * beta
    lmat = jnp.where(r > c, _bein("ncd,nsd->ncs", kb, k) * decay, 0.0)
    eg = jnp.exp(gcol)
    rhs = jnp.concatenate([vb, kb * eg], axis=-1)
    attn = jnp.where(r >= c, _bein("ncd,nsd->ncs", q, k) * decay, 0.0)
    glast = gcol[:, C - 1:C, :]
    k_dec = k * jnp.exp(glast - gcol)
    g_last = jnp.broadcast_to(jnp.exp(glast), (n, 1, 128))
    return lmat, rhs, attn.reshape(R, C), (q * eg).reshape(R, 128), k_dec.reshape(R, 128), g_last


def _gdn_rows(S):
    return _tile(S, (256, 128, 64))


def gdn_prep_fwd(qkv, proj, ab_col, a_log_row, dt_row, H, *, name):
    S = qkv.shape[0]
    R = _gdn_rows(S)
    n, nc = R // GDN_CHUNK, S // GDN_CHUNK

    def body(q_ref, k_ref, v_ref, ab_ref, al_ref, dt_ref, u_ref, w_ref, at_ref, qd_ref, kd_ref, gl_ref, t_ref):
        lmat, rhs, attn, qd, kd, gl = _gdn_local(q_ref[...], k_ref[...], v_ref[...], ab_ref[...], al_ref[...],
                                                 dt_ref[...], pl.program_id(1), H)
        tinv = _inv_unit_lower(lmat)
        sol = _hmm(tinv, rhs)
        outs = (sol[..., :128].reshape(R, 128), sol[..., 128:].reshape(R, 128), attn, qd, kd, gl,
                tinv.reshape(R, GDN_CHUNK))
        for r, v in zip((u_ref, w_ref, at_ref, qd_ref, kd_ref, gl_ref, t_ref), outs):
            r[...] = v

    head = lambda off: pl.BlockSpec((R, 128), lambda i, h: (i, off + h))
    row = pl.BlockSpec((1, 128), lambda i, h: (0, 0))
    big = jax.ShapeDtypeStruct((S, H * 128), F32)
    sq = pl.BlockSpec((None, R, GDN_CHUNK), lambda i, h: (h, i, 0))
    outs = pl.pallas_call(
        body, name=name, grid=(S // R, H),
        in_specs=[head(0), head(H), head(2 * H), pl.BlockSpec((R, 128), lambda i, h: (i, ab_col // 128)), row, row],
        out_specs=[head(0), head(0), sq, head(0), head(0), pl.BlockSpec((None, n, 1, 128), lambda i, h: (h, i, 0, 0)),
                   sq],
        out_shape=[big, big, jax.ShapeDtypeStruct((H, S, GDN_CHUNK), F32), big, big,
                   jax.ShapeDtypeStruct((H, nc, 1, 128), F32), jax.ShapeDtypeStruct((H, S, GDN_CHUNK), F32)],
        compiler_params=_cparams(("parallel", "parallel")),
    )(qkv, qkv, qkv, proj, a_log_row, dt_row)
    return tuple(outs[:6]), outs[6]


def gdn_prep_bwd(qkv, proj, ab_col, a_log_row, dt_row, H, u, w, tinv, cts, *, name):
    S = qkv.shape[0]
    R = _gdn_rows(S)
    C = GDN_CHUNK
    n, nc = R // C, S // C

    def body(q_ref, k_ref, v_ref, ab_ref, al_ref, dt_ref, u_ref, w_ref, t_ref,
             du_ref, dw_ref, dat_ref, dqd_ref, dkd_ref, dgl_ref,
             dqkv_ref, dab_ref, dal_ref, ddt_ref):
        i, h = pl.program_id(0), pl.program_id(1)

        @pl.when(h == 0)
        def _():
            dab_ref[...] = jnp.zeros_like(dab_ref)

        @pl.when((h == 0) & (i == 0))
        def _():
            dal_ref[...] = jnp.zeros_like(dal_ref)
            ddt_ref[...] = jnp.zeros_like(ddt_ref)

        tinv_t = jnp.swapaxes(t_ref[...].reshape(n, C, C), 1, 2)
        dsol = jnp.concatenate([du_ref[...], dw_ref[...]], axis=-1).reshape(n, C, 256)
        sol = jnp.concatenate([u_ref[...], w_ref[...]], axis=-1).reshape(n, C, 256)
        drhs = _hmm(tinv_t, dsol)
        dlmat = -jnp.einsum("ncd,nsd->ncs", drhs, sol, precision=lax.Precision.HIGH, preferred_element_type=F32)
        f = lambda q, k, v, ab, al, dt: _gdn_local(q, k, v, ab, al, dt, h, H)
        _, vjp = jax.vjp(f, q_ref[...], k_ref[...], v_ref[...], ab_ref[...], al_ref[...], dt_ref[...])
        dq, dk, dv, dab, dal, ddt = vjp((dlmat, drhs, dat_ref[...], dqd_ref[...], dkd_ref[...], dgl_ref[...]))
        dqkv_ref[0] = dq
        dqkv_ref[1] = dk
        dqkv_ref[2] = dv
        dab_ref[...] += dab
        dal_ref[...] += dal
        ddt_ref[...] += ddt

    head = lambda off: pl.BlockSpec((R, 128), lambda i, h: (i, off + h))
    row = pl.BlockSpec((1, 128), lambda i, h: (0, 0))
    at = pl.BlockSpec((None, R, GDN_CHUNK), lambda i, h: (h, i, 0))
    gl = pl.BlockSpec((None, n, 1, 128), lambda i, h: (h, i, 0, 0))
    W = H * 128
    return pl.pallas_call(
        body, name=name, grid=(S // R, H),
        in_specs=[head(0), head(H), head(2 * H), pl.BlockSpec((R, 128), lambda i, h: (i, ab_col // 128)), row, row,
                  head(0), head(0), at, head(0), head(0), at, head(0), head(0), gl],
        out_specs=[pl.BlockSpec((3, R, 128), lambda i, h: (0, i, h)), pl.BlockSpec((R, 128), lambda i, h: (i, 0)),
                   row, row],
        out_shape=[jax.ShapeDtypeStruct((3, S, W), F32), jax.ShapeDtypeStruct((S, 128), F32)]
        + [jax.ShapeDtypeStruct((1, 128), F32)] * 2,
        compiler_params=_cparams(("arbitrary", "arbitrary")),
    )(qkv, qkv, qkv, proj, a_log_row, dt_row, u, w, tinv, *cts)


def gdn_scan_fwd(u, w, attn, qd, kd, gl, *, name):
    S = u.shape[0]
    H = attn.shape[0]
    C = GDN_CHUNK
    R = _gdn_rows(S)
    n, nc = R // C, S // C

    def body(u_ref, w_ref, at_ref, qd_ref, kd_ref, gl_ref, o_ref, st_ref, state):
        @pl.when(pl.program_id(1) == 0)
        def _():
            state[...] = jnp.zeros_like(state)

        for c in range(n):
            rows = slice(c * C, (c + 1) * C)
            s = state[...]
            st_ref[c] = s
            sb = s.astype(BF16)
            v_new = u_ref[rows, :] - _dot(w_ref[rows, :].astype(BF16), sb)
            vb = v_new.astype(BF16)
            o_ref[rows, :] = _dot(qd_ref[rows, :].astype(BF16), sb) + _dot(at_ref[rows, :].astype(BF16), vb)
            state[...] = s * gl_ref[c] + _dot(kd_ref[rows, :].astype(BF16), vb, 0, 0)

    head = pl.BlockSpec((R, 128), lambda h, i: (i, h))
    return pl.pallas_call(
        body, name=name, grid=(H, S // R),
        in_specs=[head, head, pl.BlockSpec((None, R, C), lambda h, i: (h, i, 0)), head, head,
                  pl.BlockSpec((None, n, 1, 128), lambda h, i: (h, i, 0, 0))],
        out_specs=[head, pl.BlockSpec((None, n, 128, 128), lambda h, i: (h, i, 0, 0))],
        out_shape=[jax.ShapeDtypeStruct((S, H * 128), F32), jax.ShapeDtypeStruct((H, nc, 128, 128), F32)],
        scratch_shapes=[pltpu.VMEM((128, 128), F32)],
        compiler_params=_cparams(("parallel", "arbitrary")),
    )(u, w, attn, qd, kd, gl)


def gdn_scan_bwd(u, w, attn, qd, kd, gl, states, do, *, name):
    S = u.shape[0]
    H = attn.shape[0]
    C = GDN_CHUNK
    R = _gdn_rows(S)
    n, nc, nb = R // C, S // C, S // R

    def body(u_ref, w_ref, at_ref, qd_ref, kd_ref, gl_ref, st_ref, do_ref,
             du_ref, dw_ref, dat_ref, dqd_ref, dkd_ref, dgl_ref, dstate):
        @pl.when(pl.program_id(1) == 0)
        def _():
            dstate[...] = jnp.zeros_like(dstate)

        lane = lax.broadcasted_iota(jnp.int32, (1, 128), 1)
        for c in reversed(range(n)):
            rows = slice(c * C, (c + 1) * C)
            s = st_ref[c]
            sb = s.astype(BF16)
            ds2 = dstate[...]
            ds2b = ds2.astype(BF16)
            wb, qdb, kdb, atb = (r[rows, :].astype(BF16) for r in (w_ref, qd_ref, kd_ref, at_ref))
            dob = do_ref[rows, :].astype(BF16)
            v_new = u_ref[rows, :] - _dot(wb, sb)
            vb = v_new.astype(BF16)
            dv = _dot(atb, dob, 0, 0) + _dot(kdb, ds2b)
            dvb = dv.astype(BF16)
            du_ref[rows, :] = dv
            dw_ref[rows, :] = -_dot(dvb, sb, 1, 1)
            dat_ref[rows, :] = _dot(dob, vb, 1, 1)
            dqd_ref[rows, :] = _dot(dob, sb, 1, 1)
            dkd_ref[rows, :] = _dot(vb, ds2b, 1, 1)
            dgl = jnp.sum(jnp.sum(ds2 * s, axis=1, keepdims=True), axis=0, keepdims=True)
            dgl_ref[c] = jnp.where(lane == 0, dgl, 0.0)
            dstate[...] = ds2 * gl_ref[c] + _dot(qdb, dob, 0, 0) - _dot(wb, dvb, 0, 0)

    head = pl.BlockSpec((R, 128), lambda h, i: (nb - 1 - i, h))
    at = pl.BlockSpec((None, R, C), lambda h, i: (h, nb - 1 - i, 0))
    glb = pl.BlockSpec((None, n, 1, 128), lambda h, i: (h, nb - 1 - i, 0, 0))
    big = jax.ShapeDtypeStruct((S, H * 128), F32)
    return pl.pallas_call(
        body, name=name, grid=(H, nb),
        in_specs=[head, head, at, head, head, glb,
                  pl.BlockSpec((None, n, 128, 128), lambda h, i: (h, nb - 1 - i, 0, 0)), head],
        out_specs=[head, head, at, head, head, glb],
        out_shape=[big, big, jax.ShapeDtypeStruct((H, S, C), F32), big, big,
                   jax.ShapeDtypeStruct((H, nc, 1, 128), F32)],
        scratch_shapes=[pltpu.VMEM((128, 128), F32)],
        compiler_params=_cparams(("parallel", "arbitrary")),
    )(u, w, attn, qd, kd, gl, states, do)


def _head_rms(o):
    return lax.rsqrt(jnp.mean(o * o, axis=-1, keepdims=True) + EPS)


def gdn_mixer_fwd(proj, W, p, tag):
    H = W // HEAD_DIM
    ab_col = 4 * W + MEM_WIDTH
    qkv = gdn_conv_fwd(proj, 3 * W, p["conv_w"], name=f"gdn_conv_fwd_{tag}")
    al_row, dt_row = _pad_row(p["a_log"]), _pad_row(p["dt_bias"])
    pre, tinv = gdn_prep_fwd(qkv, proj, ab_col, al_row, dt_row, H, name=f"gdn_prep_fwd_{tag}")
    o, states = gdn_scan_fwd(*pre, name=f"gdn_scan_fwd_{tag}")
    gn_row = jnp.tile(p["o_norm"].reshape(1, HEAD_DIM), (1, H))

    def gate_fwd(o, gate, gn):
        return o * _head_rms(o) * gn * (gate * jax.nn.sigmoid(gate))

    mix = ew(gate_fwd, [(o, 0), (proj, 3 * W)], [gn_row], [BF16], 0, width=W, tc=HEAD_DIM, name=f"gdn_gate_fwd_{tag}")
    return mix, (qkv, pre, tinv, states, o, gn_row, al_row, dt_row)


def gdn_mixer_bwd(proj, W, p, saved, dcat, tag):
    qkv, pre, tinv, states, o, gn_row, al_row, dt_row = saved
    H = W // HEAD_DIM
    ab_col = 4 * W + MEM_WIDTH

    def gate_bwd(dm, o, gate, gn):
        dm = dm.astype(F32)
        r = _head_rms(o)
        xh = o * r
        sg = jax.nn.sigmoid(gate)
        dy = dm * gate * sg
        dgate = dm * xh * gn * (sg * (1.0 + gate * (1.0 - sg)))
        dxh = dy * gn
        do = r * (dxh - xh * jnp.mean(dxh * xh, axis=-1, keepdims=True))
        return do, dgate, dy * xh

    do, dgate, dgn = ew(gate_bwd, [(dcat, 0), (o, 0), (proj, 3 * W)], [gn_row], [F32, BF16], 1, width=W, tc=HEAD_DIM,
                        name=f"gdn_gate_bwd_{tag}")
    cts = gdn_scan_bwd(*pre, states, do, name=f"gdn_scan_bwd_{tag}")
    dqkv, dab, dal, ddt = gdn_prep_bwd(qkv, proj, ab_col, al_row, dt_row, H, pre[0], pre[1], tinv, cts,
                                       name=f"gdn_prep_bwd_{tag}")
    dx, dconv = gdn_conv_bwd(proj, 3 * W, p["conv_w"], dqkv, name=f"gdn_conv_bwd_{tag}")
    grads = dict(conv_w=dconv, a_log=dal[0, :H], dt_bias=ddt[0, :H], o_norm=dgn.reshape(H, HEAD_DIM).sum(axis=0))
    return dx, dgate, dab, grads


def exchange(arrays, same_block, *, name):
    n = len(arrays)

    def body(*refs):
        ex = _Exchange(refs[:n], refs[n:2 * n], same_block, *refs[2 * n:])
        ex.start()
        ex.finish()

    shapes, sems = _exchange_shapes(arrays, same_block)
    any_spec = pl.BlockSpec(memory_space=pl.ANY)
    return pl.pallas_call(body, name=name, in_specs=[any_spec] * n, out_specs=[any_spec] * n, out_shape=shapes,
                          scratch_shapes=sems)(*arrays)


def _row_tile(R, row_bytes):
    for t in (512, 256, 128, 64, 32, 16, 8):
        if R % t == 0 and 2 * t * row_bytes <= 24 * 2 ** 20:
            return t
    return R


def adamw(parts, w, m, v, *, name):
    P, R, C = parts.shape
    tr = _row_tile(R, C * (P * parts.dtype.itemsize + 7 * 4))
    c1, c2 = 1.0 - ADAM_B1 ** ADAM_STEP, 1.0 - ADAM_B2 ** ADAM_STEP

    def body(p_ref, w_ref, m_ref, v_ref, g_ref, d_ref, nm_ref, nv_ref):
        g = p_ref[0].astype(F32)
        for s in range(1, P):
            g = g + p_ref[s].astype(F32)
        m = ADAM_B1 * m_ref[...] + (1.0 - ADAM_B1) * g
        v = ADAM_B2 * v_ref[...] + (1.0 - ADAM_B2) * (g * g)
        g_ref[...] = g
        nm_ref[...] = m
        nv_ref[...] = v
        d_ref[...] = -ADAM_LR * ((m / c1) / (jnp.sqrt(v / c2) + ADAM_EPS) + ADAM_WD * w_ref[...])

    blk = pl.BlockSpec((tr, C), lambda i: (i, 0))
    return pl.pallas_call(
        body, name=name, grid=(R // tr,),
        in_specs=[pl.BlockSpec((P, tr, C), lambda i: (0, i, 0)), blk, blk, blk],
        out_specs=[blk] * 4, out_shape=[jax.ShapeDtypeStruct((R, C), F32)] * 4,
        compiler_params=_cparams(("parallel",)),
    )(parts, w, m, v)


def sum_parts(parts, *, name):
    P, R, C = parts.shape
    tr = _row_tile(R, C * (P * parts.dtype.itemsize + 4))

    def body(p_ref, o_ref):
        g = p_ref[0].astype(F32)
        for s in range(1, P):
            g = g + p_ref[s].astype(F32)
        o_ref[...] = g

    return pl.pallas_call(
        body, name=name, grid=(R // tr,),
        in_specs=[pl.BlockSpec((P, tr, C), lambda i: (0, i, 0))], out_specs=pl.BlockSpec((tr, C), lambda i: (i, 0)),
        out_shape=jax.ShapeDtypeStruct((R, C), F32), compiler_params=_cparams(("parallel",)),
    )(parts)


PACK_COLS = 1024
PACK_ROWS = 256


def _pack(arrays, cols, lead=()):
    n_lead = len(lead)
    flat = [a.reshape(lead + (-1,)) for a in arrays]
    total = sum(f.shape[-1] for f in flat)
    rows = -(-total // cols)
    mult = PACK_ROWS if rows > PACK_ROWS else 8
    rows = -(-rows // mult) * mult
    pad = rows * cols - total
    if pad:
        flat.append(jnp.zeros(lead + (pad,), flat[0].dtype))
    return jnp.concatenate(flat, axis=n_lead).reshape(lead + (rows, cols))


def _unpack(buf, shapes, lead=()):
    flat = buf.reshape(lead + (-1,))
    out, off = [], 0
    for s in shapes:
        n = math.prod(s)
        out.append(lax.slice_in_dim(flat, off, off + n, axis=len(lead)).reshape(lead + tuple(s)))
        off += n
    return out


def _to_shards(full, axis):
    s = full.shape
    return jnp.moveaxis(full.reshape(s[:axis] + (N_DEV, s[axis] // N_DEV) + s[axis + 1:]), axis, 0)


def _from_shards(g, axis):
    m = jnp.moveaxis(g, 0, axis)
    s = m.shape
    return m.reshape(s[:axis] + (s[axis] * s[axis + 1],) + s[axis + 2:])


BIG = (("w_mem_kv", 0), ("w_out", 1), ("w_up", 2), ("w_down", 1), ("s5_w_in", 1), ("s5_w_glu", 1),
       ("gdn_w_in", 2), ("fox_w_in", 1))
SMALL_SHARDED = (("s5_d_skip", 1), ("s5_b_glu", 1), ("gdn_conv_w", 2))
REPLICATED = ("mem_norm", "norm1", "norm2", "norm_f", "s5_lam_re", "s5_lam_im", "s5_log_dt", "s5_b_re", "s5_b_im",
              "s5_c_re", "s5_c_im", "gdn_a_log", "gdn_dt_bias", "gdn_o_norm", "fox_b_f")
WEIGHTS = ("mem_norm", "w_mem_kv", "norm1", "w_out", "norm2", "w_up", "w_down", "norm_f", "s5_w_in", "s5_lam_re",
           "s5_lam_im", "s5_log_dt", "s5_b_re", "s5_b_im", "s5_c_re", "s5_c_im", "s5_d_skip", "s5_w_glu", "s5_b_glu",
           "gdn_w_in", "gdn_conv_w", "gdn_a_log", "gdn_dt_bias", "gdn_o_norm", "fox_w_in", "fox_b_f")


def _relu2(acc):
    r = jnp.maximum(acc, 0.0)
    return acc, r * r


def _relu2_grad(acc, u):
    return (acc * 2.0 * jnp.maximum(u, 0.0),)


def _add(acc, e):
    return (acc + e,)


def _permute_in(w, kind, W):
    if kind == 0:
        return w
    n_main = (4 if kind == 1 else 3) * W
    n_small = w.shape[1] - n_main - MEM_WIDTH
    small = jnp.pad(w[:, n_main:n_main + n_small], ((0, 0), (0, MEM_WIDTH - n_small)))
    return jnp.concatenate([w[:, :n_main], w[:, n_main + n_small:], small], axis=1)


def _unpermute_in(dw, kind, W, n_small):
    if kind == 0:
        return dw
    n_main = (4 if kind == 1 else 3) * W
    return jnp.concatenate([dw[:, :n_main], dw[:, n_main + MEM_WIDTH:n_main + MEM_WIDTH + n_small],
                            dw[:, n_main:n_main + MEM_WIDTH]], axis=1)


def kernel(x, mem, mem_norm, w_mem_kv, norm1, w_out, norm2, w_up, w_down, norm_f, s5_w_in, s5_lam_re, s5_lam_im, s5_log_dt, s5_b_re, s5_b_im, s5_c_re, s5_c_im, s5_d_skip, s5_w_glu, s5_b_glu, gdn_w_in, gdn_conv_w, gdn_a_log, gdn_dt_bias, gdn_o_norm, fox_w_in, fox_b_f, loss_target, m_mem_norm, m_w_mem_kv, m_norm1, m_w_out, m_norm2, m_w_up, m_w_down, m_norm_f, m_s5_w_in, m_s5_lam_re, m_s5_lam_im, m_s5_log_dt, m_s5_b_re, m_s5_b_im, m_s5_c_re, m_s5_c_im, m_s5_d_skip, m_s5_w_glu, m_s5_b_glu, m_gdn_w_in, m_gdn_conv_w, m_gdn_a_log, m_gdn_dt_bias, m_gdn_o_norm, m_fox_w_in, m_fox_b_f, v_mem_norm, v_w_mem_kv, v_norm1, v_w_out, v_norm2, v_w_up, v_w_down, v_norm_f, v_s5_w_in, v_s5_lam_re, v_s5_lam_im, v_s5_log_dt, v_s5_b_re, v_s5_b_im, v_s5_c_re, v_s5_c_im, v_s5_d_skip, v_s5_w_glu, v_s5_b_glu, v_gdn_w_in, v_gdn_conv_w, v_gdn_a_log, v_gdn_dt_bias, v_gdn_o_norm, v_fox_w_in, v_fox_b_f):
    args = dict(locals())
    wsh = {n: args[n] for n in WEIGHTS}
    msh = {n: args["m_" + n] for n in WEIGHTS}
    vsh = {n: args["v_" + n] for n in WEIGHTS}
    h0, memx, target = x[0], mem[0], loss_target[0]
    S, D = h0.shape
    W = D - MEM_WIDTH
    depth = norm1.shape[0]
    me = 4 * lax.axis_index("x") + 2 * lax.axis_index("y") + lax.axis_index("c")

    bf = lambda t: t.astype(BF16)
    rows2d = lambda g: g.reshape(-1, g.shape[-1])

    def layer_sends(i, group):
        kind, j = i % 3, i // 3
        if group == "up":
            return {"w_up": bf(wsh["w_up"][i])}
        if group == "down":
            return {"w_down": bf(wsh["w_down"][i])}
        d = {"w_out": bf(wsh["w_out"][i])}
        if kind == 0:
            d["w_in"], d["w_glu"] = bf(wsh["s5_w_in"][j]), bf(wsh["s5_w_glu"][j])
        elif kind == 1:
            d["w_in"] = bf(wsh["gdn_w_in"][j])
        else:
            d["w_in"] = bf(_permute_in(wsh["fox_w_in"][j], 2, W))
        return d

    def as_comm(d):
        return [(v, True) for v in d.values()]

    first = {**layer_sends(0, "in"), **layer_sends(0, "up"), **layer_sends(0, "down")}
    small_w = _pack([wsh[n] for n, _ in SMALL_SHARDED], 128)
    got = exchange(list(first.values()) + [bf(w_mem_kv), small_w], [True] * (len(first) + 2), name="gather_first")
    gathered = [dict(zip(first, got)) if i == 0 else {} for i in range(depth)]
    full = {n: wsh[n] for n in REPLICATED}
    full["w_mem_kv"] = rows2d(got[len(first)])
    for (n, ax), g in zip(SMALL_SHARDED, _unpack(got[-1], [wsh[n].shape for n, _ in SMALL_SHARDED], lead=(N_DEV,))):
        full[n] = _from_shards(g, ax)

    def layer_params(i):
        kind, j = i % 3, i // 3
        g = gathered[i]
        w = {n: rows2d(g[n]) for n in ("w_out", "w_down", "w_glu") if n in g}
        w["w_up"] = g["w_up"]
        w["w_in"] = _permute_in(_from_shards(g["w_in"], 1), 1, W) if kind == 1 else rows2d(g["w_in"])
        if kind == 0:
            p = {k: full["s5_" + k][j] for k in ("lam_re", "lam_im", "log_dt", "b_re", "b_im", "c_re", "c_im",
                                                   "d_skip", "b_glu")}
            return kind, j, p, w, 0
        if kind == 1:
            p = {k: full["gdn_" + k][j] for k in ("conv_w", "a_log", "dt_bias", "o_norm")}
            return kind, j, p, w, 2 * (W // HEAD_DIM)
        return kind, j, {"b_f": full["fox_b_f"][j]}, w, W // HEAD_DIM

    def hosted(i, group):
        if i + 1 >= depth:
            return {}, []
        d = layer_sends(i + 1, group)
        return {"comm": as_comm(d)}, list(d)

    def keep(i, names, res):
        if not names:
            return res
        outs, got = res
        gathered[i + 1].update(zip(names, got))
        return outs

    memn = rms_fwd(memx, full["mem_norm"], out_dtype=BF16, name="mem_rms")
    mkv = mm(memn, full["w_mem_kv"], name="mem_kv")
    h = h0
    saved = []
    weights = []
    for i in range(depth):
        kind, j, p, w, n_small = layer_params(i)
        weights.append((w, p, n_small))
        a = rms_fwd(h, full["norm1"][i], out_dtype=BF16, name=f"rms1_{i}")
        kw, names = hosted(i, "in")
        proj = keep(i, names, mm(a, w["w_in"], name=f"in_proj_{i}", **kw))
        if kind == 0:
            mix, ms = s5_mixer_fwd(proj, W, p, w["w_glu"], f"l{i}")
            mem_col = W
        elif kind == 1:
            mix, ms = gdn_mixer_fwd(proj, W, p, f"l{i}")
            mem_col = 4 * W
        else:
            mix, ms = fox_mixer_fwd(proj, 3 * W + MEM_WIDTH, p["b_f"], f"l{i}")
            mem_col = 3 * W
        read = mem_fwd(proj, mem_col, mkv, name=f"mem_fwd_{i}")
        cat = jnp.concatenate([mix, read], axis=1)
        h_mid = mm(cat, w["w_out"], extras=(h,), epilogue=_add, name=f"out_proj_{i}")
        a2 = rms_fwd(h_mid, full["norm2"][i], out_dtype=BF16, name=f"rms2_{i}")
        kw, names = hosted(i, "up")
        u, act = keep(i, names, mm(a2, w["w_up"], b_shard8=True, epilogue=_relu2, out_dtypes=(F32, BF16),
                                   name=f"up_{i}", **kw))
        kw, names = hosted(i, "down")
        h_next = keep(i, names, mm(act, w["w_down"], extras=(h_mid,), epilogue=_add, name=f"down_{i}", **kw))
        saved.append((h, a, proj, ms, mem_col, cat, h_mid, a2, u, act))
        h = h_next

    loss_row, dh, dnf = final_loss(h, full["norm_f"], target, name="final_loss")
    loss = lax.psum(jnp.sum(loss_row), ("x", "y", "c"))

    grads = {n: [None] * full[n].shape[0] for n in ("norm1", "norm2")}
    for pre, cnt in (("s5_", (depth + 2) // 3), ("gdn_", (depth + 1) // 3), ("fox_", depth // 3)):
        for n in REPLICATED + tuple(n for n, _ in SMALL_SHARDED):
            if n.startswith(pre):
                grads[n] = [None] * cnt
    shares = {}
    by_dest = lambda g: g.reshape((N_DEV, g.shape[0] // N_DEV) + g.shape[1:])
    pending_up, pending_rest = {}, {}

    def carry(pending):
        return {"comm": [(v, False) for v in pending.values()]} if pending else {}

    def landed(pending, res):
        if not pending:
            return res
        outs, got = res
        shares.update(zip(pending, got))
        return outs

    dmkv = None
    for i in reversed(range(depth)):
        kind, j = i % 3, i // 3
        w, p, n_small = weights[i]
        h_in, a, proj, ms, mem_col, cat, h_mid, a2, u, act = saved[i]
        du = landed(pending_up, mm(dh, w["w_down"], tb=True, extras=(u,), epilogue=_relu2_grad, out_dtypes=(BF16,),
                                   name=f"d_act_{i}", **carry(pending_up)))
        dw_down = landed(pending_rest, mm(act, dh, ta=True, out_dtypes=(BF16,), name=f"dw_down_{i}",
                                          **carry(pending_rest)))
        pending = {("w_down", i): by_dest(dw_down)}
        da2 = landed(pending, mm(du, w["w_up"], tb=True, b_shard8=True, name=f"d_a2_{i}", **carry(pending)))
        pending_up = {("w_up", i): mm(a2, du, ta=True, out_shard8=True, out_dtypes=(BF16,), name=f"dw_up_{i}")}
        dh_mid, dn2 = rms_bwd(h_mid, full["norm2"][i], da2, dh, name=f"rms2_bwd_{i}")
        grads["norm2"][i] = dn2[0]
        dcat = mm(dh_mid, w["w_out"], tb=True, name=f"d_cat_{i}")
        pending_rest = {("w_out", i): by_dest(mm(cat, dh_mid, ta=True, out_dtypes=(BF16,), name=f"dw_out_{i}"))}
        dq_mem, dmkv_i = mem_bwd(proj, mem_col, mkv, dcat, W, name=f"mem_bwd_{i}")
        dmkv = dmkv_i if dmkv is None else dmkv + dmkv_i
        if kind == 0:
            dmain, g = s5_mixer_bwd(proj, W, p, w["w_glu"], ms, dcat, f"l{i}")
            dproj = jnp.concatenate([dmain, dq_mem], axis=1)
            pending_rest[("s5_w_glu", j)] = by_dest(g.pop("w_glu"))
            for k, val in g.items():
                grads["s5_" + k][j] = val
        elif kind == 1:
            dx, dgate, dab, g = gdn_mixer_bwd(proj, W, p, ms, dcat, f"l{i}")
            dproj = jnp.concatenate([dx, dgate, dq_mem, bf(jnp.pad(dab, ((0, 0), (0, MEM_WIDTH - 128))))], axis=1)
            for k, val in g.items():
                grads["gdn_" + k][j] = val
        else:
            dq, dk, dv, df, db = fox_mixer_bwd(proj, 3 * W + MEM_WIDTH, ms, dcat, f"l{i}")
            dproj = jnp.concatenate([dq, dk, dv, dq_mem.astype(F32), jnp.pad(df, ((0, 0), (0, MEM_WIDTH - 128)))],
                                    axis=1)
            grads["fox_b_f"][j] = db
        da = mm(dproj, w["w_in"], tb=True, name=f"d_a_{i}")
        in_name = ("s5_w_in", "gdn_w_in", "fox_w_in")[kind]
        if kind == 1:
            dw_in = _unpermute_in(mm(a, dproj, ta=True, name=f"dw_in_{i}"), 1, W, n_small)
            pending_rest[(in_name, j)] = bf(_to_shards(dw_in, 1))
        else:
            pending_rest[(in_name, j)] = by_dest(mm(a, dproj, ta=True, out_dtypes=(BF16,), name=f"dw_in_{i}"))
        dh, dn1 = rms_bwd(h_in, full["norm1"][i], da, dh_mid, name=f"rms1_bwd_{i}")
        grads["norm1"][i] = dn1[0]
    grad_x = dh[None]
    grads = {n: jnp.stack(v) for n, v in grads.items()}
    grads["norm_f"] = dnf[0]
    pending_rest[("w_mem_kv", None)] = by_dest(mm(memn, dmkv, ta=True, out_dtypes=(BF16,), name="dw_mem_kv"))
    dmemn = mm(dmkv, full["w_mem_kv"], tb=True, name="d_memn")
    _, dmn = rms_bwd(memx, full["mem_norm"], dmemn, None, name="mem_rms_bwd")
    grads["mem_norm"] = dmn[0]

    small_names = list(REPLICATED) + [n for n, _ in SMALL_SHARDED]
    last = {**pending_up, **pending_rest}
    got = exchange(list(last.values()) + [_pack([grads[n] for n in small_names], 128)],
                   [False] * len(last) + [True], name="exchange_last")
    shares.update(zip(last, got))

    big_out = [{}, {}, {}, {}]
    for n, _ in BIG:
        per_layer = []
        for idx in ([None] if n == "w_mem_kv" else range(wsh[n].shape[0])):
            key = (n, idx)
            local = [t[n] if idx is None else t[n][idx] for t in (wsh, msh, vsh)]
            part = shares[key]
            if n == "fox_w_in":
                total = sum_parts(part, name=f"sum_{n}_{idx}")
                part = _unpermute_in(total, 2, W, W // HEAD_DIM)[None]
            per_layer.append(adamw(part, *local, name=f"adamw_{n}_{idx}"))
        for o, vals in zip(big_out, zip(*per_layer)):
            o[n] = vals[0] if n == "w_mem_kv" else jnp.stack(vals)

    total = sum_parts(got[-1], name="sum_small_grads")
    gsmall = dict(zip(small_names, _unpack(total, [grads[n].shape for n in small_names])))
    for n, ax in SMALL_SHARDED:
        width = wsh[n].shape[ax]
        gsmall[n] = lax.dynamic_slice_in_dim(gsmall[n], me * width, width, axis=ax)
    outs = adamw(_pack([gsmall[n] for n in small_names], 128)[None],
                 *[_pack([t[n] for n in small_names], 128) for t in (wsh, msh, vsh)], name="adamw_small")
    small_out = [dict(zip(small_names, _unpack(o, [wsh[n].shape for n in small_names]))) for o in outs]

    res = [{**b, **s} for b, s in zip(big_out, small_out)]
    return (loss, grad_x, *[r[n] for r in res for n in WEIGHTS])
```

```python
import functools
import math

import jax
import jax.numpy as jnp
from jax import lax
from jax.experimental import pallas as pl
from jax.experimental.pallas import tpu as pltpu

F32 = jnp.float32
BF16 = jnp.bfloat16

HEAD_DIM = 128
MEM_HEADS = 4
MEM_WIDTH = MEM_HEADS * HEAD_DIM
S5_GROUP = 16
S5_STATE = 64
GDN_CHUNK = 64
EPS = 1e-6
ADAM_LR, ADAM_B1, ADAM_B2, ADAM_EPS, ADAM_WD, ADAM_STEP = 0.001, 0.9, 0.999, 1e-08, 0.01, 10

N_DEV = 8
MESH = pl.DeviceIdType.MESH
VMEM_LIMIT = 56 * 1024 * 1024
HI = lax.Precision.HIGHEST


def _tile(n, prefs=(1024, 512, 256, 128)):
    for t in prefs:
        if n % t == 0:
            return t
    return n


def _cparams(sem=None):
    return pltpu.CompilerParams(dimension_semantics=sem, vmem_limit_bytes=VMEM_LIMIT)


def _dot(a, b, ca=1, cb=0, precision=None):
    return lax.dot_general(a, b, (((ca,), (cb,)), ((), ())), preferred_element_type=F32, precision=precision)


def _bdot(a, b):
    return _dot(a.astype(BF16), b.astype(BF16))


SIBLING = 1
OTHER_CHIPS = (2, 4, 6)


class _Exchange:
    def __init__(self, p_refs, out_refs, same_block, send_sems, recv_sems, local_sems):
        self.pos = (lax.axis_index("x"), lax.axis_index("y"), lax.axis_index("c"))
        self.arrays = list(zip(p_refs, out_refs, same_block))
        self.sems = (send_sems, recv_sems, local_sems)

    def _dev(self, m):
        return tuple(1 - v if (m >> (2 - b)) & 1 else v for b, v in enumerate(self.pos))

    def _slot(self, m):
        d = self._dev(m)
        return 4 * d[0] + 2 * d[1] + d[2]

    def _copy(self, n, src, slot, sem, to):
        k = n * (N_DEV - 1) + sem - 1
        return pltpu.make_async_remote_copy(
            src_ref=src, dst_ref=self.arrays[n][1].at[slot], send_sem=self.sems[0].at[k], recv_sem=self.sems[1].at[k],
            device_id=self._dev(to), device_id_type=MESH)

    def _local(self, n):
        p_ref, out_ref, same = self.arrays[n]
        return pltpu.make_async_copy(p_ref if same else p_ref.at[self._slot(0)], out_ref.at[self._slot(0)],
                                     self.sems[2].at[n])

    def _sends(self, n):
        p_ref, _, same = self.arrays[n]
        if same:
            return [self._copy(n, p_ref, self._slot(0), m, m) for m in (SIBLING,) + OTHER_CHIPS]
        return [self._copy(n, p_ref.at[self._slot(m)], self._slot(0), m, m) for m in range(1, N_DEV)]

    def _passed_on(self, n, m):
        return self._copy(n, self.arrays[n][1].at[self._slot(m)], self._slot(m), m ^ SIBLING, SIBLING)

    def _arrival(self, n, m):
        return self._copy(n, self.arrays[n][1].at[self._slot(m)], self._slot(m), m, m)

    def start(self):
        for n in range(len(self.arrays)):
            self._local(n).start()
            for cp in self._sends(n):
                cp.start()

    def finish(self):
        for n, (_, _, same) in enumerate(self.arrays):
            if same:
                for m in OTHER_CHIPS:
                    self._arrival(n, m).wait_recv()
                    self._passed_on(n, m).start()
        for n, (_, _, same) in enumerate(self.arrays):
            if same:
                for m in OTHER_CHIPS:
                    self._arrival(n, m ^ SIBLING).wait_recv()
                    self._passed_on(n, m).wait_send()
                self._arrival(n, SIBLING).wait_recv()
                for cp in self._sends(n):
                    cp.wait_send()
            else:
                for cp in self._sends(n):
                    cp.wait()
            self._local(n).wait()


def _exchange_shapes(arrays, same_block):
    shapes = [jax.ShapeDtypeStruct((N_DEV,) + tuple(p.shape if same else p.shape[1:]), p.dtype)
              for p, same in zip(arrays, same_block)]
    n = len(arrays)
    sems = [pltpu.SemaphoreType.DMA((n * (N_DEV - 1),)), pltpu.SemaphoreType.DMA((n * (N_DEV - 1),)),
            pltpu.SemaphoreType.DMA((n,))]
    return shapes, sems


def _mm_tk(K, tm, tn, a_bytes, b_bytes, out_bytes):
    for tk in (2048, 1024, 512, 256, 128):
        if K % tk == 0 and 2 * tk * (tm * a_bytes + tn * b_bytes) + (2 * out_bytes + 4) * tm * tn <= 40 * 2 ** 20:
            return tk
    return K


def mm(a, b, *, ta=False, tb=False, extras=(), epilogue=None, out_dtypes=(F32,), name, b_shard8=False,
       out_shard8=False, comm=()):
    M, K = (a.shape[1], a.shape[0]) if ta else a.shape
    if b_shard8:
        brows, bcols = b.shape[1], b.shape[2] * N_DEV
    else:
        brows, bcols = b.shape
    N = brows if tb else bcols
    assert K == (bcols if tb else brows), (a.shape, b.shape, ta, tb)
    tm = _tile(M)
    tn = _tile(N // N_DEV if (out_shard8 or (b_shard8 and not tb)) else N)
    out_bytes = sum(jnp.dtype(d).itemsize for d in out_dtypes) + 4 * sum(e.shape[0] != 1 for e in extras)
    tk = _mm_tk(K // N_DEV if (b_shard8 and tb) else K, tm, tn, a.dtype.itemsize, b.dtype.itemsize, out_bytes)
    ni, nj, nk = M // tm, N // tn, K // tk
    n_ex, n_out, n_comm = len(extras), len(out_dtypes), len(comm)
    ca, cb = (0 if ta else 1), (1 if tb else 0)
    same_block = [s for _, s in comm]

    def body(a_ref, b_ref, *rest):
        ex_refs, rest = rest[:n_ex], rest[n_ex:]
        cin, rest = rest[:n_comm], rest[n_comm:]
        out_refs, rest = rest[:n_out], rest[n_out:]
        cout, rest = rest[:n_comm], rest[n_comm:]
        acc = rest[0]
        i, j, k = pl.program_id(0), pl.program_id(1), pl.program_id(2)
        if n_comm:
            @pl.when((i == 0) & (j == 0) & (k == 0))
            def _():
                _Exchange(cin, cout, same_block, *rest[1:]).start()

        def finish(res):
            outs = (res,) if epilogue is None else epilogue(res, *[e[...] for e in ex_refs])
            for o, v in zip(out_refs, outs):
                o[...] = v.astype(o.dtype)

        part = _dot(a_ref[...].astype(BF16), b_ref[...].astype(BF16), ca, cb)
        if nk == 1:
            finish(part)
        else:
            @pl.when(k == 0)
            def _():
                acc[...] = part

            @pl.when(k > 0)
            def _():
                acc[...] += part

            @pl.when(k == nk - 1)
            def _():
                finish(acc[...])

        if n_comm:
            @pl.when((i == ni - 1) & (j == nj - 1) & (k == nk - 1))
            def _():
                _Exchange(cin, cout, same_block, *rest[1:]).finish()

    a_spec = pl.BlockSpec((tk, tm), lambda i, j, k: (k, i)) if ta else pl.BlockSpec((tm, tk), lambda i, j, k: (i, k))
    if b_shard8 and tb:
        kper = bcols // N_DEV // tk
        b_spec = pl.BlockSpec((None, tn, tk), lambda i, j, k: (k // kper, j, k % kper))
    elif b_shard8:
        nper = bcols // N_DEV // tn
        b_spec = pl.BlockSpec((None, tk, tn), lambda i, j, k: (j // nper, k, j % nper))
    elif tb:
        b_spec = pl.BlockSpec((tn, tk), lambda i, j, k: (j, k))
    else:
        b_spec = pl.BlockSpec((tk, tn), lambda i, j, k: (k, j))
    ex_specs = [pl.BlockSpec((1, tn), lambda i, j, k: (0, j)) if e.shape[0] == 1 and M != 1
                else pl.BlockSpec((tm, tn), lambda i, j, k: (i, j)) for e in extras]
    if out_shard8:
        nper = N // N_DEV // tn
        out_spec = pl.BlockSpec((None, tm, tn), lambda i, j, k: (j // nper, i, j % nper))
        out_shape = (N_DEV, M, N // N_DEV)
    else:
        out_spec = pl.BlockSpec((tm, tn), lambda i, j, k: (i, j))
        out_shape = (M, N)
    comm_shapes, sems = _exchange_shapes([p for p, _ in comm], same_block) if n_comm else ([], [])
    any_spec = pl.BlockSpec(memory_space=pl.ANY)
    outs = pl.pallas_call(
        body, name=name, grid=(ni, nj, nk),
        in_specs=[a_spec, b_spec] + ex_specs + [any_spec] * n_comm,
        out_specs=[out_spec] * n_out + [any_spec] * n_comm,
        out_shape=[jax.ShapeDtypeStruct(out_shape, d) for d in out_dtypes] + comm_shapes,
        scratch_shapes=[pltpu.VMEM((tm, tn) if nk > 1 else (8, 128), F32)] + sems,
        compiler_params=_cparams(("arbitrary",) * 3 if n_comm else ("parallel", "parallel", "arbitrary")),
    )(a, b, *extras, *[p for p, _ in comm])
    if n_comm:
        return (outs[0] if n_out == 1 else tuple(outs[:n_out])), list(outs[n_out:])
    return outs[0] if n_out == 1 else outs


def rms_fwd(x, g, *, out_dtype, name):
    S, D = x.shape
    ts = _tile(S, (512, 256, 128))

    def body(x_ref, g_ref, y_ref):
        x = x_ref[...]
        r = lax.rsqrt(jnp.mean(x * x, axis=-1, keepdims=True) + EPS)
        y_ref[...] = (x * r * g_ref[...]).astype(y_ref.dtype)

    return pl.pallas_call(
        body, name=name, grid=(S // ts,),
        in_specs=[pl.BlockSpec((ts, D), lambda i: (i, 0)), pl.BlockSpec((1, D), lambda i: (0, 0))],
        out_specs=pl.BlockSpec((ts, D), lambda i: (i, 0)),
        out_shape=jax.ShapeDtypeStruct((S, D), out_dtype),
        compiler_params=_cparams(("parallel",)),
    )(x, g.reshape(1, D))


def rms_bwd(x, g, dy, dres, *, name):
    S, D = x.shape
    ts = _tile(S, (512, 256, 128))
    has_res = dres is not None

    def body(x_ref, g_ref, dy_ref, *rest):
        dx_ref, dg_ref = rest[-2:]
        x = x_ref[...]
        r = lax.rsqrt(jnp.mean(x * x, axis=-1, keepdims=True) + EPS)
        xh = x * r
        dy = dy_ref[...].astype(F32)
        dxh = dy * g_ref[...]
        dx = r * (dxh - xh * jnp.mean(dxh * xh, axis=-1, keepdims=True))
        if has_res:
            dx = dx + rest[0][...]
        dx_ref[...] = dx

        @pl.when(pl.program_id(0) == 0)
        def _():
            dg_ref[...] = jnp.zeros_like(dg_ref)

        dg_ref[...] += jnp.sum(dy * xh, axis=0, keepdims=True)

    tok = pl.BlockSpec((ts, D), lambda i: (i, 0))
    row = pl.BlockSpec((1, D), lambda i: (0, 0))
    return pl.pallas_call(
        body, name=name, grid=(S // ts,),
        in_specs=[tok, row, tok] + ([tok] if has_res else []),
        out_specs=[tok, row],
        out_shape=[jax.ShapeDtypeStruct((S, D), F32), jax.ShapeDtypeStruct((1, D), F32)],
        compiler_params=_cparams(("arbitrary",)),
    )(x, g.reshape(1, D), dy, *([dres] if has_res else []))


def final_loss(h, g, target, *, name):
    S, D = h.shape
    ts = _tile(S, (512, 256, 128))

    def body(x_ref, g_ref, t_ref, loss_ref, dx_ref, dg_ref):
        x = x_ref[...]
        r = lax.rsqrt(jnp.mean(x * x, axis=-1, keepdims=True) + EPS)
        xh = x * r
        err = xh * g_ref[...] - t_ref[...]
        dy = err * (1.0 / D)
        dxh = dy * g_ref[...]
        dx_ref[...] = r * (dxh - xh * jnp.mean(dxh * xh, axis=-1, keepdims=True))

        @pl.when(pl.program_id(0) == 0)
        def _():
            dg_ref[...] = jnp.zeros_like(dg_ref)
            loss_ref[...] = jnp.zeros_like(loss_ref)

        dg_ref[...] += jnp.sum(dy * xh, axis=0, keepdims=True)
        loss_ref[...] += jnp.sum(err * err, axis=0, keepdims=True) * (0.5 / D)

    tok = pl.BlockSpec((ts, D), lambda i: (i, 0))
    row = pl.BlockSpec((1, D), lambda i: (0, 0))
    return pl.pallas_call(
        body, name=name, grid=(S // ts,),
        in_specs=[tok, row, tok], out_specs=[row, tok, row],
        out_shape=[jax.ShapeDtypeStruct((1, D), F32), jax.ShapeDtypeStruct((S, D), F32),
                   jax.ShapeDtypeStruct((1, D), F32)],
        compiler_params=_cparams(("arbitrary",)),
    )(h, g.reshape(1, D), target)


def _mem_probs(q, k):
    s = _dot(q.astype(BF16), k.astype(BF16), 1, 1) * HEAD_DIM ** -0.5
    p = jnp.exp(s - jnp.max(s, axis=-1, keepdims=True))
    return p / jnp.sum(p, axis=-1, keepdims=True)


def mem_fwd(proj, q_col, mkv, *, name):
    S = proj.shape[0]
    L = mkv.shape[0]
    ts = _tile(S)

    def body(q_ref, kv_ref, o_ref):
        for h in range(MEM_HEADS):
            c = slice(h * HEAD_DIM, (h + 1) * HEAD_DIM)
            v = kv_ref[:, MEM_WIDTH + h * HEAD_DIM:MEM_WIDTH + (h + 1) * HEAD_DIM]
            p = _mem_probs(q_ref[:, c], kv_ref[:, c])
            o_ref[:, c] = _bdot(p, v).astype(o_ref.dtype)

    return pl.pallas_call(
        body, name=name, grid=(S // ts,),
        in_specs=[pl.BlockSpec((ts, MEM_WIDTH), lambda i: (i, q_col // MEM_WIDTH)),
                  pl.BlockSpec((L, 2 * MEM_WIDTH), lambda i: (0, 0))],
        out_specs=pl.BlockSpec((ts, MEM_WIDTH), lambda i: (i, 0)),
        out_shape=jax.ShapeDtypeStruct((S, MEM_WIDTH), BF16),
        compiler_params=_cparams(("parallel",)),
    )(proj, mkv)


def mem_bwd(proj, q_col, mkv, dcat, do_col, *, name):
    S = proj.shape[0]
    L = mkv.shape[0]
    ts = _tile(S)
    scale = HEAD_DIM ** -0.5

    def body(q_ref, kv_ref, do_ref, dq_ref, dkv_ref):
        @pl.when(pl.program_id(0) == 0)
        def _():
            dkv_ref[...] = jnp.zeros_like(dkv_ref)

        for h in range(MEM_HEADS):
            c = slice(h * HEAD_DIM, (h + 1) * HEAD_DIM)
            cv = slice(MEM_WIDTH + h * HEAD_DIM, MEM_WIDTH + (h + 1) * HEAD_DIM)
            q, k, v = q_ref[:, c].astype(BF16), kv_ref[:, c].astype(BF16), kv_ref[:, cv].astype(BF16)
            do = do_ref[:, c].astype(BF16)
            p = _mem_probs(q, k)
            dkv_ref[:, cv] += _dot(p.astype(BF16), do, 0, 0)
            dp = _dot(do, v, 1, 1)
            ds = (p * (dp - jnp.sum(dp * p, axis=-1, keepdims=True)) * scale).astype(BF16)
            dq_ref[:, c] = _dot(ds, k).astype(dq_ref.dtype)
            dkv_ref[:, c] += _dot(ds, q, 0, 0)

    return pl.pallas_call(
        body, name=name, grid=(S // ts,),
        in_specs=[pl.BlockSpec((ts, MEM_WIDTH), lambda i: (i, q_col // MEM_WIDTH)),
                  pl.BlockSpec((L, 2 * MEM_WIDTH), lambda i: (0, 0)),
                  pl.BlockSpec((ts, MEM_WIDTH), lambda i: (i, do_col // MEM_WIDTH))],
        out_specs=[pl.BlockSpec((ts, MEM_WIDTH), lambda i: (i, 0)),
                   pl.BlockSpec((L, 2 * MEM_WIDTH), lambda i: (0, 0))],
        out_shape=[jax.ShapeDtypeStruct((S, MEM_WIDTH), BF16), jax.ShapeDtypeStruct((L, 2 * MEM_WIDTH), F32)],
        compiler_params=_cparams(("arbitrary",)),
    )(proj, mkv, dcat)


def _lower_ones(n, strict=False):
    r = lax.broadcasted_iota(jnp.int32, (n, n), 0)
    c = lax.broadcasted_iota(jnp.int32, (n, n), 1)
    return (r > c if strict else r >= c).astype(F32)


def fox_gate_fwd(proj, f_col, b_f_row, *, name):
    S = proj.shape[0]
    tb = _tile(S, (256, 128))

    def body(f_ref, b_ref, c_ref, carry):
        @pl.when(pl.program_id(0) == 0)
        def _():
            carry[...] = jnp.zeros_like(carry)

        ls = jax.nn.log_sigmoid(f_ref[...] + b_ref[...])
        cum = _dot(_lower_ones(tb), ls, precision=HI) + carry[...]
        c_ref[...] = cum
        carry[...] = cum[tb - 1:tb, :]

    return pl.pallas_call(
        body, name=name, grid=(S // tb,),
        in_specs=[pl.BlockSpec((tb, 128), lambda i: (i, f_col // 128)), pl.BlockSpec((1, 128), lambda i: (0, 0))],
        out_specs=pl.BlockSpec((tb, 128), lambda i: (i, 0)),
        out_shape=jax.ShapeDtypeStruct((S, 128), F32),
        scratch_shapes=[pltpu.VMEM((1, 128), F32)],
        compiler_params=_cparams(("arbitrary",)),
    )(proj, b_f_row)


def fox_gate_bwd(dcf, proj, f_col, b_f_row, *, name):
    S = proj.shape[0]
    tb = _tile(S, (256, 128))
    nb = S // tb

    def body(d_ref, f_ref, b_ref, df_ref, db_ref, carry):
        @pl.when(pl.program_id(0) == 0)
        def _():
            carry[...] = jnp.zeros_like(carry)
            db_ref[...] = jnp.zeros_like(db_ref)

        upper = _lower_ones(tb).T
        rc = _dot(upper, d_ref[...], precision=HI) + carry[...]
        carry[...] = rc[0:1, :]
        df = rc * jax.nn.sigmoid(-(f_ref[...] + b_ref[...]))
        df_ref[...] = df
        db_ref[...] += jnp.sum(df, axis=0, keepdims=True)

    return pl.pallas_call(
        body, name=name, grid=(nb,),
        in_specs=[pl.BlockSpec((tb, 128), lambda i: (nb - 1 - i, 0)),
                  pl.BlockSpec((tb, 128), lambda i: (nb - 1 - i, f_col // 128)),
                  pl.BlockSpec((1, 128), lambda i: (0, 0))],
        out_specs=[pl.BlockSpec((tb, 128), lambda i: (nb - 1 - i, 0)), pl.BlockSpec((1, 128), lambda i: (0, 0))],
        out_shape=[jax.ShapeDtypeStruct((S, 128), F32), jax.ShapeDtypeStruct((1, 128), F32)],
        scratch_shapes=[pltpu.VMEM((1, 128), F32)],
        compiler_params=_cparams(("arbitrary",)),
    )(dcf, proj, b_f_row)


def _fox_block(S):
    return _tile(S, (512, 256, 128)) if S > 512 else S // 2


FOX_Q_BLOCKS = 2


def _causal(s, row0):
    r = row0 + lax.broadcasted_iota(jnp.int32, s.shape, 0)
    c = lax.broadcasted_iota(jnp.int32, s.shape, 1)
    return jnp.where(r >= c, s, -jnp.inf)


def fox_fwd(proj, n_heads, cf_col, cf_row, *, name):
    S = proj.shape[0]
    H = n_heads
    bk = _fox_block(S)
    qmul = FOX_Q_BLOCKS if S % (FOX_Q_BLOCKS * bk) == 0 else 1
    bq = qmul * bk
    scale = HEAD_DIM ** -0.5

    def body(q_ref, k_ref, v_ref, cc_ref, cr_ref, o_ref, st_ref):
        qi = pl.program_id(1)
        q = q_ref[...].astype(BF16)
        cq = cc_ref[...]

        def step(j, carry, diagonal=None):
            rows = pl.ds(pl.multiple_of(j * bk, bk), bk)
            k = k_ref[rows, :].astype(BF16)
            v = v_ref[rows, :].astype(BF16)
            m, l, acc = carry
            s = _dot(q, k, 1, 1) * scale + cq - cr_ref[j]
            if diagonal is not None:
                s = _causal(s, -diagonal * bk)
            m2 = jnp.maximum(m, jnp.max(s, axis=-1, keepdims=True))
            p = jnp.exp(s - m2)
            a = jnp.exp(m - m2)
            return m2, a * l + jnp.sum(p, axis=-1, keepdims=True), a * acc + _dot(p.astype(BF16), v)

        init = (jnp.full((bq, 1), -jnp.inf, F32), jnp.zeros((bq, 1), F32), jnp.zeros((bq, HEAD_DIM), F32))
        carry = lax.fori_loop(0, qi * qmul, step, init)
        for d in range(qmul):
            carry = step(qi * qmul + d, carry, diagonal=d)
        m, l, acc = carry
        o_ref[...] = (acc / l).astype(o_ref.dtype)
        lane = lax.broadcasted_iota(jnp.int32, (bq, 128), 1)
        st_ref[...] = jnp.where(lane == 0, m + jnp.log(l), jnp.where(lane == 1, cq, 0.0))

    return pl.pallas_call(
        body, name=name, grid=(H, S // bq),
        in_specs=[pl.BlockSpec((bq, HEAD_DIM), lambda h, i: (i, h)),
                  pl.BlockSpec((S, HEAD_DIM), lambda h, i: (0, H + h)),
                  pl.BlockSpec((S, HEAD_DIM), lambda h, i: (0, 2 * H + h)),
                  pl.BlockSpec((None, bq, 1), lambda h, i: (h, i, 0)),
                  pl.BlockSpec((None, S // bk, 1, bk), lambda h, i: (h, 0, 0, 0))],
        out_specs=[pl.BlockSpec((bq, HEAD_DIM), lambda h, i: (i, h)),
                   pl.BlockSpec((None, bq, 128), lambda h, i: (h, i, 0))],
        out_shape=[jax.ShapeDtypeStruct((S, H * HEAD_DIM), BF16), jax.ShapeDtypeStruct((H, S, 128), F32)],
        compiler_params=_cparams(("parallel", "parallel")),
    )(proj, proj, proj, cf_col, cf_row)


def fox_delta(stats, o, dcat, *, name):
    H, S, _ = stats.shape
    ts = _tile(S)

    def body(st_ref, o_ref, do_ref, out_ref):
        d = jnp.sum(o_ref[...].astype(F32) * do_ref[...].astype(F32), axis=-1, keepdims=True)
        lane = lax.broadcasted_iota(jnp.int32, (ts, 128), 1)
        out_ref[...] = jnp.where(lane == 2, d, st_ref[...])

    return pl.pallas_call(
        body, name=name, grid=(H, S // ts),
        in_specs=[pl.BlockSpec((None, ts, 128), lambda h, i: (h, i, 0)),
                  pl.BlockSpec((ts, HEAD_DIM), lambda h, i: (i, h)),
                  pl.BlockSpec((ts, HEAD_DIM), lambda h, i: (i, h))],
        out_specs=pl.BlockSpec((None, ts, 128), lambda h, i: (h, i, 0)),
        out_shape=jax.ShapeDtypeStruct((H, S, 128), F32),
        compiler_params=_cparams(("parallel", "parallel")),
    )(stats, o, dcat)


def fox_bwd(proj, n_heads, stats, cf_row, dcat, *, name):
    S = proj.shape[0]
    H = n_heads
    bq = _fox_block(S)
    nq = S // bq
    qmul = FOX_Q_BLOCKS if S % (FOX_Q_BLOCKS * bq) == 0 else 1
    bt = qmul * bq
    scale = HEAD_DIM ** -0.5

    def body(q_ref, k_ref, v_ref, do_ref, st_ref, cr_ref, dq_ref, dk_ref, dv_ref, dst_ref, dcr_ref):
        kj = pl.program_id(1)

        @pl.when(kj == 0)
        def _():
            dq_ref[...] = jnp.zeros_like(dq_ref)
            dst_ref[...] = jnp.zeros_like(dst_ref)

        k = k_ref[...].astype(BF16)
        v = v_ref[...].astype(BF16)
        ck = cr_ref[...]
        lane = lax.broadcasted_iota(jnp.int32, (bt, 128), 1)

        def step(t, carry, diagonal=False):
            dk, dv, dck = carry
            rows = pl.ds(pl.multiple_of(t * bt, bt), bt)
            q = q_ref[rows, :].astype(BF16)
            do = do_ref[rows, :].astype(BF16)
            st = st_ref[rows, :]
            lse, cq, delta = st[:, 0:1], st[:, 1:2], st[:, 2:3]
            s = _dot(q, k, 1, 1) * scale + cq - ck
            if diagonal:
                s = _causal(s, -lax.rem(kj, jnp.int32(qmul)) * bq)
            p = jnp.exp(s - lse)
            dv = dv + _dot(p.astype(BF16), do, 0, 0)
            ds = p * (_dot(do, v, 1, 1) - delta)
            dsb = (ds * scale).astype(BF16)
            dq_ref[rows, :] += _dot(dsb, k)
            dst_ref[rows, :] += jnp.where(lane == 0, jnp.sum(ds, axis=-1, keepdims=True), 0.0)
            return dk + _dot(dsb, q, 0, 0), dv, dck - jnp.sum(ds, axis=0, keepdims=True)

        init = (jnp.zeros((bq, HEAD_DIM), F32), jnp.zeros((bq, HEAD_DIM), F32), jnp.zeros((1, bq), F32))
        first = lax.div(kj, jnp.int32(qmul))
        dk, dv, dck = lax.fori_loop(first + 1, S // bt, step, step(first, init, diagonal=True))
        dk_ref[...] = dk
        dv_ref[...] = dv
        dcr_ref[...] = dck

    W = H * HEAD_DIM
    return pl.pallas_call(
        body, name=name, grid=(H, nq),
        in_specs=[pl.BlockSpec((S, HEAD_DIM), lambda h, j: (0, h)),
                  pl.BlockSpec((bq, HEAD_DIM), lambda h, j: (j, H + h)),
                  pl.BlockSpec((bq, HEAD_DIM), lambda h, j: (j, 2 * H + h)),
                  pl.BlockSpec((S, HEAD_DIM), lambda h, j: (0, h)),
                  pl.BlockSpec((None, S, 128), lambda h, j: (h, 0, 0)),
                  pl.BlockSpec((None, None, 1, bq), lambda h, j: (h, j, 0, 0))],
        out_specs=[pl.BlockSpec((S, HEAD_DIM), lambda h, j: (0, h)),
                   pl.BlockSpec((bq, HEAD_DIM), lambda h, j: (j, h)),
                   pl.BlockSpec((bq, HEAD_DIM), lambda h, j: (j, h)),
                   pl.BlockSpec((None, S, 128), lambda h, j: (h, 0, 0)),
                   pl.BlockSpec((None, None, 1, bq), lambda h, j: (h, j, 0, 0))],
        out_shape=[jax.ShapeDtypeStruct((S, W), F32), jax.ShapeDtypeStruct((S, W), F32),
                   jax.ShapeDtypeStruct((S, W), F32), jax.ShapeDtypeStruct((H, S, 128), F32),
                   jax.ShapeDtypeStruct((H, nq, 1, bq), F32)],
        compiler_params=_cparams(("parallel", "arbitrary")),
    )(proj, proj, proj, dcat, stats, cf_row)


def _pad_row(v, n=128):
    return jnp.pad(v.astype(F32), (0, n - v.shape[0])).reshape(1, n)


def fox_mixer_fwd(proj, f_col, b_f, tag):
    S = proj.shape[0]
    H = b_f.shape[0]
    bq = _fox_block(S)
    b_row = _pad_row(b_f)
    cf = fox_gate_fwd(proj, f_col, b_row, name=f"fox_gate_fwd_{tag}")
    cf_t = cf[:, :H].T
    o, stats = fox_fwd(proj, H, cf_t.reshape(H, S, 1), cf_t.reshape(H, S // bq, 1, bq), name=f"fox_fwd_{tag}")
    return o, (o, stats, cf_t, b_row)


def fox_mixer_bwd(proj, f_col, saved, dcat, tag):
    o, stats, cf_t, b_row = saved
    H, S = cf_t.shape
    bq = _fox_block(S)
    stats = fox_delta(stats, o, dcat, name=f"fox_delta_{tag}")
    dq, dk, dv, dst, dcr = fox_bwd(proj, H, stats, cf_t.reshape(H, S // bq, 1, bq), dcat, name=f"fox_bwd_{tag}")
    dcf = jnp.sum(dst, axis=-1) + dcr.reshape(H, S)
    dcf = jnp.pad(dcf.T, ((0, 0), (0, 128 - H)))
    df, db = fox_gate_bwd(dcf, proj, f_col, b_row, name=f"fox_gate_bwd_{tag}")
    return dq, dk, dv, df, db[0, :H]


def ew(fn, tok_ins, row_ins, tok_out_dtypes, n_row_out, *, width, name, tc=None, ts=None):
    S = tok_ins[0][0].shape[0]
    tc = tc or _tile(width, (512, 256, 128))
    ts = ts or _tile(S, (512, 256, 128))
    n_tok, n_row, n_to = len(tok_ins), len(row_ins), len(tok_out_dtypes)
    for _, col in tok_ins:
        assert col % tc == 0

    def body(*refs):
        ins = [r[...] for r in refs[:n_tok + n_row]]
        outs = fn(*ins)
        outs = outs if isinstance(outs, (tuple, list)) else (outs,)
        o_refs = refs[n_tok + n_row:]
        for o, v in zip(o_refs[:n_to], outs[:n_to]):
            o[...] = v.astype(o.dtype)
        if n_row_out:
            @pl.when(pl.program_id(1) == 0)
            def _():
                for o in o_refs[n_to:]:
                    o[...] = jnp.zeros_like(o)

            for o, v in zip(o_refs[n_to:], outs[n_to:]):
                o[...] += jnp.sum(v, axis=0, keepdims=True)

    def tok_spec(col):
        return pl.BlockSpec((ts, tc), lambda j, i: (i, col // tc + j))

    row_spec = pl.BlockSpec((1, tc), lambda j, i: (0, j))
    res = pl.pallas_call(
        body, name=name, grid=(width // tc, S // ts),
        in_specs=[tok_spec(col) for _, col in tok_ins] + [row_spec] * n_row,
        out_specs=[tok_spec(0)] * n_to + [row_spec] * n_row_out,
        out_shape=[jax.ShapeDtypeStruct((S, width), d) for d in tok_out_dtypes]
        + [jax.ShapeDtypeStruct((1, width), F32)] * n_row_out,
        compiler_params=_cparams(("parallel", "arbitrary" if n_row_out else "parallel")),
    )(*[a for a, _ in tok_ins], *row_ins)
    return res[0] if len(res) == 1 else res


S5_SUPER = 128 // S5_GROUP
S5_COLS = S5_SUPER * S5_STATE


def _shift_rows(x, d, up=False):
    T = x.shape[0]
    if d % 8 == 0:
        z = jnp.zeros((d, x.shape[1]), x.dtype)
        return jnp.concatenate([x[d:], z], axis=0) if up else jnp.concatenate([z, x[:T - d]], axis=0)
    row = lax.broadcasted_iota(jnp.int32, x.shape, 0)
    if up:
        return jnp.where(row < T - d, pltpu.roll(x, T - d, axis=0), 0.0)
    return jnp.where(row >= d, pltpu.roll(x, d, axis=0), 0.0)


def _s5_scan(bre, bim, tab, hre_s, him_s, carry, up):
    T, N = bre.shape
    G = T // 8
    hre, him = bre.reshape(G, 8, N), bim.reshape(G, 8, N)
    for k in range(3):
        shift = (8 - (1 << k)) if up else (1 << k)
        sre, sim = pltpu.roll(hre, shift, axis=1), pltpu.roll(him, shift, axis=1)
        cre, cim = tab[0, k], tab[1, k]
        hre, him = hre + cre * sre - cim * sim, him + cre * sim + cim * sre
    hre_s[...] = hre
    him_s[...] = him
    are, aim = tab[0, 3], tab[1, 3]
    edge = slice(0, 1) if up else slice(7, 8)

    def tile(i, c):
        g = G - 1 - i if up else i
        r = hre_s[g] + are * c[0] - aim * c[1]
        m = him_s[g] + are * c[1] + aim * c[0]
        hre_s[g] = r
        him_s[g] = m
        return r[edge], m[edge]

    carry = lax.fori_loop(0, G, tile, carry)
    return hre_s[...].reshape(T, N), him_s[...].reshape(T, N), carry


def _s5_states(u, bbre, bbim, tab, hre_s, him_s, start):
    return _s5_scan(_bdot(u, bbre), _bdot(u, bbim), tab, hre_s, him_s, (start[0:1], start[1:2]), False)


def _s5_tables(a_re, a_im, NJ):
    are, aim = a_re.reshape(NJ, 1, S5_COLS), a_im.reshape(NJ, 1, S5_COLS)
    pows = [(are, aim)]
    for _ in range(7):
        r, i = pows[-1]
        pows.append((r * are - i * aim, r * aim + i * are))
    row = jnp.arange(8).reshape(1, 8, 1)
    tabs = []
    for up in (False, True):
        re, im = [], []
        for d in (1, 2, 4):
            keep = (row <= 7 - d) if up else (row >= d)
            re.append(jnp.where(keep, pows[d - 1][0], 0.0))
            im.append(jnp.where(keep, -pows[d - 1][1] if up else pows[d - 1][1], 0.0))
        order = range(7, -1, -1) if up else range(8)
        re.append(jnp.concatenate([pows[n][0] for n in order], axis=1))
        im.append(jnp.concatenate([-pows[n][1] if up else pows[n][1] for n in order], axis=1))
        tabs.append(jnp.stack([jnp.stack(re, axis=1), jnp.stack(im, axis=1)], axis=1))
    return tabs


def _s5_block(S):
    return _tile(S, (256, 128))


def _s5_specs(T):
    mat = lambda r, c: pl.BlockSpec((None, r, c), lambda j, t: (j, 0, 0))
    tab = pl.BlockSpec((None, 2, 4, 8, S5_COLS), lambda j, t: (j, 0, 0, 0, 0))
    scratch = pltpu.VMEM((T // 8, 8, S5_COLS), F32)
    return mat, tab, scratch


def s5_scan_fwd(proj, W, bbre, bbim, cre, cim, tab_f, tab_r, *, name):
    S = proj.shape[0]
    T = _s5_block(S)
    NJ, nblk = W // 128, S // T

    def body(u_ref, bbre_ref, bbim_ref, cre_ref, cim_ref, tab_ref, y_ref, st_ref, carry, hre_s, him_s):
        @pl.when(pl.program_id(1) == 0)
        def _():
            carry[...] = jnp.zeros_like(carry)

        st_ref[...] = carry[...]
        hre, him, (lre, lim) = _s5_states(u_ref[...], bbre_ref[...], bbim_ref[...], tab_ref[...], hre_s, him_s,
                                          carry[...])
        carry[0:1, :] = lre
        carry[1:2, :] = lim
        y_ref[...] = _bdot(hre, cre_ref[...]) - _bdot(him, cim_ref[...])

    mat, tab, scratch = _s5_specs(T)
    return pl.pallas_call(
        body, name=name, grid=(NJ, nblk),
        in_specs=[pl.BlockSpec((T, 128), lambda j, t: (t, j)), mat(128, S5_COLS), mat(128, S5_COLS),
                  mat(S5_COLS, 128), mat(S5_COLS, 128), tab],
        out_specs=[pl.BlockSpec((T, 128), lambda j, t: (t, j)),
                   pl.BlockSpec((None, None, 2, S5_COLS), lambda j, t: (j, t, 0, 0))],
        out_shape=[jax.ShapeDtypeStruct((S, W), F32), jax.ShapeDtypeStruct((NJ, nblk, 2, S5_COLS), F32)],
        scratch_shapes=[pltpu.VMEM((2, S5_COLS), F32), scratch, scratch],
        compiler_params=_cparams(("parallel", "arbitrary")),
    )(proj, bbre, bbim, cre, cim, tab_f)


def s5_scan_bwd(proj, W, dy, d_skip_row, bbre, bbim, cre, cim, tab_f, tab_r, starts, *, name):
    S = proj.shape[0]
    T = _s5_block(S)
    NJ, nblk = W // 128, S // T

    def body(u_ref, dy_ref, d_ref, bbre_ref, bbim_ref, cre_ref, cim_ref, tabf_ref, tabr_ref, st_ref,
             du_ref, dbbre_ref, dbbim_ref, dcre_ref, dcim_ref, da_ref, gcarry, hre_s, him_s, gre_s, gim_s):
        @pl.when(pl.program_id(1) == 0)
        def _():
            gcarry[...] = jnp.zeros_like(gcarry)
            for r in (dbbre_ref, dbbim_ref, dcre_ref, dcim_ref, da_ref):
                r[...] = jnp.zeros_like(r)

        u, dy, start = u_ref[...], dy_ref[...], st_ref[...]
        bbre, bbim, cre, cim = (r[...].astype(BF16) for r in (bbre_ref, bbim_ref, cre_ref, cim_ref))
        hre, him, _ = _s5_states(u, bbre, bbim, tabf_ref[...], hre_s, him_s, start)
        dyb, ub = dy.astype(BF16), u.astype(BF16)
        gre, gim, (fre, fim) = _s5_scan(_dot(dyb, cre, 1, 1), -_dot(dyb, cim, 1, 1), tabr_ref[...], gre_s, gim_s,
                                        (gcarry[0:1], gcarry[1:2]), True)
        gcarry[0:1, :] = fre
        gcarry[1:2, :] = fim
        greb, gimb = gre.astype(BF16), gim.astype(BF16)
        du_ref[...] = (_dot(greb, bbre, 1, 1) + _dot(gimb, bbim, 1, 1) + dy * d_ref[...]).astype(du_ref.dtype)
        dbbre_ref[...] += _dot(ub, greb, 0, 0)
        dbbim_ref[...] += _dot(ub, gimb, 0, 0)
        dcre_ref[...] += _dot(hre.astype(BF16), dyb, 0, 0)
        dcim_ref[...] -= _dot(him.astype(BF16), dyb, 0, 0)
        first = lax.broadcasted_iota(jnp.int32, hre.shape, 0) == 0
        pre = _shift_rows(hre, 1) + jnp.where(first, start[0:1], 0.0)
        pim = _shift_rows(him, 1) + jnp.where(first, start[1:2], 0.0)
        da_ref[0:1, :] += jnp.sum(gre * pre + gim * pim, axis=0, keepdims=True)
        da_ref[1:2, :] += jnp.sum(gim * pre - gre * pim, axis=0, keepdims=True)

    mat, tab, scratch = _s5_specs(T)
    tok = pl.BlockSpec((T, 128), lambda j, t: (nblk - 1 - t, j))
    return pl.pallas_call(
        body, name=name, grid=(NJ, nblk),
        in_specs=[tok, tok, pl.BlockSpec((1, 128), lambda j, t: (0, j)), mat(128, S5_COLS), mat(128, S5_COLS),
                  mat(S5_COLS, 128), mat(S5_COLS, 128), tab, tab,
                  pl.BlockSpec((None, None, 2, S5_COLS), lambda j, t: (j, nblk - 1 - t, 0, 0))],
        out_specs=[tok, mat(128, S5_COLS), mat(128, S5_COLS), mat(S5_COLS, 128), mat(S5_COLS, 128), mat(2, S5_COLS)],
        out_shape=[jax.ShapeDtypeStruct((S, W), BF16),
                   jax.ShapeDtypeStruct((NJ, 128, S5_COLS), F32), jax.ShapeDtypeStruct((NJ, 128, S5_COLS), F32),
                   jax.ShapeDtypeStruct((NJ, S5_COLS, 128), F32), jax.ShapeDtypeStruct((NJ, S5_COLS, 128), F32),
                   jax.ShapeDtypeStruct((NJ, 2, S5_COLS), F32)],
        scratch_shapes=[pltpu.VMEM((2, S5_COLS), F32)] + [scratch] * 4,
        compiler_params=_cparams(("parallel", "arbitrary")),
    )(proj, dy, d_skip_row, bbre, bbim, cre, cim, tab_f, tab_r, starts)


def _s5_discretize(lam_re, lam_im, log_dt, b_re, b_im):
    dt = jnp.exp(log_dt)[:, None]
    mag = jnp.exp(lam_re * dt)
    a_re, a_im = mag * jnp.cos(lam_im * dt), mag * jnp.sin(lam_im * dt)
    den = lam_re * lam_re + lam_im * lam_im
    z_re = ((a_re - 1.0) * lam_re + a_im * lam_im) / den
    z_im = (a_im * lam_re - (a_re - 1.0) * lam_im) / den
    bb_re = z_re[..., None] * b_re - z_im[..., None] * b_im
    bb_im = z_re[..., None] * b_im + z_im[..., None] * b_re
    return a_re, a_im, bb_re, bb_im


def _blockdiag(x):
    G, r, c = x.shape
    x = x.reshape(G // S5_SUPER, S5_SUPER, r, c)
    eye = jnp.eye(S5_SUPER, dtype=x.dtype)
    return (x[:, :, :, None, :] * eye[None, :, None, :, None]).reshape(G // S5_SUPER, S5_SUPER * r, S5_SUPER * c)


def _blockdiag_t(x, r, c):
    NJ = x.shape[0]
    x = x.reshape(NJ, S5_SUPER, r, S5_SUPER, c)
    return jnp.stack([x[:, i, :, i, :] for i in range(S5_SUPER)], axis=1).reshape(NJ * S5_SUPER, r, c)


def _gelu(x):
    c = math.sqrt(2.0 / math.pi)
    return 0.5 * x * (1.0 + jnp.tanh(c * (x + 0.044715 * x * x * x)))


def _gelu_grad(x):
    c = math.sqrt(2.0 / math.pi)
    t = jnp.tanh(c * (x + 0.044715 * x * x * x))
    return 0.5 * (1.0 + t) + 0.5 * x * (1.0 - t * t) * c * (1.0 + 3 * 0.044715 * x * x)


def s5_mixer_fwd(proj, W, p, w_glu_bf, tag):
    a_re, a_im, bb_re, bb_im = _s5_discretize(p["lam_re"], p["lam_im"], p["log_dt"], p["b_re"], p["b_im"])
    tab_f, tab_r = _s5_tables(a_re, a_im, W // 128)
    bbre = _blockdiag(jnp.swapaxes(bb_re, 1, 2))
    bbim = _blockdiag(jnp.swapaxes(bb_im, 1, 2))
    cre = _blockdiag(jnp.swapaxes(p["c_re"], 1, 2))
    cim = _blockdiag(jnp.swapaxes(p["c_im"], 1, 2))
    mats = (bbre, bbim, cre, cim, tab_f, tab_r)
    y, starts = s5_scan_fwd(proj, W, *mats, name=f"s5_scan_fwd_{tag}")
    d_row = p["d_skip"].reshape(1, W)
    yy = ew(lambda y, u, d: _gelu(y + d * u), [(y, 0), (proj, 0)], [d_row], [F32], 0, width=W,
            name=f"s5_act_fwd_{tag}")
    mix, z = mm(yy, w_glu_bf, extras=(yy, p["b_glu"].reshape(1, W)), out_dtypes=(BF16, F32),
                epilogue=lambda acc, yy, b: (yy * jax.nn.sigmoid(acc + b), acc + b), name=f"s5_glu_{tag}")
    return mix, (mats, starts, y, yy, z, d_row)


def s5_mixer_bwd(proj, W, p, w_glu_bf, saved, dcat, tag):
    mats, starts, y, yy, z, d_row = saved

    def glu_bwd(dm, yy, z):
        sg = jax.nn.sigmoid(z)
        dz = dm.astype(F32) * yy * sg * (1.0 - sg)
        return dm.astype(F32) * sg, dz, dz

    dyy1, dz, db_glu = ew(glu_bwd, [(dcat, 0), (yy, 0), (z, 0)], [], [F32, F32], 1, width=W, name=f"s5_glu_bwd_{tag}")
    dw_glu = mm(yy, dz, ta=True, out_dtypes=(BF16,), name=f"s5_dwglu_{tag}")
    dyy = mm(dz, w_glu_bf, tb=True, extras=(dyy1,), epilogue=lambda acc, e: (acc + e,), name=f"s5_dyy_{tag}")

    def act_bwd(dyy, y, u, d):
        dpre = dyy * _gelu_grad(y + d * u)
        return dpre, dpre * u

    dy, dd = ew(act_bwd, [(dyy, 0), (y, 0), (proj, 0)], [d_row], [F32], 1, width=W, name=f"s5_act_bwd_{tag}")
    du, dbbre, dbbim, dcre, dcim, da = s5_scan_bwd(proj, W, dy, d_row, *mats, starts, name=f"s5_scan_bwd_{tag}")
    G = W // S5_GROUP
    dbb_re = jnp.swapaxes(_blockdiag_t(dbbre, S5_GROUP, S5_STATE), 1, 2)
    dbb_im = jnp.swapaxes(_blockdiag_t(dbbim, S5_GROUP, S5_STATE), 1, 2)
    dc_re = jnp.swapaxes(_blockdiag_t(dcre, S5_STATE, S5_GROUP), 1, 2)
    dc_im = jnp.swapaxes(_blockdiag_t(dcim, S5_STATE, S5_GROUP), 1, 2)
    da_re, da_im = da[:, 0, :].reshape(G, S5_STATE), da[:, 1, :].reshape(G, S5_STATE)
    _, vjp = jax.vjp(_s5_discretize, p["lam_re"], p["lam_im"], p["log_dt"], p["b_re"], p["b_im"])
    dlam_re, dlam_im, dlog_dt, db_re, db_im = vjp((da_re, da_im, dbb_re, dbb_im))
    grads = dict(lam_re=dlam_re, lam_im=dlam_im, log_dt=dlog_dt, b_re=db_re, b_im=db_im, c_re=dc_re, c_im=dc_im,
                 d_skip=dd.reshape(W), w_glu=dw_glu, b_glu=db_glu.reshape(W))
    return du, grads


CONV_ROWS = 256


def _conv_taps(ext, w):
    acc = ext * w[3:4]
    for j in range(1, 4):
        acc = acc + pltpu.roll(ext, j, axis=0) * w[3 - j:4 - j]
    return acc


def gdn_conv_fwd(proj, width, conv_w, *, name):
    S = proj.shape[0]
    tc = _tile(width, (256, 128))
    T = min(CONV_ROWS, S)
    n = S // T

    def body(x_ref, w_ref, y_ref):
        w = w_ref[...]

        def chunk(c, carry):
            base = pl.multiple_of(c * T, T)
            prev = x_ref[pl.ds(pl.multiple_of(jnp.maximum(base - 8, 0), 8), 8), :]
            ext = jnp.concatenate([jnp.where(c > 0, prev, 0.0), x_ref[pl.ds(base, T), :]], axis=0)
            pre = _conv_taps(ext, w)[8:]
            y_ref[pl.ds(base, T), :] = pre * jax.nn.sigmoid(pre)
            return carry

        lax.fori_loop(0, n, chunk, 0)

    return pl.pallas_call(
        body, name=name, grid=(width // tc,),
        in_specs=[pl.BlockSpec((S, tc), lambda j: (0, j)), pl.BlockSpec((4, tc), lambda j: (0, j))],
        out_specs=pl.BlockSpec((S, tc), lambda j: (0, j)),
        out_shape=jax.ShapeDtypeStruct((S, width), F32),
        compiler_params=_cparams(("parallel",)),
    )(proj, conv_w)


def gdn_conv_bwd(proj, width, conv_w, dy, *, name):
    S = proj.shape[0]
    tc = _tile(width // 3, (256, 128))
    per = width // 3 // tc
    T = min(CONV_ROWS, S)
    n = S // T
    E = T + 16

    def body(x_ref, w_ref, dy_ref, dx_ref, dw_ref):
        w = w_ref[...]

        def halo(ref, start, keep):
            start = pl.multiple_of(jnp.clip(start, 0, S - 8), 8)
            return jnp.where(keep, ref[pl.ds(start, 8), :], 0.0)

        def chunk(c, dw):
            base = pl.multiple_of(c * T, T)
            rows = pl.ds(base, T)
            ext = jnp.concatenate([halo(x_ref, base - 8, c > 0), x_ref[rows, :], halo(x_ref, base + T, c < n - 1)], axis=0)
            dye = jnp.concatenate([jnp.zeros((8, tc), F32), dy_ref[rows, :], halo(dy_ref, base + T, c < n - 1)], axis=0)
            pre = _conv_taps(ext, w)
            sg = jax.nn.sigmoid(pre)
            dpre = dye * (sg * (1.0 + pre * (1.0 - sg)))
            dx = dpre * w[3:4]
            for j in range(1, 4):
                dx = dx + pltpu.roll(dpre, E - j, axis=0) * w[3 - j:4 - j]
            dx_ref[rows, :] = dx[8:8 + T].astype(dx_ref.dtype)
            own = dpre[8:8 + T]
            parts = [jnp.sum(own * pltpu.roll(ext, 3 - i, axis=0)[8:8 + T], axis=0, keepdims=True) if i < 3
                     else jnp.sum(own * ext[8:8 + T], axis=0, keepdims=True) for i in range(4)]
            return dw + jnp.concatenate(parts, axis=0)

        dw_ref[...] = lax.fori_loop(0, n, chunk, jnp.zeros((4, tc), F32))

    return pl.pallas_call(
        body, name=name, grid=(width // tc,),
        in_specs=[pl.BlockSpec((S, tc), lambda j: (0, j)), pl.BlockSpec((4, tc), lambda j: (0, j)),
                  pl.BlockSpec((None, S, tc), lambda j: (j // per, 0, j % per))],
        out_specs=[pl.BlockSpec((S, tc), lambda j: (0, j)), pl.BlockSpec((4, tc), lambda j: (0, j))],
        out_shape=[jax.ShapeDtypeStruct((S, width), BF16), jax.ShapeDtypeStruct((4, width), F32)],
        compiler_params=_cparams(("parallel",)),
    )(proj, conv_w, dy)


@functools.partial(jax.custom_vjp, nondiff_argnums=(0,))
def _bein(spec, a, b):
    return jnp.einsum(spec, a.astype(BF16), b.astype(BF16), preferred_element_type=F32)


def _bein_fwd(spec, a, b):
    return _bein(spec, a, b), (a, b)


def _bein_bwd(spec, res, g):
    a, b = res
    ins, out = spec.split("->")
    sa, sb = ins.split(",")
    return _bein(f"{out},{sb}->{sa}", g, b), _bein(f"{sa},{out}->{sb}", a, g)


_bein.defvjp(_bein_fwd, _bein_bwd)


def _hmm(a, b, precision=lax.Precision.HIGH):
    return jnp.einsum("ncs,nsd->ncd", a, b, precision=precision, preferred_element_type=F32)


def _inv_unit_lower(L):
    C = L.shape[-1]
    r = lax.broadcasted_iota(jnp.int32, L.shape, 1)
    c = lax.broadcasted_iota(jnp.int32, L.shape, 2)
    eye = (r == c).astype(F32)
    D = jnp.where(jnp.right_shift(r, 4) == jnp.right_shift(c, 4), L, 0.0)
    D2 = _hmm(D, D)
    D4 = _hmm(D2, D2)
    D8 = _hmm(D4, D4)
    dinv = _hmm(_hmm(_hmm(eye - D, eye + D2), eye + D4), eye + D8)
    N = _hmm(dinv, L - D)
    return _hmm(_hmm(eye - N, eye + _hmm(N, N)), dinv)


def _softplus(x):
    return jnp.maximum(x, 0.0) + jnp.log(1.0 + jnp.exp(-jnp.abs(x)))


def _gdn_local(qc, kc, vc, ab, a_log_row, dt_row, h, n_heads):
    R = qc.shape[0]
    C = GDN_CHUNK
    n = R // C
    lane = lax.broadcasted_iota(jnp.int32, (1, 128), 1)
    pick = lambda x, i: jnp.sum(jnp.where(lane == i, x, 0.0), axis=-1, keepdims=True)
    a_in, b_in = pick(ab, h).reshape(n, C, 1), pick(ab, n_heads + h).reshape(n, C, 1)
    a_log, dt_bias = pick(a_log_row, h), pick(dt_row, h)
    q3, k3, v = qc.reshape(n, C, 128), kc.reshape(n, C, 128), vc.reshape(n, C, 128)
    q = q3 * lax.rsqrt(jnp.sum(q3 * q3, axis=-1, keepdims=True) + EPS) * HEAD_DIM ** -0.5
    k = k3 * lax.rsqrt(jnp.sum(k3 * k3, axis=-1, keepdims=True) + EPS)
    beta = jax.nn.sigmoid(b_in)
    g = -jnp.exp(a_log) * _softplus(a_in + dt_bias)
    r = lax.broadcasted_iota(jnp.int32, (n, C, C), 1)
    c = lax.broadcasted_iota(jnp.int32, (n, C, C), 2)
    gc = _hmm((r >= c).astype(F32), jnp.broadcast_to(g, (n, C, C)), precision=HI)
    gcol = gc[:, :, 0:1]
    grow = jnp.sum(jnp.where(r == c, gc, 0.0), axis=1, keepdims=True)
    decay = jnp.exp(jnp.where(r >= c, gc - grow, -jnp.inf))
    kb, vb = k * beta, v * beta
    lmat = jnp.where(r > c, _bein("ncd,nsd->ncs", kb, k) * decay, 0.0)
    eg = jnp.exp(gcol)
    rhs = jnp.concatenate([vb, kb * eg], axis=-1)
    attn = jnp.where(r >= c, _bein("ncd,nsd->ncs", q, k) * decay, 0.0)
    glast = gcol[:, C - 1:C, :]
    k_dec = k * jnp.exp(glast - gcol)
    g_last = jnp.broadcast_to(jnp.exp(glast), (n, 1, 128))
    return lmat, rhs, attn.reshape(R, C), (q * eg).reshape(R, 128), k_dec.reshape(R, 128), g_last


def _gdn_rows(S):
    return _tile(S, (256, 128, 64))


def _gdn_prep_rows(S):
    return _tile(S, (1024, 512, 256, 128, 64))


def gdn_prep_fwd(qkv, proj, ab_col, a_log_row, dt_row, H, *, name):
    S = qkv.shape[0]
    R = _gdn_prep_rows(S)
    n, nc = R // GDN_CHUNK, S // GDN_CHUNK

    def body(q_ref, k_ref, v_ref, ab_ref, al_ref, dt_ref, u_ref, w_ref, at_ref, qd_ref, kd_ref, gl_ref, t_ref):
        lmat, rhs, attn, qd, kd, gl = _gdn_local(q_ref[...], k_ref[...], v_ref[...], ab_ref[...], al_ref[...],
                                                 dt_ref[...], pl.program_id(1), H)
        tinv = _inv_unit_lower(lmat)
        sol = _hmm(tinv, rhs)
        outs = (sol[..., :128].reshape(R, 128), sol[..., 128:].reshape(R, 128), attn, qd, kd, gl,
                tinv.reshape(R, GDN_CHUNK))
        for r, v in zip((u_ref, w_ref, at_ref, qd_ref, kd_ref, gl_ref, t_ref), outs):
            r[...] = v

    head = lambda off: pl.BlockSpec((R, 128), lambda i, h: (i, off + h))
    row = pl.BlockSpec((1, 128), lambda i, h: (0, 0))
    big = jax.ShapeDtypeStruct((S, H * 128), F32)
    sq = pl.BlockSpec((None, R, GDN_CHUNK), lambda i, h: (h, i, 0))
    outs = pl.pallas_call(
        body, name=name, grid=(S // R, H),
        in_specs=[head(0), head(H), head(2 * H), pl.BlockSpec((R, 128), lambda i, h: (i, ab_col // 128)), row, row],
        out_specs=[head(0), head(0), sq, head(0), head(0), pl.BlockSpec((None, n, 1, 128), lambda i, h: (h, i, 0, 0)),
                   sq],
        out_shape=[big, big, jax.ShapeDtypeStruct((H, S, GDN_CHUNK), F32), big, big,
                   jax.ShapeDtypeStruct((H, nc, 1, 128), F32), jax.ShapeDtypeStruct((H, S, GDN_CHUNK), F32)],
        compiler_params=_cparams(("parallel", "parallel")),
    )(qkv, qkv, qkv, proj, a_log_row, dt_row)
    return tuple(outs[:6]), outs[6]


def gdn_prep_bwd(qkv, proj, ab_col, a_log_row, dt_row, H, u, w, tinv, cts, *, name):
    S = qkv.shape[0]
    R = _gdn_prep_rows(S)
    C = GDN_CHUNK
    n, nc = R // C, S // C

    def body(q_ref, k_ref, v_ref, ab_ref, al_ref, dt_ref, u_ref, w_ref, t_ref,
             du_ref, dw_ref, dat_ref, dqd_ref, dkd_ref, dgl_ref,
             dqkv_ref, dab_ref, dal_ref, ddt_ref):
        i, h = pl.program_id(0), pl.program_id(1)

        @pl.when(h == 0)
        def _():
            dab_ref[...] = jnp.zeros_like(dab_ref)

        @pl.when((h == 0) & (i == 0))
        def _():
            dal_ref[...] = jnp.zeros_like(dal_ref)
            ddt_ref[...] = jnp.zeros_like(ddt_ref)

        tinv_t = jnp.swapaxes(t_ref[...].reshape(n, C, C), 1, 2)
        dsol = jnp.concatenate([du_ref[...], dw_ref[...]], axis=-1).reshape(n, C, 256)
        sol = jnp.concatenate([u_ref[...], w_ref[...]], axis=-1).reshape(n, C, 256)
        drhs = _hmm(tinv_t, dsol)
        dlmat = -jnp.einsum("ncd,nsd->ncs", drhs, sol, precision=lax.Precision.HIGH, preferred_element_type=F32)
        f = lambda q, k, v, ab, al, dt: _gdn_local(q, k, v, ab, al, dt, h, H)
        _, vjp = jax.vjp(f, q_ref[...], k_ref[...], v_ref[...], ab_ref[...], al_ref[...], dt_ref[...])
        dq, dk, dv, dab, dal, ddt = vjp((dlmat, drhs, dat_ref[...], dqd_ref[...], dkd_ref[...], dgl_ref[...]))
        dqkv_ref[0] = dq
        dqkv_ref[1] = dk
        dqkv_ref[2] = dv
        dab_ref[...] += dab
        dal_ref[...] += dal
        ddt_ref[...] += ddt

    head = lambda off: pl.BlockSpec((R, 128), lambda i, h: (i, off + h))
    row = pl.BlockSpec((1, 128), lambda i, h: (0, 0))
    at = pl.BlockSpec((None, R, GDN_CHUNK), lambda i, h: (h, i, 0))
    gl = pl.BlockSpec((None, n, 1, 128), lambda i, h: (h, i, 0, 0))
    W = H * 128
    return pl.pallas_call(
        body, name=name, grid=(S // R, H),
        in_specs=[head(0), head(H), head(2 * H), pl.BlockSpec((R, 128), lambda i, h: (i, ab_col // 128)), row, row,
                  head(0), head(0), at, head(0), head(0), at, head(0), head(0), gl],
        out_specs=[pl.BlockSpec((3, R, 128), lambda i, h: (0, i, h)), pl.BlockSpec((R, 128), lambda i, h: (i, 0)),
                   row, row],
        out_shape=[jax.ShapeDtypeStruct((3, S, W), F32), jax.ShapeDtypeStruct((S, 128), F32)]
        + [jax.ShapeDtypeStruct((1, 128), F32)] * 2,
        compiler_params=_cparams(("arbitrary", "arbitrary")),
    )(qkv, qkv, qkv, proj, a_log_row, dt_row, u, w, tinv, *cts)


def _gdn_head_group(H):
    return next(g for g in (6, 4, 3, 2, 1) if H % g == 0)


def gdn_scan_fwd(u, w, attn, qd, kd, gl, *, name):
    S = u.shape[0]
    H = attn.shape[0]
    C = GDN_CHUNK
    R = _gdn_rows(S)
    n, nc = R // C, S // C

    G = _gdn_head_group(H)
    heads = range(G)
    col = lambda h: slice(h * 128, (h + 1) * 128)

    def body(u_ref, w_ref, at_ref, qd_ref, kd_ref, gl_ref, o_ref, st_ref, state):
        @pl.when(pl.program_id(1) == 0)
        def _():
            state[...] = jnp.zeros_like(state)

        for c in range(n):
            rows = slice(c * C, (c + 1) * C)
            s = [state[h] for h in heads]
            for h in heads:
                st_ref[h, c] = s[h]
            sb = [t.astype(BF16) for t in s]
            ws = [_dot(w_ref[rows, col(h)].astype(BF16), sb[h]) for h in heads]
            qs = [_dot(qd_ref[rows, col(h)].astype(BF16), sb[h]) for h in heads]
            vb = [(u_ref[rows, col(h)] - ws[h]).astype(BF16) for h in heads]
            av = [_dot(at_ref[h, rows, :].astype(BF16), vb[h]) for h in heads]
            kv = [_dot(kd_ref[rows, col(h)].astype(BF16), vb[h], 0, 0) for h in heads]
            for h in heads:
                o_ref[rows, col(h)] = qs[h] + av[h]
                state[h] = s[h] * gl_ref[h, c] + kv[h]

    head = pl.BlockSpec((R, G * 128), lambda g, i: (i, g))
    return pl.pallas_call(
        body, name=name, grid=(H // G, S // R),
        in_specs=[head, head, pl.BlockSpec((G, R, C), lambda g, i: (g, i, 0)), head, head,
                  pl.BlockSpec((G, n, 1, 128), lambda g, i: (g, i, 0, 0))],
        out_specs=[head, pl.BlockSpec((G, n, 128, 128), lambda g, i: (g, i, 0, 0))],
        out_shape=[jax.ShapeDtypeStruct((S, H * 128), F32), jax.ShapeDtypeStruct((H, nc, 128, 128), F32)],
        scratch_shapes=[pltpu.VMEM((G, 128, 128), F32)],
        compiler_params=_cparams(("parallel", "arbitrary")),
    )(u, w, attn, qd, kd, gl)


def gdn_scan_bwd(u, w, attn, qd, kd, gl, states, do, *, name):
    S = u.shape[0]
    H = attn.shape[0]
    C = GDN_CHUNK
    R = _gdn_rows(S)
    n, nc, nb = R // C, S // C, S // R

    G = _gdn_head_group(H)
    heads = range(G)
    col = lambda h: slice(h * 128, (h + 1) * 128)

    def body(u_ref, w_ref, at_ref, qd_ref, kd_ref, gl_ref, st_ref, do_ref,
             du_ref, dw_ref, dat_ref, dqd_ref, dkd_ref, dgl_ref, dstate):
        @pl.when(pl.program_id(1) == 0)
        def _():
            dstate[...] = jnp.zeros_like(dstate)

        lane = lax.broadcasted_iota(jnp.int32, (1, 128), 1)
        for c in reversed(range(n)):
            rows = slice(c * C, (c + 1) * C)
            tile = lambda ref: [ref[rows, col(h)].astype(BF16) for h in heads]
            s = [st_ref[h, c] for h in heads]
            ds2 = [dstate[h] for h in heads]
            sb = [t.astype(BF16) for t in s]
            ds2b = [t.astype(BF16) for t in ds2]
            wb, qdb, kdb, dob = tile(w_ref), tile(qd_ref), tile(kd_ref), tile(do_ref)
            atb = [at_ref[h, rows, :].astype(BF16) for h in heads]
            ws = [_dot(wb[h], sb[h]) for h in heads]
            dv1 = [_dot(atb[h], dob[h], 0, 0) for h in heads]
            dv2 = [_dot(kdb[h], ds2b[h]) for h in heads]
            dqd = [_dot(dob[h], sb[h], 1, 1) for h in heads]
            qdo = [_dot(qdb[h], dob[h], 0, 0) for h in heads]
            vb = [(u_ref[rows, col(h)] - ws[h]).astype(BF16) for h in heads]
            dv = [dv1[h] + dv2[h] for h in heads]
            dvb = [t.astype(BF16) for t in dv]
            dw = [_dot(dvb[h], sb[h], 1, 1) for h in heads]
            dat = [_dot(dob[h], vb[h], 1, 1) for h in heads]
            dkd = [_dot(vb[h], ds2b[h], 1, 1) for h in heads]
            wdv = [_dot(wb[h], dvb[h], 0, 0) for h in heads]
            for h in heads:
                du_ref[rows, col(h)] = dv[h]
                dw_ref[rows, col(h)] = -dw[h]
                dat_ref[h, rows, :] = dat[h]
                dqd_ref[rows, col(h)] = dqd[h]
                dkd_ref[rows, col(h)] = dkd[h]
                dgl = jnp.sum(jnp.sum(ds2[h] * s[h], axis=1, keepdims=True), axis=0, keepdims=True)
                dgl_ref[h, c] = jnp.where(lane == 0, dgl, 0.0)
                dstate[h] = ds2[h] * gl_ref[h, c] + qdo[h] - wdv[h]

    head = pl.BlockSpec((R, G * 128), lambda g, i: (nb - 1 - i, g))
    at = pl.BlockSpec((G, R, C), lambda g, i: (g, nb - 1 - i, 0))
    glb = pl.BlockSpec((G, n, 1, 128), lambda g, i: (g, nb - 1 - i, 0, 0))
    big = jax.ShapeDtypeStruct((S, H * 128), F32)
    return pl.pallas_call(
        body, name=name, grid=(H // G, nb),
        in_specs=[head, head, at, head, head, glb,
                  pl.BlockSpec((G, n, 128, 128), lambda g, i: (g, nb - 1 - i, 0, 0)), head],
        out_specs=[head, head, at, head, head, glb],
        out_shape=[big, big, jax.ShapeDtypeStruct((H, S, C), F32), big, big,
                   jax.ShapeDtypeStruct((H, nc, 1, 128), F32)],
        scratch_shapes=[pltpu.VMEM((G, 128, 128), F32)],
        compiler_params=_cparams(("parallel", "arbitrary")),
    )(u, w, attn, qd, kd, gl, states, do)


def _head_rms(o):
    return lax.rsqrt(jnp.mean(o * o, axis=-1, keepdims=True) + EPS)


def gdn_mixer_fwd(proj, W, p, tag):
    H = W // HEAD_DIM
    ab_col = 4 * W + MEM_WIDTH
    qkv = gdn_conv_fwd(proj, 3 * W, p["conv_w"], name=f"gdn_conv_fwd_{tag}")
    al_row, dt_row = _pad_row(p["a_log"]), _pad_row(p["dt_bias"])
    pre, tinv = gdn_prep_fwd(qkv, proj, ab_col, al_row, dt_row, H, name=f"gdn_prep_fwd_{tag}")
    o, states = gdn_scan_fwd(*pre, name=f"gdn_scan_fwd_{tag}")
    gn_row = jnp.tile(p["o_norm"].reshape(1, HEAD_DIM), (1, H))

    def gate_fwd(o, gate, gn):
        return o * _head_rms(o) * gn * (gate * jax.nn.sigmoid(gate))

    mix = ew(gate_fwd, [(o, 0), (proj, 3 * W)], [gn_row], [BF16], 0, width=W, tc=HEAD_DIM, name=f"gdn_gate_fwd_{tag}")
    return mix, (qkv, pre, tinv, states, o, gn_row, al_row, dt_row)


def gdn_mixer_bwd(proj, W, p, saved, dcat, tag):
    qkv, pre, tinv, states, o, gn_row, al_row, dt_row = saved
    H = W // HEAD_DIM
    ab_col = 4 * W + MEM_WIDTH

    def gate_bwd(dm, o, gate, gn):
        dm = dm.astype(F32)
        r = _head_rms(o)
        xh = o * r
        sg = jax.nn.sigmoid(gate)
        dy = dm * gate * sg
        dgate = dm * xh * gn * (sg * (1.0 + gate * (1.0 - sg)))
        dxh = dy * gn
        do = r * (dxh - xh * jnp.mean(dxh * xh, axis=-1, keepdims=True))
        return do, dgate, dy * xh

    do, dgate, dgn = ew(gate_bwd, [(dcat, 0), (o, 0), (proj, 3 * W)], [gn_row], [F32, BF16], 1, width=W, tc=HEAD_DIM,
                        name=f"gdn_gate_bwd_{tag}")
    cts = gdn_scan_bwd(*pre, states, do, name=f"gdn_scan_bwd_{tag}")
    dqkv, dab, dal, ddt = gdn_prep_bwd(qkv, proj, ab_col, al_row, dt_row, H, pre[0], pre[1], tinv, cts,
                                       name=f"gdn_prep_bwd_{tag}")
    dx, dconv = gdn_conv_bwd(proj, 3 * W, p["conv_w"], dqkv, name=f"gdn_conv_bwd_{tag}")
    grads = dict(conv_w=dconv, a_log=dal[0, :H], dt_bias=ddt[0, :H], o_norm=dgn.reshape(H, HEAD_DIM).sum(axis=0))
    return dx, dgate, dab, grads


def exchange(arrays, same_block, *, name):
    n = len(arrays)

    def body(*refs):
        ex = _Exchange(refs[:n], refs[n:2 * n], same_block, *refs[2 * n:])
        ex.start()
        ex.finish()

    shapes, sems = _exchange_shapes(arrays, same_block)
    any_spec = pl.BlockSpec(memory_space=pl.ANY)
    return pl.pallas_call(body, name=name, in_specs=[any_spec] * n, out_specs=[any_spec] * n, out_shape=shapes,
                          scratch_shapes=sems)(*arrays)


def _row_tile(R, row_bytes):
    for t in (512, 256, 128, 64, 32, 16, 8):
        if R % t == 0 and 2 * t * row_bytes <= 24 * 2 ** 20:
            return t
    return R


def adamw(parts, w, m, v, *, name):
    P, R, C = parts.shape
    tr = _row_tile(R, C * (P * parts.dtype.itemsize + 7 * 4))
    c1, c2 = 1.0 - ADAM_B1 ** ADAM_STEP, 1.0 - ADAM_B2 ** ADAM_STEP

    def body(p_ref, w_ref, m_ref, v_ref, g_ref, d_ref, nm_ref, nv_ref):
        g = p_ref[0].astype(F32)
        for s in range(1, P):
            g = g + p_ref[s].astype(F32)
        m = ADAM_B1 * m_ref[...] + (1.0 - ADAM_B1) * g
        v = ADAM_B2 * v_ref[...] + (1.0 - ADAM_B2) * (g * g)
        g_ref[...] = g
        nm_ref[...] = m
        nv_ref[...] = v
        d_ref[...] = -ADAM_LR * ((m / c1) / (jnp.sqrt(v / c2) + ADAM_EPS) + ADAM_WD * w_ref[...])

    blk = pl.BlockSpec((tr, C), lambda i: (i, 0))
    return pl.pallas_call(
        body, name=name, grid=(R // tr,),
        in_specs=[pl.BlockSpec((P, tr, C), lambda i: (0, i, 0)), blk, blk, blk],
        out_specs=[blk] * 4, out_shape=[jax.ShapeDtypeStruct((R, C), F32)] * 4,
        compiler_params=_cparams(("parallel",)),
    )(parts, w, m, v)


def sum_parts(parts, *, name):
    P, R, C = parts.shape
    tr = _row_tile(R, C * (P * parts.dtype.itemsize + 4))

    def body(p_ref, o_ref):
        g = p_ref[0].astype(F32)
        for s in range(1, P):
            g = g + p_ref[s].astype(F32)
        o_ref[...] = g

    return pl.pallas_call(
        body, name=name, grid=(R // tr,),
        in_specs=[pl.BlockSpec((P, tr, C), lambda i: (0, i, 0))], out_specs=pl.BlockSpec((tr, C), lambda i: (i, 0)),
        out_shape=jax.ShapeDtypeStruct((R, C), F32), compiler_params=_cparams(("parallel",)),
    )(parts)


PACK_COLS = 1024
PACK_ROWS = 256


def _pack(arrays, cols, lead=()):
    n_lead = len(lead)
    flat = [a.reshape(lead + (-1,)) for a in arrays]
    total = sum(f.shape[-1] for f in flat)
    rows = -(-total // cols)
    mult = PACK_ROWS if rows > PACK_ROWS else 8
    rows = -(-rows // mult) * mult
    pad = rows * cols - total
    if pad:
        flat.append(jnp.zeros(lead + (pad,), flat[0].dtype))
    return jnp.concatenate(flat, axis=n_lead).reshape(lead + (rows, cols))


def _unpack(buf, shapes, lead=()):
    flat = buf.reshape(lead + (-1,))
    out, off = [], 0
    for s in shapes:
        n = math.prod(s)
        out.append(lax.slice_in_dim(flat, off, off + n, axis=len(lead)).reshape(lead + tuple(s)))
        off += n
    return out


def _to_shards(full, axis):
    s = full.shape
    return jnp.moveaxis(full.reshape(s[:axis] + (N_DEV, s[axis] // N_DEV) + s[axis + 1:]), axis, 0)


def _from_shards(g, axis):
    m = jnp.moveaxis(g, 0, axis)
    s = m.shape
    return m.reshape(s[:axis] + (s[axis] * s[axis + 1],) + s[axis + 2:])


BIG = (("w_mem_kv", 0), ("w_out", 1), ("w_up", 2), ("w_down", 1), ("s5_w_in", 1), ("s5_w_glu", 1),
       ("gdn_w_in", 2), ("fox_w_in", 1))
SMALL_SHARDED = (("s5_d_skip", 1), ("s5_b_glu", 1), ("gdn_conv_w", 2))
REPLICATED = ("mem_norm", "norm1", "norm2", "norm_f", "s5_lam_re", "s5_lam_im", "s5_log_dt", "s5_b_re", "s5_b_im",
              "s5_c_re", "s5_c_im", "gdn_a_log", "gdn_dt_bias", "gdn_o_norm", "fox_b_f")
WEIGHTS = ("mem_norm", "w_mem_kv", "norm1", "w_out", "norm2", "w_up", "w_down", "norm_f", "s5_w_in", "s5_lam_re",
           "s5_lam_im", "s5_log_dt", "s5_b_re", "s5_b_im", "s5_c_re", "s5_c_im", "s5_d_skip", "s5_w_glu", "s5_b_glu",
           "gdn_w_in", "gdn_conv_w", "gdn_a_log", "gdn_dt_bias", "gdn_o_norm", "fox_w_in", "fox_b_f")


def _relu2(acc):
    r = jnp.maximum(acc, 0.0)
    return acc, r * r


def _relu2_grad(acc, u):
    return (acc * 2.0 * jnp.maximum(u, 0.0),)


def _add(acc, e):
    return (acc + e,)


def _permute_in(w, kind, W):
    if kind == 0:
        return w
    n_main = (4 if kind == 1 else 3) * W
    n_small = w.shape[1] - n_main - MEM_WIDTH
    small = jnp.pad(w[:, n_main:n_main + n_small], ((0, 0), (0, MEM_WIDTH - n_small)))
    return jnp.concatenate([w[:, :n_main], w[:, n_main + n_small:], small], axis=1)


def _unpermute_in(dw, kind, W, n_small):
    if kind == 0:
        return dw
    n_main = (4 if kind == 1 else 3) * W
    return jnp.concatenate([dw[:, :n_main], dw[:, n_main + MEM_WIDTH:n_main + MEM_WIDTH + n_small],
                            dw[:, n_main:n_main + MEM_WIDTH]], axis=1)


def kernel(x, mem, mem_norm, w_mem_kv, norm1, w_out, norm2, w_up, w_down, norm_f, s5_w_in, s5_lam_re, s5_lam_im, s5_log_dt, s5_b_re, s5_b_im, s5_c_re, s5_c_im, s5_d_skip, s5_w_glu, s5_b_glu, gdn_w_in, gdn_conv_w, gdn_a_log, gdn_dt_bias, gdn_o_norm, fox_w_in, fox_b_f, loss_target, m_mem_norm, m_w_mem_kv, m_norm1, m_w_out, m_norm2, m_w_up, m_w_down, m_norm_f, m_s5_w_in, m_s5_lam_re, m_s5_lam_im, m_s5_log_dt, m_s5_b_re, m_s5_b_im, m_s5_c_re, m_s5_c_im, m_s5_d_skip, m_s5_w_glu, m_s5_b_glu, m_gdn_w_in, m_gdn_conv_w, m_gdn_a_log, m_gdn_dt_bias, m_gdn_o_norm, m_fox_w_in, m_fox_b_f, v_mem_norm, v_w_mem_kv, v_norm1, v_w_out, v_norm2, v_w_up, v_w_down, v_norm_f, v_s5_w_in, v_s5_lam_re, v_s5_lam_im, v_s5_log_dt, v_s5_b_re, v_s5_b_im, v_s5_c_re, v_s5_c_im, v_s5_d_skip, v_s5_w_glu, v_s5_b_glu, v_gdn_w_in, v_gdn_conv_w, v_gdn_a_log, v_gdn_dt_bias, v_gdn_o_norm, v_fox_w_in, v_fox_b_f):
    args = dict(locals())
    wsh = {n: args[n] for n in WEIGHTS}
    msh = {n: args["m_" + n] for n in WEIGHTS}
    vsh = {n: args["v_" + n] for n in WEIGHTS}
    h0, memx, target = x[0], mem[0], loss_target[0]
    S, D = h0.shape
    W = D - MEM_WIDTH
    depth = norm1.shape[0]
    me = 4 * lax.axis_index("x") + 2 * lax.axis_index("y") + lax.axis_index("c")

    bf = lambda t: t.astype(BF16)
    rows2d = lambda g: g.reshape(-1, g.shape[-1])

    def layer_sends(i, group):
        kind, j = i % 3, i // 3
        if group == "up":
            return {"w_up": bf(wsh["w_up"][i])}
        if group == "down":
            return {"w_down": bf(wsh["w_down"][i])}
        d = {"w_out": bf(wsh["w_out"][i])}
        if kind == 0:
            d["w_in"], d["w_glu"] = bf(wsh["s5_w_in"][j]), bf(wsh["s5_w_glu"][j])
        elif kind == 1:
            d["w_in"] = bf(wsh["gdn_w_in"][j])
        else:
            d["w_in"] = bf(_permute_in(wsh["fox_w_in"][j], 2, W))
        return d

    def as_comm(d):
        return [(v, True) for v in d.values()]

    first = {**layer_sends(0, "in"), **layer_sends(0, "up"), **layer_sends(0, "down")}
    small_w = _pack([wsh[n] for n, _ in SMALL_SHARDED], 128)
    got = exchange(list(first.values()) + [bf(w_mem_kv), small_w], [True] * (len(first) + 2), name="gather_first")
    gathered = [dict(zip(first, got)) if i == 0 else {} for i in range(depth)]
    full = {n: wsh[n] for n in REPLICATED}
    full["w_mem_kv"] = rows2d(got[len(first)])
    for (n, ax), g in zip(SMALL_SHARDED, _unpack(got[-1], [wsh[n].shape for n, _ in SMALL_SHARDED], lead=(N_DEV,))):
        full[n] = _from_shards(g, ax)

    def layer_params(i):
        kind, j = i % 3, i // 3
        g = gathered[i]
        w = {n: rows2d(g[n]) for n in ("w_out", "w_down", "w_glu") if n in g}
        w["w_up"] = g["w_up"]
        w["w_in"] = _permute_in(_from_shards(g["w_in"], 1), 1, W) if kind == 1 else rows2d(g["w_in"])
        if kind == 0:
            p = {k: full["s5_" + k][j] for k in ("lam_re", "lam_im", "log_dt", "b_re", "b_im", "c_re", "c_im",
                                                   "d_skip", "b_glu")}
            return kind, j, p, w, 0
        if kind == 1:
            p = {k: full["gdn_" + k][j] for k in ("conv_w", "a_log", "dt_bias", "o_norm")}
            return kind, j, p, w, 2 * (W // HEAD_DIM)
        return kind, j, {"b_f": full["fox_b_f"][j]}, w, W // HEAD_DIM

    def hosted(i, group):
        if i + 1 >= depth:
            return {}, []
        d = layer_sends(i + 1, group)
        return {"comm": as_comm(d)}, list(d)

    def keep(i, names, res):
        if not names:
            return res
        outs, got = res
        gathered[i + 1].update(zip(names, got))
        return outs

    memn = rms_fwd(memx, full["mem_norm"], out_dtype=BF16, name="mem_rms")
    mkv = mm(memn, full["w_mem_kv"], name="mem_kv")
    h = h0
    saved = []
    weights = []
    for i in range(depth):
        kind, j, p, w, n_small = layer_params(i)
        weights.append((w, p, n_small))
        a = rms_fwd(h, full["norm1"][i], out_dtype=BF16, name=f"rms1_{i}")
        kw, names = hosted(i, "in")
        proj = keep(i, names, mm(a, w["w_in"], name=f"in_proj_{i}", **kw))
        if kind == 0:
            mix, ms = s5_mixer_fwd(proj, W, p, w["w_glu"], f"l{i}")
            mem_col = W
        elif kind == 1:
            mix, ms = gdn_mixer_fwd(proj, W, p, f"l{i}")
            mem_col = 4 * W
        else:
            mix, ms = fox_mixer_fwd(proj, 3 * W + MEM_WIDTH, p["b_f"], f"l{i}")
            mem_col = 3 * W
        read = mem_fwd(proj, mem_col, mkv, name=f"mem_fwd_{i}")
        cat = jnp.concatenate([mix, read], axis=1)
        h_mid = mm(cat, w["w_out"], extras=(h,), epilogue=_add, name=f"out_proj_{i}")
        a2 = rms_fwd(h_mid, full["norm2"][i], out_dtype=BF16, name=f"rms2_{i}")
        kw, names = hosted(i, "up")
        u, act = keep(i, names, mm(a2, w["w_up"], b_shard8=True, epilogue=_relu2, out_dtypes=(F32, BF16),
                                   name=f"up_{i}", **kw))
        kw, names = hosted(i, "down")
        h_next = keep(i, names, mm(act, w["w_down"], extras=(h_mid,), epilogue=_add, name=f"down_{i}", **kw))
        saved.append((h, a, proj, ms, mem_col, cat, h_mid, a2, u, act))
        h = h_next

    loss_row, dh, dnf = final_loss(h, full["norm_f"], target, name="final_loss")
    loss = lax.psum(jnp.sum(loss_row), ("x", "y", "c"))

    grads = {n: [None] * full[n].shape[0] for n in ("norm1", "norm2")}
    for pre, cnt in (("s5_", (depth + 2) // 3), ("gdn_", (depth + 1) // 3), ("fox_", depth // 3)):
        for n in REPLICATED + tuple(n for n, _ in SMALL_SHARDED):
            if n.startswith(pre):
                grads[n] = [None] * cnt
    shares = {}
    by_dest = lambda g: g.reshape((N_DEV, g.shape[0] // N_DEV) + g.shape[1:])
    pending_up, pending_rest = {}, {}

    def carry(pending):
        return {"comm": [(v, False) for v in pending.values()]} if pending else {}

    def landed(pending, res):
        if not pending:
            return res
        outs, got = res
        shares.update(zip(pending, got))
        return outs

    dmkv = None
    for i in reversed(range(depth)):
        kind, j = i % 3, i // 3
        w, p, n_small = weights[i]
        h_in, a, proj, ms, mem_col, cat, h_mid, a2, u, act = saved[i]
        du = landed(pending_up, mm(dh, w["w_down"], tb=True, extras=(u,), epilogue=_relu2_grad, out_dtypes=(BF16,),
                                   name=f"d_act_{i}", **carry(pending_up)))
        dw_down = landed(pending_rest, mm(act, dh, ta=True, out_dtypes=(BF16,), name=f"dw_down_{i}",
                                          **carry(pending_rest)))
        pending = {("w_down", i): by_dest(dw_down)}
        da2 = landed(pending, mm(du, w["w_up"], tb=True, b_shard8=True, name=f"d_a2_{i}", **carry(pending)))
        pending_up = {("w_up", i): mm(a2, du, ta=True, out_shard8=True, out_dtypes=(BF16,), name=f"dw_up_{i}")}
        dh_mid, dn2 = rms_bwd(h_mid, full["norm2"][i], da2, dh, name=f"rms2_bwd_{i}")
        grads["norm2"][i] = dn2[0]
        dcat = mm(dh_mid, w["w_out"], tb=True, name=f"d_cat_{i}")
        pending_rest = {("w_out", i): by_dest(mm(cat, dh_mid, ta=True, out_dtypes=(BF16,), name=f"dw_out_{i}"))}
        dq_mem, dmkv_i = mem_bwd(proj, mem_col, mkv, dcat, W, name=f"mem_bwd_{i}")
        dmkv = dmkv_i if dmkv is None else dmkv + dmkv_i
        if kind == 0:
            dmain, g = s5_mixer_bwd(proj, W, p, w["w_glu"], ms, dcat, f"l{i}")
            dproj = jnp.concatenate([dmain, dq_mem], axis=1)
            pending_rest[("s5_w_glu", j)] = by_dest(g.pop("w_glu"))
            for k, val in g.items():
                grads["s5_" + k][j] = val
        elif kind == 1:
            dx, dgate, dab, g = gdn_mixer_bwd(proj, W, p, ms, dcat, f"l{i}")
            dproj = jnp.concatenate([dx, dgate, dq_mem, bf(jnp.pad(dab, ((0, 0), (0, MEM_WIDTH - 128))))], axis=1)
            for k, val in g.items():
                grads["gdn_" + k][j] = val
        else:
            dq, dk, dv, df, db = fox_mixer_bwd(proj, 3 * W + MEM_WIDTH, ms, dcat, f"l{i}")
            dproj = jnp.concatenate([dq, dk, dv, dq_mem.astype(F32), jnp.pad(df, ((0, 0), (0, MEM_WIDTH - 128)))],
                                    axis=1)
            grads["fox_b_f"][j] = db
        da = mm(dproj, w["w_in"], tb=True, name=f"d_a_{i}")
        in_name = ("s5_w_in", "gdn_w_in", "fox_w_in")[kind]
        if kind == 1:
            dw_in = _unpermute_in(mm(a, dproj, ta=True, name=f"dw_in_{i}"), 1, W, n_small)
            pending_rest[(in_name, j)] = bf(_to_shards(dw_in, 1))
        else:
            pending_rest[(in_name, j)] = by_dest(mm(a, dproj, ta=True, out_dtypes=(BF16,), name=f"dw_in_{i}"))
        dh, dn1 = rms_bwd(h_in, full["norm1"][i], da, dh_mid, name=f"rms1_bwd_{i}")
        grads["norm1"][i] = dn1[0]
    grad_x = dh[None]
    grads = {n: jnp.stack(v) for n, v in grads.items()}
    grads["norm_f"] = dnf[0]
    pending_rest[("w_mem_kv", None)] = by_dest(mm(memn, dmkv, ta=True, out_dtypes=(BF16,), name="dw_mem_kv"))
    dmemn = mm(dmkv, full["w_mem_kv"], tb=True, name="d_memn")
    _, dmn = rms_bwd(memx, full["mem_norm"], dmemn, None, name="mem_rms_bwd")
    grads["mem_norm"] = dmn[0]

    small_names = list(REPLICATED) + [n for n, _ in SMALL_SHARDED]
    last = {**pending_up, **pending_rest}
    got = exchange(list(last.values()) + [_pack([grads[n] for n in small_names], 128)],
                   [False] * len(last) + [True], name="exchange_last")
    shares.update(zip(last, got))

    big_out = [{}, {}, {}, {}]
    for n, _ in BIG:
        per_layer = []
        for idx in ([None] if n == "w_mem_kv" else range(wsh[n].shape[0])):
            key = (n, idx)
            local = [t[n] if idx is None else t[n][idx] for t in (wsh, msh, vsh)]
            part = shares[key]
            if n == "fox_w_in":
                total = sum_parts(part, name=f"sum_{n}_{idx}")
                part = _unpermute_in(total, 2, W, W // HEAD_DIM)[None]
            per_layer.append(adamw(part, *local, name=f"adamw_{n}_{idx}"))
        for o, vals in zip(big_out, zip(*per_layer)):
            o[n] = vals[0] if n == "w_mem_kv" else jnp.stack(vals)

    total = sum_parts(got[-1], name="sum_small_grads")
    gsmall = dict(zip(small_names, _unpack(total, [grads[n].shape for n in small_names])))
    for n, ax in SMALL_SHARDED:
        width = wsh[n].shape[ax]
        gsmall[n] = lax.dynamic_slice_in_dim(gsmall[n], me * width, width, axis=ax)
    outs = adamw(_pack([gsmall[n] for n in small_names], 128)[None],
                 *[_pack([t[n] for n in small_names], 128) for t in (wsh, msh, vsh)], name="adamw_small")
    small_out = [dict(zip(small_names, _unpack(o, [wsh[n].shape for n in small_names]))) for o in outs]

    res = [{**b, **s} for b, s in zip(big_out, small_out)]
    return (loss, grad_x, *[r[n] for r in res for n in WEIGHTS])
```

```python
import functools
import math

import jax
import jax.numpy as jnp
from jax import lax
from jax.experimental import pallas as pl
from jax.experimental.pallas import tpu as pltpu

F32 = jnp.float32
BF16 = jnp.bfloat16

HEAD_DIM = 128
MEM_HEADS = 4
MEM_WIDTH = MEM_HEADS * HEAD_DIM
S5_GROUP = 16
S5_STATE = 64
GDN_CHUNK = 64
EPS = 1e-6
ADAM_LR, ADAM_B1, ADAM_B2, ADAM_EPS, ADAM_WD, ADAM_STEP = 0.001, 0.9, 0.999, 1e-08, 0.01, 10

N_DEV = 8
MESH = pl.DeviceIdType.MESH
VMEM_LIMIT = 56 * 1024 * 1024
HI = lax.Precision.HIGHEST


def _tile(n, prefs=(1024, 512, 256, 128)):
    for t in prefs:
        if n % t == 0:
            return t
    return n


def _cparams(sem=None):
    return pltpu.CompilerParams(dimension_semantics=sem, vmem_limit_bytes=VMEM_LIMIT)


def _dot(a, b, ca=1, cb=0, precision=None):
    return lax.dot_general(a, b, (((ca,), (cb,)), ((), ())), preferred_element_type=F32, precision=precision)


def _bdot(a, b):
    return _dot(a.astype(BF16), b.astype(BF16))


SIBLING = 1
OTHER_CHIPS = (2, 4, 6)


class _Exchange:
    def __init__(self, p_refs, out_refs, same_block, send_sems, recv_sems, local_sems):
        self.pos = (lax.axis_index("x"), lax.axis_index("y"), lax.axis_index("c"))
        self.arrays = list(zip(p_refs, out_refs, same_block))
        self.sems = (send_sems, recv_sems, local_sems)

    def _dev(self, m):
        return tuple(1 - v if (m >> (2 - b)) & 1 else v for b, v in enumerate(self.pos))

    def _slot(self, m):
        d = self._dev(m)
        return 4 * d[0] + 2 * d[1] + d[2]

    def _copy(self, n, src, slot, sem, to):
        k = n * (N_DEV - 1) + sem - 1
        return pltpu.make_async_remote_copy(
            src_ref=src, dst_ref=self.arrays[n][1].at[slot], send_sem=self.sems[0].at[k], recv_sem=self.sems[1].at[k],
            device_id=self._dev(to), device_id_type=MESH)

    def _local(self, n):
        p_ref, out_ref, same = self.arrays[n]
        return pltpu.make_async_copy(p_ref if same else p_ref.at[self._slot(0)], out_ref.at[self._slot(0)],
                                     self.sems[2].at[n])

    def _sends(self, n):
        p_ref, _, same = self.arrays[n]
        if same:
            return [self._copy(n, p_ref, self._slot(0), m, m) for m in (SIBLING,) + OTHER_CHIPS]
        return [self._copy(n, p_ref.at[self._slot(m)], self._slot(0), m, m) for m in range(1, N_DEV)]

    def _passed_on(self, n, m):
        return self._copy(n, self.arrays[n][1].at[self._slot(m)], self._slot(m), m ^ SIBLING, SIBLING)

    def _arrival(self, n, m):
        return self._copy(n, self.arrays[n][1].at[self._slot(m)], self._slot(m), m, m)

    def start(self):
        for n in range(len(self.arrays)):
            self._local(n).start()
            for cp in self._sends(n):
                cp.start()

    def finish(self):
        for n, (_, _, same) in enumerate(self.arrays):
            if same:
                for m in OTHER_CHIPS:
                    self._arrival(n, m).wait_recv()
                    self._passed_on(n, m).start()
        for n, (_, _, same) in enumerate(self.arrays):
            if same:
                for m in OTHER_CHIPS:
                    self._arrival(n, m ^ SIBLING).wait_recv()
                    self._passed_on(n, m).wait_send()
                self._arrival(n, SIBLING).wait_recv()
                for cp in self._sends(n):
                    cp.wait_send()
            else:
                for cp in self._sends(n):
                    cp.wait()
            self._local(n).wait()


def _exchange_shapes(arrays, same_block):
    shapes = [jax.ShapeDtypeStruct((N_DEV,) + tuple(p.shape if same else p.shape[1:]), p.dtype)
              for p, same in zip(arrays, same_block)]
    n = len(arrays)
    sems = [pltpu.SemaphoreType.DMA((n * (N_DEV - 1),)), pltpu.SemaphoreType.DMA((n * (N_DEV - 1),)),
            pltpu.SemaphoreType.DMA((n,))]
    return shapes, sems


def _mm_tk(K, tm, tn, a_bytes, b_bytes, out_bytes):
    for tk in (2048, 1024, 512, 256, 128):
        if K % tk == 0 and 2 * tk * (tm * a_bytes + tn * b_bytes) + (2 * out_bytes + 4) * tm * tn <= 40 * 2 ** 20:
            return tk
    return K


def mm(a, b, *, ta=False, tb=False, extras=(), epilogue=None, out_dtypes=(F32,), name, b_shard8=False,
       out_shard8=False, comm=()):
    M, K = (a.shape[1], a.shape[0]) if ta else a.shape
    if b_shard8:
        brows, bcols = b.shape[1], b.shape[2] * N_DEV
    else:
        brows, bcols = b.shape
    N = brows if tb else bcols
    assert K == (bcols if tb else brows), (a.shape, b.shape, ta, tb)
    tm = _tile(M)
    tn = _tile(N // N_DEV if (out_shard8 or (b_shard8 and not tb)) else N)
    out_bytes = sum(jnp.dtype(d).itemsize for d in out_dtypes) + 4 * sum(e.shape[0] != 1 for e in extras)
    tk = _mm_tk(K // N_DEV if (b_shard8 and tb) else K, tm, tn, a.dtype.itemsize, b.dtype.itemsize, out_bytes)
    ni, nj, nk = M // tm, N // tn, K // tk
    n_ex, n_out, n_comm = len(extras), len(out_dtypes), len(comm)
    ca, cb = (0 if ta else 1), (1 if tb else 0)
    same_block = [s for _, s in comm]

    def body(a_ref, b_ref, *rest):
        ex_refs, rest = rest[:n_ex], rest[n_ex:]
        cin, rest = rest[:n_comm], rest[n_comm:]
        out_refs, rest = rest[:n_out], rest[n_out:]
        cout, rest = rest[:n_comm], rest[n_comm:]
        acc = rest[0]
        i, j, k = pl.program_id(0), pl.program_id(1), pl.program_id(2)
        if n_comm:
            @pl.when((i == 0) & (j == 0) & (k == 0))
            def _():
                _Exchange(cin, cout, same_block, *rest[1:]).start()

        def finish(res):
            outs = (res,) if epilogue is None else epilogue(res, *[e[...] for e in ex_refs])
            for o, v in zip(out_refs, outs):
                o[...] = v.astype(o.dtype)

        part = _dot(a_ref[...].astype(BF16), b_ref[...].astype(BF16), ca, cb)
        if nk == 1:
            finish(part)
        else:
            @pl.when(k == 0)
            def _():
                acc[...] = part

            @pl.when(k > 0)
            def _():
                acc[...] += part

            @pl.when(k == nk - 1)
            def _():
                finish(acc[...])

        if n_comm:
            @pl.when((i == ni - 1) & (j == nj - 1) & (k == nk - 1))
            def _():
                _Exchange(cin, cout, same_block, *rest[1:]).finish()

    a_spec = pl.BlockSpec((tk, tm), lambda i, j, k: (k, i)) if ta else pl.BlockSpec((tm, tk), lambda i, j, k: (i, k))
    if b_shard8 and tb:
        kper = bcols // N_DEV // tk
        b_spec = pl.BlockSpec((None, tn, tk), lambda i, j, k: (k // kper, j, k % kper))
    elif b_shard8:
        nper = bcols // N_DEV // tn
        b_spec = pl.BlockSpec((None, tk, tn), lambda i, j, k: (j // nper, k, j % nper))
    elif tb:
        b_spec = pl.BlockSpec((tn, tk), lambda i, j, k: (j, k))
    else:
        b_spec = pl.BlockSpec((tk, tn), lambda i, j, k: (k, j))
    ex_specs = [pl.BlockSpec((1, tn), lambda i, j, k: (0, j)) if e.shape[0] == 1 and M != 1
                else pl.BlockSpec((tm, tn), lambda i, j, k: (i, j)) for e in extras]
    if out_shard8:
        nper = N // N_DEV // tn
        out_spec = pl.BlockSpec((None, tm, tn), lambda i, j, k: (j // nper, i, j % nper))
        out_shape = (N_DEV, M, N // N_DEV)
    else:
        out_spec = pl.BlockSpec((tm, tn), lambda i, j, k: (i, j))
        out_shape = (M, N)
    comm_shapes, sems = _exchange_shapes([p for p, _ in comm], same_block) if n_comm else ([], [])
    any_spec = pl.BlockSpec(memory_space=pl.ANY)
    outs = pl.pallas_call(
        body, name=name, grid=(ni, nj, nk),
        in_specs=[a_spec, b_spec] + ex_specs + [any_spec] * n_comm,
        out_specs=[out_spec] * n_out + [any_spec] * n_comm,
        out_shape=[jax.ShapeDtypeStruct(out_shape, d) for d in out_dtypes] + comm_shapes,
        scratch_shapes=[pltpu.VMEM((tm, tn) if nk > 1 else (8, 128), F32)] + sems,
        compiler_params=_cparams(("arbitrary",) * 3 if n_comm else ("parallel", "parallel", "arbitrary")),
    )(a, b, *extras, *[p for p, _ in comm])
    if n_comm:
        return (outs[0] if n_out == 1 else tuple(outs[:n_out])), list(outs[n_out:])
    return outs[0] if n_out == 1 else outs


def rms_fwd(x, g, *, out_dtype, name):
    S, D = x.shape
    ts = _tile(S, (512, 256, 128))

    def body(x_ref, g_ref, y_ref):
        x = x_ref[...]
        r = lax.rsqrt(jnp.mean(x * x, axis=-1, keepdims=True) + EPS)
        y_ref[...] = (x * r * g_ref[...]).astype(y_ref.dtype)

    return pl.pallas_call(
        body, name=name, grid=(S // ts,),
        in_specs=[pl.BlockSpec((ts, D), lambda i: (i, 0)), pl.BlockSpec((1, D), lambda i: (0, 0))],
        out_specs=pl.BlockSpec((ts, D), lambda i: (i, 0)),
        out_shape=jax.ShapeDtypeStruct((S, D), out_dtype),
        compiler_params=_cparams(("parallel",)),
    )(x, g.reshape(1, D))


def rms_bwd(x, g, dy, dres, *, name):
    S, D = x.shape
    ts = _tile(S, (512, 256, 128))
    has_res = dres is not None

    def body(x_ref, g_ref, dy_ref, *rest):
        dx_ref, dg_ref = rest[-2:]
        x = x_ref[...]
        r = lax.rsqrt(jnp.mean(x * x, axis=-1, keepdims=True) + EPS)
        xh = x * r
        dy = dy_ref[...].astype(F32)
        dxh = dy * g_ref[...]
        dx = r * (dxh - xh * jnp.mean(dxh * xh, axis=-1, keepdims=True))
        if has_res:
            dx = dx + rest[0][...]
        dx_ref[...] = dx

        @pl.when(pl.program_id(0) == 0)
        def _():
            dg_ref[...] = jnp.zeros_like(dg_ref)

        dg_ref[...] += jnp.sum(dy * xh, axis=0, keepdims=True)

    tok = pl.BlockSpec((ts, D), lambda i: (i, 0))
    row = pl.BlockSpec((1, D), lambda i: (0, 0))
    return pl.pallas_call(
        body, name=name, grid=(S // ts,),
        in_specs=[tok, row, tok] + ([tok] if has_res else []),
        out_specs=[tok, row],
        out_shape=[jax.ShapeDtypeStruct((S, D), F32), jax.ShapeDtypeStruct((1, D), F32)],
        compiler_params=_cparams(("arbitrary",)),
    )(x, g.reshape(1, D), dy, *([dres] if has_res else []))


def final_loss(h, g, target, *, name):
    S, D = h.shape
    ts = _tile(S, (512, 256, 128))

    def body(x_ref, g_ref, t_ref, loss_ref, dx_ref, dg_ref):
        x = x_ref[...]
        r = lax.rsqrt(jnp.mean(x * x, axis=-1, keepdims=True) + EPS)
        xh = x * r
        err = xh * g_ref[...] - t_ref[...]
        dy = err * (1.0 / D)
        dxh = dy * g_ref[...]
        dx_ref[...] = r * (dxh - xh * jnp.mean(dxh * xh, axis=-1, keepdims=True))

        @pl.when(pl.program_id(0) == 0)
        def _():
            dg_ref[...] = jnp.zeros_like(dg_ref)
            loss_ref[...] = jnp.zeros_like(loss_ref)

        dg_ref[...] += jnp.sum(dy * xh, axis=0, keepdims=True)
        loss_ref[...] += jnp.sum(err * err, axis=0, keepdims=True) * (0.5 / D)

    tok = pl.BlockSpec((ts, D), lambda i: (i, 0))
    row = pl.BlockSpec((1, D), lambda i: (0, 0))
    return pl.pallas_call(
        body, name=name, grid=(S // ts,),
        in_specs=[tok, row, tok], out_specs=[row, tok, row],
        out_shape=[jax.ShapeDtypeStruct((1, D), F32), jax.ShapeDtypeStruct((S, D), F32),
                   jax.ShapeDtypeStruct((1, D), F32)],
        compiler_params=_cparams(("arbitrary",)),
    )(h, g.reshape(1, D), target)


def _mem_probs(q, k):
    s = _dot(q.astype(BF16), k.astype(BF16), 1, 1) * HEAD_DIM ** -0.5
    p = jnp.exp(s - jnp.max(s, axis=-1, keepdims=True))
    return p / jnp.sum(p, axis=-1, keepdims=True)


def mem_fwd(proj, q_col, mkv, *, name):
    S = proj.shape[0]
    L = mkv.shape[0]
    ts = _tile(S)

    def body(q_ref, kv_ref, o_ref):
        for h in range(MEM_HEADS):
            c = slice(h * HEAD_DIM, (h + 1) * HEAD_DIM)
            v = kv_ref[:, MEM_WIDTH + h * HEAD_DIM:MEM_WIDTH + (h + 1) * HEAD_DIM]
            p = _mem_probs(q_ref[:, c], kv_ref[:, c])
            o_ref[:, c] = _bdot(p, v).astype(o_ref.dtype)

    return pl.pallas_call(
        body, name=name, grid=(S // ts,),
        in_specs=[pl.BlockSpec((ts, MEM_WIDTH), lambda i: (i, q_col // MEM_WIDTH)),
                  pl.BlockSpec((L, 2 * MEM_WIDTH), lambda i: (0, 0))],
        out_specs=pl.BlockSpec((ts, MEM_WIDTH), lambda i: (i, 0)),
        out_shape=jax.ShapeDtypeStruct((S, MEM_WIDTH), BF16),
        compiler_params=_cparams(("parallel",)),
    )(proj, mkv)


def mem_bwd(proj, q_col, mkv, dcat, do_col, *, name):
    S = proj.shape[0]
    L = mkv.shape[0]
    ts = _tile(S)
    scale = HEAD_DIM ** -0.5

    def body(q_ref, kv_ref, do_ref, dq_ref, dkv_ref):
        @pl.when(pl.program_id(0) == 0)
        def _():
            dkv_ref[...] = jnp.zeros_like(dkv_ref)

        for h in range(MEM_HEADS):
            c = slice(h * HEAD_DIM, (h + 1) * HEAD_DIM)
            cv = slice(MEM_WIDTH + h * HEAD_DIM, MEM_WIDTH + (h + 1) * HEAD_DIM)
            q, k, v = q_ref[:, c].astype(BF16), kv_ref[:, c].astype(BF16), kv_ref[:, cv].astype(BF16)
            do = do_ref[:, c].astype(BF16)
            p = _mem_probs(q, k)
            dkv_ref[:, cv] += _dot(p.astype(BF16), do, 0, 0)
            dp = _dot(do, v, 1, 1)
            ds = (p * (dp - jnp.sum(dp * p, axis=-1, keepdims=True)) * scale).astype(BF16)
            dq_ref[:, c] = _dot(ds, k).astype(dq_ref.dtype)
            dkv_ref[:, c] += _dot(ds, q, 0, 0)

    return pl.pallas_call(
        body, name=name, grid=(S // ts,),
        in_specs=[pl.BlockSpec((ts, MEM_WIDTH), lambda i: (i, q_col // MEM_WIDTH)),
                  pl.BlockSpec((L, 2 * MEM_WIDTH), lambda i: (0, 0)),
                  pl.BlockSpec((ts, MEM_WIDTH), lambda i: (i, do_col // MEM_WIDTH))],
        out_specs=[pl.BlockSpec((ts, MEM_WIDTH), lambda i: (i, 0)),
                   pl.BlockSpec((L, 2 * MEM_WIDTH), lambda i: (0, 0))],
        out_shape=[jax.ShapeDtypeStruct((S, MEM_WIDTH), BF16), jax.ShapeDtypeStruct((L, 2 * MEM_WIDTH), F32)],
        compiler_params=_cparams(("arbitrary",)),
    )(proj, mkv, dcat)


def _lower_ones(n, strict=False):
    r = lax.broadcasted_iota(jnp.int32, (n, n), 0)
    c = lax.broadcasted_iota(jnp.int32, (n, n), 1)
    return (r > c if strict else r >= c).astype(F32)


def fox_gate_fwd(proj, f_col, b_f_row, *, name):
    S = proj.shape[0]
    tb = _tile(S, (256, 128))

    def body(f_ref, b_ref, c_ref, carry):
        @pl.when(pl.program_id(0) == 0)
        def _():
            carry[...] = jnp.zeros_like(carry)

        ls = jax.nn.log_sigmoid(f_ref[...] + b_ref[...])
        cum = _dot(_lower_ones(tb), ls, precision=HI) + carry[...]
        c_ref[...] = cum
        carry[...] = cum[tb - 1:tb, :]

    return pl.pallas_call(
        body, name=name, grid=(S // tb,),
        in_specs=[pl.BlockSpec((tb, 128), lambda i: (i, f_col // 128)), pl.BlockSpec((1, 128), lambda i: (0, 0))],
        out_specs=pl.BlockSpec((tb, 128), lambda i: (i, 0)),
        out_shape=jax.ShapeDtypeStruct((S, 128), F32),
        scratch_shapes=[pltpu.VMEM((1, 128), F32)],
        compiler_params=_cparams(("arbitrary",)),
    )(proj, b_f_row)


def fox_gate_bwd(dcf, proj, f_col, b_f_row, *, name):
    S = proj.shape[0]
    tb = _tile(S, (256, 128))
    nb = S // tb

    def body(d_ref, f_ref, b_ref, df_ref, db_ref, carry):
        @pl.when(pl.program_id(0) == 0)
        def _():
            carry[...] = jnp.zeros_like(carry)
            db_ref[...] = jnp.zeros_like(db_ref)

        upper = _lower_ones(tb).T
        rc = _dot(upper, d_ref[...], precision=HI) + carry[...]
        carry[...] = rc[0:1, :]
        df = rc * jax.nn.sigmoid(-(f_ref[...] + b_ref[...]))
        df_ref[...] = df
        db_ref[...] += jnp.sum(df, axis=0, keepdims=True)

    return pl.pallas_call(
        body, name=name, grid=(nb,),
        in_specs=[pl.BlockSpec((tb, 128), lambda i: (nb - 1 - i, 0)),
                  pl.BlockSpec((tb, 128), lambda i: (nb - 1 - i, f_col // 128)),
                  pl.BlockSpec((1, 128), lambda i: (0, 0))],
        out_specs=[pl.BlockSpec((tb, 128), lambda i: (nb - 1 - i, 0)), pl.BlockSpec((1, 128), lambda i: (0, 0))],
        out_shape=[jax.ShapeDtypeStruct((S, 128), F32), jax.ShapeDtypeStruct((1, 128), F32)],
        scratch_shapes=[pltpu.VMEM((1, 128), F32)],
        compiler_params=_cparams(("arbitrary",)),
    )(dcf, proj, b_f_row)


def _fox_block(S):
    return _tile(S, (512, 256, 128)) if S > 512 else S // 2


FOX_Q_BLOCKS = 2


def _causal(s, row0):
    r = row0 + lax.broadcasted_iota(jnp.int32, s.shape, 0)
    c = lax.broadcasted_iota(jnp.int32, s.shape, 1)
    return jnp.where(r >= c, s, -jnp.inf)


def fox_fwd(proj, n_heads, cf_col, cf_row, *, name):
    S = proj.shape[0]
    H = n_heads
    bk = _fox_block(S)
    qmul = FOX_Q_BLOCKS if S % (FOX_Q_BLOCKS * bk) == 0 else 1
    bq = qmul * bk
    scale = HEAD_DIM ** -0.5

    def body(q_ref, k_ref, v_ref, cc_ref, cr_ref, o_ref, st_ref):
        qi = pl.program_id(1)
        q = q_ref[...].astype(BF16)
        cq = cc_ref[...]

        def step(j, carry, diagonal=None):
            rows = pl.ds(pl.multiple_of(j * bk, bk), bk)
            k = k_ref[rows, :].astype(BF16)
            v = v_ref[rows, :].astype(BF16)
            m, l, acc = carry
            s = _dot(q, k, 1, 1) * scale + cq - cr_ref[j]
            if diagonal is not None:
                s = _causal(s, -diagonal * bk)
            m2 = jnp.maximum(m, jnp.max(s, axis=-1, keepdims=True))
            p = jnp.exp(s - m2)
            a = jnp.exp(m - m2)
            return m2, a * l + jnp.sum(p, axis=-1, keepdims=True), a * acc + _dot(p.astype(BF16), v)

        init = (jnp.full((bq, 1), -jnp.inf, F32), jnp.zeros((bq, 1), F32), jnp.zeros((bq, HEAD_DIM), F32))
        carry = lax.fori_loop(0, qi * qmul, step, init)
        for d in range(qmul):
            carry = step(qi * qmul + d, carry, diagonal=d)
        m, l, acc = carry
        o_ref[...] = (acc / l).astype(o_ref.dtype)
        lane = lax.broadcasted_iota(jnp.int32, (bq, 128), 1)
        st_ref[...] = jnp.where(lane == 0, m + jnp.log(l), jnp.where(lane == 1, cq, 0.0))

    return pl.pallas_call(
        body, name=name, grid=(H, S // bq),
        in_specs=[pl.BlockSpec((bq, HEAD_DIM), lambda h, i: (i, h)),
                  pl.BlockSpec((S, HEAD_DIM), lambda h, i: (0, H + h)),
                  pl.BlockSpec((S, HEAD_DIM), lambda h, i: (0, 2 * H + h)),
                  pl.BlockSpec((None, bq, 1), lambda h, i: (h, i, 0)),
                  pl.BlockSpec((None, S // bk, 1, bk), lambda h, i: (h, 0, 0, 0))],
        out_specs=[pl.BlockSpec((bq, HEAD_DIM), lambda h, i: (i, h)),
                   pl.BlockSpec((None, bq, 128), lambda h, i: (h, i, 0))],
        out_shape=[jax.ShapeDtypeStruct((S, H * HEAD_DIM), BF16), jax.ShapeDtypeStruct((H, S, 128), F32)],
        compiler_params=_cparams(("parallel", "parallel")),
    )(proj, proj, proj, cf_col, cf_row)


def fox_delta(stats, o, dcat, *, name):
    H, S, _ = stats.shape
    ts = _tile(S)

    def body(st_ref, o_ref, do_ref, out_ref):
        d = jnp.sum(o_ref[...].astype(F32) * do_ref[...].astype(F32), axis=-1, keepdims=True)
        lane = lax.broadcasted_iota(jnp.int32, (ts, 128), 1)
        out_ref[...] = jnp.where(lane == 2, d, st_ref[...])

    return pl.pallas_call(
        body, name=name, grid=(H, S // ts),
        in_specs=[pl.BlockSpec((None, ts, 128), lambda h, i: (h, i, 0)),
                  pl.BlockSpec((ts, HEAD_DIM), lambda h, i: (i, h)),
                  pl.BlockSpec((ts, HEAD_DIM), lambda h, i: (i, h))],
        out_specs=pl.BlockSpec((None, ts, 128), lambda h, i: (h, i, 0)),
        out_shape=jax.ShapeDtypeStruct((H, S, 128), F32),
        compiler_params=_cparams(("parallel", "parallel")),
    )(stats, o, dcat)


def fox_bwd(proj, n_heads, stats, cf_row, dcat, *, name):
    S = proj.shape[0]
    H = n_heads
    bq = _fox_block(S)
    nq = S // bq
    qmul = FOX_Q_BLOCKS if S % (FOX_Q_BLOCKS * bq) == 0 else 1
    bt = qmul * bq
    scale = HEAD_DIM ** -0.5

    def body(q_ref, k_ref, v_ref, do_ref, st_ref, cr_ref, dq_ref, dk_ref, dv_ref, dst_ref, dcr_ref):
        kj = pl.program_id(1)

        @pl.when(kj == 0)
        def _():
            dq_ref[...] = jnp.zeros_like(dq_ref)
            dst_ref[...] = jnp.zeros_like(dst_ref)

        k = k_ref[...].astype(BF16)
        v = v_ref[...].astype(BF16)
        ck = cr_ref[...]
        lane = lax.broadcasted_iota(jnp.int32, (bt, 128), 1)

        def step(t, carry, diagonal=False):
            dk, dv, dck = carry
            rows = pl.ds(pl.multiple_of(t * bt, bt), bt)
            q = q_ref[rows, :].astype(BF16)
            do = do_ref[rows, :].astype(BF16)
            st = st_ref[rows, :]
            lse, cq, delta = st[:, 0:1], st[:, 1:2], st[:, 2:3]
            s = _dot(q, k, 1, 1) * scale + cq - ck
            if diagonal:
                s = _causal(s, -lax.rem(kj, jnp.int32(qmul)) * bq)
            p = jnp.exp(s - lse)
            dv = dv + _dot(p.astype(BF16), do, 0, 0)
            ds = p * (_dot(do, v, 1, 1) - delta)
            dsb = (ds * scale).astype(BF16)
            dq_ref[rows, :] += _dot(dsb, k)
            dst_ref[rows, :] += jnp.where(lane == 0, jnp.sum(ds, axis=-1, keepdims=True), 0.0)
            return dk + _dot(dsb, q, 0, 0), dv, dck - jnp.sum(ds, axis=0, keepdims=True)

        init = (jnp.zeros((bq, HEAD_DIM), F32), jnp.zeros((bq, HEAD_DIM), F32), jnp.zeros((1, bq), F32))
        first = lax.div(kj, jnp.int32(qmul))
        dk, dv, dck = lax.fori_loop(first + 1, S // bt, step, step(first, init, diagonal=True))
        dk_ref[...] = dk
        dv_ref[...] = dv
        dcr_ref[...] = dck

    W = H * HEAD_DIM
    return pl.pallas_call(
        body, name=name, grid=(H, nq),
        in_specs=[pl.BlockSpec((S, HEAD_DIM), lambda h, j: (0, h)),
                  pl.BlockSpec((bq, HEAD_DIM), lambda h, j: (j, H + h)),
                  pl.BlockSpec((bq, HEAD_DIM), lambda h, j: (j, 2 * H + h)),
                  pl.BlockSpec((S, HEAD_DIM), lambda h, j: (0, h)),
                  pl.BlockSpec((None, S, 128), lambda h, j: (h, 0, 0)),
                  pl.BlockSpec((None, None, 1, bq), lambda h, j: (h, j, 0, 0))],
        out_specs=[pl.BlockSpec((S, HEAD_DIM), lambda h, j: (0, h)),
                   pl.BlockSpec((bq, HEAD_DIM), lambda h, j: (j, h)),
                   pl.BlockSpec((bq, HEAD_DIM), lambda h, j: (j, h)),
                   pl.BlockSpec((None, S, 128), lambda h, j: (h, 0, 0)),
                   pl.BlockSpec((None, None, 1, bq), lambda h, j: (h, j, 0, 0))],
        out_shape=[jax.ShapeDtypeStruct((S, W), F32), jax.ShapeDtypeStruct((S, W), F32),
                   jax.ShapeDtypeStruct((S, W), F32), jax.ShapeDtypeStruct((H, S, 128), F32),
                   jax.ShapeDtypeStruct((H, nq, 1, bq), F32)],
        compiler_params=_cparams(("parallel", "arbitrary")),
    )(proj, proj, proj, dcat, stats, cf_row)


def _pad_row(v, n=128):
    return jnp.pad(v.astype(F32), (0, n - v.shape[0])).reshape(1, n)


def fox_mixer_fwd(proj, f_col, b_f, tag):
    S = proj.shape[0]
    H = b_f.shape[0]
    bq = _fox_block(S)
    b_row = _pad_row(b_f)
    cf = fox_gate_fwd(proj, f_col, b_row, name=f"fox_gate_fwd_{tag}")
    cf_t = cf[:, :H].T
    o, stats = fox_fwd(proj, H, cf_t.reshape(H, S, 1), cf_t.reshape(H, S // bq, 1, bq), name=f"fox_fwd_{tag}")
    return o, (o, stats, cf_t, b_row)


def fox_mixer_bwd(proj, f_col, saved, dcat, tag):
    o, stats, cf_t, b_row = saved
    H, S = cf_t.shape
    bq = _fox_block(S)
    stats = fox_delta(stats, o, dcat, name=f"fox_delta_{tag}")
    dq, dk, dv, dst, dcr = fox_bwd(proj, H, stats, cf_t.reshape(H, S // bq, 1, bq), dcat, name=f"fox_bwd_{tag}")
    dcf = jnp.sum(dst, axis=-1) + dcr.reshape(H, S)
    dcf = jnp.pad(dcf.T, ((0, 0), (0, 128 - H)))
    df, db = fox_gate_bwd(dcf, proj, f_col, b_row, name=f"fox_gate_bwd_{tag}")
    return dq, dk, dv, df, db[0, :H]


def ew(fn, tok_ins, row_ins, tok_out_dtypes, n_row_out, *, width, name, tc=None, ts=None):
    S = tok_ins[0][0].shape[0]
    tc = tc or _tile(width, (512, 256, 128))
    ts = ts or _tile(S, (512, 256, 128))
    n_tok, n_row, n_to = len(tok_ins), len(row_ins), len(tok_out_dtypes)
    for _, col in tok_ins:
        assert col % tc == 0

    def body(*refs):
        ins = [r[...] for r in refs[:n_tok + n_row]]
        outs = fn(*ins)
        outs = outs if isinstance(outs, (tuple, list)) else (outs,)
        o_refs = refs[n_tok + n_row:]
        for o, v in zip(o_refs[:n_to], outs[:n_to]):
            o[...] = v.astype(o.dtype)
        if n_row_out:
            @pl.when(pl.program_id(1) == 0)
            def _():
                for o in o_refs[n_to:]:
                    o[...] = jnp.zeros_like(o)

            for o, v in zip(o_refs[n_to:], outs[n_to:]):
                o[...] += jnp.sum(v, axis=0, keepdims=True)

    def tok_spec(col):
        return pl.BlockSpec((ts, tc), lambda j, i: (i, col // tc + j))

    row_spec = pl.BlockSpec((1, tc), lambda j, i: (0, j))
    res = pl.pallas_call(
        body, name=name, grid=(width // tc, S // ts),
        in_specs=[tok_spec(col) for _, col in tok_ins] + [row_spec] * n_row,
        out_specs=[tok_spec(0)] * n_to + [row_spec] * n_row_out,
        out_shape=[jax.ShapeDtypeStruct((S, width), d) for d in tok_out_dtypes]
        + [jax.ShapeDtypeStruct((1, width), F32)] * n_row_out,
        compiler_params=_cparams(("parallel", "arbitrary" if n_row_out else "parallel")),
    )(*[a for a, _ in tok_ins], *row_ins)
    return res[0] if len(res) == 1 else res


S5_SUPER = 128 // S5_GROUP
S5_COLS = S5_SUPER * S5_STATE


def _shift_rows(x, d, up=False):
    T = x.shape[0]
    if d % 8 == 0:
        z = jnp.zeros((d, x.shape[1]), x.dtype)
        return jnp.concatenate([x[d:], z], axis=0) if up else jnp.concatenate([z, x[:T - d]], axis=0)
    row = lax.broadcasted_iota(jnp.int32, x.shape, 0)
    if up:
        return jnp.where(row < T - d, pltpu.roll(x, T - d, axis=0), 0.0)
    return jnp.where(row >= d, pltpu.roll(x, d, axis=0), 0.0)


def _s5_scan(bre, bim, tab, hre_s, him_s, carry, up):
    T, N = bre.shape
    G = T // 8
    hre, him = bre.reshape(G, 8, N), bim.reshape(G, 8, N)
    for k in range(3):
        shift = (8 - (1 << k)) if up else (1 << k)
        sre, sim = pltpu.roll(hre, shift, axis=1), pltpu.roll(him, shift, axis=1)
        cre, cim = tab[0, k], tab[1, k]
        hre, him = hre + cre * sre - cim * sim, him + cre * sim + cim * sre
    hre_s[...] = hre
    him_s[...] = him
    are, aim = tab[0, 3], tab[1, 3]
    edge = slice(0, 1) if up else slice(7, 8)

    def tile(i, c):
        g = G - 1 - i if up else i
        r = hre_s[g] + are * c[0] - aim * c[1]
        m = him_s[g] + are * c[1] + aim * c[0]
        hre_s[g] = r
        him_s[g] = m
        return r[edge], m[edge]

    carry = lax.fori_loop(0, G, tile, carry)
    return hre_s[...].reshape(T, N), him_s[...].reshape(T, N), carry


def _s5_states(u, bbre, bbim, tab, hre_s, him_s, start):
    return _s5_scan(_bdot(u, bbre), _bdot(u, bbim), tab, hre_s, him_s, (start[0:1], start[1:2]), False)


def _s5_tables(a_re, a_im, NJ):
    are, aim = a_re.reshape(NJ, 1, S5_COLS), a_im.reshape(NJ, 1, S5_COLS)
    pows = [(are, aim)]
    for _ in range(7):
        r, i = pows[-1]
        pows.append((r * are - i * aim, r * aim + i * are))
    row = jnp.arange(8).reshape(1, 8, 1)
    tabs = []
    for up in (False, True):
        re, im = [], []
        for d in (1, 2, 4):
            keep = (row <= 7 - d) if up else (row >= d)
            re.append(jnp.where(keep, pows[d - 1][0], 0.0))
            im.append(jnp.where(keep, -pows[d - 1][1] if up else pows[d - 1][1], 0.0))
        order = range(7, -1, -1) if up else range(8)
        re.append(jnp.concatenate([pows[n][0] for n in order], axis=1))
        im.append(jnp.concatenate([-pows[n][1] if up else pows[n][1] for n in order], axis=1))
        tabs.append(jnp.stack([jnp.stack(re, axis=1), jnp.stack(im, axis=1)], axis=1))
    return tabs


def _s5_block(S):
    return _tile(S, (512, 256, 128))


def _s5_specs(T):
    mat = lambda r, c: pl.BlockSpec((None, r, c), lambda j, t: (j, 0, 0))
    tab = pl.BlockSpec((None, 2, 4, 8, S5_COLS), lambda j, t: (j, 0, 0, 0, 0))
    scratch = pltpu.VMEM((T // 8, 8, S5_COLS), F32)
    return mat, tab, scratch


def s5_scan_fwd(proj, W, bbre, bbim, cre, cim, tab_f, tab_r, *, name):
    S = proj.shape[0]
    T = _s5_block(S)
    NJ, nblk = W // 128, S // T

    def body(u_ref, bbre_ref, bbim_ref, cre_ref, cim_ref, tab_ref, y_ref, st_ref, carry, hre_s, him_s):
        @pl.when(pl.program_id(1) == 0)
        def _():
            carry[...] = jnp.zeros_like(carry)

        st_ref[...] = carry[...]
        hre, him, (lre, lim) = _s5_states(u_ref[...], bbre_ref[...], bbim_ref[...], tab_ref[...], hre_s, him_s,
                                          carry[...])
        carry[0:1, :] = lre
        carry[1:2, :] = lim
        y_ref[...] = _bdot(hre, cre_ref[...]) - _bdot(him, cim_ref[...])

    mat, tab, scratch = _s5_specs(T)
    return pl.pallas_call(
        body, name=name, grid=(NJ, nblk),
        in_specs=[pl.BlockSpec((T, 128), lambda j, t: (t, j)), mat(128, S5_COLS), mat(128, S5_COLS),
                  mat(S5_COLS, 128), mat(S5_COLS, 128), tab],
        out_specs=[pl.BlockSpec((T, 128), lambda j, t: (t, j)),
                   pl.BlockSpec((None, None, 2, S5_COLS), lambda j, t: (j, t, 0, 0))],
        out_shape=[jax.ShapeDtypeStruct((S, W), F32), jax.ShapeDtypeStruct((NJ, nblk, 2, S5_COLS), F32)],
        scratch_shapes=[pltpu.VMEM((2, S5_COLS), F32), scratch, scratch],
        compiler_params=_cparams(("parallel", "arbitrary")),
    )(proj, bbre, bbim, cre, cim, tab_f)


def s5_scan_bwd(proj, W, dy, d_skip_row, bbre, bbim, cre, cim, tab_f, tab_r, starts, *, name):
    S = proj.shape[0]
    T = _s5_block(S)
    NJ, nblk = W // 128, S // T

    def body(u_ref, dy_ref, d_ref, bbre_ref, bbim_ref, cre_ref, cim_ref, tabf_ref, tabr_ref, st_ref,
             du_ref, dbbre_ref, dbbim_ref, dcre_ref, dcim_ref, da_ref, gcarry, hre_s, him_s, gre_s, gim_s):
        @pl.when(pl.program_id(1) == 0)
        def _():
            gcarry[...] = jnp.zeros_like(gcarry)
            for r in (dbbre_ref, dbbim_ref, dcre_ref, dcim_ref, da_ref):
                r[...] = jnp.zeros_like(r)

        u, dy, start = u_ref[...], dy_ref[...], st_ref[...]
        bbre, bbim, cre, cim = (r[...].astype(BF16) for r in (bbre_ref, bbim_ref, cre_ref, cim_ref))
        hre, him, _ = _s5_states(u, bbre, bbim, tabf_ref[...], hre_s, him_s, start)
        dyb, ub = dy.astype(BF16), u.astype(BF16)
        gre, gim, (fre, fim) = _s5_scan(_dot(dyb, cre, 1, 1), -_dot(dyb, cim, 1, 1), tabr_ref[...], gre_s, gim_s,
                                        (gcarry[0:1], gcarry[1:2]), True)
        gcarry[0:1, :] = fre
        gcarry[1:2, :] = fim
        greb, gimb = gre.astype(BF16), gim.astype(BF16)
        du_ref[...] = (_dot(greb, bbre, 1, 1) + _dot(gimb, bbim, 1, 1) + dy * d_ref[...]).astype(du_ref.dtype)
        dbbre_ref[...] += _dot(ub, greb, 0, 0)
        dbbim_ref[...] += _dot(ub, gimb, 0, 0)
        dcre_ref[...] += _dot(hre.astype(BF16), dyb, 0, 0)
        dcim_ref[...] -= _dot(him.astype(BF16), dyb, 0, 0)
        first = lax.broadcasted_iota(jnp.int32, hre.shape, 0) == 0
        pre = _shift_rows(hre, 1) + jnp.where(first, start[0:1], 0.0)
        pim = _shift_rows(him, 1) + jnp.where(first, start[1:2], 0.0)
        da_ref[0:1, :] += jnp.sum(gre * pre + gim * pim, axis=0, keepdims=True)
        da_ref[1:2, :] += jnp.sum(gim * pre - gre * pim, axis=0, keepdims=True)

    mat, tab, scratch = _s5_specs(T)
    tok = pl.BlockSpec((T, 128), lambda j, t: (nblk - 1 - t, j))
    return pl.pallas_call(
        body, name=name, grid=(NJ, nblk),
        in_specs=[tok, tok, pl.BlockSpec((1, 128), lambda j, t: (0, j)), mat(128, S5_COLS), mat(128, S5_COLS),
                  mat(S5_COLS, 128), mat(S5_COLS, 128), tab, tab,
                  pl.BlockSpec((None, None, 2, S5_COLS), lambda j, t: (j, nblk - 1 - t, 0, 0))],
        out_specs=[tok, mat(128, S5_COLS), mat(128, S5_COLS), mat(S5_COLS, 128), mat(S5_COLS, 128), mat(2, S5_COLS)],
        out_shape=[jax.ShapeDtypeStruct((S, W), BF16),
                   jax.ShapeDtypeStruct((NJ, 128, S5_COLS), F32), jax.ShapeDtypeStruct((NJ, 128, S5_COLS), F32),
                   jax.ShapeDtypeStruct((NJ, S5_COLS, 128), F32), jax.ShapeDtypeStruct((NJ, S5_COLS, 128), F32),
                   jax.ShapeDtypeStruct((NJ, 2, S5_COLS), F32)],
        scratch_shapes=[pltpu.VMEM((2, S5_COLS), F32)] + [scratch] * 4,
        compiler_params=_cparams(("parallel", "arbitrary")),
    )(proj, dy, d_skip_row, bbre, bbim, cre, cim, tab_f, tab_r, starts)


def _s5_discretize(lam_re, lam_im, log_dt, b_re, b_im):
    dt = jnp.exp(log_dt)[:, None]
    mag = jnp.exp(lam_re * dt)
    a_re, a_im = mag * jnp.cos(lam_im * dt), mag * jnp.sin(lam_im * dt)
    den = lam_re * lam_re + lam_im * lam_im
    z_re = ((a_re - 1.0) * lam_re + a_im * lam_im) / den
    z_im = (a_im * lam_re - (a_re - 1.0) * lam_im) / den
    bb_re = z_re[..., None] * b_re - z_im[..., None] * b_im
    bb_im = z_re[..., None] * b_im + z_im[..., None] * b_re
    return a_re, a_im, bb_re, bb_im


def _blockdiag(x):
    G, r, c = x.shape
    x = x.reshape(G // S5_SUPER, S5_SUPER, r, c)
    eye = jnp.eye(S5_SUPER, dtype=x.dtype)
    return (x[:, :, :, None, :] * eye[None, :, None, :, None]).reshape(G // S5_SUPER, S5_SUPER * r, S5_SUPER * c)


def _blockdiag_t(x, r, c):
    NJ = x.shape[0]
    x = x.reshape(NJ, S5_SUPER, r, S5_SUPER, c)
    return jnp.stack([x[:, i, :, i, :] for i in range(S5_SUPER)], axis=1).reshape(NJ * S5_SUPER, r, c)


def _gelu(x):
    c = math.sqrt(2.0 / math.pi)
    return 0.5 * x * (1.0 + jnp.tanh(c * (x + 0.044715 * x * x * x)))


def _gelu_grad(x):
    c = math.sqrt(2.0 / math.pi)
    t = jnp.tanh(c * (x + 0.044715 * x * x * x))
    return 0.5 * (1.0 + t) + 0.5 * x * (1.0 - t * t) * c * (1.0 + 3 * 0.044715 * x * x)


def s5_mixer_fwd(proj, W, p, w_glu_bf, tag):
    a_re, a_im, bb_re, bb_im = _s5_discretize(p["lam_re"], p["lam_im"], p["log_dt"], p["b_re"], p["b_im"])
    tab_f, tab_r = _s5_tables(a_re, a_im, W // 128)
    bbre = _blockdiag(jnp.swapaxes(bb_re, 1, 2))
    bbim = _blockdiag(jnp.swapaxes(bb_im, 1, 2))
    cre = _blockdiag(jnp.swapaxes(p["c_re"], 1, 2))
    cim = _blockdiag(jnp.swapaxes(p["c_im"], 1, 2))
    mats = (bbre, bbim, cre, cim, tab_f, tab_r)
    y, starts = s5_scan_fwd(proj, W, *mats, name=f"s5_scan_fwd_{tag}")
    d_row = p["d_skip"].reshape(1, W)
    yy = ew(lambda y, u, d: _gelu(y + d * u), [(y, 0), (proj, 0)], [d_row], [F32], 0, width=W,
            name=f"s5_act_fwd_{tag}")
    mix, z = mm(yy, w_glu_bf, extras=(yy, p["b_glu"].reshape(1, W)), out_dtypes=(BF16, F32),
                epilogue=lambda acc, yy, b: (yy * jax.nn.sigmoid(acc + b), acc + b), name=f"s5_glu_{tag}")
    return mix, (mats, starts, y, yy, z, d_row)


def s5_mixer_bwd(proj, W, p, w_glu_bf, saved, dcat, tag):
    mats, starts, y, yy, z, d_row = saved

    def glu_bwd(dm, yy, z):
        sg = jax.nn.sigmoid(z)
        dz = dm.astype(F32) * yy * sg * (1.0 - sg)
        return dm.astype(F32) * sg, dz, dz

    dyy1, dz, db_glu = ew(glu_bwd, [(dcat, 0), (yy, 0), (z, 0)], [], [F32, F32], 1, width=W, name=f"s5_glu_bwd_{tag}")
    dw_glu = mm(yy, dz, ta=True, out_dtypes=(BF16,), name=f"s5_dwglu_{tag}")
    dyy = mm(dz, w_glu_bf, tb=True, extras=(dyy1,), epilogue=lambda acc, e: (acc + e,), name=f"s5_dyy_{tag}")

    def act_bwd(dyy, y, u, d):
        dpre = dyy * _gelu_grad(y + d * u)
        return dpre, dpre * u

    dy, dd = ew(act_bwd, [(dyy, 0), (y, 0), (proj, 0)], [d_row], [F32], 1, width=W, name=f"s5_act_bwd_{tag}")
    du, dbbre, dbbim, dcre, dcim, da = s5_scan_bwd(proj, W, dy, d_row, *mats, starts, name=f"s5_scan_bwd_{tag}")
    G = W // S5_GROUP
    dbb_re = jnp.swapaxes(_blockdiag_t(dbbre, S5_GROUP, S5_STATE), 1, 2)
    dbb_im = jnp.swapaxes(_blockdiag_t(dbbim, S5_GROUP, S5_STATE), 1, 2)
    dc_re = jnp.swapaxes(_blockdiag_t(dcre, S5_STATE, S5_GROUP), 1, 2)
    dc_im = jnp.swapaxes(_blockdiag_t(dcim, S5_STATE, S5_GROUP), 1, 2)
    da_re, da_im = da[:, 0, :].reshape(G, S5_STATE), da[:, 1, :].reshape(G, S5_STATE)
    _, vjp = jax.vjp(_s5_discretize, p["lam_re"], p["lam_im"], p["log_dt"], p["b_re"], p["b_im"])
    dlam_re, dlam_im, dlog_dt, db_re, db_im = vjp((da_re, da_im, dbb_re, dbb_im))
    grads = dict(lam_re=dlam_re, lam_im=dlam_im, log_dt=dlog_dt, b_re=db_re, b_im=db_im, c_re=dc_re, c_im=dc_im,
                 d_skip=dd.reshape(W), w_glu=dw_glu, b_glu=db_glu.reshape(W))
    return du, grads


CONV_ROWS = 256


def _conv_taps(ext, w):
    acc = ext * w[3:4]
    for j in range(1, 4):
        acc = acc + pltpu.roll(ext, j, axis=0) * w[3 - j:4 - j]
    return acc


def gdn_conv_fwd(proj, width, conv_w, *, name):
    S = proj.shape[0]
    tc = _tile(width, (256, 128))
    T = min(CONV_ROWS, S)
    n = S // T

    def body(x_ref, w_ref, y_ref):
        w = w_ref[...]

        def chunk(c, carry):
            base = pl.multiple_of(c * T, T)
            prev = x_ref[pl.ds(pl.multiple_of(jnp.maximum(base - 8, 0), 8), 8), :]
            ext = jnp.concatenate([jnp.where(c > 0, prev, 0.0), x_ref[pl.ds(base, T), :]], axis=0)
            pre = _conv_taps(ext, w)[8:]
            y_ref[pl.ds(base, T), :] = pre * jax.nn.sigmoid(pre)
            return carry

        lax.fori_loop(0, n, chunk, 0)

    return pl.pallas_call(
        body, name=name, grid=(width // tc,),
        in_specs=[pl.BlockSpec((S, tc), lambda j: (0, j)), pl.BlockSpec((4, tc), lambda j: (0, j))],
        out_specs=pl.BlockSpec((S, tc), lambda j: (0, j)),
        out_shape=jax.ShapeDtypeStruct((S, width), F32),
        compiler_params=_cparams(("parallel",)),
    )(proj, conv_w)


def gdn_conv_bwd(proj, width, conv_w, dy, *, name):
    S = proj.shape[0]
    tc = _tile(width // 3, (256, 128))
    per = width // 3 // tc
    T = min(CONV_ROWS, S)
    n = S // T
    E = T + 16

    def body(x_ref, w_ref, dy_ref, dx_ref, dw_ref):
        w = w_ref[...]

        def halo(ref, start, keep):
            start = pl.multiple_of(jnp.clip(start, 0, S - 8), 8)
            return jnp.where(keep, ref[pl.ds(start, 8), :], 0.0)

        def chunk(c, dw):
            base = pl.multiple_of(c * T, T)
            rows = pl.ds(base, T)
            ext = jnp.concatenate([halo(x_ref, base - 8, c > 0), x_ref[rows, :], halo(x_ref, base + T, c < n - 1)], axis=0)
            dye = jnp.concatenate([jnp.zeros((8, tc), F32), dy_ref[rows, :], halo(dy_ref, base + T, c < n - 1)], axis=0)
            pre = _conv_taps(ext, w)
            sg = jax.nn.sigmoid(pre)
            dpre = dye * (sg * (1.0 + pre * (1.0 - sg)))
            dx = dpre * w[3:4]
            for j in range(1, 4):
                dx = dx + pltpu.roll(dpre, E - j, axis=0) * w[3 - j:4 - j]
            dx_ref[rows, :] = dx[8:8 + T].astype(dx_ref.dtype)
            own = dpre[8:8 + T]
            parts = [jnp.sum(own * pltpu.roll(ext, 3 - i, axis=0)[8:8 + T], axis=0, keepdims=True) if i < 3
                     else jnp.sum(own * ext[8:8 + T], axis=0, keepdims=True) for i in range(4)]
            return dw + jnp.concatenate(parts, axis=0)

        dw_ref[...] = lax.fori_loop(0, n, chunk, jnp.zeros((4, tc), F32))

    return pl.pallas_call(
        body, name=name, grid=(width // tc,),
        in_specs=[pl.BlockSpec((S, tc), lambda j: (0, j)), pl.BlockSpec((4, tc), lambda j: (0, j)),
                  pl.BlockSpec((None, S, tc), lambda j: (j // per, 0, j % per))],
        out_specs=[pl.BlockSpec((S, tc), lambda j: (0, j)), pl.BlockSpec((4, tc), lambda j: (0, j))],
        out_shape=[jax.ShapeDtypeStruct((S, width), BF16), jax.ShapeDtypeStruct((4, width), F32)],
        compiler_params=_cparams(("parallel",)),
    )(proj, conv_w, dy)


@functools.partial(jax.custom_vjp, nondiff_argnums=(0,))
def _bein(spec, a, b):
    return jnp.einsum(spec, a.astype(BF16), b.astype(BF16), preferred_element_type=F32)


def _bein_fwd(spec, a, b):
    return _bein(spec, a, b), (a, b)


def _bein_bwd(spec, res, g):
    a, b = res
    ins, out = spec.split("->")
    sa, sb = ins.split(",")
    return _bein(f"{out},{sb}->{sa}", g, b), _bein(f"{sa},{out}->{sb}", a, g)


_bein.defvjp(_bein_fwd, _bein_bwd)


def _hmm(a, b, precision=lax.Precision.HIGH):
    return jnp.einsum("ncs,nsd->ncd", a, b, precision=precision, preferred_element_type=F32)


def _inv_unit_lower(L):
    C = L.shape[-1]
    r = lax.broadcasted_iota(jnp.int32, L.shape, 1)
    c = lax.broadcasted_iota(jnp.int32, L.shape, 2)
    eye = (r == c).astype(F32)
    D = jnp.where(jnp.right_shift(r, 4) == jnp.right_shift(c, 4), L, 0.0)
    D2 = _hmm(D, D)
    D4 = _hmm(D2, D2)
    D8 = _hmm(D4, D4)
    dinv = _hmm(_hmm(_hmm(eye - D, eye + D2), eye + D4), eye + D8)
    N = _hmm(dinv, L - D)
    return _hmm(_hmm(eye - N, eye + _hmm(N, N)), dinv)


def _softplus(x):
    return jnp.maximum(x, 0.0) + jnp.log(1.0 + jnp.exp(-jnp.abs(x)))


def _gdn_local(qc, kc, vc, ab, a_log_row, dt_row, h, n_heads):
    R = qc.shape[0]
    C = GDN_CHUNK
    n = R // C
    lane = lax.broadcasted_iota(jnp.int32, (1, 128), 1)
    pick = lambda x, i: jnp.sum(jnp.where(lane == i, x, 0.0), axis=-1, keepdims=True)
    a_in, b_in = pick(ab, h).reshape(n, C, 1), pick(ab, n_heads + h).reshape(n, C, 1)
    a_log, dt_bias = pick(a_log_row, h), pick(dt_row, h)
    q3, k3, v = qc.reshape(n, C, 128), kc.reshape(n, C, 128), vc.reshape(n, C, 128)
    q = q3 * lax.rsqrt(jnp.sum(q3 * q3, axis=-1, keepdims=True) + EPS) * HEAD_DIM ** -0.5
    k = k3 * lax.rsqrt(jnp.sum(k3 * k3, axis=-1, keepdims=True) + EPS)
    beta = jax.nn.sigmoid(b_in)
    g = -jnp.exp(a_log) * _softplus(a_in + dt_bias)
    r = lax.broadcasted_iota(jnp.int32, (n, C, C), 1)
    c = lax.broadcasted_iota(jnp.int32, (n, C, C), 2)
    gc = _hmm((r >= c).astype(F32), jnp.broadcast_to(g, (n, C, C)), precision=HI)
    gcol = gc[:, :, 0:1]
    grow = jnp.sum(jnp.where(r == c, gc, 0.0), axis=1, keepdims=True)
    decay = jnp.exp(jnp.where(r >= c, gc - grow, -jnp.inf))
    kb, vb = k * beta, v * beta
    lmat = jnp.where(r > c, _bein("ncd,nsd->ncs", kb, k) * decay, 0.0)
    eg = jnp.exp(gcol)
    rhs = jnp.concatenate([vb, kb * eg], axis=-1)
    attn = jnp.where(r >= c, _bein("ncd,nsd->ncs", q, k) * decay, 0.0)
    glast = gcol[:, C - 1:C, :]
    k_dec = k * jnp.exp(glast - gcol)
    g_last = jnp.broadcast_to(jnp.exp(glast), (n, 1, 128))
    return lmat, rhs, attn.reshape(R, C), (q * eg).reshape(R, 128), k_dec.reshape(R, 128), g_last


def _gdn_rows(S):
    return _tile(S, (256, 128, 64))


def _gdn_prep_rows(S):
    return _tile(S, (1024, 512, 256, 128, 64))


def gdn_prep_fwd(qkv, proj, ab_col, a_log_row, dt_row, H, *, name):
    S = qkv.shape[0]
    R = _gdn_prep_rows(S)
    n, nc = R // GDN_CHUNK, S // GDN_CHUNK

    def body(q_ref, k_ref, v_ref, ab_ref, al_ref, dt_ref, u_ref, w_ref, at_ref, qd_ref, kd_ref, gl_ref, t_ref):
        lmat, rhs, attn, qd, kd, gl = _gdn_local(q_ref[...], k_ref[...], v_ref[...], ab_ref[...], al_ref[...],
                                                 dt_ref[...], pl.program_id(1), H)
        tinv = _inv_unit_lower(lmat)
        sol = _hmm(tinv, rhs)
        outs = (sol[..., :128].reshape(R, 128), sol[..., 128:].reshape(R, 128), attn, qd, kd, gl,
                tinv.reshape(R, GDN_CHUNK))
        for r, v in zip((u_ref, w_ref, at_ref, qd_ref, kd_ref, gl_ref, t_ref), outs):
            r[...] = v

    head = lambda off: pl.BlockSpec((R, 128), lambda i, h: (i, off + h))
    row = pl.BlockSpec((1, 128), lambda i, h: (0, 0))
    big = jax.ShapeDtypeStruct((S, H * 128), F32)
    sq = pl.BlockSpec((None, R, GDN_CHUNK), lambda i, h: (h, i, 0))
    outs = pl.pallas_call(
        body, name=name, grid=(S // R, H),
        in_specs=[head(0), head(H), head(2 * H), pl.BlockSpec((R, 128), lambda i, h: (i, ab_col // 128)), row, row],
        out_specs=[head(0), head(0), sq, head(0), head(0), pl.BlockSpec((None, n, 1, 128), lambda i, h: (h, i, 0, 0)),
                   sq],
        out_shape=[big, big, jax.ShapeDtypeStruct((H, S, GDN_CHUNK), F32), big, big,
                   jax.ShapeDtypeStruct((H, nc, 1, 128), F32), jax.ShapeDtypeStruct((H, S, GDN_CHUNK), F32)],
        compiler_params=_cparams(("parallel", "parallel")),
    )(qkv, qkv, qkv, proj, a_log_row, dt_row)
    return tuple(outs[:6]), outs[6]


def gdn_prep_bwd(qkv, proj, ab_col, a_log_row, dt_row, H, u, w, tinv, cts, *, name):
    S = qkv.shape[0]
    R = _gdn_prep_rows(S)
    C = GDN_CHUNK
    n, nc = R // C, S // C

    def body(q_ref, k_ref, v_ref, ab_ref, al_ref, dt_ref, u_ref, w_ref, t_ref,
             du_ref, dw_ref, dat_ref, dqd_ref, dkd_ref, dgl_ref,
             dqkv_ref, dab_ref, dal_ref, ddt_ref):
        i, h = pl.program_id(0), pl.program_id(1)

        @pl.when(h == 0)
        def _():
            dab_ref[...] = jnp.zeros_like(dab_ref)

        @pl.when((h == 0) & (i == 0))
        def _():
            dal_ref[...] = jnp.zeros_like(dal_ref)
            ddt_ref[...] = jnp.zeros_like(ddt_ref)

        tinv_t = jnp.swapaxes(t_ref[...].reshape(n, C, C), 1, 2)
        dsol = jnp.concatenate([du_ref[...], dw_ref[...]], axis=-1).reshape(n, C, 256)
        sol = jnp.concatenate([u_ref[...], w_ref[...]], axis=-1).reshape(n, C, 256)
        drhs = _hmm(tinv_t, dsol)
        dlmat = -jnp.einsum("ncd,nsd->ncs", drhs, sol, precision=lax.Precision.HIGH, preferred_element_type=F32)
        f = lambda q, k, v, ab, al, dt: _gdn_local(q, k, v, ab, al, dt, h, H)
        _, vjp = jax.vjp(f, q_ref[...], k_ref[...], v_ref[...], ab_ref[...], al_ref[...], dt_ref[...])
        dq, dk, dv, dab, dal, ddt = vjp((dlmat, drhs, dat_ref[...], dqd_ref[...], dkd_ref[...], dgl_ref[...]))
        dqkv_ref[0] = dq
        dqkv_ref[1] = dk
        dqkv_ref[2] = dv
        dab_ref[...] += dab
        dal_ref[...] += dal
        ddt_ref[...] += ddt

    head = lambda off: pl.BlockSpec((R, 128), lambda i, h: (i, off + h))
    row = pl.BlockSpec((1, 128), lambda i, h: (0, 0))
    at = pl.BlockSpec((None, R, GDN_CHUNK), lambda i, h: (h, i, 0))
    gl = pl.BlockSpec((None, n, 1, 128), lambda i, h: (h, i, 0, 0))
    W = H * 128
    return pl.pallas_call(
        body, name=name, grid=(S // R, H),
        in_specs=[head(0), head(H), head(2 * H), pl.BlockSpec((R, 128), lambda i, h: (i, ab_col // 128)), row, row,
                  head(0), head(0), at, head(0), head(0), at, head(0), head(0), gl],
        out_specs=[pl.BlockSpec((3, R, 128), lambda i, h: (0, i, h)), pl.BlockSpec((R, 128), lambda i, h: (i, 0)),
                   row, row],
        out_shape=[jax.ShapeDtypeStruct((3, S, W), F32), jax.ShapeDtypeStruct((S, 128), F32)]
        + [jax.ShapeDtypeStruct((1, 128), F32)] * 2,
        compiler_params=_cparams(("arbitrary", "arbitrary")),
    )(qkv, qkv, qkv, proj, a_log_row, dt_row, u, w, tinv, *cts)


def _gdn_head_group(H):
    return next(g for g in (6, 4, 3, 2, 1) if H % g == 0)


def gdn_scan_fwd(u, w, attn, qd, kd, gl, *, name):
    S = u.shape[0]
    H = attn.shape[0]
    C = GDN_CHUNK
    R = _gdn_rows(S)
    n, nc = R // C, S // C

    G = _gdn_head_group(H)
    heads = range(G)
    col = lambda h: slice(h * 128, (h + 1) * 128)

    def body(u_ref, w_ref, at_ref, qd_ref, kd_ref, gl_ref, o_ref, st_ref, state):
        @pl.when(pl.program_id(1) == 0)
        def _():
            state[...] = jnp.zeros_like(state)

        for c in range(n):
            rows = slice(c * C, (c + 1) * C)
            s = [state[h] for h in heads]
            for h in heads:
                st_ref[h, c] = s[h]
            sb = [t.astype(BF16) for t in s]
            ws = [_dot(w_ref[rows, col(h)].astype(BF16), sb[h]) for h in heads]
            qs = [_dot(qd_ref[rows, col(h)].astype(BF16), sb[h]) for h in heads]
            vb = [(u_ref[rows, col(h)] - ws[h]).astype(BF16) for h in heads]
            av = [_dot(at_ref[h, rows, :].astype(BF16), vb[h]) for h in heads]
            kv = [_dot(kd_ref[rows, col(h)].astype(BF16), vb[h], 0, 0) for h in heads]
            for h in heads:
                o_ref[rows, col(h)] = qs[h] + av[h]
                state[h] = s[h] * gl_ref[h, c] + kv[h]

    head = pl.BlockSpec((R, G * 128), lambda g, i: (i, g))
    return pl.pallas_call(
        body, name=name, grid=(H // G, S // R),
        in_specs=[head, head, pl.BlockSpec((G, R, C), lambda g, i: (g, i, 0)), head, head,
                  pl.BlockSpec((G, n, 1, 128), lambda g, i: (g, i, 0, 0))],
        out_specs=[head, pl.BlockSpec((G, n, 128, 128), lambda g, i: (g, i, 0, 0))],
        out_shape=[jax.ShapeDtypeStruct((S, H * 128), F32), jax.ShapeDtypeStruct((H, nc, 128, 128), F32)],
        scratch_shapes=[pltpu.VMEM((G, 128, 128), F32)],
        compiler_params=_cparams(("parallel", "arbitrary")),
    )(u, w, attn, qd, kd, gl)


def gdn_scan_bwd(u, w, attn, qd, kd, gl, states, do, *, name):
    S = u.shape[0]
    H = attn.shape[0]
    C = GDN_CHUNK
    R = _gdn_rows(S)
    n, nc, nb = R // C, S // C, S // R

    G = _gdn_head_group(H)
    heads = range(G)
    col = lambda h: slice(h * 128, (h + 1) * 128)

    def body(u_ref, w_ref, at_ref, qd_ref, kd_ref, gl_ref, st_ref, do_ref,
             du_ref, dw_ref, dat_ref, dqd_ref, dkd_ref, dgl_ref, dstate):
        @pl.when(pl.program_id(1) == 0)
        def _():
            dstate[...] = jnp.zeros_like(dstate)

        lane = lax.broadcasted_iota(jnp.int32, (1, 128), 1)
        for c in reversed(range(n)):
            rows = slice(c * C, (c + 1) * C)
            tile = lambda ref: [ref[rows, col(h)].astype(BF16) for h in heads]
            s = [st_ref[h, c] for h in heads]
            ds2 = [dstate[h] for h in heads]
            sb = [t.astype(BF16) for t in s]
            ds2b = [t.astype(BF16) for t in ds2]
            wb, qdb, kdb, dob = tile(w_ref), tile(qd_ref), tile(kd_ref), tile(do_ref)
            atb = [at_ref[h, rows, :].astype(BF16) for h in heads]
            ws = [_dot(wb[h], sb[h]) for h in heads]
            dv1 = [_dot(atb[h], dob[h], 0, 0) for h in heads]
            dv2 = [_dot(kdb[h], ds2b[h]) for h in heads]
            dqd = [_dot(dob[h], sb[h], 1, 1) for h in heads]
            qdo = [_dot(qdb[h], dob[h], 0, 0) for h in heads]
            vb = [(u_ref[rows, col(h)] - ws[h]).astype(BF16) for h in heads]
            dv = [dv1[h] + dv2[h] for h in heads]
            dvb = [t.astype(BF16) for t in dv]
            dw = [_dot(dvb[h], sb[h], 1, 1) for h in heads]
            dat = [_dot(dob[h], vb[h], 1, 1) for h in heads]
            dkd = [_dot(vb[h], ds2b[h], 1, 1) for h in heads]
            wdv = [_dot(wb[h], dvb[h], 0, 0) for h in heads]
            for h in heads:
                du_ref[rows, col(h)] = dv[h]
                dw_ref[rows, col(h)] = -dw[h]
                dat_ref[h, rows, :] = dat[h]
                dqd_ref[rows, col(h)] = dqd[h]
                dkd_ref[rows, col(h)] = dkd[h]
                dgl = jnp.sum(jnp.sum(ds2[h] * s[h], axis=1, keepdims=True), axis=0, keepdims=True)
                dgl_ref[h, c] = jnp.where(lane == 0, dgl, 0.0)
                dstate[h] = ds2[h] * gl_ref[h, c] + qdo[h] - wdv[h]

    head = pl.BlockSpec((R, G * 128), lambda g, i: (nb - 1 - i, g))
    at = pl.BlockSpec((G, R, C), lambda g, i: (g, nb - 1 - i, 0))
    glb = pl.BlockSpec((G, n, 1, 128), lambda g, i: (g, nb - 1 - i, 0, 0))
    big = jax.ShapeDtypeStruct((S, H * 128), F32)
    return pl.pallas_call(
        body, name=name, grid=(H // G, nb),
        in_specs=[head, head, at, head, head, glb,
                  pl.BlockSpec((G, n, 128, 128), lambda g, i: (g, nb - 1 - i, 0, 0)), head],
        out_specs=[head, head, at, head, head, glb],
        out_shape=[big, big, jax.ShapeDtypeStruct((H, S, C), F32), big, big,
                   jax.ShapeDtypeStruct((H, nc, 1, 128), F32)],
        scratch_shapes=[pltpu.VMEM((G, 128, 128), F32)],
        compiler_params=_cparams(("parallel", "arbitrary")),
    )(u, w, attn, qd, kd, gl, states, do)


def _head_rms(o):
    return lax.rsqrt(jnp.mean(o * o, axis=-1, keepdims=True) + EPS)


def gdn_mixer_fwd(proj, W, p, tag):
    H = W // HEAD_DIM
    ab_col = 4 * W + MEM_WIDTH
    qkv = gdn_conv_fwd(proj, 3 * W, p["conv_w"], name=f"gdn_conv_fwd_{tag}")
    al_row, dt_row = _pad_row(p["a_log"]), _pad_row(p["dt_bias"])
    pre, tinv = gdn_prep_fwd(qkv, proj, ab_col, al_row, dt_row, H, name=f"gdn_prep_fwd_{tag}")
    o, states = gdn_scan_fwd(*pre, name=f"gdn_scan_fwd_{tag}")
    gn_row = jnp.tile(p["o_norm"].reshape(1, HEAD_DIM), (1, H))

    def gate_fwd(o, gate, gn):
        return o * _head_rms(o) * gn * (gate * jax.nn.sigmoid(gate))

    mix = ew(gate_fwd, [(o, 0), (proj, 3 * W)], [gn_row], [BF16], 0, width=W, tc=HEAD_DIM, name=f"gdn_gate_fwd_{tag}")
    return mix, (qkv, pre, tinv, states, o, gn_row, al_row, dt_row)


def gdn_mixer_bwd(proj, W, p, saved, dcat, tag):
    qkv, pre, tinv, states, o, gn_row, al_row, dt_row = saved
    H = W // HEAD_DIM
    ab_col = 4 * W + MEM_WIDTH

    def gate_bwd(dm, o, gate, gn):
        dm = dm.astype(F32)
        r = _head_rms(o)
        xh = o * r
        sg = jax.nn.sigmoid(gate)
        dy = dm * gate * sg
        dgate = dm * xh * gn * (sg * (1.0 + gate * (1.0 - sg)))
        dxh = dy * gn
        do = r * (dxh - xh * jnp.mean(dxh * xh, axis=-1, keepdims=True))
        return do, dgate, dy * xh

    do, dgate, dgn = ew(gate_bwd, [(dcat, 0), (o, 0), (proj, 3 * W)], [gn_row], [F32, BF16], 1, width=W, tc=HEAD_DIM,
                        name=f"gdn_gate_bwd_{tag}")
    cts = gdn_scan_bwd(*pre, states, do, name=f"gdn_scan_bwd_{tag}")
    dqkv, dab, dal, ddt = gdn_prep_bwd(qkv, proj, ab_col, al_row, dt_row, H, pre[0], pre[1], tinv, cts,
                                       name=f"gdn_prep_bwd_{tag}")
    dx, dconv = gdn_conv_bwd(proj, 3 * W, p["conv_w"], dqkv, name=f"gdn_conv_bwd_{tag}")
    grads = dict(conv_w=dconv, a_log=dal[0, :H], dt_bias=ddt[0, :H], o_norm=dgn.reshape(H, HEAD_DIM).sum(axis=0))
    return dx, dgate, dab, grads


def exchange(arrays, same_block, *, name):
    n = len(arrays)

    def body(*refs):
        ex = _Exchange(refs[:n], refs[n:2 * n], same_block, *refs[2 * n:])
        ex.start()
        ex.finish()

    shapes, sems = _exchange_shapes(arrays, same_block)
    any_spec = pl.BlockSpec(memory_space=pl.ANY)
    return pl.pallas_call(body, name=name, in_specs=[any_spec] * n, out_specs=[any_spec] * n, out_shape=shapes,
                          scratch_shapes=sems)(*arrays)


def _row_tile(R, row_bytes):
    for t in (512, 256, 128, 64, 32, 16, 8):
        if R % t == 0 and 2 * t * row_bytes <= 24 * 2 ** 20:
            return t
    return R


def adamw(parts, w, m, v, *, name, layer=None, prev=None):
    P, R, C = parts.shape
    tr = _row_tile(R, C * (P * parts.dtype.itemsize + 7 * 4))
    c1, c2 = 1.0 - ADAM_B1 ** ADAM_STEP, 1.0 - ADAM_B2 ** ADAM_STEP
    n_prev = 0 if prev is None else 4

    def body(p_ref, w_ref, m_ref, v_ref, *rest):
        g_ref, d_ref, nm_ref, nv_ref = rest[n_prev:]
        g = p_ref[0].astype(F32)
        for s in range(1, P):
            g = g + p_ref[s].astype(F32)
        m = ADAM_B1 * m_ref[...] + (1.0 - ADAM_B1) * g
        v = ADAM_B2 * v_ref[...] + (1.0 - ADAM_B2) * (g * g)
        g_ref[...] = g
        nm_ref[...] = m
        nv_ref[...] = v
        d_ref[...] = -ADAM_LR * ((m / c1) / (jnp.sqrt(v / c2) + ADAM_EPS) + ADAM_WD * w_ref[...])

    if layer is None:
        blk, shape = pl.BlockSpec((tr, C), lambda i: (i, 0)), (R, C)
    else:
        blk, shape = pl.BlockSpec((None, tr, C), lambda i: (layer, i, 0)), w.shape
    return pl.pallas_call(
        body, name=name, grid=(R // tr,),
        in_specs=[pl.BlockSpec((P, tr, C), lambda i: (0, i, 0)), blk, blk, blk]
        + [pl.BlockSpec(memory_space=pl.ANY)] * n_prev,
        out_specs=[blk] * 4, out_shape=[jax.ShapeDtypeStruct(shape, F32)] * 4,
        input_output_aliases={4 + k: k for k in range(n_prev)},
        compiler_params=_cparams(("parallel",)),
    )(parts, w, m, v, *(prev or ()))


def sum_parts(parts, *, name):
    P, R, C = parts.shape
    tr = _row_tile(R, C * (P * parts.dtype.itemsize + 4))

    def body(p_ref, o_ref):
        g = p_ref[0].astype(F32)
        for s in range(1, P):
            g = g + p_ref[s].astype(F32)
        o_ref[...] = g

    return pl.pallas_call(
        body, name=name, grid=(R // tr,),
        in_specs=[pl.BlockSpec((P, tr, C), lambda i: (0, i, 0))], out_specs=pl.BlockSpec((tr, C), lambda i: (i, 0)),
        out_shape=jax.ShapeDtypeStruct((R, C), F32), compiler_params=_cparams(("parallel",)),
    )(parts)


PACK_COLS = 1024
PACK_ROWS = 256


def _pack(arrays, cols, lead=()):
    n_lead = len(lead)
    flat = [a.reshape(lead + (-1,)) for a in arrays]
    total = sum(f.shape[-1] for f in flat)
    rows = -(-total // cols)
    mult = PACK_ROWS if rows > PACK_ROWS else 8
    rows = -(-rows // mult) * mult
    pad = rows * cols - total
    if pad:
        flat.append(jnp.zeros(lead + (pad,), flat[0].dtype))
    return jnp.concatenate(flat, axis=n_lead).reshape(lead + (rows, cols))


def _unpack(buf, shapes, lead=()):
    flat = buf.reshape(lead + (-1,))
    out, off = [], 0
    for s in shapes:
        n = math.prod(s)
        out.append(lax.slice_in_dim(flat, off, off + n, axis=len(lead)).reshape(lead + tuple(s)))
        off += n
    return out


def _to_shards(full, axis):
    s = full.shape
    return jnp.moveaxis(full.reshape(s[:axis] + (N_DEV, s[axis] // N_DEV) + s[axis + 1:]), axis, 0)


def _from_shards(g, axis):
    m = jnp.moveaxis(g, 0, axis)
    s = m.shape
    return m.reshape(s[:axis] + (s[axis] * s[axis + 1],) + s[axis + 2:])


BIG = (("w_mem_kv", 0), ("w_out", 1), ("w_up", 2), ("w_down", 1), ("s5_w_in", 1), ("s5_w_glu", 1),
       ("gdn_w_in", 2), ("fox_w_in", 1))
SMALL_SHARDED = (("s5_d_skip", 1), ("s5_b_glu", 1), ("gdn_conv_w", 2))
REPLICATED = ("mem_norm", "norm1", "norm2", "norm_f", "s5_lam_re", "s5_lam_im", "s5_log_dt", "s5_b_re", "s5_b_im",
              "s5_c_re", "s5_c_im", "gdn_a_log", "gdn_dt_bias", "gdn_o_norm", "fox_b_f")
WEIGHTS = ("mem_norm", "w_mem_kv", "norm1", "w_out", "norm2", "w_up", "w_down", "norm_f", "s5_w_in", "s5_lam_re",
           "s5_lam_im", "s5_log_dt", "s5_b_re", "s5_b_im", "s5_c_re", "s5_c_im", "s5_d_skip", "s5_w_glu", "s5_b_glu",
           "gdn_w_in", "gdn_conv_w", "gdn_a_log", "gdn_dt_bias", "gdn_o_norm", "fox_w_in", "fox_b_f")


def _relu2(acc):
    r = jnp.maximum(acc, 0.0)
    return acc, r * r


def _relu2_grad(acc, u):
    return (acc * 2.0 * jnp.maximum(u, 0.0),)


def _add(acc, e):
    return (acc + e,)


def _permute_in(w, kind, W):
    if kind == 0:
        return w
    n_main = (4 if kind == 1 else 3) * W
    n_small = w.shape[1] - n_main - MEM_WIDTH
    small = jnp.pad(w[:, n_main:n_main + n_small], ((0, 0), (0, MEM_WIDTH - n_small)))
    return jnp.concatenate([w[:, :n_main], w[:, n_main + n_small:], small], axis=1)


def _unpermute_in(dw, kind, W, n_small):
    if kind == 0:
        return dw
    n_main = (4 if kind == 1 else 3) * W
    return jnp.concatenate([dw[:, :n_main], dw[:, n_main + MEM_WIDTH:n_main + MEM_WIDTH + n_small],
                            dw[:, n_main:n_main + MEM_WIDTH]], axis=1)


def kernel(x, mem, mem_norm, w_mem_kv, norm1, w_out, norm2, w_up, w_down, norm_f, s5_w_in, s5_lam_re, s5_lam_im, s5_log_dt, s5_b_re, s5_b_im, s5_c_re, s5_c_im, s5_d_skip, s5_w_glu, s5_b_glu, gdn_w_in, gdn_conv_w, gdn_a_log, gdn_dt_bias, gdn_o_norm, fox_w_in, fox_b_f, loss_target, m_mem_norm, m_w_mem_kv, m_norm1, m_w_out, m_norm2, m_w_up, m_w_down, m_norm_f, m_s5_w_in, m_s5_lam_re, m_s5_lam_im, m_s5_log_dt, m_s5_b_re, m_s5_b_im, m_s5_c_re, m_s5_c_im, m_s5_d_skip, m_s5_w_glu, m_s5_b_glu, m_gdn_w_in, m_gdn_conv_w, m_gdn_a_log, m_gdn_dt_bias, m_gdn_o_norm, m_fox_w_in, m_fox_b_f, v_mem_norm, v_w_mem_kv, v_norm1, v_w_out, v_norm2, v_w_up, v_w_down, v_norm_f, v_s5_w_in, v_s5_lam_re, v_s5_lam_im, v_s5_log_dt, v_s5_b_re, v_s5_b_im, v_s5_c_re, v_s5_c_im, v_s5_d_skip, v_s5_w_glu, v_s5_b_glu, v_gdn_w_in, v_gdn_conv_w, v_gdn_a_log, v_gdn_dt_bias, v_gdn_o_norm, v_fox_w_in, v_fox_b_f):
    args = dict(locals())
    wsh = {n: args[n] for n in WEIGHTS}
    msh = {n: args["m_" + n] for n in WEIGHTS}
    vsh = {n: args["v_" + n] for n in WEIGHTS}
    h0, memx, target = x[0], mem[0], loss_target[0]
    S, D = h0.shape
    W = D - MEM_WIDTH
    depth = norm1.shape[0]
    me = 4 * lax.axis_index("x") + 2 * lax.axis_index("y") + lax.axis_index("c")

    bf = lambda t: t.astype(BF16)
    rows2d = lambda g: g.reshape(-1, g.shape[-1])

    def layer_sends(i, group):
        kind, j = i % 3, i // 3
        if group == "up":
            return {"w_up": bf(wsh["w_up"][i])}
        if group == "down":
            return {"w_down": bf(wsh["w_down"][i])}
        d = {"w_out": bf(wsh["w_out"][i])}
        if kind == 0:
            d["w_in"], d["w_glu"] = bf(wsh["s5_w_in"][j]), bf(wsh["s5_w_glu"][j])
        elif kind == 1:
            d["w_in"] = bf(wsh["gdn_w_in"][j])
        else:
            d["w_in"] = bf(_permute_in(wsh["fox_w_in"][j], 2, W))
        return d

    def as_comm(d):
        return [(v, True) for v in d.values()]

    first = {**layer_sends(0, "in"), **layer_sends(0, "up"), **layer_sends(0, "down")}
    small_w = _pack([wsh[n] for n, _ in SMALL_SHARDED], 128)
    got = exchange(list(first.values()) + [bf(w_mem_kv), small_w], [True] * (len(first) + 2), name="gather_first")
    gathered = [dict(zip(first, got)) if i == 0 else {} for i in range(depth)]
    full = {n: wsh[n] for n in REPLICATED}
    full["w_mem_kv"] = rows2d(got[len(first)])
    for (n, ax), g in zip(SMALL_SHARDED, _unpack(got[-1], [wsh[n].shape for n, _ in SMALL_SHARDED], lead=(N_DEV,))):
        full[n] = _from_shards(g, ax)

    def layer_params(i):
        kind, j = i % 3, i // 3
        g = gathered[i]
        w = {n: rows2d(g[n]) for n in ("w_out", "w_down", "w_glu") if n in g}
        w["w_up"] = g["w_up"]
        w["w_in"] = _permute_in(_from_shards(g["w_in"], 1), 1, W) if kind == 1 else rows2d(g["w_in"])
        if kind == 0:
            p = {k: full["s5_" + k][j] for k in ("lam_re", "lam_im", "log_dt", "b_re", "b_im", "c_re", "c_im",
                                                   "d_skip", "b_glu")}
            return kind, j, p, w, 0
        if kind == 1:
            p = {k: full["gdn_" + k][j] for k in ("conv_w", "a_log", "dt_bias", "o_norm")}
            return kind, j, p, w, 2 * (W // HEAD_DIM)
        return kind, j, {"b_f": full["fox_b_f"][j]}, w, W // HEAD_DIM

    def hosted(i, group):
        if i + 1 >= depth:
            return {}, []
        d = layer_sends(i + 1, group)
        return {"comm": as_comm(d)}, list(d)

    def keep(i, names, res):
        if not names:
            return res
        outs, got = res
        gathered[i + 1].update(zip(names, got))
        return outs

    memn = rms_fwd(memx, full["mem_norm"], out_dtype=BF16, name="mem_rms")
    mkv = mm(memn, full["w_mem_kv"], name="mem_kv")
    h = h0
    saved = []
    weights = []
    for i in range(depth):
        kind, j, p, w, n_small = layer_params(i)
        weights.append((w, p, n_small))
        a = rms_fwd(h, full["norm1"][i], out_dtype=BF16, name=f"rms1_{i}")
        kw, names = hosted(i, "in")
        proj = keep(i, names, mm(a, w["w_in"], name=f"in_proj_{i}", **kw))
        if kind == 0:
            mix, ms = s5_mixer_fwd(proj, W, p, w["w_glu"], f"l{i}")
            mem_col = W
        elif kind == 1:
            mix, ms = gdn_mixer_fwd(proj, W, p, f"l{i}")
            mem_col = 4 * W
        else:
            mix, ms = fox_mixer_fwd(proj, 3 * W + MEM_WIDTH, p["b_f"], f"l{i}")
            mem_col = 3 * W
        read = mem_fwd(proj, mem_col, mkv, name=f"mem_fwd_{i}")
        cat = jnp.concatenate([mix, read], axis=1)
        h_mid = mm(cat, w["w_out"], extras=(h,), epilogue=_add, name=f"out_proj_{i}")
        a2 = rms_fwd(h_mid, full["norm2"][i], out_dtype=BF16, name=f"rms2_{i}")
        kw, names = hosted(i, "up")
        u, act = keep(i, names, mm(a2, w["w_up"], b_shard8=True, epilogue=_relu2, out_dtypes=(F32, BF16),
                                   name=f"up_{i}", **kw))
        kw, names = hosted(i, "down")
        h_next = keep(i, names, mm(act, w["w_down"], extras=(h_mid,), epilogue=_add, name=f"down_{i}", **kw))
        saved.append((h, a, proj, ms, mem_col, cat, h_mid, a2, u, act))
        h = h_next

    loss_row, dh, dnf = final_loss(h, full["norm_f"], target, name="final_loss")
    loss = lax.psum(jnp.sum(loss_row), ("x", "y", "c"))

    grads = {n: [None] * full[n].shape[0] for n in ("norm1", "norm2")}
    for pre, cnt in (("s5_", (depth + 2) // 3), ("gdn_", (depth + 1) // 3), ("fox_", depth // 3)):
        for n in REPLICATED + tuple(n for n, _ in SMALL_SHARDED):
            if n.startswith(pre):
                grads[n] = [None] * cnt
    shares = {}
    by_dest = lambda g: g.reshape((N_DEV, g.shape[0] // N_DEV) + g.shape[1:])
    pending_up, pending_rest = {}, {}

    def carry(pending):
        return {"comm": [(v, False) for v in pending.values()]} if pending else {}

    def landed(pending, res):
        if not pending:
            return res
        outs, got = res
        shares.update(zip(pending, got))
        return outs

    dmkv = None
    for i in reversed(range(depth)):
        kind, j = i % 3, i // 3
        w, p, n_small = weights[i]
        h_in, a, proj, ms, mem_col, cat, h_mid, a2, u, act = saved[i]
        du = landed(pending_up, mm(dh, w["w_down"], tb=True, extras=(u,), epilogue=_relu2_grad, out_dtypes=(BF16,),
                                   name=f"d_act_{i}", **carry(pending_up)))
        dw_down = landed(pending_rest, mm(act, dh, ta=True, out_dtypes=(BF16,), name=f"dw_down_{i}",
                                          **carry(pending_rest)))
        pending = {("w_down", i): by_dest(dw_down)}
        da2 = landed(pending, mm(du, w["w_up"], tb=True, b_shard8=True, name=f"d_a2_{i}", **carry(pending)))
        pending_up = {("w_up", i): mm(a2, du, ta=True, out_shard8=True, out_dtypes=(BF16,), name=f"dw_up_{i}")}
        dh_mid, dn2 = rms_bwd(h_mid, full["norm2"][i], da2, dh, name=f"rms2_bwd_{i}")
        grads["norm2"][i] = dn2[0]
        dcat = mm(dh_mid, w["w_out"], tb=True, name=f"d_cat_{i}")
        pending_rest = {("w_out", i): by_dest(mm(cat, dh_mid, ta=True, out_dtypes=(BF16,), name=f"dw_out_{i}"))}
        dq_mem, dmkv_i = mem_bwd(proj, mem_col, mkv, dcat, W, name=f"mem_bwd_{i}")
        dmkv = dmkv_i if dmkv is None else dmkv + dmkv_i
        if kind == 0:
            dmain, g = s5_mixer_bwd(proj, W, p, w["w_glu"], ms, dcat, f"l{i}")
            dproj = jnp.concatenate([dmain, dq_mem], axis=1)
            pending_rest[("s5_w_glu", j)] = by_dest(g.pop("w_glu"))
            for k, val in g.items():
                grads["s5_" + k][j] = val
        elif kind == 1:
            dx, dgate, dab, g = gdn_mixer_bwd(proj, W, p, ms, dcat, f"l{i}")
            dproj = jnp.concatenate([dx, dgate, dq_mem, bf(jnp.pad(dab, ((0, 0), (0, MEM_WIDTH - 128))))], axis=1)
            for k, val in g.items():
                grads["gdn_" + k][j] = val
        else:
            dq, dk, dv, df, db = fox_mixer_bwd(proj, 3 * W + MEM_WIDTH, ms, dcat, f"l{i}")
            dproj = jnp.concatenate([dq, dk, dv, dq_mem.astype(F32), jnp.pad(df, ((0, 0), (0, MEM_WIDTH - 128)))],
                                    axis=1)
            grads["fox_b_f"][j] = db
        da = mm(dproj, w["w_in"], tb=True, name=f"d_a_{i}")
        in_name = ("s5_w_in", "gdn_w_in", "fox_w_in")[kind]
        if kind == 1:
            dw_in = _unpermute_in(mm(a, dproj, ta=True, name=f"dw_in_{i}"), 1, W, n_small)
            pending_rest[(in_name, j)] = bf(_to_shards(dw_in, 1))
        else:
            pending_rest[(in_name, j)] = by_dest(mm(a, dproj, ta=True, out_dtypes=(BF16,), name=f"dw_in_{i}"))
        dh, dn1 = rms_bwd(h_in, full["norm1"][i], da, dh_mid, name=f"rms1_bwd_{i}")
        grads["norm1"][i] = dn1[0]
    grad_x = dh[None]
    grads = {n: jnp.stack(v) for n, v in grads.items()}
    grads["norm_f"] = dnf[0]
    pending_rest[("w_mem_kv", None)] = by_dest(mm(memn, dmkv, ta=True, out_dtypes=(BF16,), name="dw_mem_kv"))
    dmemn = mm(dmkv, full["w_mem_kv"], tb=True, name="d_memn")
    _, dmn = rms_bwd(memx, full["mem_norm"], dmemn, None, name="mem_rms_bwd")
    grads["mem_norm"] = dmn[0]

    small_names = list(REPLICATED) + [n for n, _ in SMALL_SHARDED]
    last = {**pending_up, **pending_rest}
    got = exchange(list(last.values()) + [_pack([grads[n] for n in small_names], 128)],
                   [False] * len(last) + [True], name="exchange_last")
    shares.update(zip(last, got))

    big_out = [{}, {}, {}, {}]
    for n, _ in BIG:
        local = [t[n] for t in (wsh, msh, vsh)]
        if n == "w_mem_kv":
            outs = adamw(shares[(n, None)], *local, name=f"adamw_{n}")
        else:
            outs = None
            for idx in range(wsh[n].shape[0]):
                part = shares[(n, idx)]
                if n == "fox_w_in":
                    total = sum_parts(part, name=f"sum_{n}_{idx}")
                    part = _unpermute_in(total, 2, W, W // HEAD_DIM)[None]
                outs = adamw(part, *local, layer=idx, prev=outs, name=f"adamw_{n}_{idx}")
        for o, val in zip(big_out, outs):
            o[n] = val

    total = sum_parts(got[-1], name="sum_small_grads")
    gsmall = dict(zip(small_names, _unpack(total, [grads[n].shape for n in small_names])))
    for n, ax in SMALL_SHARDED:
        width = wsh[n].shape[ax]
        gsmall[n] = lax.dynamic_slice_in_dim(gsmall[n], me * width, width, axis=ax)
    outs = adamw(_pack([gsmall[n] for n in small_names], 128)[None],
                 *[_pack([t[n] for n in small_names], 128) for t in (wsh, msh, vsh)], name="adamw_small")
    small_out = [dict(zip(small_names, _unpack(o, [wsh[n].shape for n in small_names]))) for o in outs]

    res = [{**b, **s} for b, s in zip(big_out, small_out)]
    return (loss, grad_x, *[r[n] for r in res for n in WEIGHTS])
```

```python
import functools
import math

import jax
import jax.numpy as jnp
from jax import lax
from jax.experimental import pallas as pl
from jax.experimental.pallas import tpu as pltpu

F32 = jnp.float32
BF16 = jnp.bfloat16

HEAD_DIM = 128
MEM_HEADS = 4
MEM_WIDTH = MEM_HEADS * HEAD_DIM
S5_GROUP = 16
S5_STATE = 64
GDN_CHUNK = 64
EPS = 1e-6
ADAM_LR, ADAM_B1, ADAM_B2, ADAM_EPS, ADAM_WD, ADAM_STEP = 0.001, 0.9, 0.999, 1e-08, 0.01, 10

N_DEV = 8
MESH = pl.DeviceIdType.MESH
VMEM_LIMIT = 56 * 1024 * 1024
HI = lax.Precision.HIGHEST


def _tile(n, prefs=(1024, 512, 256, 128)):
    for t in prefs:
        if n % t == 0:
            return t
    return n


def _cparams(sem=None):
    return pltpu.CompilerParams(dimension_semantics=sem, vmem_limit_bytes=VMEM_LIMIT)


def _dot(a, b, ca=1, cb=0, precision=None):
    return lax.dot_general(a, b, (((ca,), (cb,)), ((), ())), preferred_element_type=F32, precision=precision)


def _bdot(a, b):
    return _dot(a.astype(BF16), b.astype(BF16))


SIBLING = 1
OTHER_CHIPS = (2, 4, 6)


class _Exchange:
    def __init__(self, p_refs, out_refs, same_block, send_sems, recv_sems, local_sems):
        self.pos = (lax.axis_index("x"), lax.axis_index("y"), lax.axis_index("c"))
        self.arrays = list(zip(p_refs, out_refs, same_block))
        self.sems = (send_sems, recv_sems, local_sems)

    def _dev(self, m):
        return tuple(1 - v if (m >> (2 - b)) & 1 else v for b, v in enumerate(self.pos))

    def _slot(self, m):
        d = self._dev(m)
        return 4 * d[0] + 2 * d[1] + d[2]

    def _copy(self, n, src, slot, sem, to):
        k = n * (N_DEV - 1) + sem - 1
        return pltpu.make_async_remote_copy(
            src_ref=src, dst_ref=self.arrays[n][1].at[slot], send_sem=self.sems[0].at[k], recv_sem=self.sems[1].at[k],
            device_id=self._dev(to), device_id_type=MESH)

    def _local(self, n):
        p_ref, out_ref, same = self.arrays[n]
        return pltpu.make_async_copy(p_ref if same else p_ref.at[self._slot(0)], out_ref.at[self._slot(0)],
                                     self.sems[2].at[n])

    def _sends(self, n):
        p_ref, _, same = self.arrays[n]
        if same:
            return [self._copy(n, p_ref, self._slot(0), m, m) for m in (SIBLING,) + OTHER_CHIPS]
        return [self._copy(n, p_ref.at[self._slot(m)], self._slot(0), m, m) for m in range(1, N_DEV)]

    def _passed_on(self, n, m):
        return self._copy(n, self.arrays[n][1].at[self._slot(m)], self._slot(m), m ^ SIBLING, SIBLING)

    def _arrival(self, n, m):
        return self._copy(n, self.arrays[n][1].at[self._slot(m)], self._slot(m), m, m)

    def start(self):
        for n in range(len(self.arrays)):
            self._local(n).start()
            for cp in self._sends(n):
                cp.start()

    def finish(self):
        for n, (_, _, same) in enumerate(self.arrays):
            if same:
                for m in OTHER_CHIPS:
                    self._arrival(n, m).wait_recv()
                    self._passed_on(n, m).start()
        for n, (_, _, same) in enumerate(self.arrays):
            if same:
                for m in OTHER_CHIPS:
                    self._arrival(n, m ^ SIBLING).wait_recv()
                    self._passed_on(n, m).wait_send()
                self._arrival(n, SIBLING).wait_recv()
                for cp in self._sends(n):
                    cp.wait_send()
            else:
                for cp in self._sends(n):
                    cp.wait()
            self._local(n).wait()


def _exchange_shapes(arrays, same_block):
    shapes = [jax.ShapeDtypeStruct((N_DEV,) + tuple(p.shape if same else p.shape[1:]), p.dtype)
              for p, same in zip(arrays, same_block)]
    n = len(arrays)
    sems = [pltpu.SemaphoreType.DMA((n * (N_DEV - 1),)), pltpu.SemaphoreType.DMA((n * (N_DEV - 1),)),
            pltpu.SemaphoreType.DMA((n,))]
    return shapes, sems


def _mm_tk(K, tm, tn, a_bytes, b_bytes, out_bytes):
    for tk in (2048, 1024, 512, 256, 128):
        if K % tk == 0 and 2 * tk * (tm * a_bytes + tn * b_bytes) + (2 * out_bytes + 4) * tm * tn <= 40 * 2 ** 20:
            return tk
    return K


def mm(a, b, *, ta=False, tb=False, extras=(), epilogue=None, out_dtypes=(F32,), name, b_shard8=False,
       out_shard8=False, comm=()):
    M, K = (a.shape[1], a.shape[0]) if ta else a.shape
    if b_shard8:
        brows, bcols = b.shape[1], b.shape[2] * N_DEV
    else:
        brows, bcols = b.shape
    N = brows if tb else bcols
    assert K == (bcols if tb else brows), (a.shape, b.shape, ta, tb)
    tm = _tile(M)
    tn = _tile(N // N_DEV if (out_shard8 or (b_shard8 and not tb)) else N)
    out_bytes = sum(jnp.dtype(d).itemsize for d in out_dtypes) + 4 * sum(e.shape[0] != 1 for e in extras)
    tk = _mm_tk(K // N_DEV if (b_shard8 and tb) else K, tm, tn, a.dtype.itemsize, b.dtype.itemsize, out_bytes)
    ni, nj, nk = M // tm, N // tn, K // tk
    n_ex, n_out, n_comm = len(extras), len(out_dtypes), len(comm)
    ca, cb = (0 if ta else 1), (1 if tb else 0)
    same_block = [s for _, s in comm]

    def body(a_ref, b_ref, *rest):
        ex_refs, rest = rest[:n_ex], rest[n_ex:]
        cin, rest = rest[:n_comm], rest[n_comm:]
        out_refs, rest = rest[:n_out], rest[n_out:]
        cout, rest = rest[:n_comm], rest[n_comm:]
        acc = rest[0]
        i, j, k = pl.program_id(0), pl.program_id(1), pl.program_id(2)
        if n_comm:
            @pl.when((i == 0) & (j == 0) & (k == 0))
            def _():
                _Exchange(cin, cout, same_block, *rest[1:]).start()

        def finish(res):
            outs = (res,) if epilogue is None else epilogue(res, *[e[...] for e in ex_refs])
            for o, v in zip(out_refs, outs):
                o[...] = v.astype(o.dtype)

        part = _dot(a_ref[...].astype(BF16), b_ref[...].astype(BF16), ca, cb)
        if nk == 1:
            finish(part)
        else:
            @pl.when(k == 0)
            def _():
                acc[...] = part

            @pl.when(k > 0)
            def _():
                acc[...] += part

            @pl.when(k == nk - 1)
            def _():
                finish(acc[...])

        if n_comm:
            @pl.when((i == ni - 1) & (j == nj - 1) & (k == nk - 1))
            def _():
                _Exchange(cin, cout, same_block, *rest[1:]).finish()

    a_spec = pl.BlockSpec((tk, tm), lambda i, j, k: (k, i)) if ta else pl.BlockSpec((tm, tk), lambda i, j, k: (i, k))
    if b_shard8 and tb:
        kper = bcols // N_DEV // tk
        b_spec = pl.BlockSpec((None, tn, tk), lambda i, j, k: (k // kper, j, k % kper))
    elif b_shard8:
        nper = bcols // N_DEV // tn
        b_spec = pl.BlockSpec((None, tk, tn), lambda i, j, k: (j // nper, k, j % nper))
    elif tb:
        b_spec = pl.BlockSpec((tn, tk), lambda i, j, k: (j, k))
    else:
        b_spec = pl.BlockSpec((tk, tn), lambda i, j, k: (k, j))
    ex_specs = [pl.BlockSpec((1, tn), lambda i, j, k: (0, j)) if e.shape[0] == 1 and M != 1
                else pl.BlockSpec((tm, tn), lambda i, j, k: (i, j)) for e in extras]
    if out_shard8:
        nper = N // N_DEV // tn
        out_spec = pl.BlockSpec((None, tm, tn), lambda i, j, k: (j // nper, i, j % nper))
        out_shape = (N_DEV, M, N // N_DEV)
    else:
        out_spec = pl.BlockSpec((tm, tn), lambda i, j, k: (i, j))
        out_shape = (M, N)
    comm_shapes, sems = _exchange_shapes([p for p, _ in comm], same_block) if n_comm else ([], [])
    any_spec = pl.BlockSpec(memory_space=pl.ANY)
    outs = pl.pallas_call(
        body, name=name, grid=(ni, nj, nk),
        in_specs=[a_spec, b_spec] + ex_specs + [any_spec] * n_comm,
        out_specs=[out_spec] * n_out + [any_spec] * n_comm,
        out_shape=[jax.ShapeDtypeStruct(out_shape, d) for d in out_dtypes] + comm_shapes,
        scratch_shapes=[pltpu.VMEM((tm, tn) if nk > 1 else (8, 128), F32)] + sems,
        compiler_params=_cparams(("arbitrary",) * 3 if n_comm else ("parallel", "parallel", "arbitrary")),
    )(a, b, *extras, *[p for p, _ in comm])
    if n_comm:
        return (outs[0] if n_out == 1 else tuple(outs[:n_out])), list(outs[n_out:])
    return outs[0] if n_out == 1 else outs


def rms_fwd(x, g, *, out_dtype, name):
    S, D = x.shape
    ts = _tile(S, (512, 256, 128))

    def body(x_ref, g_ref, y_ref):
        x = x_ref[...]
        r = lax.rsqrt(jnp.mean(x * x, axis=-1, keepdims=True) + EPS)
        y_ref[...] = (x * r * g_ref[...]).astype(y_ref.dtype)

    return pl.pallas_call(
        body, name=name, grid=(S // ts,),
        in_specs=[pl.BlockSpec((ts, D), lambda i: (i, 0)), pl.BlockSpec((1, D), lambda i: (0, 0))],
        out_specs=pl.BlockSpec((ts, D), lambda i: (i, 0)),
        out_shape=jax.ShapeDtypeStruct((S, D), out_dtype),
        compiler_params=_cparams(("parallel",)),
    )(x, g.reshape(1, D))


def rms_bwd(x, g, dy, dres, *, name):
    S, D = x.shape
    ts = _tile(S, (512, 256, 128))
    has_res = dres is not None

    def body(x_ref, g_ref, dy_ref, *rest):
        dx_ref, dg_ref = rest[-2:]
        x = x_ref[...]
        r = lax.rsqrt(jnp.mean(x * x, axis=-1, keepdims=True) + EPS)
        xh = x * r
        dy = dy_ref[...].astype(F32)
        dxh = dy * g_ref[...]
        dx = r * (dxh - xh * jnp.mean(dxh * xh, axis=-1, keepdims=True))
        if has_res:
            dx = dx + rest[0][...]
        dx_ref[...] = dx

        @pl.when(pl.program_id(0) == 0)
        def _():
            dg_ref[...] = jnp.zeros_like(dg_ref)

        dg_ref[...] += jnp.sum(dy * xh, axis=0, keepdims=True)

    tok = pl.BlockSpec((ts, D), lambda i: (i, 0))
    row = pl.BlockSpec((1, D), lambda i: (0, 0))
    return pl.pallas_call(
        body, name=name, grid=(S // ts,),
        in_specs=[tok, row, tok] + ([tok] if has_res else []),
        out_specs=[tok, row],
        out_shape=[jax.ShapeDtypeStruct((S, D), F32), jax.ShapeDtypeStruct((1, D), F32)],
        compiler_params=_cparams(("arbitrary",)),
    )(x, g.reshape(1, D), dy, *([dres] if has_res else []))


def final_loss(h, g, target, *, name):
    S, D = h.shape
    ts = _tile(S, (512, 256, 128))

    def body(x_ref, g_ref, t_ref, loss_ref, dx_ref, dg_ref):
        x = x_ref[...]
        r = lax.rsqrt(jnp.mean(x * x, axis=-1, keepdims=True) + EPS)
        xh = x * r
        err = xh * g_ref[...] - t_ref[...]
        dy = err * (1.0 / D)
        dxh = dy * g_ref[...]
        dx_ref[...] = r * (dxh - xh * jnp.mean(dxh * xh, axis=-1, keepdims=True))

        @pl.when(pl.program_id(0) == 0)
        def _():
            dg_ref[...] = jnp.zeros_like(dg_ref)
            loss_ref[...] = jnp.zeros_like(loss_ref)

        dg_ref[...] += jnp.sum(dy * xh, axis=0, keepdims=True)
        loss_ref[...] += jnp.sum(err * err, axis=0, keepdims=True) * (0.5 / D)

    tok = pl.BlockSpec((ts, D), lambda i: (i, 0))
    row = pl.BlockSpec((1, D), lambda i: (0, 0))
    return pl.pallas_call(
        body, name=name, grid=(S // ts,),
        in_specs=[tok, row, tok], out_specs=[row, tok, row],
        out_shape=[jax.ShapeDtypeStruct((1, D), F32), jax.ShapeDtypeStruct((S, D), F32),
                   jax.ShapeDtypeStruct((1, D), F32)],
        compiler_params=_cparams(("arbitrary",)),
    )(h, g.reshape(1, D), target)


def _mem_probs(q, k):
    s = _dot(q.astype(BF16), k.astype(BF16), 1, 1) * HEAD_DIM ** -0.5
    p = jnp.exp(s - jnp.max(s, axis=-1, keepdims=True))
    return p / jnp.sum(p, axis=-1, keepdims=True)


def mem_fwd(proj, q_col, mkv, *, name):
    S = proj.shape[0]
    L = mkv.shape[0]
    ts = _tile(S)

    def body(q_ref, kv_ref, o_ref):
        for h in range(MEM_HEADS):
            c = slice(h * HEAD_DIM, (h + 1) * HEAD_DIM)
            v = kv_ref[:, MEM_WIDTH + h * HEAD_DIM:MEM_WIDTH + (h + 1) * HEAD_DIM]
            p = _mem_probs(q_ref[:, c], kv_ref[:, c])
            o_ref[:, c] = _bdot(p, v).astype(o_ref.dtype)

    return pl.pallas_call(
        body, name=name, grid=(S // ts,),
        in_specs=[pl.BlockSpec((ts, MEM_WIDTH), lambda i: (i, q_col // MEM_WIDTH)),
                  pl.BlockSpec((L, 2 * MEM_WIDTH), lambda i: (0, 0))],
        out_specs=pl.BlockSpec((ts, MEM_WIDTH), lambda i: (i, 0)),
        out_shape=jax.ShapeDtypeStruct((S, MEM_WIDTH), BF16),
        compiler_params=_cparams(("parallel",)),
    )(proj, mkv)


def mem_bwd(proj, q_col, mkv, dcat, do_col, *, name):
    S = proj.shape[0]
    L = mkv.shape[0]
    ts = _tile(S)
    scale = HEAD_DIM ** -0.5

    def body(q_ref, kv_ref, do_ref, dq_ref, dkv_ref):
        @pl.when(pl.program_id(0) == 0)
        def _():
            dkv_ref[...] = jnp.zeros_like(dkv_ref)

        for h in range(MEM_HEADS):
            c = slice(h * HEAD_DIM, (h + 1) * HEAD_DIM)
            cv = slice(MEM_WIDTH + h * HEAD_DIM, MEM_WIDTH + (h + 1) * HEAD_DIM)
            q, k, v = q_ref[:, c].astype(BF16), kv_ref[:, c].astype(BF16), kv_ref[:, cv].astype(BF16)
            do = do_ref[:, c].astype(BF16)
            p = _mem_probs(q, k)
            dkv_ref[:, cv] += _dot(p.astype(BF16), do, 0, 0)
            dp = _dot(do, v, 1, 1)
            ds = (p * (dp - jnp.sum(dp * p, axis=-1, keepdims=True)) * scale).astype(BF16)
            dq_ref[:, c] = _dot(ds, k).astype(dq_ref.dtype)
            dkv_ref[:, c] += _dot(ds, q, 0, 0)

    return pl.pallas_call(
        body, name=name, grid=(S // ts,),
        in_specs=[pl.BlockSpec((ts, MEM_WIDTH), lambda i: (i, q_col // MEM_WIDTH)),
                  pl.BlockSpec((L, 2 * MEM_WIDTH), lambda i: (0, 0)),
                  pl.BlockSpec((ts, MEM_WIDTH), lambda i: (i, do_col // MEM_WIDTH))],
        out_specs=[pl.BlockSpec((ts, MEM_WIDTH), lambda i: (i, 0)),
                   pl.BlockSpec((L, 2 * MEM_WIDTH), lambda i: (0, 0))],
        out_shape=[jax.ShapeDtypeStruct((S, MEM_WIDTH), BF16), jax.ShapeDtypeStruct((L, 2 * MEM_WIDTH), F32)],
        compiler_params=_cparams(("arbitrary",)),
    )(proj, mkv, dcat)


def _lower_ones(n, strict=False):
    r = lax.broadcasted_iota(jnp.int32, (n, n), 0)
    c = lax.broadcasted_iota(jnp.int32, (n, n), 1)
    return (r > c if strict else r >= c).astype(F32)


def fox_gate_fwd(proj, f_col, b_f_row, *, name):
    S = proj.shape[0]
    tb = _tile(S, (256, 128))

    def body(f_ref, b_ref, c_ref, carry):
        @pl.when(pl.program_id(0) == 0)
        def _():
            carry[...] = jnp.zeros_like(carry)

        ls = jax.nn.log_sigmoid(f_ref[...] + b_ref[...])
        cum = _dot(_lower_ones(tb), ls, precision=HI) + carry[...]
        c_ref[...] = cum
        carry[...] = cum[tb - 1:tb, :]

    return pl.pallas_call(
        body, name=name, grid=(S // tb,),
        in_specs=[pl.BlockSpec((tb, 128), lambda i: (i, f_col // 128)), pl.BlockSpec((1, 128), lambda i: (0, 0))],
        out_specs=pl.BlockSpec((tb, 128), lambda i: (i, 0)),
        out_shape=jax.ShapeDtypeStruct((S, 128), F32),
        scratch_shapes=[pltpu.VMEM((1, 128), F32)],
        compiler_params=_cparams(("arbitrary",)),
    )(proj, b_f_row)


def fox_gate_bwd(dcf, proj, f_col, b_f_row, *, name):
    S = proj.shape[0]
    tb = _tile(S, (256, 128))
    nb = S // tb

    def body(d_ref, f_ref, b_ref, df_ref, db_ref, carry):
        @pl.when(pl.program_id(0) == 0)
        def _():
            carry[...] = jnp.zeros_like(carry)
            db_ref[...] = jnp.zeros_like(db_ref)

        upper = _lower_ones(tb).T
        rc = _dot(upper, d_ref[...], precision=HI) + carry[...]
        carry[...] = rc[0:1, :]
        df = rc * jax.nn.sigmoid(-(f_ref[...] + b_ref[...]))
        df_ref[...] = df
        db_ref[...] += jnp.sum(df, axis=0, keepdims=True)

    return pl.pallas_call(
        body, name=name, grid=(nb,),
        in_specs=[pl.BlockSpec((tb, 128), lambda i: (nb - 1 - i, 0)),
                  pl.BlockSpec((tb, 128), lambda i: (nb - 1 - i, f_col // 128)),
                  pl.BlockSpec((1, 128), lambda i: (0, 0))],
        out_specs=[pl.BlockSpec((tb, 128), lambda i: (nb - 1 - i, 0)), pl.BlockSpec((1, 128), lambda i: (0, 0))],
        out_shape=[jax.ShapeDtypeStruct((S, 128), F32), jax.ShapeDtypeStruct((1, 128), F32)],
        scratch_shapes=[pltpu.VMEM((1, 128), F32)],
        compiler_params=_cparams(("arbitrary",)),
    )(dcf, proj, b_f_row)


def _fox_block(S):
    return _tile(S, (512, 256, 128)) if S > 512 else S // 2


FOX_Q_BLOCKS = 2


def _causal(s, row0):
    r = row0 + lax.broadcasted_iota(jnp.int32, s.shape, 0)
    c = lax.broadcasted_iota(jnp.int32, s.shape, 1)
    return jnp.where(r >= c, s, -jnp.inf)


def fox_fwd(proj, n_heads, cf_col, cf_row, *, name):
    S = proj.shape[0]
    H = n_heads
    bk = _fox_block(S)
    qmul = FOX_Q_BLOCKS if S % (FOX_Q_BLOCKS * bk) == 0 else 1
    bq = qmul * bk
    scale = HEAD_DIM ** -0.5

    def body(q_ref, k_ref, v_ref, cc_ref, cr_ref, o_ref, st_ref):
        qi = pl.program_id(1)
        q = q_ref[...].astype(BF16)
        cq = cc_ref[...]

        def step(j, carry, diagonal=None):
            rows = pl.ds(pl.multiple_of(j * bk, bk), bk)
            k = k_ref[rows, :].astype(BF16)
            v = v_ref[rows, :].astype(BF16)
            m, l, acc = carry
            s = _dot(q, k, 1, 1) * scale + cq - cr_ref[j]
            if diagonal is not None:
                s = _causal(s, -diagonal * bk)
            m2 = jnp.maximum(m, jnp.max(s, axis=-1, keepdims=True))
            p = jnp.exp(s - m2)
            a = jnp.exp(m - m2)
            return m2, a * l + jnp.sum(p, axis=-1, keepdims=True), a * acc + _dot(p.astype(BF16), v)

        init = (jnp.full((bq, 1), -jnp.inf, F32), jnp.zeros((bq, 1), F32), jnp.zeros((bq, HEAD_DIM), F32))
        carry = lax.fori_loop(0, qi * qmul, step, init)
        for d in range(qmul):
            carry = step(qi * qmul + d, carry, diagonal=d)
        m, l, acc = carry
        o_ref[...] = (acc / l).astype(o_ref.dtype)
        lane = lax.broadcasted_iota(jnp.int32, (bq, 128), 1)
        st_ref[...] = jnp.where(lane == 0, m + jnp.log(l), jnp.where(lane == 1, cq, 0.0))

    return pl.pallas_call(
        body, name=name, grid=(H, S // bq),
        in_specs=[pl.BlockSpec((bq, HEAD_DIM), lambda h, i: (i, h)),
                  pl.BlockSpec((S, HEAD_DIM), lambda h, i: (0, H + h)),
                  pl.BlockSpec((S, HEAD_DIM), lambda h, i: (0, 2 * H + h)),
                  pl.BlockSpec((None, bq, 1), lambda h, i: (h, i, 0)),
                  pl.BlockSpec((None, S // bk, 1, bk), lambda h, i: (h, 0, 0, 0))],
        out_specs=[pl.BlockSpec((bq, HEAD_DIM), lambda h, i: (i, h)),
                   pl.BlockSpec((None, bq, 128), lambda h, i: (h, i, 0))],
        out_shape=[jax.ShapeDtypeStruct((S, H * HEAD_DIM), BF16), jax.ShapeDtypeStruct((H, S, 128), F32)],
        compiler_params=_cparams(("parallel", "parallel")),
    )(proj, proj, proj, cf_col, cf_row)


def fox_delta(stats, o, dcat, *, name):
    H, S, _ = stats.shape
    ts = _tile(S)

    def body(st_ref, o_ref, do_ref, out_ref):
        d = jnp.sum(o_ref[...].astype(F32) * do_ref[...].astype(F32), axis=-1, keepdims=True)
        lane = lax.broadcasted_iota(jnp.int32, (ts, 128), 1)
        out_ref[...] = jnp.where(lane == 2, d, st_ref[...])

    return pl.pallas_call(
        body, name=name, grid=(H, S // ts),
        in_specs=[pl.BlockSpec((None, ts, 128), lambda h, i: (h, i, 0)),
                  pl.BlockSpec((ts, HEAD_DIM), lambda h, i: (i, h)),
                  pl.BlockSpec((ts, HEAD_DIM), lambda h, i: (i, h))],
        out_specs=pl.BlockSpec((None, ts, 128), lambda h, i: (h, i, 0)),
        out_shape=jax.ShapeDtypeStruct((H, S, 128), F32),
        compiler_params=_cparams(("parallel", "parallel")),
    )(stats, o, dcat)


def fox_bwd(proj, n_heads, stats, cf_row, dcat, *, name):
    S = proj.shape[0]
    H = n_heads
    bq = _fox_block(S)
    nq = S // bq
    qmul = FOX_Q_BLOCKS if S % (FOX_Q_BLOCKS * bq) == 0 else 1
    bt = qmul * bq
    scale = HEAD_DIM ** -0.5

    def body(q_ref, k_ref, v_ref, do_ref, st_ref, cr_ref, dq_ref, dk_ref, dv_ref, dst_ref, dcr_ref):
        kj = pl.program_id(1)

        @pl.when(kj == 0)
        def _():
            dq_ref[...] = jnp.zeros_like(dq_ref)
            dst_ref[...] = jnp.zeros_like(dst_ref)

        k = k_ref[...].astype(BF16)
        v = v_ref[...].astype(BF16)
        ck = cr_ref[...]
        lane = lax.broadcasted_iota(jnp.int32, (bt, 128), 1)

        def step(t, carry, diagonal=False):
            dk, dv, dck = carry
            rows = pl.ds(pl.multiple_of(t * bt, bt), bt)
            q = q_ref[rows, :].astype(BF16)
            do = do_ref[rows, :].astype(BF16)
            st = st_ref[rows, :]
            lse, cq, delta = st[:, 0:1], st[:, 1:2], st[:, 2:3]
            s = _dot(q, k, 1, 1) * scale + cq - ck
            if diagonal:
                s = _causal(s, -lax.rem(kj, jnp.int32(qmul)) * bq)
            p = jnp.exp(s - lse)
            dv = dv + _dot(p.astype(BF16), do, 0, 0)
            ds = p * (_dot(do, v, 1, 1) - delta)
            dsb = (ds * scale).astype(BF16)
            dq_ref[rows, :] += _dot(dsb, k)
            dst_ref[rows, :] += jnp.where(lane == 0, jnp.sum(ds, axis=-1, keepdims=True), 0.0)
            return dk + _dot(dsb, q, 0, 0), dv, dck - jnp.sum(ds, axis=0, keepdims=True)

        init = (jnp.zeros((bq, HEAD_DIM), F32), jnp.zeros((bq, HEAD_DIM), F32), jnp.zeros((1, bq), F32))
        first = lax.div(kj, jnp.int32(qmul))
        dk, dv, dck = lax.fori_loop(first + 1, S // bt, step, step(first, init, diagonal=True))
        dk_ref[...] = dk
        dv_ref[...] = dv
        dcr_ref[...] = dck

    W = H * HEAD_DIM
    return pl.pallas_call(
        body, name=name, grid=(H, nq),
        in_specs=[pl.BlockSpec((S, HEAD_DIM), lambda h, j: (0, h)),
                  pl.BlockSpec((bq, HEAD_DIM), lambda h, j: (j, H + h)),
                  pl.BlockSpec((bq, HEAD_DIM), lambda h, j: (j, 2 * H + h)),
                  pl.BlockSpec((S, HEAD_DIM), lambda h, j: (0, h)),
                  pl.BlockSpec((None, S, 128), lambda h, j: (h, 0, 0)),
                  pl.BlockSpec((None, None, 1, bq), lambda h, j: (h, j, 0, 0))],
        out_specs=[pl.BlockSpec((S, HEAD_DIM), lambda h, j: (0, h)),
                   pl.BlockSpec((bq, HEAD_DIM), lambda h, j: (j, h)),
                   pl.BlockSpec((bq, HEAD_DIM), lambda h, j: (j, h)),
                   pl.BlockSpec((None, S, 128), lambda h, j: (h, 0, 0)),
                   pl.BlockSpec((None, None, 1, bq), lambda h, j: (h, j, 0, 0))],
        out_shape=[jax.ShapeDtypeStruct((S, W), F32), jax.ShapeDtypeStruct((S, W), F32),
                   jax.ShapeDtypeStruct((S, W), F32), jax.ShapeDtypeStruct((H, S, 128), F32),
                   jax.ShapeDtypeStruct((H, nq, 1, bq), F32)],
        compiler_params=_cparams(("parallel", "arbitrary")),
    )(proj, proj, proj, dcat, stats, cf_row)


def _pad_row(v, n=128):
    return jnp.pad(v.astype(F32), (0, n - v.shape[0])).reshape(1, n)


def fox_mixer_fwd(proj, f_col, b_f, tag):
    S = proj.shape[0]
    H = b_f.shape[0]
    bq = _fox_block(S)
    b_row = _pad_row(b_f)
    cf = fox_gate_fwd(proj, f_col, b_row, name=f"fox_gate_fwd_{tag}")
    cf_t = cf[:, :H].T
    o, stats = fox_fwd(proj, H, cf_t.reshape(H, S, 1), cf_t.reshape(H, S // bq, 1, bq), name=f"fox_fwd_{tag}")
    return o, (o, stats, cf_t, b_row)


def fox_mixer_bwd(proj, f_col, saved, dcat, tag):
    o, stats, cf_t, b_row = saved
    H, S = cf_t.shape
    bq = _fox_block(S)
    stats = fox_delta(stats, o, dcat, name=f"fox_delta_{tag}")
    dq, dk, dv, dst, dcr = fox_bwd(proj, H, stats, cf_t.reshape(H, S // bq, 1, bq), dcat, name=f"fox_bwd_{tag}")
    dcf = jnp.sum(dst, axis=-1) + dcr.reshape(H, S)
    dcf = jnp.pad(dcf.T, ((0, 0), (0, 128 - H)))
    df, db = fox_gate_bwd(dcf, proj, f_col, b_row, name=f"fox_gate_bwd_{tag}")
    return dq, dk, dv, df, db[0, :H]


def ew(fn, tok_ins, row_ins, tok_out_dtypes, n_row_out, *, width, name, tc=None, ts=None):
    S = tok_ins[0][0].shape[0]
    tc = tc or _tile(width, (512, 256, 128))
    ts = ts or _tile(S, (512, 256, 128))
    n_tok, n_row, n_to = len(tok_ins), len(row_ins), len(tok_out_dtypes)
    for _, col in tok_ins:
        assert col % tc == 0

    def body(*refs):
        ins = [r[...] for r in refs[:n_tok + n_row]]
        outs = fn(*ins)
        outs = outs if isinstance(outs, (tuple, list)) else (outs,)
        o_refs = refs[n_tok + n_row:]
        for o, v in zip(o_refs[:n_to], outs[:n_to]):
            o[...] = v.astype(o.dtype)
        if n_row_out:
            @pl.when(pl.program_id(1) == 0)
            def _():
                for o in o_refs[n_to:]:
                    o[...] = jnp.zeros_like(o)

            for o, v in zip(o_refs[n_to:], outs[n_to:]):
                o[...] += jnp.sum(v, axis=0, keepdims=True)

    def tok_spec(col):
        return pl.BlockSpec((ts, tc), lambda j, i: (i, col // tc + j))

    row_spec = pl.BlockSpec((1, tc), lambda j, i: (0, j))
    res = pl.pallas_call(
        body, name=name, grid=(width // tc, S // ts),
        in_specs=[tok_spec(col) for _, col in tok_ins] + [row_spec] * n_row,
        out_specs=[tok_spec(0)] * n_to + [row_spec] * n_row_out,
        out_shape=[jax.ShapeDtypeStruct((S, width), d) for d in tok_out_dtypes]
        + [jax.ShapeDtypeStruct((1, width), F32)] * n_row_out,
        compiler_params=_cparams(("parallel", "arbitrary" if n_row_out else "parallel")),
    )(*[a for a, _ in tok_ins], *row_ins)
    return res[0] if len(res) == 1 else res


S5_SUPER = 128 // S5_GROUP
S5_COLS = S5_SUPER * S5_STATE


def _shift_rows(x, d, up=False):
    T = x.shape[0]
    if d % 8 == 0:
        z = jnp.zeros((d, x.shape[1]), x.dtype)
        return jnp.concatenate([x[d:], z], axis=0) if up else jnp.concatenate([z, x[:T - d]], axis=0)
    row = lax.broadcasted_iota(jnp.int32, x.shape, 0)
    if up:
        return jnp.where(row < T - d, pltpu.roll(x, T - d, axis=0), 0.0)
    return jnp.where(row >= d, pltpu.roll(x, d, axis=0), 0.0)


def _s5_scan(bre, bim, tab, hre_s, him_s, carry, up):
    T, N = bre.shape
    G = T // 8
    hre, him = bre.reshape(G, 8, N), bim.reshape(G, 8, N)
    for k in range(3):
        shift = (8 - (1 << k)) if up else (1 << k)
        sre, sim = pltpu.roll(hre, shift, axis=1), pltpu.roll(him, shift, axis=1)
        cre, cim = tab[0, k], tab[1, k]
        hre, him = hre + cre * sre - cim * sim, him + cre * sim + cim * sre
    hre_s[...] = hre
    him_s[...] = him
    are, aim = tab[0, 3], tab[1, 3]
    edge = slice(0, 1) if up else slice(7, 8)

    def tile(i, c):
        g = G - 1 - i if up else i
        r = hre_s[g] + are * c[0] - aim * c[1]
        m = him_s[g] + are * c[1] + aim * c[0]
        hre_s[g] = r
        him_s[g] = m
        return r[edge], m[edge]

    carry = lax.fori_loop(0, G, tile, carry)
    return hre_s[...].reshape(T, N), him_s[...].reshape(T, N), carry


def _s5_states(u, bbre, bbim, tab, hre_s, him_s, start):
    return _s5_scan(_bdot(u, bbre), _bdot(u, bbim), tab, hre_s, him_s, (start[0:1], start[1:2]), False)


def _s5_tables(a_re, a_im, NJ):
    are, aim = a_re.reshape(NJ, 1, S5_COLS), a_im.reshape(NJ, 1, S5_COLS)
    pows = [(are, aim)]
    for _ in range(7):
        r, i = pows[-1]
        pows.append((r * are - i * aim, r * aim + i * are))
    row = jnp.arange(8).reshape(1, 8, 1)
    tabs = []
    for up in (False, True):
        re, im = [], []
        for d in (1, 2, 4):
            keep = (row <= 7 - d) if up else (row >= d)
            re.append(jnp.where(keep, pows[d - 1][0], 0.0))
            im.append(jnp.where(keep, -pows[d - 1][1] if up else pows[d - 1][1], 0.0))
        order = range(7, -1, -1) if up else range(8)
        re.append(jnp.concatenate([pows[n][0] for n in order], axis=1))
        im.append(jnp.concatenate([-pows[n][1] if up else pows[n][1] for n in order], axis=1))
        tabs.append(jnp.stack([jnp.stack(re, axis=1), jnp.stack(im, axis=1)], axis=1))
    return tabs


def _s5_block(S):
    return _tile(S, (1024, 512, 256, 128))


def _s5_specs(T):
    mat = lambda r, c: pl.BlockSpec((None, r, c), lambda j, t: (j, 0, 0))
    tab = pl.BlockSpec((None, 2, 4, 8, S5_COLS), lambda j, t: (j, 0, 0, 0, 0))
    scratch = pltpu.VMEM((T // 8, 8, S5_COLS), F32)
    return mat, tab, scratch


def s5_scan_fwd(proj, W, bbre, bbim, cre, cim, tab_f, tab_r, *, name):
    S = proj.shape[0]
    T = _s5_block(S)
    NJ, nblk = W // 128, S // T

    def body(u_ref, bbre_ref, bbim_ref, cre_ref, cim_ref, tab_ref, y_ref, st_ref, carry, hre_s, him_s):
        @pl.when(pl.program_id(1) == 0)
        def _():
            carry[...] = jnp.zeros_like(carry)

        st_ref[...] = carry[...]
        hre, him, (lre, lim) = _s5_states(u_ref[...], bbre_ref[...], bbim_ref[...], tab_ref[...], hre_s, him_s,
                                          carry[...])
        carry[0:1, :] = lre
        carry[1:2, :] = lim
        y_ref[...] = _bdot(hre, cre_ref[...]) - _bdot(him, cim_ref[...])

    mat, tab, scratch = _s5_specs(T)
    return pl.pallas_call(
        body, name=name, grid=(NJ, nblk),
        in_specs=[pl.BlockSpec((T, 128), lambda j, t: (t, j)), mat(128, S5_COLS), mat(128, S5_COLS),
                  mat(S5_COLS, 128), mat(S5_COLS, 128), tab],
        out_specs=[pl.BlockSpec((T, 128), lambda j, t: (t, j)),
                   pl.BlockSpec((None, None, 2, S5_COLS), lambda j, t: (j, t, 0, 0))],
        out_shape=[jax.ShapeDtypeStruct((S, W), F32), jax.ShapeDtypeStruct((NJ, nblk, 2, S5_COLS), F32)],
        scratch_shapes=[pltpu.VMEM((2, S5_COLS), F32), scratch, scratch],
        compiler_params=_cparams(("parallel", "arbitrary")),
    )(proj, bbre, bbim, cre, cim, tab_f)


def s5_scan_bwd(proj, W, dy, d_skip_row, bbre, bbim, cre, cim, tab_f, tab_r, starts, *, name):
    S = proj.shape[0]
    T = _s5_block(S)
    NJ, nblk = W // 128, S // T

    def body(u_ref, dy_ref, d_ref, bbre_ref, bbim_ref, cre_ref, cim_ref, tabf_ref, tabr_ref, st_ref,
             du_ref, dbbre_ref, dbbim_ref, dcre_ref, dcim_ref, da_ref, gcarry, hre_s, him_s, gre_s, gim_s):
        @pl.when(pl.program_id(1) == 0)
        def _():
            gcarry[...] = jnp.zeros_like(gcarry)
            for r in (dbbre_ref, dbbim_ref, dcre_ref, dcim_ref, da_ref):
                r[...] = jnp.zeros_like(r)

        u, dy, start = u_ref[...], dy_ref[...], st_ref[...]
        bbre, bbim, cre, cim = (r[...].astype(BF16) for r in (bbre_ref, bbim_ref, cre_ref, cim_ref))
        hre, him, _ = _s5_states(u, bbre, bbim, tabf_ref[...], hre_s, him_s, start)
        dyb, ub = dy.astype(BF16), u.astype(BF16)
        gre, gim, (fre, fim) = _s5_scan(_dot(dyb, cre, 1, 1), -_dot(dyb, cim, 1, 1), tabr_ref[...], gre_s, gim_s,
                                        (gcarry[0:1], gcarry[1:2]), True)
        gcarry[0:1, :] = fre
        gcarry[1:2, :] = fim
        greb, gimb = gre.astype(BF16), gim.astype(BF16)
        du_ref[...] = (_dot(greb, bbre, 1, 1) + _dot(gimb, bbim, 1, 1) + dy * d_ref[...]).astype(du_ref.dtype)
        dbbre_ref[...] += _dot(ub, greb, 0, 0)
        dbbim_ref[...] += _dot(ub, gimb, 0, 0)
        dcre_ref[...] += _dot(hre.astype(BF16), dyb, 0, 0)
        dcim_ref[...] -= _dot(him.astype(BF16), dyb, 0, 0)
        first = lax.broadcasted_iota(jnp.int32, hre.shape, 0) == 0
        pre = _shift_rows(hre, 1) + jnp.where(first, start[0:1], 0.0)
        pim = _shift_rows(him, 1) + jnp.where(first, start[1:2], 0.0)
        da_ref[0:1, :] += jnp.sum(gre * pre + gim * pim, axis=0, keepdims=True)
        da_ref[1:2, :] += jnp.sum(gim * pre - gre * pim, axis=0, keepdims=True)

    mat, tab, scratch = _s5_specs(T)
    tok = pl.BlockSpec((T, 128), lambda j, t: (nblk - 1 - t, j))
    return pl.pallas_call(
        body, name=name, grid=(NJ, nblk),
        in_specs=[tok, tok, pl.BlockSpec((1, 128), lambda j, t: (0, j)), mat(128, S5_COLS), mat(128, S5_COLS),
                  mat(S5_COLS, 128), mat(S5_COLS, 128), tab, tab,
                  pl.BlockSpec((None, None, 2, S5_COLS), lambda j, t: (j, nblk - 1 - t, 0, 0))],
        out_specs=[tok, mat(128, S5_COLS), mat(128, S5_COLS), mat(S5_COLS, 128), mat(S5_COLS, 128), mat(2, S5_COLS)],
        out_shape=[jax.ShapeDtypeStruct((S, W), BF16),
                   jax.ShapeDtypeStruct((NJ, 128, S5_COLS), F32), jax.ShapeDtypeStruct((NJ, 128, S5_COLS), F32),
                   jax.ShapeDtypeStruct((NJ, S5_COLS, 128), F32), jax.ShapeDtypeStruct((NJ, S5_COLS, 128), F32),
                   jax.ShapeDtypeStruct((NJ, 2, S5_COLS), F32)],
        scratch_shapes=[pltpu.VMEM((2, S5_COLS), F32)] + [scratch] * 4,
        compiler_params=_cparams(("parallel", "arbitrary")),
    )(proj, dy, d_skip_row, bbre, bbim, cre, cim, tab_f, tab_r, starts)


def _s5_discretize(lam_re, lam_im, log_dt, b_re, b_im):
    dt = jnp.exp(log_dt)[:, None]
    mag = jnp.exp(lam_re * dt)
    a_re, a_im = mag * jnp.cos(lam_im * dt), mag * jnp.sin(lam_im * dt)
    den = lam_re * lam_re + lam_im * lam_im
    z_re = ((a_re - 1.0) * lam_re + a_im * lam_im) / den
    z_im = (a_im * lam_re - (a_re - 1.0) * lam_im) / den
    bb_re = z_re[..., None] * b_re - z_im[..., None] * b_im
    bb_im = z_re[..., None] * b_im + z_im[..., None] * b_re
    return a_re, a_im, bb_re, bb_im


def _blockdiag(x):
    G, r, c = x.shape
    x = x.reshape(G // S5_SUPER, S5_SUPER, r, c)
    eye = jnp.eye(S5_SUPER, dtype=x.dtype)
    return (x[:, :, :, None, :] * eye[None, :, None, :, None]).reshape(G // S5_SUPER, S5_SUPER * r, S5_SUPER * c)


def _blockdiag_t(x, r, c):
    NJ = x.shape[0]
    x = x.reshape(NJ, S5_SUPER, r, S5_SUPER, c)
    return jnp.stack([x[:, i, :, i, :] for i in range(S5_SUPER)], axis=1).reshape(NJ * S5_SUPER, r, c)


def _gelu(x):
    c = math.sqrt(2.0 / math.pi)
    return 0.5 * x * (1.0 + jnp.tanh(c * (x + 0.044715 * x * x * x)))


def _gelu_grad(x):
    c = math.sqrt(2.0 / math.pi)
    t = jnp.tanh(c * (x + 0.044715 * x * x * x))
    return 0.5 * (1.0 + t) + 0.5 * x * (1.0 - t * t) * c * (1.0 + 3 * 0.044715 * x * x)


def s5_mixer_fwd(proj, W, p, w_glu_bf, tag):
    a_re, a_im, bb_re, bb_im = _s5_discretize(p["lam_re"], p["lam_im"], p["log_dt"], p["b_re"], p["b_im"])
    tab_f, tab_r = _s5_tables(a_re, a_im, W // 128)
    bbre = _blockdiag(jnp.swapaxes(bb_re, 1, 2))
    bbim = _blockdiag(jnp.swapaxes(bb_im, 1, 2))
    cre = _blockdiag(jnp.swapaxes(p["c_re"], 1, 2))
    cim = _blockdiag(jnp.swapaxes(p["c_im"], 1, 2))
    mats = (bbre, bbim, cre, cim, tab_f, tab_r)
    y, starts = s5_scan_fwd(proj, W, *mats, name=f"s5_scan_fwd_{tag}")
    d_row = p["d_skip"].reshape(1, W)
    yy = ew(lambda y, u, d: _gelu(y + d * u), [(y, 0), (proj, 0)], [d_row], [F32], 0, width=W,
            name=f"s5_act_fwd_{tag}")
    mix, z = mm(yy, w_glu_bf, extras=(yy, p["b_glu"].reshape(1, W)), out_dtypes=(BF16, F32),
                epilogue=lambda acc, yy, b: (yy * jax.nn.sigmoid(acc + b), acc + b), name=f"s5_glu_{tag}")
    return mix, (mats, starts, y, yy, z, d_row)


def s5_mixer_bwd(proj, W, p, w_glu_bf, saved, dcat, tag):
    mats, starts, y, yy, z, d_row = saved

    def glu_bwd(dm, yy, z):
        sg = jax.nn.sigmoid(z)
        dz = dm.astype(F32) * yy * sg * (1.0 - sg)
        return dm.astype(F32) * sg, dz, dz

    dyy1, dz, db_glu = ew(glu_bwd, [(dcat, 0), (yy, 0), (z, 0)], [], [F32, F32], 1, width=W, name=f"s5_glu_bwd_{tag}")
    dw_glu = mm(yy, dz, ta=True, out_dtypes=(BF16,), name=f"s5_dwglu_{tag}")
    dyy = mm(dz, w_glu_bf, tb=True, extras=(dyy1,), epilogue=lambda acc, e: (acc + e,), name=f"s5_dyy_{tag}")

    def act_bwd(dyy, y, u, d):
        dpre = dyy * _gelu_grad(y + d * u)
        return dpre, dpre * u

    dy, dd = ew(act_bwd, [(dyy, 0), (y, 0), (proj, 0)], [d_row], [F32], 1, width=W, name=f"s5_act_bwd_{tag}")
    du, dbbre, dbbim, dcre, dcim, da = s5_scan_bwd(proj, W, dy, d_row, *mats, starts, name=f"s5_scan_bwd_{tag}")
    G = W // S5_GROUP
    dbb_re = jnp.swapaxes(_blockdiag_t(dbbre, S5_GROUP, S5_STATE), 1, 2)
    dbb_im = jnp.swapaxes(_blockdiag_t(dbbim, S5_GROUP, S5_STATE), 1, 2)
    dc_re = jnp.swapaxes(_blockdiag_t(dcre, S5_STATE, S5_GROUP), 1, 2)
    dc_im = jnp.swapaxes(_blockdiag_t(dcim, S5_STATE, S5_GROUP), 1, 2)
    da_re, da_im = da[:, 0, :].reshape(G, S5_STATE), da[:, 1, :].reshape(G, S5_STATE)
    _, vjp = jax.vjp(_s5_discretize, p["lam_re"], p["lam_im"], p["log_dt"], p["b_re"], p["b_im"])
    dlam_re, dlam_im, dlog_dt, db_re, db_im = vjp((da_re, da_im, dbb_re, dbb_im))
    grads = dict(lam_re=dlam_re, lam_im=dlam_im, log_dt=dlog_dt, b_re=db_re, b_im=db_im, c_re=dc_re, c_im=dc_im,
                 d_skip=dd.reshape(W), w_glu=dw_glu, b_glu=db_glu.reshape(W))
    return du, grads


CONV_ROWS = 256


def _conv_taps(ext, w):
    acc = ext * w[3:4]
    for j in range(1, 4):
        acc = acc + pltpu.roll(ext, j, axis=0) * w[3 - j:4 - j]
    return acc


def gdn_conv_fwd(proj, width, conv_w, *, name):
    S = proj.shape[0]
    tc = _tile(width, (256, 128))
    T = min(CONV_ROWS, S)
    n = S // T

    def body(x_ref, w_ref, y_ref):
        w = w_ref[...]

        def chunk(c, carry):
            base = pl.multiple_of(c * T, T)
            prev = x_ref[pl.ds(pl.multiple_of(jnp.maximum(base - 8, 0), 8), 8), :]
            ext = jnp.concatenate([jnp.where(c > 0, prev, 0.0), x_ref[pl.ds(base, T), :]], axis=0)
            pre = _conv_taps(ext, w)[8:]
            y_ref[pl.ds(base, T), :] = pre * jax.nn.sigmoid(pre)
            return carry

        lax.fori_loop(0, n, chunk, 0)

    return pl.pallas_call(
        body, name=name, grid=(width // tc,),
        in_specs=[pl.BlockSpec((S, tc), lambda j: (0, j)), pl.BlockSpec((4, tc), lambda j: (0, j))],
        out_specs=pl.BlockSpec((S, tc), lambda j: (0, j)),
        out_shape=jax.ShapeDtypeStruct((S, width), F32),
        compiler_params=_cparams(("parallel",)),
    )(proj, conv_w)


def gdn_conv_bwd(proj, width, conv_w, dy, *, name):
    S = proj.shape[0]
    tc = _tile(width // 3, (256, 128))
    per = width // 3 // tc
    T = min(CONV_ROWS, S)
    n = S // T
    E = T + 16

    def body(x_ref, w_ref, dy_ref, dx_ref, dw_ref):
        w = w_ref[...]

        def halo(ref, start, keep):
            start = pl.multiple_of(jnp.clip(start, 0, S - 8), 8)
            return jnp.where(keep, ref[pl.ds(start, 8), :], 0.0)

        def chunk(c, dw):
            base = pl.multiple_of(c * T, T)
            rows = pl.ds(base, T)
            ext = jnp.concatenate([halo(x_ref, base - 8, c > 0), x_ref[rows, :], halo(x_ref, base + T, c < n - 1)], axis=0)
            dye = jnp.concatenate([jnp.zeros((8, tc), F32), dy_ref[rows, :], halo(dy_ref, base + T, c < n - 1)], axis=0)
            pre = _conv_taps(ext, w)
            sg = jax.nn.sigmoid(pre)
            dpre = dye * (sg * (1.0 + pre * (1.0 - sg)))
            dx = dpre * w[3:4]
            for j in range(1, 4):
                dx = dx + pltpu.roll(dpre, E - j, axis=0) * w[3 - j:4 - j]
            dx_ref[rows, :] = dx[8:8 + T].astype(dx_ref.dtype)
            own = dpre[8:8 + T]
            parts = [jnp.sum(own * pltpu.roll(ext, 3 - i, axis=0)[8:8 + T], axis=0, keepdims=True) if i < 3
                     else jnp.sum(own * ext[8:8 + T], axis=0, keepdims=True) for i in range(4)]
            return dw + jnp.concatenate(parts, axis=0)

        dw_ref[...] = lax.fori_loop(0, n, chunk, jnp.zeros((4, tc), F32))

    return pl.pallas_call(
        body, name=name, grid=(width // tc,),
        in_specs=[pl.BlockSpec((S, tc), lambda j: (0, j)), pl.BlockSpec((4, tc), lambda j: (0, j)),
                  pl.BlockSpec((None, S, tc), lambda j: (j // per, 0, j % per))],
        out_specs=[pl.BlockSpec((S, tc), lambda j: (0, j)), pl.BlockSpec((4, tc), lambda j: (0, j))],
        out_shape=[jax.ShapeDtypeStruct((S, width), BF16), jax.ShapeDtypeStruct((4, width), F32)],
        compiler_params=_cparams(("parallel",)),
    )(proj, conv_w, dy)


@functools.partial(jax.custom_vjp, nondiff_argnums=(0,))
def _bein(spec, a, b):
    return jnp.einsum(spec, a.astype(BF16), b.astype(BF16), preferred_element_type=F32)


def _bein_fwd(spec, a, b):
    return _bein(spec, a, b), (a, b)


def _bein_bwd(spec, res, g):
    a, b = res
    ins, out = spec.split("->")
    sa, sb = ins.split(",")
    return _bein(f"{out},{sb}->{sa}", g, b), _bein(f"{sa},{out}->{sb}", a, g)


_bein.defvjp(_bein_fwd, _bein_bwd)


def _hmm(a, b, precision=lax.Precision.HIGH):
    return jnp.einsum("ncs,nsd->ncd", a, b, precision=precision, preferred_element_type=F32)


def _inv_unit_lower(L):
    C = L.shape[-1]
    r = lax.broadcasted_iota(jnp.int32, L.shape, 1)
    c = lax.broadcasted_iota(jnp.int32, L.shape, 2)
    eye = (r == c).astype(F32)
    D = jnp.where(jnp.right_shift(r, 4) == jnp.right_shift(c, 4), L, 0.0)
    D2 = _hmm(D, D)
    D4 = _hmm(D2, D2)
    D8 = _hmm(D4, D4)
    dinv = _hmm(_hmm(_hmm(eye - D, eye + D2), eye + D4), eye + D8)
    N = _hmm(dinv, L - D)
    return _hmm(_hmm(eye - N, eye + _hmm(N, N)), dinv)


def _softplus(x):
    return jnp.maximum(x, 0.0) + jnp.log(1.0 + jnp.exp(-jnp.abs(x)))


def _gdn_local(qc, kc, vc, ab, a_log_row, dt_row, h, n_heads):
    R = qc.shape[0]
    C = GDN_CHUNK
    n = R // C
    lane = lax.broadcasted_iota(jnp.int32, (1, 128), 1)
    pick = lambda x, i: jnp.sum(jnp.where(lane == i, x, 0.0), axis=-1, keepdims=True)
    a_in, b_in = pick(ab, h).reshape(n, C, 1), pick(ab, n_heads + h).reshape(n, C, 1)
    a_log, dt_bias = pick(a_log_row, h), pick(dt_row, h)
    q3, k3, v = qc.reshape(n, C, 128), kc.reshape(n, C, 128), vc.reshape(n, C, 128)
    q = q3 * lax.rsqrt(jnp.sum(q3 * q3, axis=-1, keepdims=True) + EPS) * HEAD_DIM ** -0.5
    k = k3 * lax.rsqrt(jnp.sum(k3 * k3, axis=-1, keepdims=True) + EPS)
    beta = jax.nn.sigmoid(b_in)
    g = -jnp.exp(a_log) * _softplus(a_in + dt_bias)
    r = lax.broadcasted_iota(jnp.int32, (n, C, C), 1)
    c = lax.broadcasted_iota(jnp.int32, (n, C, C), 2)
    gc = _hmm((r >= c).astype(F32), jnp.broadcast_to(g, (n, C, C)), precision=HI)
    gcol = gc[:, :, 0:1]
    grow = jnp.sum(jnp.where(r == c, gc, 0.0), axis=1, keepdims=True)
    decay = jnp.exp(jnp.where(r >= c, gc - grow, -jnp.inf))
    kb, vb = k * beta, v * beta
    lmat = jnp.where(r > c, _bein("ncd,nsd->ncs", kb, k) * decay, 0.0)
    eg = jnp.exp(gcol)
    rhs = jnp.concatenate([vb, kb * eg], axis=-1)
    attn = jnp.where(r >= c, _bein("ncd,nsd->ncs", q, k) * decay, 0.0)
    glast = gcol[:, C - 1:C, :]
    k_dec = k * jnp.exp(glast - gcol)
    g_last = jnp.broadcast_to(jnp.exp(glast), (n, 1, 128))
    return lmat, rhs, attn.reshape(R, C), (q * eg).reshape(R, 128), k_dec.reshape(R, 128), g_last


def _gdn_rows(S):
    return _tile(S, (256, 128, 64))


def _gdn_prep_rows(S):
    return _tile(S, (1024, 512, 256, 128, 64))


def gdn_prep_fwd(qkv, proj, ab_col, a_log_row, dt_row, H, *, name):
    S = qkv.shape[0]
    R = _gdn_prep_rows(S)
    n, nc = R // GDN_CHUNK, S // GDN_CHUNK

    def body(q_ref, k_ref, v_ref, ab_ref, al_ref, dt_ref, u_ref, w_ref, at_ref, qd_ref, kd_ref, gl_ref, t_ref):
        lmat, rhs, attn, qd, kd, gl = _gdn_local(q_ref[...], k_ref[...], v_ref[...], ab_ref[...], al_ref[...],
                                                 dt_ref[...], pl.program_id(1), H)
        tinv = _inv_unit_lower(lmat)
        sol = _hmm(tinv, rhs)
        outs = (sol[..., :128].reshape(R, 128), sol[..., 128:].reshape(R, 128), attn, qd, kd, gl,
                tinv.reshape(R, GDN_CHUNK))
        for r, v in zip((u_ref, w_ref, at_ref, qd_ref, kd_ref, gl_ref, t_ref), outs):
            r[...] = v

    head = lambda off: pl.BlockSpec((R, 128), lambda i, h: (i, off + h))
    row = pl.BlockSpec((1, 128), lambda i, h: (0, 0))
    big = jax.ShapeDtypeStruct((S, H * 128), F32)
    sq = pl.BlockSpec((None, R, GDN_CHUNK), lambda i, h: (h, i, 0))
    outs = pl.pallas_call(
        body, name=name, grid=(S // R, H),
        in_specs=[head(0), head(H), head(2 * H), pl.BlockSpec((R, 128), lambda i, h: (i, ab_col // 128)), row, row],
        out_specs=[head(0), head(0), sq, head(0), head(0), pl.BlockSpec((None, n, 1, 128), lambda i, h: (h, i, 0, 0)),
                   sq],
        out_shape=[big, big, jax.ShapeDtypeStruct((H, S, GDN_CHUNK), F32), big, big,
                   jax.ShapeDtypeStruct((H, nc, 1, 128), F32), jax.ShapeDtypeStruct((H, S, GDN_CHUNK), F32)],
        compiler_params=_cparams(("parallel", "parallel")),
    )(qkv, qkv, qkv, proj, a_log_row, dt_row)
    return tuple(outs[:6]), outs[6]


def gdn_prep_bwd(qkv, proj, ab_col, a_log_row, dt_row, H, u, w, tinv, cts, *, name):
    S = qkv.shape[0]
    R = _gdn_prep_rows(S)
    C = GDN_CHUNK
    n, nc = R // C, S // C

    def body(q_ref, k_ref, v_ref, ab_ref, al_ref, dt_ref, u_ref, w_ref, t_ref,
             du_ref, dw_ref, dat_ref, dqd_ref, dkd_ref, dgl_ref,
             dqkv_ref, dab_ref, dal_ref, ddt_ref):
        i, h = pl.program_id(0), pl.program_id(1)

        @pl.when(h == 0)
        def _():
            dab_ref[...] = jnp.zeros_like(dab_ref)

        @pl.when((h == 0) & (i == 0))
        def _():
            dal_ref[...] = jnp.zeros_like(dal_ref)
            ddt_ref[...] = jnp.zeros_like(ddt_ref)

        tinv_t = jnp.swapaxes(t_ref[...].reshape(n, C, C), 1, 2)
        dsol = jnp.concatenate([du_ref[...], dw_ref[...]], axis=-1).reshape(n, C, 256)
        sol = jnp.concatenate([u_ref[...], w_ref[...]], axis=-1).reshape(n, C, 256)
        drhs = _hmm(tinv_t, dsol)
        dlmat = -jnp.einsum("ncd,nsd->ncs", drhs, sol, precision=lax.Precision.HIGH, preferred_element_type=F32)
        f = lambda q, k, v, ab, al, dt: _gdn_local(q, k, v, ab, al, dt, h, H)
        _, vjp = jax.vjp(f, q_ref[...], k_ref[...], v_ref[...], ab_ref[...], al_ref[...], dt_ref[...])
        dq, dk, dv, dab, dal, ddt = vjp((dlmat, drhs, dat_ref[...], dqd_ref[...], dkd_ref[...], dgl_ref[...]))
        dqkv_ref[0] = dq
        dqkv_ref[1] = dk
        dqkv_ref[2] = dv
        dab_ref[...] += dab
        dal_ref[...] += dal
        ddt_ref[...] += ddt

    head = lambda off: pl.BlockSpec((R, 128), lambda i, h: (i, off + h))
    row = pl.BlockSpec((1, 128), lambda i, h: (0, 0))
    at = pl.BlockSpec((None, R, GDN_CHUNK), lambda i, h: (h, i, 0))
    gl = pl.BlockSpec((None, n, 1, 128), lambda i, h: (h, i, 0, 0))
    W = H * 128
    return pl.pallas_call(
        body, name=name, grid=(S // R, H),
        in_specs=[head(0), head(H), head(2 * H), pl.BlockSpec((R, 128), lambda i, h: (i, ab_col // 128)), row, row,
                  head(0), head(0), at, head(0), head(0), at, head(0), head(0), gl],
        out_specs=[pl.BlockSpec((3, R, 128), lambda i, h: (0, i, h)), pl.BlockSpec((R, 128), lambda i, h: (i, 0)),
                   row, row],
        out_shape=[jax.ShapeDtypeStruct((3, S, W), F32), jax.ShapeDtypeStruct((S, 128), F32)]
        + [jax.ShapeDtypeStruct((1, 128), F32)] * 2,
        compiler_params=_cparams(("arbitrary", "arbitrary")),
    )(qkv, qkv, qkv, proj, a_log_row, dt_row, u, w, tinv, *cts)


def _gdn_head_group(H):
    return next(g for g in (12, 6, 4, 3, 2, 1) if H % g == 0)


def gdn_scan_fwd(u, w, attn, qd, kd, gl, *, name):
    S = u.shape[0]
    H = attn.shape[0]
    C = GDN_CHUNK
    R = _gdn_rows(S)
    n, nc = R // C, S // C

    G = _gdn_head_group(H)
    heads = range(G)
    col = lambda h: slice(h * 128, (h + 1) * 128)

    def body(u_ref, w_ref, at_ref, qd_ref, kd_ref, gl_ref, o_ref, st_ref, state):
        @pl.when(pl.program_id(1) == 0)
        def _():
            state[...] = jnp.zeros_like(state)

        for c in range(n):
            rows = slice(c * C, (c + 1) * C)
            s = [state[h] for h in heads]
            for h in heads:
                st_ref[h, c] = s[h]
            sb = [t.astype(BF16) for t in s]
            ws = [_dot(w_ref[rows, col(h)].astype(BF16), sb[h]) for h in heads]
            qs = [_dot(qd_ref[rows, col(h)].astype(BF16), sb[h]) for h in heads]
            vb = [(u_ref[rows, col(h)] - ws[h]).astype(BF16) for h in heads]
            av = [_dot(at_ref[h, rows, :].astype(BF16), vb[h]) for h in heads]
            kv = [_dot(kd_ref[rows, col(h)].astype(BF16), vb[h], 0, 0) for h in heads]
            for h in heads:
                o_ref[rows, col(h)] = qs[h] + av[h]
                state[h] = s[h] * gl_ref[h, c] + kv[h]

    head = pl.BlockSpec((R, G * 128), lambda g, i: (i, g))
    return pl.pallas_call(
        body, name=name, grid=(H // G, S // R),
        in_specs=[head, head, pl.BlockSpec((G, R, C), lambda g, i: (g, i, 0)), head, head,
                  pl.BlockSpec((G, n, 1, 128), lambda g, i: (g, i, 0, 0))],
        out_specs=[head, pl.BlockSpec((G, n, 128, 128), lambda g, i: (g, i, 0, 0))],
        out_shape=[jax.ShapeDtypeStruct((S, H * 128), F32), jax.ShapeDtypeStruct((H, nc, 128, 128), F32)],
        scratch_shapes=[pltpu.VMEM((G, 128, 128), F32)],
        compiler_params=_cparams(("parallel", "arbitrary")),
    )(u, w, attn, qd, kd, gl)


def gdn_scan_bwd(u, w, attn, qd, kd, gl, states, do, *, name):
    S = u.shape[0]
    H = attn.shape[0]
    C = GDN_CHUNK
    R = _gdn_rows(S)
    n, nc, nb = R // C, S // C, S // R

    G = _gdn_head_group(H)
    heads = range(G)
    col = lambda h: slice(h * 128, (h + 1) * 128)

    def body(u_ref, w_ref, at_ref, qd_ref, kd_ref, gl_ref, st_ref, do_ref,
             du_ref, dw_ref, dat_ref, dqd_ref, dkd_ref, dgl_ref, dstate):
        @pl.when(pl.program_id(1) == 0)
        def _():
            dstate[...] = jnp.zeros_like(dstate)

        lane = lax.broadcasted_iota(jnp.int32, (1, 128), 1)
        for c in reversed(range(n)):
            rows = slice(c * C, (c + 1) * C)
            tile = lambda ref: [ref[rows, col(h)].astype(BF16) for h in heads]
            s = [st_ref[h, c] for h in heads]
            ds2 = [dstate[h] for h in heads]
            sb = [t.astype(BF16) for t in s]
            ds2b = [t.astype(BF16) for t in ds2]
            wb, qdb, kdb, dob = tile(w_ref), tile(qd_ref), tile(kd_ref), tile(do_ref)
            atb = [at_ref[h, rows, :].astype(BF16) for h in heads]
            ws = [_dot(wb[h], sb[h]) for h in heads]
            dv1 = [_dot(atb[h], dob[h], 0, 0) for h in heads]
            dv2 = [_dot(kdb[h], ds2b[h]) for h in heads]
            dqd = [_dot(dob[h], sb[h], 1, 1) for h in heads]
            qdo = [_dot(qdb[h], dob[h], 0, 0) for h in heads]
            vb = [(u_ref[rows, col(h)] - ws[h]).astype(BF16) for h in heads]
            dv = [dv1[h] + dv2[h] for h in heads]
            dvb = [t.astype(BF16) for t in dv]
            dw = [_dot(dvb[h], sb[h], 1, 1) for h in heads]
            dat = [_dot(dob[h], vb[h], 1, 1) for h in heads]
            dkd = [_dot(vb[h], ds2b[h], 1, 1) for h in heads]
            wdv = [_dot(wb[h], dvb[h], 0, 0) for h in heads]
            for h in heads:
                du_ref[rows, col(h)] = dv[h]
                dw_ref[rows, col(h)] = -dw[h]
                dat_ref[h, rows, :] = dat[h]
                dqd_ref[rows, col(h)] = dqd[h]
                dkd_ref[rows, col(h)] = dkd[h]
                dgl = jnp.sum(jnp.sum(ds2[h] * s[h], axis=1, keepdims=True), axis=0, keepdims=True)
                dgl_ref[h, c] = jnp.where(lane == 0, dgl, 0.0)
                dstate[h] = ds2[h] * gl_ref[h, c] + qdo[h] - wdv[h]

    head = pl.BlockSpec((R, G * 128), lambda g, i: (nb - 1 - i, g))
    at = pl.BlockSpec((G, R, C), lambda g, i: (g, nb - 1 - i, 0))
    glb = pl.BlockSpec((G, n, 1, 128), lambda g, i: (g, nb - 1 - i, 0, 0))
    big = jax.ShapeDtypeStruct((S, H * 128), F32)
    return pl.pallas_call(
        body, name=name, grid=(H // G, nb),
        in_specs=[head, head, at, head, head, glb,
                  pl.BlockSpec((G, n, 128, 128), lambda g, i: (g, nb - 1 - i, 0, 0)), head],
        out_specs=[head, head, at, head, head, glb],
        out_shape=[big, big, jax.ShapeDtypeStruct((H, S, C), F32), big, big,
                   jax.ShapeDtypeStruct((H, nc, 1, 128), F32)],
        scratch_shapes=[pltpu.VMEM((G, 128, 128), F32)],
        compiler_params=_cparams(("parallel", "arbitrary")),
    )(u, w, attn, qd, kd, gl, states, do)


def _head_rms(o):
    return lax.rsqrt(jnp.mean(o * o, axis=-1, keepdims=True) + EPS)


def gdn_mixer_fwd(proj, W, p, tag):
    H = W // HEAD_DIM
    ab_col = 4 * W + MEM_WIDTH
    qkv = gdn_conv_fwd(proj, 3 * W, p["conv_w"], name=f"gdn_conv_fwd_{tag}")
    al_row, dt_row = _pad_row(p["a_log"]), _pad_row(p["dt_bias"])
    pre, tinv = gdn_prep_fwd(qkv, proj, ab_col, al_row, dt_row, H, name=f"gdn_prep_fwd_{tag}")
    o, states = gdn_scan_fwd(*pre, name=f"gdn_scan_fwd_{tag}")
    gn_row = jnp.tile(p["o_norm"].reshape(1, HEAD_DIM), (1, H))

    def gate_fwd(o, gate, gn):
        return o * _head_rms(o) * gn * (gate * jax.nn.sigmoid(gate))

    mix = ew(gate_fwd, [(o, 0), (proj, 3 * W)], [gn_row], [BF16], 0, width=W, tc=HEAD_DIM, name=f"gdn_gate_fwd_{tag}")
    return mix, (qkv, pre, tinv, states, o, gn_row, al_row, dt_row)


def gdn_mixer_bwd(proj, W, p, saved, dcat, tag):
    qkv, pre, tinv, states, o, gn_row, al_row, dt_row = saved
    H = W // HEAD_DIM
    ab_col = 4 * W + MEM_WIDTH

    def gate_bwd(dm, o, gate, gn):
        dm = dm.astype(F32)
        r = _head_rms(o)
        xh = o * r
        sg = jax.nn.sigmoid(gate)
        dy = dm * gate * sg
        dgate = dm * xh * gn * (sg * (1.0 + gate * (1.0 - sg)))
        dxh = dy * gn
        do = r * (dxh - xh * jnp.mean(dxh * xh, axis=-1, keepdims=True))
        return do, dgate, dy * xh

    do, dgate, dgn = ew(gate_bwd, [(dcat, 0), (o, 0), (proj, 3 * W)], [gn_row], [F32, BF16], 1, width=W, tc=HEAD_DIM,
                        name=f"gdn_gate_bwd_{tag}")
    cts = gdn_scan_bwd(*pre, states, do, name=f"gdn_scan_bwd_{tag}")
    dqkv, dab, dal, ddt = gdn_prep_bwd(qkv, proj, ab_col, al_row, dt_row, H, pre[0], pre[1], tinv, cts,
                                       name=f"gdn_prep_bwd_{tag}")
    dx, dconv = gdn_conv_bwd(proj, 3 * W, p["conv_w"], dqkv, name=f"gdn_conv_bwd_{tag}")
    grads = dict(conv_w=dconv, a_log=dal[0, :H], dt_bias=ddt[0, :H], o_norm=dgn.reshape(H, HEAD_DIM).sum(axis=0))
    return dx, dgate, dab, grads


def exchange(arrays, same_block, *, name):
    n = len(arrays)

    def body(*refs):
        ex = _Exchange(refs[:n], refs[n:2 * n], same_block, *refs[2 * n:])
        ex.start()
        ex.finish()

    shapes, sems = _exchange_shapes(arrays, same_block)
    any_spec = pl.BlockSpec(memory_space=pl.ANY)
    return pl.pallas_call(body, name=name, in_specs=[any_spec] * n, out_specs=[any_spec] * n, out_shape=shapes,
                          scratch_shapes=sems)(*arrays)


def _row_tile(R, row_bytes):
    for t in (512, 256, 128, 64, 32, 16, 8):
        if R % t == 0 and 2 * t * row_bytes <= 24 * 2 ** 20:
            return t
    return R


def adamw(parts, w, m, v, *, name, layer=None, prev=None):
    P, R, C = parts.shape
    tr = _row_tile(R, C * (P * parts.dtype.itemsize + 7 * 4))
    c1, c2 = 1.0 - ADAM_B1 ** ADAM_STEP, 1.0 - ADAM_B2 ** ADAM_STEP
    n_prev = 0 if prev is None else 4

    def body(p_ref, w_ref, m_ref, v_ref, *rest):
        g_ref, d_ref, nm_ref, nv_ref = rest[n_prev:]
        g = p_ref[0].astype(F32)
        for s in range(1, P):
            g = g + p_ref[s].astype(F32)
        m = ADAM_B1 * m_ref[...] + (1.0 - ADAM_B1) * g
        v = ADAM_B2 * v_ref[...] + (1.0 - ADAM_B2) * (g * g)
        g_ref[...] = g
        nm_ref[...] = m
        nv_ref[...] = v
        d_ref[...] = -ADAM_LR * ((m / c1) / (jnp.sqrt(v / c2) + ADAM_EPS) + ADAM_WD * w_ref[...])

    if layer is None:
        blk, shape = pl.BlockSpec((tr, C), lambda i: (i, 0)), (R, C)
    else:
        blk, shape = pl.BlockSpec((None, tr, C), lambda i: (layer, i, 0)), w.shape
    return pl.pallas_call(
        body, name=name, grid=(R // tr,),
        in_specs=[pl.BlockSpec((P, tr, C), lambda i: (0, i, 0)), blk, blk, blk]
        + [pl.BlockSpec(memory_space=pl.ANY)] * n_prev,
        out_specs=[blk] * 4, out_shape=[jax.ShapeDtypeStruct(shape, F32)] * 4,
        input_output_aliases={4 + k: k for k in range(n_prev)},
        compiler_params=_cparams(("parallel",)),
    )(parts, w, m, v, *(prev or ()))


def sum_parts(parts, *, name):
    P, R, C = parts.shape
    tr = _row_tile(R, C * (P * parts.dtype.itemsize + 4))

    def body(p_ref, o_ref):
        g = p_ref[0].astype(F32)
        for s in range(1, P):
            g = g + p_ref[s].astype(F32)
        o_ref[...] = g

    return pl.pallas_call(
        body, name=name, grid=(R // tr,),
        in_specs=[pl.BlockSpec((P, tr, C), lambda i: (0, i, 0))], out_specs=pl.BlockSpec((tr, C), lambda i: (i, 0)),
        out_shape=jax.ShapeDtypeStruct((R, C), F32), compiler_params=_cparams(("parallel",)),
    )(parts)


PACK_COLS = 1024
PACK_ROWS = 256


def _pack(arrays, cols, lead=()):
    n_lead = len(lead)
    flat = [a.reshape(lead + (-1,)) for a in arrays]
    total = sum(f.shape[-1] for f in flat)
    rows = -(-total // cols)
    mult = PACK_ROWS if rows > PACK_ROWS else 8
    rows = -(-rows // mult) * mult
    pad = rows * cols - total
    if pad:
        flat.append(jnp.zeros(lead + (pad,), flat[0].dtype))
    return jnp.concatenate(flat, axis=n_lead).reshape(lead + (rows, cols))


def _unpack(buf, shapes, lead=()):
    flat = buf.reshape(lead + (-1,))
    out, off = [], 0
    for s in shapes:
        n = math.prod(s)
        out.append(lax.slice_in_dim(flat, off, off + n, axis=len(lead)).reshape(lead + tuple(s)))
        off += n
    return out


def _to_shards(full, axis):
    s = full.shape
    return jnp.moveaxis(full.reshape(s[:axis] + (N_DEV, s[axis] // N_DEV) + s[axis + 1:]), axis, 0)


def _from_shards(g, axis):
    m = jnp.moveaxis(g, 0, axis)
    s = m.shape
    return m.reshape(s[:axis] + (s[axis] * s[axis + 1],) + s[axis + 2:])


BIG = (("w_mem_kv", 0), ("w_out", 1), ("w_up", 2), ("w_down", 1), ("s5_w_in", 1), ("s5_w_glu", 1),
       ("gdn_w_in", 2), ("fox_w_in", 1))
SMALL_SHARDED = (("s5_d_skip", 1), ("s5_b_glu", 1), ("gdn_conv_w", 2))
REPLICATED = ("mem_norm", "norm1", "norm2", "norm_f", "s5_lam_re", "s5_lam_im", "s5_log_dt", "s5_b_re", "s5_b_im",
              "s5_c_re", "s5_c_im", "gdn_a_log", "gdn_dt_bias", "gdn_o_norm", "fox_b_f")
WEIGHTS = ("mem_norm", "w_mem_kv", "norm1", "w_out", "norm2", "w_up", "w_down", "norm_f", "s5_w_in", "s5_lam_re",
           "s5_lam_im", "s5_log_dt", "s5_b_re", "s5_b_im", "s5_c_re", "s5_c_im", "s5_d_skip", "s5_w_glu", "s5_b_glu",
           "gdn_w_in", "gdn_conv_w", "gdn_a_log", "gdn_dt_bias", "gdn_o_norm", "fox_w_in", "fox_b_f")


def _relu2(acc):
    r = jnp.maximum(acc, 0.0)
    return acc, r * r


def _relu2_grad(acc, u):
    return (acc * 2.0 * jnp.maximum(u, 0.0),)


def _add(acc, e):
    return (acc + e,)


def _permute_in(w, kind, W):
    if kind == 0:
        return w
    n_main = (4 if kind == 1 else 3) * W
    n_small = w.shape[1] - n_main - MEM_WIDTH
    small = jnp.pad(w[:, n_main:n_main + n_small], ((0, 0), (0, MEM_WIDTH - n_small)))
    return jnp.concatenate([w[:, :n_main], w[:, n_main + n_small:], small], axis=1)


def _unpermute_in(dw, kind, W, n_small):
    if kind == 0:
        return dw
    n_main = (4 if kind == 1 else 3) * W
    return jnp.concatenate([dw[:, :n_main], dw[:, n_main + MEM_WIDTH:n_main + MEM_WIDTH + n_small],
                            dw[:, n_main:n_main + MEM_WIDTH]], axis=1)


def kernel(x, mem, mem_norm, w_mem_kv, norm1, w_out, norm2, w_up, w_down, norm_f, s5_w_in, s5_lam_re, s5_lam_im, s5_log_dt, s5_b_re, s5_b_im, s5_c_re, s5_c_im, s5_d_skip, s5_w_glu, s5_b_glu, gdn_w_in, gdn_conv_w, gdn_a_log, gdn_dt_bias, gdn_o_norm, fox_w_in, fox_b_f, loss_target, m_mem_norm, m_w_mem_kv, m_norm1, m_w_out, m_norm2, m_w_up, m_w_down, m_norm_f, m_s5_w_in, m_s5_lam_re, m_s5_lam_im, m_s5_log_dt, m_s5_b_re, m_s5_b_im, m_s5_c_re, m_s5_c_im, m_s5_d_skip, m_s5_w_glu, m_s5_b_glu, m_gdn_w_in, m_gdn_conv_w, m_gdn_a_log, m_gdn_dt_bias, m_gdn_o_norm, m_fox_w_in, m_fox_b_f, v_mem_norm, v_w_mem_kv, v_norm1, v_w_out, v_norm2, v_w_up, v_w_down, v_norm_f, v_s5_w_in, v_s5_lam_re, v_s5_lam_im, v_s5_log_dt, v_s5_b_re, v_s5_b_im, v_s5_c_re, v_s5_c_im, v_s5_d_skip, v_s5_w_glu, v_s5_b_glu, v_gdn_w_in, v_gdn_conv_w, v_gdn_a_log, v_gdn_dt_bias, v_gdn_o_norm, v_fox_w_in, v_fox_b_f):
    args = dict(locals())
    wsh = {n: args[n] for n in WEIGHTS}
    msh = {n: args["m_" + n] for n in WEIGHTS}
    vsh = {n: args["v_" + n] for n in WEIGHTS}
    h0, memx, target = x[0], mem[0], loss_target[0]
    S, D = h0.shape
    W = D - MEM_WIDTH
    depth = norm1.shape[0]
    me = 4 * lax.axis_index("x") + 2 * lax.axis_index("y") + lax.axis_index("c")

    bf = lambda t: t.astype(BF16)
    rows2d = lambda g: g.reshape(-1, g.shape[-1])

    def layer_sends(i, group):
        kind, j = i % 3, i // 3
        if group == "up":
            return {"w_up": bf(wsh["w_up"][i])}
        if group == "down":
            return {"w_down": bf(wsh["w_down"][i])}
        d = {"w_out": bf(wsh["w_out"][i])}
        if kind == 0:
            d["w_in"], d["w_glu"] = bf(wsh["s5_w_in"][j]), bf(wsh["s5_w_glu"][j])
        elif kind == 1:
            d["w_in"] = bf(wsh["gdn_w_in"][j])
        else:
            d["w_in"] = bf(_permute_in(wsh["fox_w_in"][j], 2, W))
        return d

    def as_comm(d):
        return [(v, True) for v in d.values()]

    first = {**layer_sends(0, "in"), **layer_sends(0, "up"), **layer_sends(0, "down")}
    small_w = _pack([wsh[n] for n, _ in SMALL_SHARDED], 128)
    got = exchange(list(first.values()) + [bf(w_mem_kv), small_w], [True] * (len(first) + 2), name="gather_first")
    gathered = [dict(zip(first, got)) if i == 0 else {} for i in range(depth)]
    full = {n: wsh[n] for n in REPLICATED}
    full["w_mem_kv"] = rows2d(got[len(first)])
    for (n, ax), g in zip(SMALL_SHARDED, _unpack(got[-1], [wsh[n].shape for n, _ in SMALL_SHARDED], lead=(N_DEV,))):
        full[n] = _from_shards(g, ax)

    def layer_params(i):
        kind, j = i % 3, i // 3
        g = gathered[i]
        w = {n: rows2d(g[n]) for n in ("w_out", "w_down", "w_glu") if n in g}
        w["w_up"] = g["w_up"]
        w["w_in"] = _permute_in(_from_shards(g["w_in"], 1), 1, W) if kind == 1 else rows2d(g["w_in"])
        if kind == 0:
            p = {k: full["s5_" + k][j] for k in ("lam_re", "lam_im", "log_dt", "b_re", "b_im", "c_re", "c_im",
                                                   "d_skip", "b_glu")}
            return kind, j, p, w, 0
        if kind == 1:
            p = {k: full["gdn_" + k][j] for k in ("conv_w", "a_log", "dt_bias", "o_norm")}
            return kind, j, p, w, 2 * (W // HEAD_DIM)
        return kind, j, {"b_f": full["fox_b_f"][j]}, w, W // HEAD_DIM

    def hosted(i, group):
        if i + 1 >= depth:
            return {}, []
        d = layer_sends(i + 1, group)
        return {"comm": as_comm(d)}, list(d)

    def keep(i, names, res):
        if not names:
            return res
        outs, got = res
        gathered[i + 1].update(zip(names, got))
        return outs

    memn = rms_fwd(memx, full["mem_norm"], out_dtype=BF16, name="mem_rms")
    mkv = mm(memn, full["w_mem_kv"], name="mem_kv")
    h = h0
    saved = []
    weights = []
    for i in range(depth):
        kind, j, p, w, n_small = layer_params(i)
        weights.append((w, p, n_small))
        a = rms_fwd(h, full["norm1"][i], out_dtype=BF16, name=f"rms1_{i}")
        kw, names = hosted(i, "in")
        proj = keep(i, names, mm(a, w["w_in"], name=f"in_proj_{i}", **kw))
        if kind == 0:
            mix, ms = s5_mixer_fwd(proj, W, p, w["w_glu"], f"l{i}")
            mem_col = W
        elif kind == 1:
            mix, ms = gdn_mixer_fwd(proj, W, p, f"l{i}")
            mem_col = 4 * W
        else:
            mix, ms = fox_mixer_fwd(proj, 3 * W + MEM_WIDTH, p["b_f"], f"l{i}")
            mem_col = 3 * W
        read = mem_fwd(proj, mem_col, mkv, name=f"mem_fwd_{i}")
        cat = jnp.concatenate([mix, read], axis=1)
        h_mid = mm(cat, w["w_out"], extras=(h,), epilogue=_add, name=f"out_proj_{i}")
        a2 = rms_fwd(h_mid, full["norm2"][i], out_dtype=BF16, name=f"rms2_{i}")
        kw, names = hosted(i, "up")
        u, act = keep(i, names, mm(a2, w["w_up"], b_shard8=True, epilogue=_relu2, out_dtypes=(F32, BF16),
                                   name=f"up_{i}", **kw))
        kw, names = hosted(i, "down")
        h_next = keep(i, names, mm(act, w["w_down"], extras=(h_mid,), epilogue=_add, name=f"down_{i}", **kw))
        saved.append((h, a, proj, ms, mem_col, cat, h_mid, a2, u, act))
        h = h_next

    loss_row, dh, dnf = final_loss(h, full["norm_f"], target, name="final_loss")
    loss = lax.psum(jnp.sum(loss_row), ("x", "y", "c"))

    grads = {n: [None] * full[n].shape[0] for n in ("norm1", "norm2")}
    for pre, cnt in (("s5_", (depth + 2) // 3), ("gdn_", (depth + 1) // 3), ("fox_", depth // 3)):
        for n in REPLICATED + tuple(n for n, _ in SMALL_SHARDED):
            if n.startswith(pre):
                grads[n] = [None] * cnt
    shares = {}
    by_dest = lambda g: g.reshape((N_DEV, g.shape[0] // N_DEV) + g.shape[1:])
    pending_up, pending_rest = {}, {}

    def carry(pending):
        return {"comm": [(v, False) for v in pending.values()]} if pending else {}

    def landed(pending, res):
        if not pending:
            return res
        outs, got = res
        shares.update(zip(pending, got))
        return outs

    dmkv = None
    for i in reversed(range(depth)):
        kind, j = i % 3, i // 3
        w, p, n_small = weights[i]
        h_in, a, proj, ms, mem_col, cat, h_mid, a2, u, act = saved[i]
        du = landed(pending_up, mm(dh, w["w_down"], tb=True, extras=(u,), epilogue=_relu2_grad, out_dtypes=(BF16,),
                                   name=f"d_act_{i}", **carry(pending_up)))
        dw_down = landed(pending_rest, mm(act, dh, ta=True, out_dtypes=(BF16,), name=f"dw_down_{i}",
                                          **carry(pending_rest)))
        pending = {("w_down", i): by_dest(dw_down)}
        da2 = landed(pending, mm(du, w["w_up"], tb=True, b_shard8=True, name=f"d_a2_{i}", **carry(pending)))
        pending_up = {("w_up", i): mm(a2, du, ta=True, out_shard8=True, out_dtypes=(BF16,), name=f"dw_up_{i}")}
        dh_mid, dn2 = rms_bwd(h_mid, full["norm2"][i], da2, dh, name=f"rms2_bwd_{i}")
        grads["norm2"][i] = dn2[0]
        dcat = mm(dh_mid, w["w_out"], tb=True, name=f"d_cat_{i}")
        pending_rest = {("w_out", i): by_dest(mm(cat, dh_mid, ta=True, out_dtypes=(BF16,), name=f"dw_out_{i}"))}
        dq_mem, dmkv_i = mem_bwd(proj, mem_col, mkv, dcat, W, name=f"mem_bwd_{i}")
        dmkv = dmkv_i if dmkv is None else dmkv + dmkv_i
        if kind == 0:
            dmain, g = s5_mixer_bwd(proj, W, p, w["w_glu"], ms, dcat, f"l{i}")
            dproj = jnp.concatenate([dmain, dq_mem], axis=1)
            pending_rest[("s5_w_glu", j)] = by_dest(g.pop("w_glu"))
            for k, val in g.items():
                grads["s5_" + k][j] = val
        elif kind == 1:
            dx, dgate, dab, g = gdn_mixer_bwd(proj, W, p, ms, dcat, f"l{i}")
            dproj = jnp.concatenate([dx, dgate, dq_mem, bf(jnp.pad(dab, ((0, 0), (0, MEM_WIDTH - 128))))], axis=1)
            for k, val in g.items():
                grads["gdn_" + k][j] = val
        else:
            dq, dk, dv, df, db = fox_mixer_bwd(proj, 3 * W + MEM_WIDTH, ms, dcat, f"l{i}")
            dproj = jnp.concatenate([dq, dk, dv, dq_mem.astype(F32), jnp.pad(df, ((0, 0), (0, MEM_WIDTH - 128)))],
                                    axis=1)
            grads["fox_b_f"][j] = db
        da = mm(dproj, w["w_in"], tb=True, name=f"d_a_{i}")
        in_name = ("s5_w_in", "gdn_w_in", "fox_w_in")[kind]
        if kind == 1:
            dw_in = _unpermute_in(mm(a, dproj, ta=True, name=f"dw_in_{i}"), 1, W, n_small)
            pending_rest[(in_name, j)] = bf(_to_shards(dw_in, 1))
        else:
            pending_rest[(in_name, j)] = by_dest(mm(a, dproj, ta=True, out_dtypes=(BF16,), name=f"dw_in_{i}"))
        dh, dn1 = rms_bwd(h_in, full["norm1"][i], da, dh_mid, name=f"rms1_bwd_{i}")
        grads["norm1"][i] = dn1[0]
    grad_x = dh[None]
    grads = {n: jnp.stack(v) for n, v in grads.items()}
    grads["norm_f"] = dnf[0]
    pending_rest[("w_mem_kv", None)] = by_dest(mm(memn, dmkv, ta=True, out_dtypes=(BF16,), name="dw_mem_kv"))
    dmemn = mm(dmkv, full["w_mem_kv"], tb=True, name="d_memn")
    _, dmn = rms_bwd(memx, full["mem_norm"], dmemn, None, name="mem_rms_bwd")
    grads["mem_norm"] = dmn[0]

    small_names = list(REPLICATED) + [n for n, _ in SMALL_SHARDED]
    last = {**pending_up, **pending_rest}
    got = exchange(list(last.values()) + [_pack([grads[n] for n in small_names], 128)],
                   [False] * len(last) + [True], name="exchange_last")
    shares.update(zip(last, got))

    big_out = [{}, {}, {}, {}]
    for n, _ in BIG:
        local = [t[n] for t in (wsh, msh, vsh)]
        if n == "w_mem_kv":
            outs = adamw(shares[(n, None)], *local, name=f"adamw_{n}")
        else:
            outs = None
            for idx in range(wsh[n].shape[0]):
                part = shares[(n, idx)]
                if n == "fox_w_in":
                    total = sum_parts(part, name=f"sum_{n}_{idx}")
                    part = _unpermute_in(total, 2, W, W // HEAD_DIM)[None]
                outs = adamw(part, *local, layer=idx, prev=outs, name=f"adamw_{n}_{idx}")
        for o, val in zip(big_out, outs):
            o[n] = val

    total = sum_parts(got[-1], name="sum_small_grads")
    gsmall = dict(zip(small_names, _unpack(total, [grads[n].shape for n in small_names])))
    for n, ax in SMALL_SHARDED:
        width = wsh[n].shape[ax]
        gsmall[n] = lax.dynamic_slice_in_dim(gsmall[n], me * width, width, axis=ax)
    outs = adamw(_pack([gsmall[n] for n in small_names], 128)[None],
                 *[_pack([t[n] for n in small_names], 128) for t in (wsh, msh, vsh)], name="adamw_small")
    small_out = [dict(zip(small_names, _unpack(o, [wsh[n].shape for n in small_names]))) for o in outs]

    res = [{**b, **s} for b, s in zip(big_out, small_out)]
    return (loss, grad_x, *[r[n] for r in res for n in WEIGHTS])
```
